```python
import jax, jax.numpy as jnp
from jax import lax
import numpy as np

D_MODEL = 1024
BATCH = 8
SEQ = 2048
DEPTH = 1
DEC_BATCH = 32
DEC_SEQ = 1
PAST_LEN = 16384
PAGE_SIZE = 128

HEAD_DIM = 64
N_HEADS_TOTAL = D_MODEL // HEAD_DIM
N_ATT_HEADS = N_HEADS_TOTAL // 2
N_RWKV_HEADS = N_HEADS_TOTAL - N_ATT_HEADS
ATT_WIDTH = N_ATT_HEADS * HEAD_DIM
RWKV_WIDTH = N_RWKV_HEADS * HEAD_DIM
DIL_PATTERNS = ((128, 1), (512, 4), (2048, 16))
MAX_WIN = max(w for w, _ in DIL_PATTERNS)
Q_BLOCK = 128
DECAY_LORA = max(32, int(round(1.8 * D_MODEL ** 0.5 / 32.0)) * 32)
AAA_LORA = max(32, int(round(1.8 * D_MODEL ** 0.5 / 32.0)) * 32)
GATE_LORA = max(32, int(round(0.6 * D_MODEL ** 0.8 / 32.0)) * 32)
D_FF = 256 * int(round(8 * D_MODEL / 3 / 256))
CONV_W = 3
LN_EPS = 1e-5
GN_EPS = 64e-5
ALPHA = (2.0 * DEPTH) ** 0.25
BETA = (8.0 * DEPTH) ** -0.25
RW_SPLITS = (RWKV_WIDTH, 2 * RWKV_WIDTH, 3 * RWKV_WIDTH, 3 * RWKV_WIDTH + DECAY_LORA, 3 * RWKV_WIDTH + DECAY_LORA + AAA_LORA)
RW_COLS = 3 * RWKV_WIDTH + DECAY_LORA + AAA_LORA + GATE_LORA
IN_COLS = 3 * ATT_WIDTH + RW_COLS

kernel_name = 'hymba_dilated_alibi_rwkv7_convffn_deepnorm_step'


def layer_norm(x, g, b):
    xf = x.astype(jnp.float32)
    mu = jnp.mean(xf, -1, keepdims=True)
    xc = xf - mu
    var = jnp.mean(xc * xc, -1, keepdims=True)
    return (xc * lax.rsqrt(var + LN_EPS) * g + b).astype(x.dtype)


def alibi_slopes(n):
    return jnp.exp2(-8.0 * jnp.arange(1, n + 1, dtype=jnp.float32) / n)


def dilated_window_attention(q, k_all, v_all, prefix):
    B, T, H, Dh = q.shape
    slopes = alibi_slopes(H)
    scale = HEAD_DIM ** -0.5
    pad = ((0, 0), (MAX_WIN, 0), (0, 0), (0, 0))
    k_pad = jnp.pad(k_all, pad)
    v_pad = jnp.pad(v_all, pad)
    qb = Q_BLOCK if T % Q_BLOCK == 0 else T
    nb = T // qb
    q_blocks = q.reshape(B, nb, qb, H, Dh).transpose(1, 0, 2, 3, 4)
    starts = prefix + jnp.arange(nb, dtype=jnp.int32) * qb

    def one_block(args):
        q_blk, start = args
        pos = start + jnp.arange(qb, dtype=jnp.int32)
        lses, outs = [], []
        for win, dil in DIL_PATTERNS:
            dist = jnp.arange(win // dil + 1, dtype=jnp.int32) * dil
            idx = pos[:, None] - dist[None, :]
            kg = jnp.take(k_pad, idx + MAX_WIN, axis=1)
            vg = jnp.take(v_pad, idx + MAX_WIN, axis=1)
            s = jnp.einsum('bqhd,bqnhd->bhqn', q_blk, kg, preferred_element_type=jnp.float32) * scale
            s = s - slopes[:, None, None] * dist.astype(jnp.float32)
            s = jnp.where((idx >= 0)[None, None], s, -jnp.inf)
            lse = jax.nn.logsumexp(s, axis=-1)
            p = jnp.exp(s - lse[..., None])
            outs.append(jnp.einsum('bhqn,bqnhd->bqhd', p, vg.astype(jnp.float32)))
            lses.append(lse)
        lse_all = jnp.stack(lses)
        wts = jnp.exp(lse_all - jax.nn.logsumexp(lse_all, axis=0, keepdims=True))
        wts = jnp.transpose(wts, (0, 1, 3, 2))[..., None]
        return jnp.sum(wts * jnp.stack(outs), axis=0).astype(q.dtype)

    out = lax.map(one_block, (q_blocks, starts))
    return out.transpose(1, 0, 2, 3, 4).reshape(B, T, H, Dh)


def rwkv7_time_mix(rw, wkv0, w0, w_lora_up, a0, a_lora_up, g_lora_up, k_k, k_a, r_k, gn_w, gn_b):
    f32 = jnp.float32
    B, T, _ = rw.shape
    r, k, v, w_low, a_low, g_low = jnp.split(rw.astype(f32), RW_SPLITS, axis=-1)
    w_log = -jax.nn.softplus(-(w0 + jnp.tanh(w_low) @ w_lora_up)) - 0.5
    decay = jnp.exp(-jnp.exp(w_log))
    a = jax.nn.sigmoid(a0 + a_low @ a_lora_up)
    g = jax.nn.sigmoid(g_low) @ g_lora_up
    heads = lambda t: t.reshape(B, T, N_RWKV_HEADS, HEAD_DIM)
    kk = heads(k * k_k)
    kk = kk / jnp.maximum(jnp.linalg.norm(kk, axis=-1, keepdims=True), 1e-12)
    k = heads(k * (1.0 + (a - 1.0) * k_a))
    r, v, decay, a = heads(r), heads(v), heads(decay), heads(a)

    def step(S, inp):
        r_t, w_t, k_t, v_t, kk_t, a_t = inp
        sa = jnp.einsum('bhvk,bhk->bhv', S, -kk_t)
        S = S * w_t[:, :, None, :] + sa[..., None] * (kk_t * a_t)[:, :, None, :] + v_t[..., None] * k_t[:, :, None, :]
        return S, jnp.einsum('bhvk,bhk->bhv', S, r_t)

    xs = tuple(jnp.swapaxes(t, 0, 1) for t in (r, decay, k, v, kk, a))
    S_T, ys = lax.scan(step, wkv0.astype(f32), xs)
    y = jnp.swapaxes(ys, 0, 1)
    mu = jnp.mean(y, -1, keepdims=True)
    yc = y - mu
    y = yc * lax.rsqrt(jnp.mean(yc * yc, -1, keepdims=True) + GN_EPS)
    y = y.reshape(B, T, RWKV_WIDTH) * gn_w + gn_b
    bonus = jnp.sum(r * k * r_k, -1, keepdims=True) * v
    y = (y + bonus.reshape(B, T, RWKV_WIDTH)) * g
    return y, S_T


def token_mixer(x, x_prev_row, k_cache, v_cache, wkv0, w_in, mu_shift, w0, w_lora_up, a0, a_lora_up,
                g_lora_up, k_k, k_a, r_k, gn_w, gn_b, w_out):
    B, T, _ = x.shape
    x_ext = jnp.concatenate([x_prev_row[:, None, :].astype(x.dtype), x], axis=1)
    proj = x_ext @ w_in
    cur = proj[:, 1:]
    q = cur[..., :ATT_WIDTH].reshape(B, T, N_ATT_HEADS, HEAD_DIM)
    k = cur[..., ATT_WIDTH:2 * ATT_WIDTH].reshape(B, T, N_ATT_HEADS, HEAD_DIM)
    v = cur[..., 2 * ATT_WIDTH:3 * ATT_WIDTH].reshape(B, T, N_ATT_HEADS, HEAD_DIM)
    if k_cache is None:
        k_all, v_all, prefix = k, v, 0
    else:
        k_all = jnp.concatenate([k_cache.astype(k.dtype), k], axis=1)
        v_all = jnp.concatenate([v_cache.astype(v.dtype), v], axis=1)
        prefix = k_cache.shape[1]
    att = dilated_window_attention(q, k_all, v_all, prefix)
    rw_cur = cur[..., 3 * ATT_WIDTH:]
    rw_prev = proj[:, :-1, 3 * ATT_WIDTH:]
    rw = rw_cur + (rw_prev - rw_cur) * mu_shift
    y_rw, wkv_T = rwkv7_time_mix(rw, wkv0, w0, w_lora_up, a0, a_lora_up, g_lora_up, k_k, k_a, r_k, gn_w, gn_b)
    mixed = jnp.concatenate([att.reshape(B, T, ATT_WIDTH).astype(jnp.float32), y_rw], axis=-1)
    return mixed.astype(x.dtype) @ w_out, k, v, wkv_T.astype(x.dtype)


def conv_ffn(x, conv_prev, w_ffn_in, conv_w, conv_b, w_ffn_out):
    hu = x @ w_ffn_in
    gate_pre, up = hu[..., :D_FF], hu[..., D_FF:]
    ext = jnp.concatenate([conv_prev.astype(gate_pre.dtype), gate_pre], axis=1)
    c = lax.conv_general_dilated(ext, conv_w[:, None, :].astype(ext.dtype), window_strides=(1,), padding='VALID',
                                 dimension_numbers=('NWC', 'WIO', 'NWC'), feature_group_count=D_FF) + conv_b
    h = jax.nn.silu(c) * up
    return h @ w_ffn_out, ext[:, -(CONV_W - 1):]


def decoder_layer(x, x_prev_row, k_cache, v_cache, wkv0, conv_prev, w_in, mu_shift, w0, w_lora_up, a0,
                  a_lora_up, g_lora_up, k_k, k_a, r_k, gn_w, gn_b, w_out, ln1_g, ln1_b, w_ffn_in, conv_w,
                  conv_b, w_ffn_out, ln2_g, ln2_b):
    h, k_new, v_new, wkv_T = token_mixer(x, x_prev_row, k_cache, v_cache, wkv0, w_in, mu_shift, w0, w_lora_up,
                                         a0, a_lora_up, g_lora_up, k_k, k_a, r_k, gn_w, gn_b, w_out)
    x1 = layer_norm(ALPHA * x + h, ln1_g, ln1_b)
    f, conv_T = conv_ffn(x1, conv_prev, w_ffn_in, conv_w, conv_b, w_ffn_out)
    y = layer_norm(ALPHA * x1 + f, ln2_g, ln2_b)
    return y, k_new, v_new, x[:, -1, :], wkv_T, conv_T


def setup_inputs(seed: int = 0) -> dict:
    key = jax.random.key(seed)
    ks = jax.random.split(key, 32)
    f32 = jnp.float32
    nrm = lambda k, shape, s: s * jax.random.normal(k, shape, f32)
    wbuf = min(MAX_WIN, PAST_LEN)
    col_scale = np.ones((IN_COLS,), np.float32)
    col_scale[2 * ATT_WIDTH:3 * ATT_WIDTH] = BETA
    col_scale[3 * ATT_WIDTH + 2 * RWKV_WIDTH:3 * ATT_WIDTH + 3 * RWKV_WIDTH] = BETA
    L = DEPTH
    return {
        'x_prompt': nrm(ks[0], (BATCH, SEQ, D_MODEL), 1.0),
        'x_sample': nrm(ks[1], (DEC_BATCH, DEC_SEQ, D_MODEL), 1.0),
        'cache_k_win': nrm(ks[2], (L, DEC_BATCH, wbuf, N_ATT_HEADS, HEAD_DIM), 1.0),
        'cache_v_win': nrm(ks[3], (L, DEC_BATCH, wbuf, N_ATT_HEADS, HEAD_DIM), BETA),
        'state_shift': nrm(ks[4], (L, DEC_BATCH, D_MODEL), 1.0),
        'state_wkv': nrm(ks[5], (L, DEC_BATCH, N_RWKV_HEADS, HEAD_DIM, HEAD_DIM), 0.3),
        'state_conv': nrm(ks[6], (L, DEC_BATCH, CONV_W - 1, D_FF), BETA),
        'w_in': nrm(ks[7], (L, D_MODEL, IN_COLS), D_MODEL ** -0.5) * jnp.asarray(col_scale),
        'mu_shift': jax.random.uniform(ks[8], (L, RW_COLS), f32, 0.1, 0.9),
        'w0': jax.random.uniform(ks[9], (L, RWKV_WIDTH), f32, -6.0, -1.0),
        'w_lora_up': nrm(ks[10], (L, DECAY_LORA, RWKV_WIDTH), 0.1 * DECAY_LORA ** -0.5),
        'a0': nrm(ks[11], (L, RWKV_WIDTH), 0.1),
        'a_lora_up': nrm(ks[12], (L, AAA_LORA, RWKV_WIDTH), 0.5 * AAA_LORA ** -0.5),
        'g_lora_up': nrm(ks[13], (L, GATE_LORA, RWKV_WIDTH), GATE_LORA ** -0.5),
        'k_k': 0.85 + nrm(ks[14], (L, RWKV_WIDTH), 0.02),
        'k_a': 1.0 + nrm(ks[15], (L, RWKV_WIDTH), 0.02),
        'r_k': nrm(ks[16], (L, N_RWKV_HEADS, HEAD_DIM), 0.1),
        'gn_w': 1.0 + nrm(ks[17], (L, RWKV_WIDTH), 0.02),
        'gn_b': nrm(ks[18], (L, RWKV_WIDTH), 0.01),
        'w_out': nrm(ks[19], (L, D_MODEL, D_MODEL), BETA * D_MODEL ** -0.5),
        'ln1_g': 1.0 + nrm(ks[20], (L, D_MODEL), 0.02),
        'ln1_b': nrm(ks[21], (L, D_MODEL), 0.01),
        'w_ffn_in': nrm(ks[22], (L, D_MODEL, 2 * D_FF), BETA * D_MODEL ** -0.5),
        'conv_w': nrm(ks[23], (L, CONV_W, D_FF), CONV_W ** -0.5),
        'conv_b': nrm(ks[24], (L, D_FF), 0.01),
        'w_ffn_out': nrm(ks[25], (L, D_FF, D_MODEL), BETA * D_FF ** -0.5),
        'ln2_g': 1.0 + nrm(ks[26], (L, D_MODEL), 0.02),
        'ln2_b': nrm(ks[27], (L, D_MODEL), 0.01),
    }


def reference(x_prompt, x_sample, cache_k_win, cache_v_win, state_shift, state_wkv, state_conv, w_in, mu_shift,
              w0, w_lora_up, a0, a_lora_up, g_lora_up, k_k, k_a, r_k, gn_w, gn_b, w_out, ln1_g, ln1_b, w_ffn_in,
              conv_w, conv_b, w_ffn_out, ln2_g, ln2_b):
    hp, hs = x_prompt, x_sample
    Bp = hp.shape[0]
    kp_l, vp_l, ks_l, vs_l, sp_l, ss_l, wp_l, ws_l, cp_l, cs_l = [], [], [], [], [], [], [], [], [], []
    for l in range(DEPTH):
        lw = (w_in[l], mu_shift[l], w0[l], w_lora_up[l], a0[l], a_lora_up[l], g_lora_up[l], k_k[l], k_a[l],
              r_k[l], gn_w[l], gn_b[l], w_out[l], ln1_g[l], ln1_b[l], w_ffn_in[l], conv_w[l], conv_b[l],
              w_ffn_out[l], ln2_g[l], ln2_b[l])
        zero_row = jnp.zeros((Bp, D_MODEL), hp.dtype)
        zero_wkv = jnp.zeros((Bp, N_RWKV_HEADS, HEAD_DIM, HEAD_DIM), hp.dtype)
        zero_conv = jnp.zeros((Bp, CONV_W - 1, D_FF), hp.dtype)
        x_in_p = hp
        hp, kp, vp, sp, wp, cp = decoder_layer(x_in_p, zero_row, None, None, zero_wkv, zero_conv, *lw)
        hs, kn, vn, sn, wn, cn = decoder_layer(hs, state_shift[l], cache_k_win[l], cache_v_win[l], state_wkv[l],
                                               state_conv[l], *lw)
        n_keep = min(MAX_WIN, kp.shape[1])
        kp_l.append(kp[:, kp.shape[1] - n_keep:])
        vp_l.append(vp[:, vp.shape[1] - n_keep:])
        ks_l.append(kn)
        vs_l.append(vn)
        sp_l.append(sp)
        ss_l.append(sn)
        wp_l.append(wp)
        ws_l.append(wn)
        cp_l.append(cp)
        cs_l.append(cn)
    k_win_prompt = jnp.stack(kp_l)
    v_win_prompt = jnp.stack(vp_l)
    k_new_sample = jnp.stack(ks_l)
    v_new_sample = jnp.stack(vs_l)
    shift_prompt = jnp.stack(sp_l)
    shift_sample = jnp.stack(ss_l)
    wkv_prompt = jnp.stack(wp_l)
    wkv_sample = jnp.stack(ws_l)
    conv_prompt = jnp.stack(cp_l)
    conv_sample = jnp.stack(cs_l)
    return (hp, hs, k_win_prompt, v_win_prompt, k_new_sample, v_new_sample, shift_prompt, shift_sample,
            wkv_prompt, wkv_sample, conv_prompt, conv_sample)
```

```python
import functools

import jax
import jax.numpy as jnp
from jax import lax
from jax.experimental import pallas as pl
from jax.experimental.pallas import tpu as pltpu

F32 = jnp.float32
BF16 = jnp.bfloat16

HEAD_DIM = 64
LANES = 128
DIL_PATTERNS = ((128, 1), (512, 4), (2048, 16))
WIN_STEPS = 128
Q_TILE = 128
CHUNK = 64
LN_EPS = 1e-5
GN_EPS = 64e-5
NEG = -1e30
VMEM_LIMIT = 56 * 1024 * 1024

_DN = {"NN": (((1,), (0,)), ((), ())), "NT": (((1,), (1,)), ((), ())), "TN": (((0,), (0,)), ((), ()))}


def _dot(a, b, dims="NN"):
    return lax.dot_general(a.astype(BF16), b.astype(BF16), _DN[dims], preferred_element_type=F32)


def _split2(x):
    hi = x.astype(BF16)
    lo = (x - hi.astype(F32)).astype(BF16)
    return hi, lo


def _dot_const(x, c):
    hi, lo = _split2(x)
    return _dot(hi, c) + _dot(lo, c)


def _dot_const_l(c, x):
    hi = x.astype(BF16)
    r1 = x - hi.astype(F32)
    mid = r1.astype(BF16)
    lo = (r1 - mid.astype(F32)).astype(BF16)
    return _dot(c, hi) + _dot(c, mid) + _dot(c, lo)


def _sigmoid(x):
    return 1.0 / (1.0 + jnp.exp(-x))


def _layer_norm(x, g, b):
    mu = jnp.mean(x, axis=-1, keepdims=True)
    xc = x - mu
    var = jnp.mean(xc * xc, axis=-1, keepdims=True)
    return xc * lax.rsqrt(var + LN_EPS) * g + b


def _cparams(n_grid):
    return pltpu.CompilerParams(dimension_semantics=("arbitrary",) * n_grid, vmem_limit_bytes=VMEM_LIMIT)


def _rwkv_prep(rw, w0, a0, k_k, k_a, r_k, wl_pad, al_pad, g_pad, bd, aw):
    r = rw[:, 0:aw]
    k = rw[:, aw:2 * aw]
    v = rw[:, 2 * aw:3 * aw]
    lo = rw[:, 3 * aw:3 * aw + 128]
    glo = rw[:, 3 * aw + 128:]
    z = -(w0 + _dot(jnp.tanh(lo), wl_pad))
    softplus = jnp.maximum(z, 0.0) + jnp.log(1.0 + jnp.exp(-jnp.abs(z)))
    lw = -jnp.exp(-softplus - 0.5)
    a = _sigmoid(a0 + _dot(lo, al_pad))
    g = _dot(_sigmoid(glo), g_pad)
    kk = k * k_k
    norm = jnp.sqrt(_dot(kk * kk, bd))
    kkn = kk / jnp.maximum(norm, 1e-12)
    k2 = k * (1.0 + (a - 1.0) * k_a)
    bonus = _dot_const(r * k2 * r_k, bd) * v
    return r, lw, k2, v, -kkn, kkn * a, g, bonus


def _inproj_seq_kernel(x_ref, w_ref, mu_ref, w0_ref, a0_ref, kk_ref, ka_ref, rk_ref, wl_ref, al_ref, gp_ref,
                       bd_ref, q_ref, k_ref, v_ref, r_o, lw_o, k2_o, vr_o, al_o, be_o, g_o, bo_o, carry_ref, *, aw):
    @pl.when(pl.program_id(1) == 0)
    def _():
        carry_ref[...] = jnp.zeros_like(carry_ref)

    p = _dot(x_ref[0], w_ref[...])
    q_ref[0] = p[:, 0:aw]
    k_ref[0] = p[:, aw:2 * aw]
    v_ref[0] = p[:, 2 * aw:3 * aw]
    prw = p[:, 3 * aw:]
    tm = prw.shape[0]
    row = lax.broadcasted_iota(jnp.int32, prw.shape, 0)
    prev = jnp.where(row == 0, carry_ref[0:1, :], pltpu.roll(prw, 1, axis=0))
    carry_ref[0:1, :] = prw[tm - 1:tm, :]
    rw = prw + (prev - prw) * mu_ref[...]
    outs = _rwkv_prep(rw, w0_ref[...], a0_ref[...], kk_ref[...], ka_ref[...], rk_ref[...], wl_ref[...],
                      al_ref[...], gp_ref[...], bd_ref[...], aw)
    for o_ref, val in zip((r_o, lw_o, k2_o, vr_o, al_o, be_o, g_o, bo_o), outs):
        o_ref[0] = val


def _inproj_step_kernel(x_ref, w_ref, mu_ref, w0_ref, a0_ref, kk_ref, ka_ref, rk_ref, wl_ref, al_ref, gp_ref,
                        bd_ref, q_ref, k_ref, v_ref, r_o, lw_o, k2_o, vr_o, al_o, be_o, g_o, bo_o, *, aw, nb):
    p = _dot(x_ref[...], w_ref[...])
    q_ref[...] = p[0:nb, 0:aw]
    k_ref[...] = p[0:nb, aw:2 * aw]
    v_ref[...] = p[0:nb, 2 * aw:3 * aw]
    prw = p[0:nb, 3 * aw:]
    prev = p[nb:2 * nb, 3 * aw:]
    rw = prw + (prev - prw) * mu_ref[...]
    outs = _rwkv_prep(rw, w0_ref[...], a0_ref[...], kk_ref[...], ka_ref[...], rk_ref[...], wl_ref[...],
                      al_ref[...], gp_ref[...], bd_ref[...], aw)
    for o_ref, val in zip((r_o, lw_o, k2_o, vr_o, al_o, be_o, g_o, bo_o), outs):
        o_ref[...] = val


def _full(shape):
    nd = len(shape)
    return pl.BlockSpec(shape, lambda *_: (0,) * nd)


def _inproj_seq(x, prm, tm):
    B, T, D = x.shape
    aw = prm["aw"]
    ncol = prm["w_in"].shape[1]
    nrw = ncol - 3 * aw
    small = [prm[n] for n in ("mu", "w0", "a0", "k_k", "k_a", "r_k", "wl_pad", "al_pad", "g_pad", "bd")]
    row_spec = pl.BlockSpec((1, tm, aw), lambda b, t: (b, t, 0))
    out_sds = jax.ShapeDtypeStruct((B, T, aw), F32)
    return pl.pallas_call(
        functools.partial(_inproj_seq_kernel, aw=aw),
        grid=(B, T // tm),
        in_specs=[pl.BlockSpec((1, tm, D), lambda b, t: (b, t, 0)), _full(prm["w_in"].shape)]
        + [_full(s.shape) for s in small],
        out_specs=[row_spec] * 11,
        out_shape=[out_sds] * 11,
        scratch_shapes=[pltpu.VMEM((8, nrw), F32)],
        compiler_params=_cparams(2),
        name="inproj_seq",
    )(x, prm["w_in"], *small)


def _inproj_step(x2, prm):
    nb = x2.shape[0] // 2
    aw = prm["aw"]
    small = [prm[n] for n in ("mu", "w0", "a0", "k_k", "k_a", "r_k", "wl_pad", "al_pad", "g_pad", "bd")]
    out_sds = jax.ShapeDtypeStruct((nb, aw), F32)
    return pl.pallas_call(
        functools.partial(_inproj_step_kernel, aw=aw, nb=nb),
        grid=(1,),
        in_specs=[_full(x2.shape), _full(prm["w_in"].shape)] + [_full(s.shape) for s in small],
        out_specs=[_full((nb, aw))] * 11,
        out_shape=[out_sds] * 11,
        compiler_params=_cparams(1),
        name="inproj_step",
    )(x2, prm["w_in"], *small)


def _win_attn_kernel(q_ref, k_ref, v_ref, sl_ref, acc_ref, m_ref, l_ref, *, L, dil):
    hp = pl.program_id(2)
    nb = L // Q_TILE
    scale = HEAD_DIM ** -0.5
    lane = lax.broadcasted_iota(jnp.int32, (1, LANES), 1)
    head_masks = ((lane < HEAD_DIM).astype(F32), (lane >= HEAD_DIM).astype(F32))
    qi = lax.broadcasted_iota(jnp.int32, (Q_TILE, Q_TILE), 0)
    ki = lax.broadcasted_iota(jnp.int32, (Q_TILE, Q_TILE), 1)
    dist_cur = ((qi - ki) * dil).astype(F32)
    dist_prev = ((qi - ki + Q_TILE) * dil).astype(F32)
    ok_cur = ki <= qi
    ok_prev = ki >= qi
    first_lanes = lax.broadcasted_iota(jnp.int32, (Q_TILE, LANES), 1) < HEAD_DIM

    def tile(start, prev_start):
        q = q_ref[0, pl.ds(start, Q_TILE), :]
        kc = k_ref[0, pl.ds(start, Q_TILE), :]
        vc = v_ref[0, pl.ds(start, Q_TILE), :]
        if prev_start is not None:
            kp = k_ref[0, pl.ds(prev_start, Q_TILE), :]
            vp = v_ref[0, pl.ds(prev_start, Q_TILE), :]
        res = []
        for j in range(2):
            slope = sl_ref[pl.ds(2 * hp + j, 1), :]
            qm = q * head_masks[j]
            s_c = jnp.where(ok_cur, _dot(qm, kc, "NT") * scale - slope * dist_cur, NEG)
            m = jnp.max(s_c, axis=-1, keepdims=True)
            if prev_start is not None:
                s_p = jnp.where(ok_prev, _dot(qm, kp, "NT") * scale - slope * dist_prev, NEG)
                m = jnp.maximum(m, jnp.max(s_p, axis=-1, keepdims=True))
            p_c = jnp.exp(s_c - m)
            l = jnp.sum(p_c, axis=-1, keepdims=True)
            acc = _dot(p_c, vc)
            if prev_start is not None:
                p_p = jnp.exp(s_p - m)
                l = l + jnp.sum(p_p, axis=-1, keepdims=True)
                acc = acc + _dot(p_p, vp)
            res.append((acc, m, l))
        (a0, m0, l0), (a1, m1, l1) = res
        acc_ref[0, pl.ds(start, Q_TILE), :] = jnp.where(first_lanes, a0, a1)
        m_ref[0, pl.ds(start, Q_TILE), :] = jnp.where(first_lanes, m0, m1)
        l_ref[0, pl.ds(start, Q_TILE), :] = jnp.where(first_lanes, l0, l1)

    tile(0, None)
    if nb > 1:
        def body(jb, carry):
            start = pl.multiple_of(jb * Q_TILE, Q_TILE)
            tile(start, pl.multiple_of(start - Q_TILE, Q_TILE))
            return carry

        lax.fori_loop(1, nb, body, 0)


def _win_attn(q, k, v, slopes_tab, dil):
    B, T, W = q.shape
    L = T // dil
    npair = W // LANES
    view = lambda a: a.reshape(B, L, dil * W)
    spec = pl.BlockSpec((1, L, LANES), lambda b, s, hp: (b, 0, s * npair + hp))
    sds = jax.ShapeDtypeStruct((B, L, dil * W), F32)
    acc, m, l = pl.pallas_call(
        functools.partial(_win_attn_kernel, L=L, dil=dil),
        grid=(B, dil, npair),
        in_specs=[spec, spec, spec, _full(slopes_tab.shape)],
        out_specs=[spec] * 3,
        out_shape=[sds] * 3,
        compiler_params=_cparams(3),
        name=f"win_attn_d{dil}",
    )(view(q), view(k), view(v), slopes_tab)
    back = lambda a: a.reshape(B, T, W)
    return back(acc), back(m), back(l)


def _rwkv_chunk_kernel(r_ref, lw_ref, k_ref, v_ref, al_ref, be_ref, g_ref, bo_ref, gnw_ref, gnb_ref,
                       y_ref, s_ref, *, T):
    C = CHUNK
    n2 = 2 * C
    lane = lax.broadcasted_iota(jnp.int32, (1, LANES), 1)
    m0 = (lane < HEAD_DIM).astype(F32)
    m1 = 1.0 - m0
    ri = lax.broadcasted_iota(jnp.int32, (n2, n2), 0)
    ci = lax.broadcasted_iota(jnp.int32, (n2, n2), 1)
    same = (ri < C) == (ci < C)
    strict = same & ((ri & (C - 1)) > (ci & (C - 1)))
    incl = same & ((ri & (C - 1)) >= (ci & (C - 1)))
    eye = (ri == ci).astype(F32)
    tri = (lax.broadcasted_iota(jnp.int32, (C, C), 0) >= lax.broadcasted_iota(jnp.int32, (C, C), 1)).astype(BF16)
    li = lax.broadcasted_iota(jnp.int32, (LANES, LANES), 0)
    lj = lax.broadcasted_iota(jnp.int32, (LANES, LANES), 1)
    head_mean = jnp.where((li < HEAD_DIM) == (lj < HEAD_DIM), 1.0 / HEAD_DIM, 0.0).astype(BF16)
    gnw = gnw_ref[...]
    gnb = gnb_ref[...]

    def stack(x):
        return jnp.concatenate([x * m0, x * m1], axis=0)

    def twice(x):
        return jnp.concatenate([x, x], axis=0)

    def body(c, S):
        rows = pl.ds(pl.multiple_of(c * C, C), C)
        r = r_ref[0, rows, :]
        lw = lw_ref[0, rows, :]
        k = k_ref[0, rows, :]
        v = v_ref[0, rows, :]
        al = al_ref[0, rows, :]
        be = be_ref[0, rows, :]
        cl = _dot_const_l(tri, lw)
        e_pos = jnp.exp(cl)
        e_neg = jnp.exp(-cl)
        g_end = e_pos[C - 1:C, :]
        rt = r * e_pos
        kt = k * e_neg
        bt = be * e_neg
        at = al * jnp.exp(cl - lw)
        a_st, r_st, v_st = stack(at), stack(rt), stack(v)
        k2, b2 = twice(kt), twice(bt)
        l_ak = jnp.where(strict, _dot(a_st, k2, "NT"), 0.0)
        l_ab = jnp.where(strict, _dot(a_st, b2, "NT"), 0.0)
        l_rk = jnp.where(incl, _dot(r_st, k2, "NT"), 0.0)
        l_rb = jnp.where(incl, _dot(r_st, b2, "NT"), 0.0)
        tinv = eye + l_ab
        x = l_ab
        for _ in range(C.bit_length() - 2):
            x = _dot(x, x)
            tinv = tinv + _dot(tinv, x)
        w = _dot(a_st, S, "NT") + _dot(l_ak, v_st)
        u = _dot(tinv, w)
        y_st = _dot(r_st, S, "NT") + _dot(l_rk, v_st) + _dot(l_rb, u)
        y = y_st[0:C, :] + y_st[C:n2, :]
        s_new = S * g_end + _dot(v_st, stack(kt * g_end), "TN") + _dot(u, stack(bt * g_end), "TN")
        mu = _dot_const(y, head_mean)
        yc = y - mu
        var = _dot_const(yc * yc, head_mean)
        yn = yc * lax.rsqrt(var + GN_EPS) * gnw + gnb
        y_ref[0, rows, :] = (yn + bo_ref[0, rows, :]) * g_ref[0, rows, :]
        return s_new

    s_ref[0, 0] = lax.fori_loop(0, T // C, body, jnp.zeros((LANES, LANES), F32))


def _rwkv_chunked(r, lw, k2, v, al, be, g, bonus, gn_w, gn_b):
    B, T, W = r.shape
    npair = W // LANES
    spec = pl.BlockSpec((1, T, LANES), lambda b, hp: (b, 0, hp))
    vec = pl.BlockSpec((1, LANES), lambda b, hp: (0, hp))
    y, s = pl.pallas_call(
        functools.partial(_rwkv_chunk_kernel, T=T),
        grid=(B, npair),
        in_specs=[spec] * 8 + [vec, vec],
        out_specs=[spec, pl.BlockSpec((1, 1, LANES, LANES), lambda b, hp: (b, hp, 0, 0))],
        out_shape=[jax.ShapeDtypeStruct((B, T, W), F32), jax.ShapeDtypeStruct((B, npair, LANES, LANES), F32)],
        compiler_params=_cparams(2),
        name="rwkv_chunk",
    )(r, lw, k2, v, al, be, g, bonus, gn_w, gn_b)
    s = s.reshape(B, npair, 2, HEAD_DIM, 2, HEAD_DIM)
    s = jnp.stack([s[:, :, 0, :, 0, :], s[:, :, 1, :, 1, :]], axis=2)
    return y, s.reshape(B, 2 * npair, HEAD_DIM, HEAD_DIM)


def _rwkv_step_kernel(s_ref, r_ref, lw_ref, k_ref, al_ref, be_ref, v_ref, g_ref, bo_ref, gnw_ref, gnb_ref,
                      y_ref, so_ref):
    S = s_ref[0]
    sa = jnp.sum(S * al_ref[0], axis=-1, keepdims=True)
    s_new = S * jnp.exp(lw_ref[0]) + sa * be_ref[0] + v_ref[0] * k_ref[0]
    so_ref[0] = s_new
    y = jnp.sum(s_new * r_ref[0], axis=-1, keepdims=True)
    mu = jnp.mean(y, axis=1, keepdims=True)
    yc = y - mu
    var = jnp.mean(yc * yc, axis=1, keepdims=True)
    yn = yc * lax.rsqrt(var + GN_EPS) * gnw_ref[...] + gnb_ref[...]
    y_ref[0] = (yn + bo_ref[0]) * g_ref[0]


def _rwkv_step(state, r, lw, k2, v, al, be, g, bonus, gn_w, gn_b):
    nb, H = state.shape[0], state.shape[1]
    rowv = lambda a: a.reshape(nb, H, 1, HEAD_DIM)
    colv = lambda a: a.reshape(nb, H, HEAD_DIM, 1)
    s_spec = pl.BlockSpec((1, H, HEAD_DIM, HEAD_DIM), lambda b: (b, 0, 0, 0))
    r_spec = pl.BlockSpec((1, H, 1, HEAD_DIM), lambda b: (b, 0, 0, 0))
    c_spec = pl.BlockSpec((1, H, HEAD_DIM, 1), lambda b: (b, 0, 0, 0))
    p_spec = _full((H, HEAD_DIM, 1))
    y, s_new = pl.pallas_call(
        _rwkv_step_kernel,
        grid=(nb,),
        in_specs=[s_spec] + [r_spec] * 5 + [c_spec] * 3 + [p_spec] * 2,
        out_specs=[c_spec, s_spec],
        out_shape=[jax.ShapeDtypeStruct((nb, H, HEAD_DIM, 1), F32), jax.ShapeDtypeStruct(state.shape, F32)],
        compiler_params=_cparams(1),
        name="rwkv_step",
    )(state, rowv(r), rowv(lw), rowv(k2), rowv(al), rowv(be), colv(v), colv(g), colv(bonus),
      gn_w.reshape(H, HEAD_DIM, 1), gn_b.reshape(H, HEAD_DIM, 1))
    return y.reshape(nb, H * HEAD_DIM), s_new


def _attn_step_kernel(q_ref, kn_ref, vn_ref, k1_ref, v1_ref, k2_ref, v2_ref, k3_ref, v3_ref, sl_ref, bd_ref,
                      o_ref):
    scale = HEAD_DIM ** -0.5
    q = q_ref[0]
    bd = bd_ref[...]
    slope = sl_ref[...]
    step = lax.broadcasted_iota(jnp.int32, (WIN_STEPS, 1), 0)
    back = (WIN_STEPS - step).astype(F32)
    s0 = _dot_const(q * kn_ref[0], bd) * scale
    scores = []
    m = s0
    for (_, dil), k_ref in zip(DIL_PATTERNS, (k1_ref, k2_ref, k3_ref)):
        s = _dot_const(k_ref[0] * q, bd) * scale - slope * (back * dil)
        scores.append(s)
        m = jnp.maximum(m, jnp.max(s, axis=0, keepdims=True))
    e0 = len(DIL_PATTERNS) * jnp.exp(s0 - m)
    l = e0
    acc = e0 * vn_ref[0]
    for s, v_ref in zip(scores, (v1_ref, v2_ref, v3_ref)):
        p = jnp.exp(s - m)
        l = l + jnp.sum(p, axis=0, keepdims=True)
        acc = acc + jnp.sum(p * v_ref[0], axis=0, keepdims=True)
    o_ref[0] = acc / l


def _attn_step(q, k_new, v_new, k_cache, v_cache, slope_lane, bd):
    nb, W = q.shape
    P = k_cache.shape[1]
    row3 = lambda a: a.reshape(nb, 1, W)
    args, specs = [], []
    for _, dil in DIL_PATTERNS:
        blk = P // dil // WIN_STEPS - 1
        spec = pl.BlockSpec((1, WIN_STEPS, W), lambda b, blk=blk: (b, blk, 0))
        for c in (k_cache, v_cache):
            args.append(c.reshape(nb, P // dil, dil * W))
            specs.append(spec)
    vec = pl.BlockSpec((1, 1, W), lambda b: (b, 0, 0))
    out = pl.pallas_call(
        _attn_step_kernel,
        grid=(nb,),
        in_specs=[vec, vec, vec] + specs + [_full(slope_lane.shape), _full(bd.shape)],
        out_specs=vec,
        out_shape=jax.ShapeDtypeStruct((nb, 1, W), F32),
        compiler_params=_cparams(1),
        name="attn_step",
    )(row3(q), row3(k_new), row3(v_new), *args, slope_lane, bd)
    return out.reshape(nb, W)


def _outproj_kernel(*refs, n_branch, aw, alpha):
    if n_branch:
        stats = refs[:3 * n_branch]
        refs = refs[3 * n_branch:]
        accs, ms, ls = stats[0::3], stats[1::3], stats[2::3]
        m = ms[0][...]
        for mr in ms[1:]:
            m = jnp.maximum(m, mr[...])
        num = 0.0
        den = 0.0
        for ar, mr, lr in zip(accs, ms, ls):
            e = jnp.exp(mr[...] - m)
            num = num + ar[...] * e
            den = den + lr[...] * e
        att = num / den
    else:
        att = refs[0][...]
        refs = refs[1:]
    y_ref, x_ref, w_ref, g_ref, b_ref, o_ref = refs
    h = _dot(att, w_ref[0:aw, :]) + _dot(y_ref[...], w_ref[aw:, :])
    o_ref[...] = _layer_norm(alpha * x_ref[...] + h, g_ref[...], b_ref[...])


def _outproj(att_parts, y_rw, x, w_out, ln_g, ln_b, tm, alpha):
    rows, D = x.shape
    aw = y_rw.shape[1]
    n_branch = len(att_parts) // 3
    half = pl.BlockSpec((tm, aw), lambda i: (i, 0))
    return pl.pallas_call(
        functools.partial(_outproj_kernel, n_branch=n_branch, aw=aw, alpha=alpha),
        grid=(rows // tm,),
        in_specs=[half] * (len(att_parts) + 1)
        + [pl.BlockSpec((tm, D), lambda i: (i, 0)), _full(w_out.shape), _full(ln_g.shape), _full(ln_b.shape)],
        out_specs=pl.BlockSpec((tm, D), lambda i: (i, 0)),
        out_shape=jax.ShapeDtypeStruct((rows, D), F32),
        compiler_params=_cparams(1),
        name="outproj_ln",
    )(*att_parts, y_rw, x, w_out, ln_g, ln_b)


def _ffn_tail(x1, gate_pre, prev1, prev2, up, cw_ref, cb_ref, w2_ref, g_ref, b_ref, alpha):
    c = cw_ref[0:1, :] * prev2 + cw_ref[1:2, :] * prev1 + cw_ref[2:3, :] * gate_pre + cb_ref[...]
    h = c * _sigmoid(c) * up
    return _layer_norm(alpha * x1 + _dot(h, w2_ref[...]), g_ref[...], b_ref[...])


def _ffn_seq_kernel(x_ref, w1_ref, cw_ref, cb_ref, w2_ref, g_ref, b_ref, o_ref, tail_ref, carry_ref, *, dff, alpha):
    @pl.when(pl.program_id(1) == 0)
    def _():
        carry_ref[...] = jnp.zeros_like(carry_ref)

    x1 = x_ref[0]
    hu = _dot(x1, w1_ref[...])
    gate_pre = hu[:, 0:dff]
    up = hu[:, dff:]
    tm = gate_pre.shape[0]
    row = lax.broadcasted_iota(jnp.int32, gate_pre.shape, 0)
    c0 = carry_ref[0:1, :]
    c1 = carry_ref[1:2, :]
    prev1 = jnp.where(row == 0, c1, pltpu.roll(gate_pre, 1, axis=0))
    prev2 = jnp.where(row == 0, c0, jnp.where(row == 1, c1, pltpu.roll(gate_pre, 2, axis=0)))
    last2 = gate_pre[tm - 2:tm, :]
    carry_ref[0:2, :] = last2
    tail_ref[0] = last2
    o_ref[0] = _ffn_tail(x1, gate_pre, prev1, prev2, up, cw_ref, cb_ref, w2_ref, g_ref, b_ref, alpha)


def _ffn_step_kernel(x_ref, p2_ref, p1_ref, w1_ref, cw_ref, cb_ref, w2_ref, g_ref, b_ref, o_ref, gate_ref, *,
                     dff, alpha):
    x1 = x_ref[...]
    hu = _dot(x1, w1_ref[...])
    gate_pre = hu[:, 0:dff]
    gate_ref[...] = gate_pre
    o_ref[...] = _ffn_tail(x1, gate_pre, p1_ref[...], p2_ref[...], hu[:, dff:], cw_ref, cb_ref, w2_ref, g_ref,
                           b_ref, alpha)


def _resident(shape):
    nd = len(shape)
    return pl.BlockSpec(shape, lambda *_: (0,) * nd, pipeline_mode=pl.Buffered(1))


def _ffn_seq(x1, w1, conv_w, conv_b, w2, ln_g, ln_b, tm, alpha):
    B, T, D = x1.shape
    dff = w2.shape[0]
    return pl.pallas_call(
        functools.partial(_ffn_seq_kernel, dff=dff, alpha=alpha),
        grid=(B, T // tm),
        in_specs=[pl.BlockSpec((1, tm, D), lambda b, t: (b, t, 0)), _resident(w1.shape), _full(conv_w.shape),
                  _full(conv_b.shape), _resident(w2.shape), _full(ln_g.shape), _full(ln_b.shape)],
        out_specs=[pl.BlockSpec((1, tm, D), lambda b, t: (b, t, 0)),
                   pl.BlockSpec((1, 2, dff), lambda b, t: (b, 0, 0))],
        out_shape=[jax.ShapeDtypeStruct((B, T, D), F32), jax.ShapeDtypeStruct((B, 2, dff), F32)],
        scratch_shapes=[pltpu.VMEM((8, dff), F32)],
        compiler_params=_cparams(2),
        name="ffn_seq",
    )(x1, w1, conv_w, conv_b, w2, ln_g, ln_b)


def _ffn_step(x1, prev2, prev1, w1, conv_w, conv_b, w2, ln_g, ln_b, alpha):
    nb, D = x1.shape
    dff = w2.shape[0]
    return pl.pallas_call(
        functools.partial(_ffn_step_kernel, dff=dff, alpha=alpha),
        grid=(1,),
        in_specs=[_full(x1.shape), _full(prev2.shape), _full(prev1.shape), _resident(w1.shape),
                  _full(conv_w.shape), _full(conv_b.shape), _resident(w2.shape), _full(ln_g.shape),
                  _full(ln_b.shape)],
        out_specs=[_full((nb, D)), _full((nb, dff))],
        out_shape=[jax.ShapeDtypeStruct((nb, D), F32), jax.ShapeDtypeStruct((nb, dff), F32)],
        compiler_params=_cparams(1),
        name="ffn_step",
    )(x1, prev2, prev1, w1, conv_w, conv_b, w2, ln_g, ln_b)


def _layer_params(w_in, mu_shift, w0, w_lora_up, a0, a_lora_up, g_lora_up, k_k, k_a, r_k, aw):
    D, ncol = w_in.shape
    n_w, n_a, n_g = w_lora_up.shape[0], a_lora_up.shape[0], g_lora_up.shape[0]
    assert n_w + n_a == LANES and 3 * aw + 3 * aw + n_w + n_a + n_g == ncol
    g_cols = -(-n_g // LANES) * LANES
    pad = g_cols - n_g
    row = lambda a: a.reshape(1, -1).astype(F32)
    lane = jnp.arange(aw)
    bd = (lane[:, None] // HEAD_DIM == lane[None, :] // HEAD_DIM).astype(BF16)
    return dict(
        aw=aw,
        w_in=jnp.pad(w_in, ((0, 0), (0, pad))).astype(BF16),
        mu=jnp.pad(row(mu_shift), ((0, 0), (0, pad))),
        w0=row(w0), a0=row(a0), k_k=row(k_k), k_a=row(k_a), r_k=row(r_k),
        wl_pad=jnp.pad(w_lora_up, ((0, n_a), (0, 0))).astype(BF16),
        al_pad=jnp.pad(a_lora_up, ((n_w, 0), (0, 0))).astype(BF16),
        g_pad=jnp.pad(g_lora_up, ((0, pad), (0, 0))).astype(BF16),
        bd=bd,
    )


def kernel(x_prompt, x_sample, cache_k_win, cache_v_win, state_shift, state_wkv, state_conv, w_in, mu_shift, w0, w_lora_up, a0, a_lora_up, g_lora_up, k_k, k_a, r_k, gn_w, gn_b, w_out, ln1_g, ln1_b, w_ffn_in, conv_w, conv_b, w_ffn_out, ln2_g, ln2_b):
    depth = w_in.shape[0]
    alpha = (2.0 * depth) ** 0.25
    B, T, D = x_prompt.shape
    nb = x_sample.shape[0]
    assert x_sample.shape[1] == 1
    n_att = cache_k_win.shape[3]
    n_rw = state_wkv.shape[2]
    aw = n_att * HEAD_DIM
    assert n_rw * HEAD_DIM == aw and T % (Q_TILE * DIL_PATTERNS[-1][1]) == 0
    tm = min(256, T)
    heads = jnp.arange(1, n_att + 1, dtype=F32)
    slopes = jnp.exp2(-8.0 * heads / n_att)
    slopes_tab = jnp.broadcast_to(slopes[:, None], (n_att, LANES))
    slope_lane = jnp.repeat(slopes, HEAD_DIM).reshape(1, aw)
    row = lambda a: a.reshape(1, -1)

    hp, hs = x_prompt, x_sample.reshape(nb, D)
    outs = [[] for _ in range(10)]
    for l in range(depth):
        prm = _layer_params(w_in[l], mu_shift[l], w0[l], w_lora_up[l], a0[l], a_lora_up[l], g_lora_up[l],
                            k_k[l], k_a[l], r_k[l], aw)
        w_out_b = w_out[l].astype(BF16)
        w1_b = w_ffn_in[l].astype(BF16)
        w2_b = w_ffn_out[l].astype(BF16)
        gnw, gnb = row(gn_w[l]), row(gn_b[l])

        q, k, v, r, lw, k2, vr, al, be, g, bonus = _inproj_seq(hp, prm, tm)
        parts = []
        for _, dil in DIL_PATTERNS:
            parts.extend(_win_attn(q, k, v, slopes_tab, dil))
        y_rw, wkv_p = _rwkv_chunked(r, lw, k2, vr, al, be, g, bonus, gnw, gnb)
        flat = lambda a: a.reshape(B * T, a.shape[-1])
        x1 = _outproj([flat(a) for a in parts], flat(y_rw), flat(hp), w_out_b, row(ln1_g[l]), row(ln1_b[l]),
                      tm, alpha)
        shift_p = hp[:, -1, :]
        hp, conv_p = _ffn_seq(x1.reshape(B, T, D), w1_b, conv_w[l], row(conv_b[l]), w2_b, row(ln2_g[l]),
                              row(ln2_b[l]), tm, alpha)

        x2 = jnp.concatenate([hs, state_shift[l]], axis=0)
        qs, ks, vs, r, lw, k2, vr, al, be, g, bonus = _inproj_step(x2, prm)
        P = cache_k_win.shape[2]
        att_s = _attn_step(qs, ks, vs, cache_k_win[l].reshape(nb, P, aw), cache_v_win[l].reshape(nb, P, aw),
                           slope_lane, prm["bd"])
        y_rw_s, wkv_s = _rwkv_step(state_wkv[l], r, lw, k2, vr, al, be, g, bonus, gn_w[l], gn_b[l])
        x1s = _outproj([att_s], y_rw_s, hs, w_out_b, row(ln1_g[l]), row(ln1_b[l]), nb, alpha)
        shift_s = hs
        hs, gate_s = _ffn_step(x1s, state_conv[l][:, 0, :], state_conv[l][:, 1, :], w1_b, conv_w[l],
                               row(conv_b[l]), w2_b, row(ln2_g[l]), row(ln2_b[l]), alpha)
        conv_s = jnp.stack([state_conv[l][:, 1, :], gate_s], axis=1)

        n_keep = min(DIL_PATTERNS[-1][0], T)
        vals = (k.reshape(B, T, n_att, HEAD_DIM)[:, T - n_keep:], v.reshape(B, T, n_att, HEAD_DIM)[:, T - n_keep:],
                ks.reshape(nb, 1, n_att, HEAD_DIM), vs.reshape(nb, 1, n_att, HEAD_DIM), shift_p, shift_s,
                wkv_p, wkv_s, conv_p, conv_s)
        for lst, val in zip(outs, vals):
            lst.append(val)
    return (hp, hs.reshape(nb, 1, D)) + tuple(jnp.stack(lst) for lst in outs)
```

```python
import functools

import jax
import jax.numpy as jnp
from jax import lax
from jax.experimental import pallas as pl
from jax.experimental.pallas import tpu as pltpu

F32 = jnp.float32
BF16 = jnp.bfloat16

HEAD_DIM = 64
LANES = 128
DIL_PATTERNS = ((128, 1), (512, 4), (2048, 16))
WIN_STEPS = 128
Q_TILE = 128
CHUNK = 64
LN_EPS = 1e-5
GN_EPS = 64e-5
NEG = -1e30
VMEM_LIMIT = 56 * 1024 * 1024

_DN = {"NN": (((1,), (0,)), ((), ())), "NT": (((1,), (1,)), ((), ())), "TN": (((0,), (0,)), ((), ()))}


def _dot(a, b, dims="NN"):
    return lax.dot_general(a.astype(BF16), b.astype(BF16), _DN[dims], preferred_element_type=F32)


def _split2(x):
    hi = x.astype(BF16)
    lo = (x - hi.astype(F32)).astype(BF16)
    return hi, lo


def _dot_const(x, c):
    hi, lo = _split2(x)
    return _dot(hi, c) + _dot(lo, c)


def _dot_const_l(c, x):
    hi = x.astype(BF16)
    r1 = x - hi.astype(F32)
    mid = r1.astype(BF16)
    lo = (r1 - mid.astype(F32)).astype(BF16)
    return _dot(c, hi) + _dot(c, mid) + _dot(c, lo)


def _sigmoid(x):
    return 1.0 / (1.0 + jnp.exp(-x))


def _layer_norm(x, g, b):
    mu = jnp.mean(x, axis=-1, keepdims=True)
    xc = x - mu
    var = jnp.mean(xc * xc, axis=-1, keepdims=True)
    return xc * lax.rsqrt(var + LN_EPS) * g + b


def _cparams(n_grid):
    return pltpu.CompilerParams(dimension_semantics=("arbitrary",) * n_grid, vmem_limit_bytes=VMEM_LIMIT)


def _full(shape):
    nd = len(shape)
    return pl.BlockSpec(shape, lambda *_: (0,) * nd)


def _rwkv_prep(rw, w0, a0, k_k, k_a, r_k, wl_pad, al_pad, g_pad, bd, aw):
    r = rw[:, 0:aw]
    k = rw[:, aw:2 * aw]
    v = rw[:, 2 * aw:3 * aw]
    lo = rw[:, 3 * aw:3 * aw + 128]
    glo = rw[:, 3 * aw + 128:]
    z = -(w0 + _dot(jnp.tanh(lo), wl_pad))
    softplus = jnp.maximum(z, 0.0) + jnp.log(1.0 + jnp.exp(-jnp.abs(z)))
    lw = -jnp.exp(-softplus - 0.5)
    a = _sigmoid(a0 + _dot(lo, al_pad))
    g = _dot(_sigmoid(glo), g_pad)
    kk = k * k_k
    norm = jnp.sqrt(_dot(kk * kk, bd))
    kkn = kk / jnp.maximum(norm, 1e-12)
    k2 = k * (1.0 + (a - 1.0) * k_a)
    bonus = _dot_const(r * k2 * r_k, bd) * v
    return r, lw, k2, v, -kkn, kkn * a, g, bonus


def _inproj_seq_kernel(x_ref, w_ref, mu_ref, w0_ref, a0_ref, kk_ref, ka_ref, rk_ref, wl_ref, al_ref, gp_ref,
                       bd_ref, q_ref, k_ref, v_ref, kt_ref, vt_ref, r_o, lw_o, k2_o, vr_o, al_o, be_o, g_o, bo_o,
                       carry_ref, *, aw):
    @pl.when(pl.program_id(1) == 0)
    def _():
        carry_ref[...] = jnp.zeros_like(carry_ref)

    p = _dot(x_ref[0], w_ref[...])
    q_ref[0] = p[:, 0:aw]
    k = p[:, aw:2 * aw]
    v = p[:, 2 * aw:3 * aw]
    k_ref[0] = k
    v_ref[0] = v
    kt_ref[0] = k.T
    vt_ref[0] = v.T
    prw = p[:, 3 * aw:]
    tm = prw.shape[0]
    row = lax.broadcasted_iota(jnp.int32, prw.shape, 0)
    prev = jnp.where(row == 0, carry_ref[0:1, :], pltpu.roll(prw, 1, axis=0))
    carry_ref[0:1, :] = prw[tm - 1:tm, :]
    rw = prw + (prev - prw) * mu_ref[...]
    outs = _rwkv_prep(rw, w0_ref[...], a0_ref[...], kk_ref[...], ka_ref[...], rk_ref[...], wl_ref[...],
                      al_ref[...], gp_ref[...], bd_ref[...], aw)
    for o_ref, val in zip((r_o, lw_o, k2_o, vr_o, al_o, be_o, g_o, bo_o), outs):
        o_ref[0] = val


def _inproj_step_kernel(x_ref, w_ref, mu_ref, w0_ref, a0_ref, kk_ref, ka_ref, rk_ref, wl_ref, al_ref, gp_ref,
                        bd_ref, q_ref, k_ref, v_ref, r_o, lw_o, k2_o, vr_o, al_o, be_o, g_o, bo_o, *, aw, nb):
    p = _dot(x_ref[...], w_ref[...])
    q_ref[...] = p[0:nb, 0:aw]
    k_ref[...] = p[0:nb, aw:2 * aw]
    v_ref[...] = p[0:nb, 2 * aw:3 * aw]
    prw = p[0:nb, 3 * aw:]
    prev = p[nb:2 * nb, 3 * aw:]
    rw = prw + (prev - prw) * mu_ref[...]
    outs = _rwkv_prep(rw, w0_ref[...], a0_ref[...], kk_ref[...], ka_ref[...], rk_ref[...], wl_ref[...],
                      al_ref[...], gp_ref[...], bd_ref[...], aw)
    for o_ref, val in zip((r_o, lw_o, k2_o, vr_o, al_o, be_o, g_o, bo_o), outs):
        o_ref[...] = val


_PREP_PARAMS = ("mu", "w0", "a0", "k_k", "k_a", "r_k", "wl_pad", "al_pad", "g_pad", "bd")


def _inproj_seq(x, prm, tm):
    B, T, D = x.shape
    aw = prm["aw"]
    nrw = prm["w_in"].shape[1] - 3 * aw
    small = [prm[n] for n in _PREP_PARAMS]
    row_spec = pl.BlockSpec((1, tm, aw), lambda b, t: (b, t, 0))
    col_spec = pl.BlockSpec((1, aw, tm), lambda b, t: (b, 0, t))
    row_sds = jax.ShapeDtypeStruct((B, T, aw), F32)
    col_sds = jax.ShapeDtypeStruct((B, aw, T), F32)
    return pl.pallas_call(
        functools.partial(_inproj_seq_kernel, aw=aw),
        grid=(B, T // tm),
        in_specs=[pl.BlockSpec((1, tm, D), lambda b, t: (b, t, 0)), _full(prm["w_in"].shape)]
        + [_full(s.shape) for s in small],
        out_specs=[row_spec] * 3 + [col_spec] * 2 + [row_spec] * 8,
        out_shape=[row_sds] * 3 + [col_sds] * 2 + [row_sds] * 8,
        scratch_shapes=[pltpu.VMEM((8, nrw), F32)],
        compiler_params=_cparams(2),
        name="inproj_seq",
    )(x, prm["w_in"], *small)


def _inproj_step(x2, prm):
    nb = x2.shape[0] // 2
    aw = prm["aw"]
    small = [prm[n] for n in _PREP_PARAMS]
    out_sds = jax.ShapeDtypeStruct((nb, aw), F32)
    return pl.pallas_call(
        functools.partial(_inproj_step_kernel, aw=aw, nb=nb),
        grid=(1,),
        in_specs=[_full(x2.shape), _full(prm["w_in"].shape)] + [_full(s.shape) for s in small],
        out_specs=[_full((nb, aw))] * 11,
        out_shape=[out_sds] * 11,
        compiler_params=_cparams(1),
        name="inproj_step",
    )(x2, prm["w_in"], *small)


def _prompt_attn_kernel(q_ref, k_ref, v_ref, sl_ref, o_ref, acc_s, m_s, l_s, *, T):
    hp = pl.program_id(1)
    scale = HEAD_DIM ** -0.5
    lane = lax.broadcasted_iota(jnp.int32, (1, LANES), 1)
    head_masks = ((lane < HEAD_DIM).astype(F32), (lane >= HEAD_DIM).astype(F32))
    qi = lax.broadcasted_iota(jnp.int32, (Q_TILE, Q_TILE), 0)
    ki = lax.broadcasted_iota(jnp.int32, (Q_TILE, Q_TILE), 1)
    steps_cur = (qi - ki).astype(F32)
    steps_prev = (qi - ki + Q_TILE).astype(F32)
    ok_cur = ki <= qi
    ok_prev = ki >= qi
    first_lanes = lax.broadcasted_iota(jnp.int32, (Q_TILE, LANES), 1) < HEAD_DIM
    slopes = [sl_ref[pl.ds(2 * hp + j, 1), :] for j in range(2)]

    def tile(rows, prev_rows, dil, first_branch):
        q = q_ref[0, rows, :]
        kc = k_ref[0, rows, :]
        vc = v_ref[0, rows, :]
        if prev_rows is not None:
            kp = k_ref[0, prev_rows, :]
            vp = v_ref[0, prev_rows, :]
        res = []
        for j in range(2):
            bias = slopes[j] * float(dil)
            qm = q * head_masks[j]
            s_c = jnp.where(ok_cur, _dot(qm, kc, "NT") * scale - bias * steps_cur, NEG)
            m = jnp.max(s_c, axis=-1, keepdims=True)
            if prev_rows is not None:
                s_p = jnp.where(ok_prev, _dot(qm, kp, "NT") * scale - bias * steps_prev, NEG)
                m = jnp.maximum(m, jnp.max(s_p, axis=-1, keepdims=True))
            p_c = jnp.exp(s_c - m)
            l = jnp.sum(p_c, axis=-1, keepdims=True)
            acc = _dot(p_c, vc)
            if prev_rows is not None:
                p_p = jnp.exp(s_p - m)
                l = l + jnp.sum(p_p, axis=-1, keepdims=True)
                acc = acc + _dot(p_p, vp)
            res.append((acc, m, l))
        (a0, m0, l0), (a1, m1, l1) = res
        acc_n = jnp.where(first_lanes, a0, a1)
        m_n = jnp.where(first_lanes, m0, m1)
        l_n = jnp.where(first_lanes, l0, l1)
        if first_branch:
            acc_s[rows, :] = acc_n
            m_s[rows, :] = m_n
            l_s[rows, :] = l_n
        else:
            m_o = m_s[rows, :]
            m_new = jnp.maximum(m_o, m_n)
            e_o = jnp.exp(m_o - m_new)
            e_n = jnp.exp(m_n - m_new)
            acc_s[rows, :] = acc_s[rows, :] * e_o + acc_n * e_n
            l_s[rows, :] = l_s[rows, :] * e_o + l_n * e_n
            m_s[rows, :] = m_new

    for bi, (win, dil) in enumerate(DIL_PATTERNS):
        assert win // dil == WIN_STEPS
        nb = T // dil // Q_TILE
        if dil == 1:
            tile(pl.ds(0, Q_TILE), None, 1, bi == 0)

            def body(jb, carry, bi=bi):
                start = pl.multiple_of(jb * Q_TILE, Q_TILE)
                tile(pl.ds(start, Q_TILE), pl.ds(pl.multiple_of(start - Q_TILE, Q_TILE), Q_TILE), 1, bi == 0)
                return carry

            lax.fori_loop(1, nb, body, 0)
        else:
            def stream(s, carry, dil=dil, nb=nb, bi=bi):
                tile(pl.ds(s, Q_TILE, stride=dil), None, dil, bi == 0)
                for jb in range(1, nb):
                    tile(pl.ds(s + dil * Q_TILE * jb, Q_TILE, stride=dil),
                         pl.ds(s + dil * Q_TILE * (jb - 1), Q_TILE, stride=dil), dil, bi == 0)
                return carry

            lax.fori_loop(0, dil, stream, 0)
    o_ref[0] = acc_s[...] / l_s[...]


def _prompt_attn(q, k, v, slopes_tab):
    B, T, W = q.shape
    spec = pl.BlockSpec((1, T, LANES), lambda b, hp: (b, 0, hp))
    return pl.pallas_call(
        functools.partial(_prompt_attn_kernel, T=T),
        grid=(B, W // LANES),
        in_specs=[spec, spec, spec, _full(slopes_tab.shape)],
        out_specs=spec,
        out_shape=jax.ShapeDtypeStruct((B, T, W), F32),
        scratch_shapes=[pltpu.VMEM((T, LANES), F32)] * 3,
        compiler_params=_cparams(2),
        name="prompt_attn",
    )(q, k, v, slopes_tab)


def _rwkv_chunk_kernel(r_ref, lw_ref, k_ref, v_ref, al_ref, be_ref, g_ref, bo_ref, gnw_ref, gnb_ref,
                       y_ref, s_ref, st_scr, *, Tb, npair):
    C = CHUNK
    n2 = 2 * C
    t = pl.program_id(1)

    @pl.when(t == 0)
    def _():
        st_scr[...] = jnp.zeros_like(st_scr)

    lane = lax.broadcasted_iota(jnp.int32, (1, LANES), 1)
    m0 = (lane < HEAD_DIM).astype(F32)
    m1 = 1.0 - m0
    ri = lax.broadcasted_iota(jnp.int32, (n2, n2), 0)
    ci = lax.broadcasted_iota(jnp.int32, (n2, n2), 1)
    same = (ri < C) == (ci < C)
    strict = same & ((ri & (C - 1)) > (ci & (C - 1)))
    incl = same & ((ri & (C - 1)) >= (ci & (C - 1)))
    eye = (ri == ci).astype(F32)
    tri = (lax.broadcasted_iota(jnp.int32, (C, C), 0) >= lax.broadcasted_iota(jnp.int32, (C, C), 1)).astype(BF16)
    li = lax.broadcasted_iota(jnp.int32, (LANES, LANES), 0)
    lj = lax.broadcasted_iota(jnp.int32, (LANES, LANES), 1)
    head_mean = jnp.where((li < HEAD_DIM) == (lj < HEAD_DIM), 1.0 / HEAD_DIM, 0.0).astype(BF16)

    def stack(x):
        return jnp.concatenate([x * m0, x * m1], axis=0)

    def twice(x):
        return jnp.concatenate([x, x], axis=0)

    def pair_chunk(S, r, lw, k, v, al, be):
        cl = _dot_const_l(tri, lw)
        e_pos = jnp.exp(cl)
        e_neg = jnp.exp(-cl)
        g_end = e_pos[C - 1:C, :]
        rt = r * e_pos
        kt = k * e_neg
        bt = be * e_neg
        at = al * jnp.exp(cl - lw)
        a_st, r_st, v_st = stack(at), stack(rt), stack(v)
        k2, b2 = twice(kt), twice(bt)
        l_ak = jnp.where(strict, _dot(a_st, k2, "NT"), 0.0)
        l_ab = jnp.where(strict, _dot(a_st, b2, "NT"), 0.0)
        l_rk = jnp.where(incl, _dot(r_st, k2, "NT"), 0.0)
        l_rb = jnp.where(incl, _dot(r_st, b2, "NT"), 0.0)
        tinv = eye + l_ab
        x = l_ab
        for _ in range(C.bit_length() - 2):
            x = _dot(x, x)
            tinv = tinv + _dot(tinv, x)
        w = _dot(a_st, S, "NT") + _dot(l_ak, v_st)
        u = _dot(tinv, w)
        y_st = _dot(r_st, S, "NT") + _dot(l_rk, v_st) + _dot(l_rb, u)
        y = y_st[0:C, :] + y_st[C:n2, :]
        s_new = S * g_end + _dot(v_st, stack(kt * g_end), "TN") + _dot(u, stack(bt * g_end), "TN")
        return s_new, y

    def body(c, carry):
        rows = pl.ds(pl.multiple_of(c * C, C), C)
        for hp in range(npair):
            cols = slice(hp * LANES, (hp + 1) * LANES)
            s_new, y = pair_chunk(st_scr[hp], r_ref[0, rows, cols], lw_ref[0, rows, cols], k_ref[0, rows, cols],
                                  v_ref[0, rows, cols], al_ref[0, rows, cols], be_ref[0, rows, cols])
            st_scr[hp] = s_new
            mu = _dot_const(y, head_mean)
            yc = y - mu
            var = _dot_const(yc * yc, head_mean)
            yn = yc * lax.rsqrt(var + GN_EPS) * gnw_ref[:, cols] + gnb_ref[:, cols]
            y_ref[0, rows, cols] = (yn + bo_ref[0, rows, cols]) * g_ref[0, rows, cols]
        return carry

    lax.fori_loop(0, Tb // C, body, 0)

    @pl.when(t == pl.num_programs(1) - 1)
    def _():
        s_ref[0] = st_scr[...]


def _rwkv_chunked(r, lw, k2, v, al, be, g, bonus, gn_w, gn_b, tb):
    B, T, W = r.shape
    npair = W // LANES
    spec = pl.BlockSpec((1, tb, W), lambda b, t: (b, t, 0))
    st_spec = pl.BlockSpec((1, npair, LANES, LANES), lambda b, t: (b, 0, 0, 0))
    y, s = pl.pallas_call(
        functools.partial(_rwkv_chunk_kernel, Tb=tb, npair=npair),
        grid=(B, T // tb),
        in_specs=[spec] * 8 + [_full(gn_w.shape), _full(gn_b.shape)],
        out_specs=[spec, st_spec],
        out_shape=[jax.ShapeDtypeStruct((B, T, W), F32), jax.ShapeDtypeStruct((B, npair, LANES, LANES), F32)],
        scratch_shapes=[pltpu.VMEM((npair, LANES, LANES), F32)],
        compiler_params=_cparams(2),
        name="rwkv_chunk",
    )(r, lw, k2, v, al, be, g, bonus, gn_w, gn_b)
    s = s.reshape(B, npair, 2, HEAD_DIM, 2, HEAD_DIM)
    s = jnp.stack([s[:, :, 0, :, 0, :], s[:, :, 1, :, 1, :]], axis=2)
    return y, s.reshape(B, 2 * npair, HEAD_DIM, HEAD_DIM)


def _rwkv_step_kernel(s_ref, r_ref, lw_ref, k_ref, al_ref, be_ref, v_ref, g_ref, bo_ref, gnw_ref, gnb_ref,
                      y_ref, so_ref):
    S = s_ref[0]
    sa = jnp.sum(S * al_ref[0], axis=-1, keepdims=True)
    s_new = S * jnp.exp(lw_ref[0]) + sa * be_ref[0] + v_ref[0] * k_ref[0]
    so_ref[0] = s_new
    y = jnp.sum(s_new * r_ref[0], axis=-1, keepdims=True)
    mu = jnp.mean(y, axis=1, keepdims=True)
    yc = y - mu
    var = jnp.mean(yc * yc, axis=1, keepdims=True)
    yn = yc * lax.rsqrt(var + GN_EPS) * gnw_ref[...] + gnb_ref[...]
    y_ref[0] = (yn + bo_ref[0]) * g_ref[0]


def _rwkv_step(state, r, lw, k2, v, al, be, g, bonus, gn_w, gn_b):
    nb, H = state.shape[0], state.shape[1]
    rowv = lambda a: a.reshape(nb, H, 1, HEAD_DIM)
    colv = lambda a: a.reshape(nb, H, HEAD_DIM, 1)
    s_spec = pl.BlockSpec((1, H, HEAD_DIM, HEAD_DIM), lambda b: (b, 0, 0, 0))
    r_spec = pl.BlockSpec((1, H, 1, HEAD_DIM), lambda b: (b, 0, 0, 0))
    c_spec = pl.BlockSpec((1, H, HEAD_DIM, 1), lambda b: (b, 0, 0, 0))
    p_spec = _full((H, HEAD_DIM, 1))
    y, s_new = pl.pallas_call(
        _rwkv_step_kernel,
        grid=(nb,),
        in_specs=[s_spec] + [r_spec] * 5 + [c_spec] * 3 + [p_spec] * 2,
        out_specs=[c_spec, s_spec],
        out_shape=[jax.ShapeDtypeStruct((nb, H, HEAD_DIM, 1), F32), jax.ShapeDtypeStruct(state.shape, F32)],
        compiler_params=_cparams(1),
        name="rwkv_step",
    )(state, rowv(r), rowv(lw), rowv(k2), rowv(al), rowv(be), colv(v), colv(g), colv(bonus),
      gn_w.reshape(H, HEAD_DIM, 1), gn_b.reshape(H, HEAD_DIM, 1))
    return y.reshape(nb, H * HEAD_DIM), s_new


def _attn_step_kernel(q_ref, kn_ref, vn_ref, kt_ref, vt_ref, sl_ref, o_ref, *, P):
    scale = HEAD_DIM ** -0.5
    q = q_ref[0]
    dist = P - lax.broadcasted_iota(jnp.int32, (1, 1, P), 2)
    count = jnp.zeros((1, 1, P), F32)
    for win, dil in DIL_PATTERNS:
        assert dil & (dil - 1) == 0
        count = count + ((dist <= win) & ((dist & (dil - 1)) == 0)).astype(F32)
    s = jnp.sum(kt_ref[0] * q, axis=1, keepdims=True) * scale - sl_ref[...] * dist.astype(F32)
    s = jnp.where(count > 0.0, s, NEG)
    s0 = jnp.sum(kn_ref[0] * q, axis=1, keepdims=True) * scale
    m = jnp.maximum(jnp.max(s, axis=-1, keepdims=True), s0)
    p = count * jnp.exp(s - m)
    e0 = len(DIL_PATTERNS) * jnp.exp(s0 - m)
    l = jnp.sum(p, axis=-1, keepdims=True) + e0
    acc = jnp.sum(vt_ref[0] * p, axis=-1, keepdims=True) + e0 * vn_ref[0]
    o_ref[0] = acc / l


def _attn_step(q, k_new, v_new, k_cache, v_cache, slopes):
    nb, P, H, _ = k_cache.shape
    col = lambda a: a.reshape(nb, H, HEAD_DIM, 1)
    c_spec = pl.BlockSpec((1, H, HEAD_DIM, 1), lambda b: (b, 0, 0, 0))
    t_spec = pl.BlockSpec((1, H, HEAD_DIM, P), lambda b: (b, 0, 0, 0))
    out = pl.pallas_call(
        functools.partial(_attn_step_kernel, P=P),
        grid=(nb,),
        in_specs=[c_spec, c_spec, c_spec, t_spec, t_spec, _full((H, 1, 1))],
        out_specs=c_spec,
        out_shape=jax.ShapeDtypeStruct((nb, H, HEAD_DIM, 1), F32),
        compiler_params=_cparams(1),
        name="attn_step",
    )(col(q), col(k_new), col(v_new), jnp.transpose(k_cache, (0, 2, 3, 1)), jnp.transpose(v_cache, (0, 2, 3, 1)),
      slopes.reshape(H, 1, 1))
    return out.reshape(nb, H * HEAD_DIM)


def _outproj_kernel(att_ref, y_ref, x_ref, w_ref, g_ref, b_ref, o_ref, *, aw, alpha):
    h = _dot(att_ref[...], w_ref[0:aw, :]) + _dot(y_ref[...], w_ref[aw:, :])
    o_ref[...] = _layer_norm(alpha * x_ref[...] + h, g_ref[...], b_ref[...])


def _outproj(att, y_rw, x, w_out, ln_g, ln_b, tm, alpha):
    rows, D = x.shape
    aw = y_rw.shape[1]
    half = pl.BlockSpec((tm, aw), lambda i: (i, 0))
    return pl.pallas_call(
        functools.partial(_outproj_kernel, aw=aw, alpha=alpha),
        grid=(rows // tm,),
        in_specs=[half, half, pl.BlockSpec((tm, D), lambda i: (i, 0)), _full(w_out.shape), _full(ln_g.shape),
                  _full(ln_b.shape)],
        out_specs=pl.BlockSpec((tm, D), lambda i: (i, 0)),
        out_shape=jax.ShapeDtypeStruct((rows, D), F32),
        compiler_params=_cparams(1),
        name="outproj_ln",
    )(att, y_rw, x, w_out, ln_g, ln_b)


def _ffn_tail(x1, gate_pre, prev1, prev2, up, cw_ref, cb_ref, w2_ref, g_ref, b_ref, alpha):
    c = cw_ref[0:1, :] * prev2 + cw_ref[1:2, :] * prev1 + cw_ref[2:3, :] * gate_pre + cb_ref[...]
    h = c * _sigmoid(c) * up
    return _layer_norm(alpha * x1 + _dot(h, w2_ref[...]), g_ref[...], b_ref[...])


def _ffn_seq_kernel(x_ref, w1_ref, cw_ref, cb_ref, w2_ref, g_ref, b_ref, o_ref, tail_ref, carry_ref, *, dff, alpha):
    @pl.when(pl.program_id(1) == 0)
    def _():
        carry_ref[...] = jnp.zeros_like(carry_ref)

    x1 = x_ref[0]
    hu = _dot(x1, w1_ref[...])
    gate_pre = hu[:, 0:dff]
    up = hu[:, dff:]
    tm = gate_pre.shape[0]
    row = lax.broadcasted_iota(jnp.int32, gate_pre.shape, 0)
    c0 = carry_ref[0:1, :]
    c1 = carry_ref[1:2, :]
    prev1 = jnp.where(row == 0, c1, pltpu.roll(gate_pre, 1, axis=0))
    prev2 = jnp.where(row == 0, c0, jnp.where(row == 1, c1, pltpu.roll(gate_pre, 2, axis=0)))
    last2 = gate_pre[tm - 2:tm, :]
    carry_ref[0:2, :] = last2
    tail_ref[0] = last2
    o_ref[0] = _ffn_tail(x1, gate_pre, prev1, prev2, up, cw_ref, cb_ref, w2_ref, g_ref, b_ref, alpha)


def _ffn_step_kernel(x_ref, p2_ref, p1_ref, w1_ref, cw_ref, cb_ref, w2_ref, g_ref, b_ref, o_ref, gate_ref, *,
                     dff, alpha):
    x1 = x_ref[...]
    hu = _dot(x1, w1_ref[...])
    gate_pre = hu[:, 0:dff]
    gate_ref[...] = gate_pre
    o_ref[...] = _ffn_tail(x1, gate_pre, p1_ref[...], p2_ref[...], hu[:, dff:], cw_ref, cb_ref, w2_ref, g_ref,
                           b_ref, alpha)


def _resident(shape):
    nd = len(shape)
    return pl.BlockSpec(shape, lambda *_: (0,) * nd, pipeline_mode=pl.Buffered(1))


def _ffn_seq(x1, w1, conv_w, conv_b, w2, ln_g, ln_b, tm, alpha):
    B, T, D = x1.shape
    dff = w2.shape[0]
    return pl.pallas_call(
        functools.partial(_ffn_seq_kernel, dff=dff, alpha=alpha),
        grid=(B, T // tm),
        in_specs=[pl.BlockSpec((1, tm, D), lambda b, t: (b, t, 0)), _resident(w1.shape), _full(conv_w.shape),
                  _full(conv_b.shape), _resident(w2.shape), _full(ln_g.shape), _full(ln_b.shape)],
        out_specs=[pl.BlockSpec((1, tm, D), lambda b, t: (b, t, 0)),
                   pl.BlockSpec((1, 2, dff), lambda b, t: (b, 0, 0))],
        out_shape=[jax.ShapeDtypeStruct((B, T, D), F32), jax.ShapeDtypeStruct((B, 2, dff), F32)],
        scratch_shapes=[pltpu.VMEM((8, dff), F32)],
        compiler_params=_cparams(2),
        name="ffn_seq",
    )(x1, w1, conv_w, conv_b, w2, ln_g, ln_b)


def _ffn_step(x1, prev2, prev1, w1, conv_w, conv_b, w2, ln_g, ln_b, alpha):
    nb, D = x1.shape
    dff = w2.shape[0]
    return pl.pallas_call(
        functools.partial(_ffn_step_kernel, dff=dff, alpha=alpha),
        grid=(1,),
        in_specs=[_full(x1.shape), _full(prev2.shape), _full(prev1.shape), _resident(w1.shape),
                  _full(conv_w.shape), _full(conv_b.shape), _resident(w2.shape), _full(ln_g.shape),
                  _full(ln_b.shape)],
        out_specs=[_full((nb, D)), _full((nb, dff))],
        out_shape=[jax.ShapeDtypeStruct((nb, D), F32), jax.ShapeDtypeStruct((nb, dff), F32)],
        compiler_params=_cparams(1),
        name="ffn_step",
    )(x1, prev2, prev1, w1, conv_w, conv_b, w2, ln_g, ln_b)


def _layer_params(w_in, mu_shift, w0, w_lora_up, a0, a_lora_up, g_lora_up, k_k, k_a, r_k, aw):
    D, ncol = w_in.shape
    n_w, n_a, n_g = w_lora_up.shape[0], a_lora_up.shape[0], g_lora_up.shape[0]
    assert n_w + n_a == LANES and 3 * aw + 3 * aw + n_w + n_a + n_g == ncol
    g_cols = -(-n_g // LANES) * LANES
    pad = g_cols - n_g
    row = lambda a: a.reshape(1, -1).astype(F32)
    lane = jnp.arange(aw)
    bd = (lane[:, None] // HEAD_DIM == lane[None, :] // HEAD_DIM).astype(BF16)
    return dict(
        aw=aw,
        w_in=jnp.pad(w_in, ((0, 0), (0, pad))).astype(BF16),
        mu=jnp.pad(row(mu_shift), ((0, 0), (0, pad))),
        w0=row(w0), a0=row(a0), k_k=row(k_k), k_a=row(k_a), r_k=row(r_k),
        wl_pad=jnp.pad(w_lora_up, ((0, n_a), (0, 0))).astype(BF16),
        al_pad=jnp.pad(a_lora_up, ((n_w, 0), (0, 0))).astype(BF16),
        g_pad=jnp.pad(g_lora_up, ((0, pad), (0, 0))).astype(BF16),
        bd=bd,
    )


def kernel(x_prompt, x_sample, cache_k_win, cache_v_win, state_shift, state_wkv, state_conv, w_in, mu_shift, w0, w_lora_up, a0, a_lora_up, g_lora_up, k_k, k_a, r_k, gn_w, gn_b, w_out, ln1_g, ln1_b, w_ffn_in, conv_w, conv_b, w_ffn_out, ln2_g, ln2_b):
    depth = w_in.shape[0]
    alpha = (2.0 * depth) ** 0.25
    B, T, D = x_prompt.shape
    nb = x_sample.shape[0]
    assert x_sample.shape[1] == 1
    n_att = cache_k_win.shape[3]
    n_rw = state_wkv.shape[2]
    aw = n_att * HEAD_DIM
    assert n_rw * HEAD_DIM == aw and T % (Q_TILE * DIL_PATTERNS[-1][1]) == 0
    tm = min(256, T)
    tb = min(512, T)
    heads = jnp.arange(1, n_att + 1, dtype=F32)
    slopes = jnp.exp2(-8.0 * heads / n_att)
    slopes_tab = jnp.broadcast_to(slopes[:, None], (n_att, LANES))
    row = lambda a: a.reshape(1, -1)

    hp, hs = x_prompt, x_sample.reshape(nb, D)
    outs = [[] for _ in range(10)]
    for l in range(depth):
        prm = _layer_params(w_in[l], mu_shift[l], w0[l], w_lora_up[l], a0[l], a_lora_up[l], g_lora_up[l],
                            k_k[l], k_a[l], r_k[l], aw)
        w_out_b = w_out[l].astype(BF16)
        w1_b = w_ffn_in[l].astype(BF16)
        w2_b = w_ffn_out[l].astype(BF16)
        gnw, gnb = row(gn_w[l]), row(gn_b[l])

        q, k, v, kt, vt, r, lw, k2, vr, al, be, g, bonus = _inproj_seq(hp, prm, tm)
        att = _prompt_attn(q, k, v, slopes_tab)
        y_rw, wkv_p = _rwkv_chunked(r, lw, k2, vr, al, be, g, bonus, gnw, gnb, tb)
        flat = lambda a: a.reshape(B * T, a.shape[-1])
        x1 = _outproj(flat(att), flat(y_rw), flat(hp), w_out_b, row(ln1_g[l]), row(ln1_b[l]), tm, alpha)
        shift_p = hp[:, -1, :]
        hp, conv_p = _ffn_seq(x1.reshape(B, T, D), w1_b, conv_w[l], row(conv_b[l]), w2_b, row(ln2_g[l]),
                              row(ln2_b[l]), tm, alpha)

        x2 = jnp.concatenate([hs, state_shift[l]], axis=0)
        qs, ks, vs, r, lw, k2, vr, al, be, g, bonus = _inproj_step(x2, prm)
        att_s = _attn_step(qs, ks, vs, cache_k_win[l], cache_v_win[l], slopes)
        y_rw_s, wkv_s = _rwkv_step(state_wkv[l], r, lw, k2, vr, al, be, g, bonus, gn_w[l], gn_b[l])
        x1s = _outproj(att_s, y_rw_s, hs, w_out_b, row(ln1_g[l]), row(ln1_b[l]), nb, alpha)
        shift_s = hs
        hs, gate_s = _ffn_step(x1s, state_conv[l][:, 0, :], state_conv[l][:, 1, :], w1_b, conv_w[l],
                               row(conv_b[l]), w2_b, row(ln2_g[l]), row(ln2_b[l]), alpha)
        conv_s = jnp.stack([state_conv[l][:, 1, :], gate_s], axis=1)

        n_keep = min(DIL_PATTERNS[-1][0], T)
        win = lambda a: jnp.transpose(a.reshape(B, n_att, HEAD_DIM, T), (0, 3, 1, 2))[:, T - n_keep:]
        vals = (win(kt), win(vt), ks.reshape(nb, 1, n_att, HEAD_DIM), vs.reshape(nb, 1, n_att, HEAD_DIM),
                shift_p, shift_s, wkv_p, wkv_s, conv_p, conv_s)
        for lst, val in zip(outs, vals):
            lst.append(val)
    return (hp, hs.reshape(nb, 1, D)) + tuple(jnp.stack(lst) for lst in outs)
```

```python
import functools

import jax
import jax.numpy as jnp
from jax import lax
from jax.experimental import pallas as pl
from jax.experimental.pallas import tpu as pltpu

F32 = jnp.float32
BF16 = jnp.bfloat16

HEAD_DIM = 64
LANES = 128
DIL_PATTERNS = ((128, 1), (512, 4), (2048, 16))
WIN_STEPS = 128
Q_TILE = 128
GROUP = 4
CHUNK = 64
LN_EPS = 1e-5
GN_EPS = 64e-5
NEG = -1e30
VMEM_LIMIT = 56 * 1024 * 1024

_DN = {"NN": (((1,), (0,)), ((), ())), "NT": (((1,), (1,)), ((), ())), "TN": (((0,), (0,)), ((), ()))}


def _dot(a, b, dims="NN"):
    return lax.dot_general(a.astype(BF16), b.astype(BF16), _DN[dims], preferred_element_type=F32)


def _split2(x):
    hi = x.astype(BF16)
    lo = (x - hi.astype(F32)).astype(BF16)
    return hi, lo


def _dot_const(x, c):
    hi, lo = _split2(x)
    return _dot(hi, c) + _dot(lo, c)


def _dot_const_l(c, x):
    hi = x.astype(BF16)
    r1 = x - hi.astype(F32)
    mid = r1.astype(BF16)
    lo = (r1 - mid.astype(F32)).astype(BF16)
    return _dot(c, hi) + _dot(c, mid) + _dot(c, lo)


def _sigmoid(x):
    return 1.0 / (1.0 + jnp.exp(-x))


def _layer_norm(x, g, b):
    mu = jnp.mean(x, axis=-1, keepdims=True)
    xc = x - mu
    var = jnp.mean(xc * xc, axis=-1, keepdims=True)
    return xc * lax.rsqrt(var + LN_EPS) * g + b


def _cparams(n_grid):
    return pltpu.CompilerParams(dimension_semantics=("arbitrary",) * n_grid, vmem_limit_bytes=VMEM_LIMIT)


def _full(shape):
    nd = len(shape)
    return pl.BlockSpec(shape, lambda *_: (0,) * nd)


def _rwkv_prep(rw, w0, a0, k_k, k_a, r_k, wl_pad, al_pad, g_pad, bd, aw):
    r = rw[:, 0:aw]
    k = rw[:, aw:2 * aw]
    v = rw[:, 2 * aw:3 * aw]
    lo = rw[:, 3 * aw:3 * aw + 128]
    glo = rw[:, 3 * aw + 128:]
    z = -(w0 + _dot(jnp.tanh(lo), wl_pad))
    softplus = jnp.maximum(z, 0.0) + jnp.log(1.0 + jnp.exp(-jnp.abs(z)))
    lw = -jnp.exp(-softplus - 0.5)
    a = _sigmoid(a0 + _dot(lo, al_pad))
    g = _dot(_sigmoid(glo), g_pad)
    kk = k * k_k
    norm = jnp.sqrt(_dot(kk * kk, bd))
    kkn = kk / jnp.maximum(norm, 1e-12)
    k2 = k * (1.0 + (a - 1.0) * k_a)
    bonus = _dot_const(r * k2 * r_k, bd) * v
    return r, lw, k2, v, -kkn, kkn * a, g, bonus


def _inproj_seq_kernel(x_ref, w_ref, mu_ref, w0_ref, a0_ref, kk_ref, ka_ref, rk_ref, wl_ref, al_ref, gp_ref,
                       bd_ref, q_ref, k_ref, v_ref, kt_ref, vt_ref, r_o, lw_o, k2_o, vr_o, al_o, be_o, g_o, bo_o,
                       carry_ref, *, aw):
    @pl.when(pl.program_id(1) == 0)
    def _():
        carry_ref[...] = jnp.zeros_like(carry_ref)

    p = _dot(x_ref[0], w_ref[...])
    q_ref[0] = p[:, 0:aw]
    k = p[:, aw:2 * aw]
    v = p[:, 2 * aw:3 * aw]
    k_ref[0] = k
    v_ref[0] = v
    kt_ref[0] = k.T
    vt_ref[0] = v.T
    prw = p[:, 3 * aw:]
    tm = prw.shape[0]
    row = lax.broadcasted_iota(jnp.int32, prw.shape, 0)
    prev = jnp.where(row == 0, carry_ref[0:1, :], pltpu.roll(prw, 1, axis=0))
    carry_ref[0:1, :] = prw[tm - 1:tm, :]
    rw = prw + (prev - prw) * mu_ref[...]
    outs = _rwkv_prep(rw, w0_ref[...], a0_ref[...], kk_ref[...], ka_ref[...], rk_ref[...], wl_ref[...],
                      al_ref[...], gp_ref[...], bd_ref[...], aw)
    for o_ref, val in zip((r_o, lw_o, k2_o, vr_o, al_o, be_o, g_o, bo_o), outs):
        o_ref[0] = val


def _inproj_step_kernel(x_ref, w_ref, mu_ref, w0_ref, a0_ref, kk_ref, ka_ref, rk_ref, wl_ref, al_ref, gp_ref,
                        bd_ref, q_ref, k_ref, v_ref, r_o, lw_o, k2_o, vr_o, al_o, be_o, g_o, bo_o, *, aw, nb):
    p = _dot(x_ref[...], w_ref[...])
    q_ref[...] = p[0:nb, 0:aw]
    k_ref[...] = p[0:nb, aw:2 * aw]
    v_ref[...] = p[0:nb, 2 * aw:3 * aw]
    prw = p[0:nb, 3 * aw:]
    prev = p[nb:2 * nb, 3 * aw:]
    rw = prw + (prev - prw) * mu_ref[...]
    outs = _rwkv_prep(rw, w0_ref[...], a0_ref[...], kk_ref[...], ka_ref[...], rk_ref[...], wl_ref[...],
                      al_ref[...], gp_ref[...], bd_ref[...], aw)
    for o_ref, val in zip((r_o, lw_o, k2_o, vr_o, al_o, be_o, g_o, bo_o), outs):
        o_ref[...] = val


_PREP_PARAMS = ("mu", "w0", "a0", "k_k", "k_a", "r_k", "wl_pad", "al_pad", "g_pad", "bd")


def _inproj_seq(x, prm, tm):
    B, T, D = x.shape
    aw = prm["aw"]
    nrw = prm["w_in"].shape[1] - 3 * aw
    small = [prm[n] for n in _PREP_PARAMS]
    row_spec = pl.BlockSpec((1, tm, aw), lambda b, t: (b, t, 0))
    col_spec = pl.BlockSpec((1, aw, tm), lambda b, t: (b, 0, t))
    row_sds = jax.ShapeDtypeStruct((B, T, aw), F32)
    col_sds = jax.ShapeDtypeStruct((B, aw, T), F32)
    return pl.pallas_call(
        functools.partial(_inproj_seq_kernel, aw=aw),
        grid=(B, T // tm),
        in_specs=[pl.BlockSpec((1, tm, D), lambda b, t: (b, t, 0)), _full(prm["w_in"].shape)]
        + [_full(s.shape) for s in small],
        out_specs=[row_spec] * 3 + [col_spec] * 2 + [row_spec] * 8,
        out_shape=[row_sds] * 3 + [col_sds] * 2 + [row_sds] * 8,
        scratch_shapes=[pltpu.VMEM((8, nrw), F32)],
        compiler_params=_cparams(2),
        name="inproj_seq",
    )(x, prm["w_in"], *small)


def _inproj_step(x2, prm):
    nb = x2.shape[0] // 2
    aw = prm["aw"]
    small = [prm[n] for n in _PREP_PARAMS]
    out_sds = jax.ShapeDtypeStruct((nb, aw), F32)
    return pl.pallas_call(
        functools.partial(_inproj_step_kernel, aw=aw, nb=nb),
        grid=(1,),
        in_specs=[_full(x2.shape), _full(prm["w_in"].shape)] + [_full(s.shape) for s in small],
        out_specs=[_full((nb, aw))] * 11,
        out_shape=[out_sds] * 11,
        compiler_params=_cparams(1),
        name="inproj_step",
    )(x2, prm["w_in"], *small)


def _prompt_attn_kernel(q_ref, k_ref, v_ref, sl_ref, o_ref, acc_s, m_s, l_s, *, T):
    hp = pl.program_id(1)
    scale = HEAD_DIM ** -0.5
    lane = lax.broadcasted_iota(jnp.int32, (1, LANES), 1)
    head_masks = ((lane < HEAD_DIM).astype(F32), (lane >= HEAD_DIM).astype(F32))
    qi = lax.broadcasted_iota(jnp.int32, (Q_TILE, Q_TILE), 0)
    ki = lax.broadcasted_iota(jnp.int32, (Q_TILE, Q_TILE), 1)
    steps_cur = (qi - ki).astype(F32)
    steps_prev = (qi - ki + Q_TILE).astype(F32)
    ok_cur = ki <= qi
    ok_prev = ki >= qi
    first_lanes = lax.broadcasted_iota(jnp.int32, (Q_TILE, LANES), 1) < HEAD_DIM
    slopes = [sl_ref[pl.ds(2 * hp + j, 1), :] for j in range(2)]

    def tile_group(tiles, bias_c, bias_p, first_branch):
        cases = [(t, j) for t in range(len(tiles)) for j in range(2)]
        q = [q_ref[0, rows, :] * scale for rows, _ in tiles]
        kc = [k_ref[0, rows, :].astype(BF16) for rows, _ in tiles]
        vc = [v_ref[0, rows, :].astype(BF16) for rows, _ in tiles]
        kp = [None if prev is None else k_ref[0, prev, :].astype(BF16) for _, prev in tiles]
        vp = [None if prev is None else v_ref[0, prev, :].astype(BF16) for _, prev in tiles]
        qm = {(t, j): (q[t] * head_masks[j]).astype(BF16) for t, j in cases}
        s_c = {(t, j): _dot(qm[t, j], kc[t], "NT") + bias_c[j] for t, j in cases}
        s_p = {(t, j): _dot(qm[t, j], kp[t], "NT") + bias_p[j] for t, j in cases if kp[t] is not None}
        m = {c: jnp.max(jnp.maximum(s_c[c], s_p[c]) if c in s_p else s_c[c], axis=-1, keepdims=True)
             for c in cases}
        p_c = {c: jnp.exp(s_c[c] - m[c]) for c in cases}
        p_p = {c: jnp.exp(s_p[c] - m[c]) for c in s_p}
        l = {c: jnp.sum(p_c[c] + p_p[c] if c in p_p else p_c[c], axis=-1, keepdims=True) for c in cases}
        acc = {(t, j): _dot(p_c[t, j], vc[t]) for t, j in cases}
        for t, j in s_p:
            acc[t, j] = acc[t, j] + _dot(p_p[t, j], vp[t])
        for t, (rows, _) in enumerate(tiles):
            acc_n = jnp.where(first_lanes, acc[t, 0], acc[t, 1])
            m_n = jnp.where(first_lanes, m[t, 0], m[t, 1])
            l_n = jnp.where(first_lanes, l[t, 0], l[t, 1])
            if first_branch:
                acc_s[rows, :] = acc_n
                m_s[rows, :] = m_n
                l_s[rows, :] = l_n
            else:
                m_o = m_s[rows, :]
                m_new = jnp.maximum(m_o, m_n)
                e_o = jnp.exp(m_o - m_new)
                e_n = jnp.exp(m_n - m_new)
                acc_s[rows, :] = acc_s[rows, :] * e_o + acc_n * e_n
                l_s[rows, :] = l_s[rows, :] * e_o + l_n * e_n
                m_s[rows, :] = m_new

    for bi, (win, dil) in enumerate(DIL_PATTERNS):
        assert win // dil == WIN_STEPS
        nb = T // dil // Q_TILE
        bias_c = [jnp.where(ok_cur, -(slopes[j] * float(dil)) * steps_cur, NEG) for j in range(2)]
        bias_p = [jnp.where(ok_prev, -(slopes[j] * float(dil)) * steps_prev, NEG) for j in range(2)]

        def rows_of(s, jb, dil=dil):
            if isinstance(jb, int) and jb < 0:
                return None
            if dil == 1:
                return pl.ds(pl.multiple_of(jb * Q_TILE, Q_TILE), Q_TILE)
            return pl.ds(s + dil * Q_TILE * jb, Q_TILE, stride=dil)

        if nb >= GROUP:
            assert nb % GROUP == 0

            def stream(s, carry, nb=nb, bi=bi, rows_of=rows_of, bias_c=bias_c, bias_p=bias_p):
                tile_group([(rows_of(s, jb), rows_of(s, jb - 1)) for jb in range(GROUP)], bias_c, bias_p, bi == 0)

                def later(g, c2):
                    tile_group([(rows_of(s, g * GROUP + i), rows_of(s, g * GROUP + i - 1)) for i in range(GROUP)],
                               bias_c, bias_p, bi == 0)
                    return c2

                if nb > GROUP:
                    lax.fori_loop(1, nb // GROUP, later, 0)
                return carry

            if dil == 1:
                stream(0, 0)
            else:
                lax.fori_loop(0, dil, stream, 0)
        else:
            assert dil % GROUP == 0

            def streams(g, carry, nb=nb, bi=bi, rows_of=rows_of, bias_c=bias_c, bias_p=bias_p):
                for jb in range(nb):
                    tile_group([(rows_of(g * GROUP + i, jb), rows_of(g * GROUP + i, jb - 1)) for i in range(GROUP)],
                               bias_c, bias_p, bi == 0)
                return carry

            lax.fori_loop(0, dil // GROUP, streams, 0)
    o_ref[0] = acc_s[...] / l_s[...]


def _prompt_attn(q, k, v, slopes_tab):
    B, T, W = q.shape
    spec = pl.BlockSpec((1, T, LANES), lambda b, hp: (b, 0, hp))
    return pl.pallas_call(
        functools.partial(_prompt_attn_kernel, T=T),
        grid=(B, W // LANES),
        in_specs=[spec, spec, spec, _full(slopes_tab.shape)],
        out_specs=spec,
        out_shape=jax.ShapeDtypeStruct((B, T, W), F32),
        scratch_shapes=[pltpu.VMEM((T, LANES), F32)] * 3,
        compiler_params=_cparams(2),
        name="prompt_attn",
    )(q, k, v, slopes_tab)


def _rwkv_chunk_kernel(r_ref, lw_ref, k_ref, v_ref, al_ref, be_ref, g_ref, bo_ref, gnw_ref, gnb_ref,
                       y_ref, s_ref, st_scr, *, Tb, npair):
    C = CHUNK
    n2 = 2 * C
    t = pl.program_id(1)

    @pl.when(t == 0)
    def _():
        st_scr[...] = jnp.zeros_like(st_scr)

    lane = lax.broadcasted_iota(jnp.int32, (1, LANES), 1)
    m0 = (lane < HEAD_DIM).astype(F32)
    m1 = 1.0 - m0
    ri = lax.broadcasted_iota(jnp.int32, (n2, n2), 0)
    ci = lax.broadcasted_iota(jnp.int32, (n2, n2), 1)
    same = (ri < C) == (ci < C)
    strict = same & ((ri & (C - 1)) > (ci & (C - 1)))
    incl = same & ((ri & (C - 1)) >= (ci & (C - 1)))
    eye = (ri == ci).astype(F32)
    tri = (lax.broadcasted_iota(jnp.int32, (C, C), 0) >= lax.broadcasted_iota(jnp.int32, (C, C), 1)).astype(BF16)
    li = lax.broadcasted_iota(jnp.int32, (LANES, LANES), 0)
    lj = lax.broadcasted_iota(jnp.int32, (LANES, LANES), 1)
    head_mean = jnp.where((li < HEAD_DIM) == (lj < HEAD_DIM), 1.0 / HEAD_DIM, 0.0).astype(BF16)

    def stack(x):
        return jnp.concatenate([x * m0, x * m1], axis=0)

    def twice(x):
        return jnp.concatenate([x, x], axis=0)

    def body(c, carry):
        rows = pl.ds(pl.multiple_of(c * C, C), C)
        pairs = range(npair)
        cols = [slice(hp * LANES, (hp + 1) * LANES) for hp in pairs]
        S = [st_scr[hp] for hp in pairs]
        lw_all = lw_ref[0, rows, :]
        cl_all = _dot_const_l(tri, lw_all)
        e_pos_all = jnp.exp(cl_all)
        e_neg_all = jnp.exp(-cl_all)
        e_exc_all = jnp.exp(cl_all - lw_all)
        g_end = [e_pos_all[C - 1:C, cs] for cs in cols]
        rt = [r_ref[0, rows, cs] * e_pos_all[:, cs] for cs in cols]
        kt = [k_ref[0, rows, cs] * e_neg_all[:, cs] for cs in cols]
        bt = [be_ref[0, rows, cs] * e_neg_all[:, cs] for cs in cols]
        at = [al_ref[0, rows, cs] * e_exc_all[:, cs] for cs in cols]
        ar_st = [jnp.concatenate([stack(at[p]), stack(rt[p])], axis=0) for p in pairs]
        kb2 = [jnp.concatenate([twice(kt[p]), twice(bt[p])], axis=0) for p in pairs]
        v_st = [stack(v_ref[0, rows, cs]) for cs in cols]
        gram = [_dot(ar_st[p], kb2[p], "NT") for p in pairs]
        l_ab = [jnp.where(strict, gram[p][0:n2, n2:], 0.0) for p in pairs]
        l_kk = [jnp.concatenate([jnp.where(strict, gram[p][0:n2, 0:n2], 0.0),
                                 jnp.where(incl, gram[p][n2:, 0:n2], 0.0)], axis=0) for p in pairs]
        l_rb = [jnp.where(incl, gram[p][n2:, n2:], 0.0) for p in pairs]
        from_state = [_dot(ar_st[p], S[p], "NT") for p in pairs]
        from_v = [_dot(l_kk[p], v_st[p]) for p in pairs]
        tinv = [eye + l_ab[p] for p in pairs]
        x = l_ab
        for _ in range(C.bit_length() - 2):
            x = [_dot(x[p], x[p]) for p in pairs]
            tinv = [tinv[p] + _dot(tinv[p], x[p]) for p in pairs]
        u = [_dot(tinv[p], from_state[p][0:n2] + from_v[p][0:n2]) for p in pairs]
        y_st = [from_state[p][n2:] + from_v[p][n2:] + _dot(l_rb[p], u[p]) for p in pairs]
        for p in pairs:
            vu = jnp.concatenate([v_st[p], u[p]], axis=0)
            kb_end = jnp.concatenate([stack(kt[p] * g_end[p]), stack(bt[p] * g_end[p])], axis=0)
            st_scr[p] = S[p] * g_end[p] + _dot(vu, kb_end, "TN")
        for p, cs in zip(pairs, cols):
            y = y_st[p][0:C, :] + y_st[p][C:n2, :]
            mu = _dot_const(y, head_mean)
            yc = y - mu
            var = _dot_const(yc * yc, head_mean)
            yn = yc * lax.rsqrt(var + GN_EPS) * gnw_ref[:, cs] + gnb_ref[:, cs]
            y_ref[0, rows, cs] = (yn + bo_ref[0, rows, cs]) * g_ref[0, rows, cs]
        return carry

    lax.fori_loop(0, Tb // C, body, 0)

    @pl.when(t == pl.num_programs(1) - 1)
    def _():
        s_ref[0] = st_scr[...]


def _rwkv_chunked(r, lw, k2, v, al, be, g, bonus, gn_w, gn_b, tb):
    B, T, W = r.shape
    npair = W // LANES
    spec = pl.BlockSpec((1, tb, W), lambda b, t: (b, t, 0))
    st_spec = pl.BlockSpec((1, npair, LANES, LANES), lambda b, t: (b, 0, 0, 0))
    y, s = pl.pallas_call(
        functools.partial(_rwkv_chunk_kernel, Tb=tb, npair=npair),
        grid=(B, T // tb),
        in_specs=[spec] * 8 + [_full(gn_w.shape), _full(gn_b.shape)],
        out_specs=[spec, st_spec],
        out_shape=[jax.ShapeDtypeStruct((B, T, W), F32), jax.ShapeDtypeStruct((B, npair, LANES, LANES), F32)],
        scratch_shapes=[pltpu.VMEM((npair, LANES, LANES), F32)],
        compiler_params=_cparams(2),
        name="rwkv_chunk",
    )(r, lw, k2, v, al, be, g, bonus, gn_w, gn_b)
    s = s.reshape(B, npair, 2, HEAD_DIM, 2, HEAD_DIM)
    s = jnp.stack([s[:, :, 0, :, 0, :], s[:, :, 1, :, 1, :]], axis=2)
    return y, s.reshape(B, 2 * npair, HEAD_DIM, HEAD_DIM)


def _rwkv_step_kernel(s_ref, r_ref, lw_ref, k_ref, al_ref, be_ref, v_ref, g_ref, bo_ref, gnw_ref, gnb_ref,
                      y_ref, so_ref):
    S = s_ref[0]
    sa = jnp.sum(S * al_ref[0], axis=-1, keepdims=True)
    s_new = S * jnp.exp(lw_ref[0]) + sa * be_ref[0] + v_ref[0] * k_ref[0]
    so_ref[0] = s_new
    y = jnp.sum(s_new * r_ref[0], axis=-1, keepdims=True)
    mu = jnp.mean(y, axis=1, keepdims=True)
    yc = y - mu
    var = jnp.mean(yc * yc, axis=1, keepdims=True)
    yn = yc * lax.rsqrt(var + GN_EPS) * gnw_ref[...] + gnb_ref[...]
    y_ref[0] = (yn + bo_ref[0]) * g_ref[0]


def _rwkv_step(state, r, lw, k2, v, al, be, g, bonus, gn_w, gn_b):
    nb, H = state.shape[0], state.shape[1]
    rowv = lambda a: a.reshape(nb, H, 1, HEAD_DIM)
    colv = lambda a: a.reshape(nb, H, HEAD_DIM, 1)
    s_spec = pl.BlockSpec((1, H, HEAD_DIM, HEAD_DIM), lambda b: (b, 0, 0, 0))
    r_spec = pl.BlockSpec((1, H, 1, HEAD_DIM), lambda b: (b, 0, 0, 0))
    c_spec = pl.BlockSpec((1, H, HEAD_DIM, 1), lambda b: (b, 0, 0, 0))
    p_spec = _full((H, HEAD_DIM, 1))
    y, s_new = pl.pallas_call(
        _rwkv_step_kernel,
        grid=(nb,),
        in_specs=[s_spec] + [r_spec] * 5 + [c_spec] * 3 + [p_spec] * 2,
        out_specs=[c_spec, s_spec],
        out_shape=[jax.ShapeDtypeStruct((nb, H, HEAD_DIM, 1), F32), jax.ShapeDtypeStruct(state.shape, F32)],
        compiler_params=_cparams(1),
        name="rwkv_step",
    )(state, rowv(r), rowv(lw), rowv(k2), rowv(al), rowv(be), colv(v), colv(g), colv(bonus),
      gn_w.reshape(H, HEAD_DIM, 1), gn_b.reshape(H, HEAD_DIM, 1))
    return y.reshape(nb, H * HEAD_DIM), s_new


def _attn_step_kernel(q_ref, kn_ref, vn_ref, kt_ref, vt_ref, sl_ref, o_ref, *, P):
    scale = HEAD_DIM ** -0.5
    q = q_ref[0]
    dist = P - lax.broadcasted_iota(jnp.int32, (1, 1, P), 2)
    count = jnp.zeros((1, 1, P), F32)
    for win, dil in DIL_PATTERNS:
        assert dil & (dil - 1) == 0
        count = count + ((dist <= win) & ((dist & (dil - 1)) == 0)).astype(F32)
    s = jnp.sum(kt_ref[0] * q, axis=1, keepdims=True) * scale - sl_ref[...] * dist.astype(F32)
    s = jnp.where(count > 0.0, s, NEG)
    s0 = jnp.sum(kn_ref[0] * q, axis=1, keepdims=True) * scale
    m = jnp.maximum(jnp.max(s, axis=-1, keepdims=True), s0)
    p = count * jnp.exp(s - m)
    e0 = len(DIL_PATTERNS) * jnp.exp(s0 - m)
    l = jnp.sum(p, axis=-1, keepdims=True) + e0
    acc = jnp.sum(vt_ref[0] * p, axis=-1, keepdims=True) + e0 * vn_ref[0]
    o_ref[0] = acc / l


def _attn_step(q, k_new, v_new, k_cache, v_cache, slopes):
    nb, P, H, _ = k_cache.shape
    col = lambda a: a.reshape(nb, H, HEAD_DIM, 1)
    c_spec = pl.BlockSpec((1, H, HEAD_DIM, 1), lambda b: (b, 0, 0, 0))
    t_spec = pl.BlockSpec((1, H, HEAD_DIM, P), lambda b: (b, 0, 0, 0))
    out = pl.pallas_call(
        functools.partial(_attn_step_kernel, P=P),
        grid=(nb,),
        in_specs=[c_spec, c_spec, c_spec, t_spec, t_spec, _full((H, 1, 1))],
        out_specs=c_spec,
        out_shape=jax.ShapeDtypeStruct((nb, H, HEAD_DIM, 1), F32),
        compiler_params=_cparams(1),
        name="attn_step",
    )(col(q), col(k_new), col(v_new), jnp.transpose(k_cache, (0, 2, 3, 1)), jnp.transpose(v_cache, (0, 2, 3, 1)),
      slopes.reshape(H, 1, 1))
    return out.reshape(nb, H * HEAD_DIM)


def _outproj_kernel(att_ref, y_ref, x_ref, w_ref, g_ref, b_ref, o_ref, *, aw, alpha):
    h = _dot(att_ref[...], w_ref[0:aw, :]) + _dot(y_ref[...], w_ref[aw:, :])
    o_ref[...] = _layer_norm(alpha * x_ref[...] + h, g_ref[...], b_ref[...])


def _outproj(att, y_rw, x, w_out, ln_g, ln_b, tm, alpha):
    rows, D = x.shape
    aw = y_rw.shape[1]
    half = pl.BlockSpec((tm, aw), lambda i: (i, 0))
    return pl.pallas_call(
        functools.partial(_outproj_kernel, aw=aw, alpha=alpha),
        grid=(rows // tm,),
        in_specs=[half, half, pl.BlockSpec((tm, D), lambda i: (i, 0)), _full(w_out.shape), _full(ln_g.shape),
                  _full(ln_b.shape)],
        out_specs=pl.BlockSpec((tm, D), lambda i: (i, 0)),
        out_shape=jax.ShapeDtypeStruct((rows, D), F32),
        compiler_params=_cparams(1),
        name="outproj_ln",
    )(att, y_rw, x, w_out, ln_g, ln_b)


def _ffn_tail(x1, gate_pre, prev1, prev2, up, cw_ref, cb_ref, w2_ref, g_ref, b_ref, alpha):
    c = cw_ref[0:1, :] * prev2 + cw_ref[1:2, :] * prev1 + cw_ref[2:3, :] * gate_pre + cb_ref[...]
    h = c * _sigmoid(c) * up
    return _layer_norm(alpha * x1 + _dot(h, w2_ref[...]), g_ref[...], b_ref[...])


def _ffn_seq_kernel(x_ref, w1_ref, cw_ref, cb_ref, w2_ref, g_ref, b_ref, o_ref, tail_ref, carry_ref, *, dff, alpha):
    @pl.when(pl.program_id(1) == 0)
    def _():
        carry_ref[...] = jnp.zeros_like(carry_ref)

    x1 = x_ref[0]
    hu = _dot(x1, w1_ref[...])
    gate_pre = hu[:, 0:dff]
    up = hu[:, dff:]
    tm = gate_pre.shape[0]
    row = lax.broadcasted_iota(jnp.int32, gate_pre.shape, 0)
    c0 = carry_ref[0:1, :]
    c1 = carry_ref[1:2, :]
    prev1 = jnp.where(row == 0, c1, pltpu.roll(gate_pre, 1, axis=0))
    prev2 = jnp.where(row == 0, c0, jnp.where(row == 1, c1, pltpu.roll(gate_pre, 2, axis=0)))
    last2 = gate_pre[tm - 2:tm, :]
    carry_ref[0:2, :] = last2
    tail_ref[0] = last2
    o_ref[0] = _ffn_tail(x1, gate_pre, prev1, prev2, up, cw_ref, cb_ref, w2_ref, g_ref, b_ref, alpha)


def _ffn_step_kernel(x_ref, p2_ref, p1_ref, w1_ref, cw_ref, cb_ref, w2_ref, g_ref, b_ref, o_ref, gate_ref, *,
                     dff, alpha):
    x1 = x_ref[...]
    hu = _dot(x1, w1_ref[...])
    gate_pre = hu[:, 0:dff]
    gate_ref[...] = gate_pre
    o_ref[...] = _ffn_tail(x1, gate_pre, p1_ref[...], p2_ref[...], hu[:, dff:], cw_ref, cb_ref, w2_ref, g_ref,
                           b_ref, alpha)


def _resident(shape):
    nd = len(shape)
    return pl.BlockSpec(shape, lambda *_: (0,) * nd, pipeline_mode=pl.Buffered(1))


def _ffn_seq(x1, w1, conv_w, conv_b, w2, ln_g, ln_b, tm, alpha):
    B, T, D = x1.shape
    dff = w2.shape[0]
    return pl.pallas_call(
        functools.partial(_ffn_seq_kernel, dff=dff, alpha=alpha),
        grid=(B, T // tm),
        in_specs=[pl.BlockSpec((1, tm, D), lambda b, t: (b, t, 0)), _resident(w1.shape), _full(conv_w.shape),
                  _full(conv_b.shape), _resident(w2.shape), _full(ln_g.shape), _full(ln_b.shape)],
        out_specs=[pl.BlockSpec((1, tm, D), lambda b, t: (b, t, 0)),
                   pl.BlockSpec((1, 2, dff), lambda b, t: (b, 0, 0))],
        out_shape=[jax.ShapeDtypeStruct((B, T, D), F32), jax.ShapeDtypeStruct((B, 2, dff), F32)],
        scratch_shapes=[pltpu.VMEM((8, dff), F32)],
        compiler_params=_cparams(2),
        name="ffn_seq",
    )(x1, w1, conv_w, conv_b, w2, ln_g, ln_b)


def _ffn_step(x1, prev2, prev1, w1, conv_w, conv_b, w2, ln_g, ln_b, alpha):
    nb, D = x1.shape
    dff = w2.shape[0]
    return pl.pallas_call(
        functools.partial(_ffn_step_kernel, dff=dff, alpha=alpha),
        grid=(1,),
        in_specs=[_full(x1.shape), _full(prev2.shape), _full(prev1.shape), _resident(w1.shape),
                  _full(conv_w.shape), _full(conv_b.shape), _resident(w2.shape), _full(ln_g.shape),
                  _full(ln_b.shape)],
        out_specs=[_full((nb, D)), _full((nb, dff))],
        out_shape=[jax.ShapeDtypeStruct((nb, D), F32), jax.ShapeDtypeStruct((nb, dff), F32)],
        compiler_params=_cparams(1),
        name="ffn_step",
    )(x1, prev2, prev1, w1, conv_w, conv_b, w2, ln_g, ln_b)


def _layer_params(w_in, mu_shift, w0, w_lora_up, a0, a_lora_up, g_lora_up, k_k, k_a, r_k, aw):
    D, ncol = w_in.shape
    n_w, n_a, n_g = w_lora_up.shape[0], a_lora_up.shape[0], g_lora_up.shape[0]
    assert n_w + n_a == LANES and 3 * aw + 3 * aw + n_w + n_a + n_g == ncol
    g_cols = -(-n_g // LANES) * LANES
    pad = g_cols - n_g
    row = lambda a: a.reshape(1, -1).astype(F32)
    lane = jnp.arange(aw)
    bd = (lane[:, None] // HEAD_DIM == lane[None, :] // HEAD_DIM).astype(BF16)
    return dict(
        aw=aw,
        w_in=jnp.pad(w_in, ((0, 0), (0, pad))).astype(BF16),
        mu=jnp.pad(row(mu_shift), ((0, 0), (0, pad))),
        w0=row(w0), a0=row(a0), k_k=row(k_k), k_a=row(k_a), r_k=row(r_k),
        wl_pad=jnp.pad(w_lora_up, ((0, n_a), (0, 0))).astype(BF16),
        al_pad=jnp.pad(a_lora_up, ((n_w, 0), (0, 0))).astype(BF16),
        g_pad=jnp.pad(g_lora_up, ((0, pad), (0, 0))).astype(BF16),
        bd=bd,
    )


def kernel(x_prompt, x_sample, cache_k_win, cache_v_win, state_shift, state_wkv, state_conv, w_in, mu_shift, w0, w_lora_up, a0, a_lora_up, g_lora_up, k_k, k_a, r_k, gn_w, gn_b, w_out, ln1_g, ln1_b, w_ffn_in, conv_w, conv_b, w_ffn_out, ln2_g, ln2_b):
    depth = w_in.shape[0]
    alpha = (2.0 * depth) ** 0.25
    B, T, D = x_prompt.shape
    nb = x_sample.shape[0]
    assert x_sample.shape[1] == 1
    n_att = cache_k_win.shape[3]
    n_rw = state_wkv.shape[2]
    aw = n_att * HEAD_DIM
    assert n_rw * HEAD_DIM == aw and T % (Q_TILE * DIL_PATTERNS[-1][1]) == 0
    tm = min(256, T)
    tb = min(512, T)
    heads = jnp.arange(1, n_att + 1, dtype=F32)
    slopes = jnp.exp2(-8.0 * heads / n_att)
    slopes_tab = jnp.broadcast_to(slopes[:, None], (n_att, LANES))
    row = lambda a: a.reshape(1, -1)

    hp, hs = x_prompt, x_sample.reshape(nb, D)
    outs = [[] for _ in range(10)]
    for l in range(depth):
        prm = _layer_params(w_in[l], mu_shift[l], w0[l], w_lora_up[l], a0[l], a_lora_up[l], g_lora_up[l],
                            k_k[l], k_a[l], r_k[l], aw)
        w_out_b = w_out[l].astype(BF16)
        w1_b = w_ffn_in[l].astype(BF16)
        w2_b = w_ffn_out[l].astype(BF16)
        gnw, gnb = row(gn_w[l]), row(gn_b[l])

        q, k, v, kt, vt, r, lw, k2, vr, al, be, g, bonus = _inproj_seq(hp, prm, tm)
        att = _prompt_attn(q, k, v, slopes_tab)
        y_rw, wkv_p = _rwkv_chunked(r, lw, k2, vr, al, be, g, bonus, gnw, gnb, tb)
        flat = lambda a: a.reshape(B * T, a.shape[-1])
        x1 = _outproj(flat(att), flat(y_rw), flat(hp), w_out_b, row(ln1_g[l]), row(ln1_b[l]), tm, alpha)
        shift_p = hp[:, -1, :]
        hp, conv_p = _ffn_seq(x1.reshape(B, T, D), w1_b, conv_w[l], row(conv_b[l]), w2_b, row(ln2_g[l]),
                              row(ln2_b[l]), tm, alpha)

        x2 = jnp.concatenate([hs, state_shift[l]], axis=0)
        qs, ks, vs, r, lw, k2, vr, al, be, g, bonus = _inproj_step(x2, prm)
        att_s = _attn_step(qs, ks, vs, cache_k_win[l], cache_v_win[l], slopes)
        y_rw_s, wkv_s = _rwkv_step(state_wkv[l], r, lw, k2, vr, al, be, g, bonus, gn_w[l], gn_b[l])
        x1s = _outproj(att_s, y_rw_s, hs, w_out_b, row(ln1_g[l]), row(ln1_b[l]), nb, alpha)
        shift_s = hs
        hs, gate_s = _ffn_step(x1s, state_conv[l][:, 0, :], state_conv[l][:, 1, :], w1_b, conv_w[l],
                               row(conv_b[l]), w2_b, row(ln2_g[l]), row(ln2_b[l]), alpha)
        conv_s = jnp.stack([state_conv[l][:, 1, :], gate_s], axis=1)

        n_keep = min(DIL_PATTERNS[-1][0], T)
        win = lambda a: jnp.transpose(a.reshape(B, n_att, HEAD_DIM, T), (0, 3, 1, 2))[:, T - n_keep:]
        vals = (win(kt), win(vt), ks.reshape(nb, 1, n_att, HEAD_DIM), vs.reshape(nb, 1, n_att, HEAD_DIM),
                shift_p, shift_s, wkv_p, wkv_s, conv_p, conv_s)
        for lst, val in zip(outs, vals):
            lst.append(val)
    return (hp, hs.reshape(nb, 1, D)) + tuple(jnp.stack(lst) for lst in outs)
```

```python
import functools

import jax
import jax.numpy as jnp
from jax import lax
from jax.experimental import pallas as pl
from jax.experimental.pallas import tpu as pltpu

F32 = jnp.float32
BF16 = jnp.bfloat16

HEAD_DIM = 64
LANES = 128
DIL_PATTERNS = ((128, 1), (512, 4), (2048, 16))
WIN_STEPS = 128
Q_TILE = 128
GROUP = 4
CHUNK = 64
LN_EPS = 1e-5
GN_EPS = 64e-5
NEG = -1e30
VMEM_LIMIT = 56 * 1024 * 1024

_DN = {"NN": (((1,), (0,)), ((), ())), "NT": (((1,), (1,)), ((), ())), "TN": (((0,), (0,)), ((), ()))}


def _dot(a, b, dims="NN"):
    return lax.dot_general(a.astype(BF16), b.astype(BF16), _DN[dims], preferred_element_type=F32)


def _split2(x):
    hi = x.astype(BF16)
    lo = (x - hi.astype(F32)).astype(BF16)
    return hi, lo


def _dot_const(x, c):
    hi, lo = _split2(x)
    return _dot(hi, c) + _dot(lo, c)


def _dot_const_l(c, x):
    hi = x.astype(BF16)
    r1 = x - hi.astype(F32)
    mid = r1.astype(BF16)
    lo = (r1 - mid.astype(F32)).astype(BF16)
    return _dot(c, hi) + _dot(c, mid) + _dot(c, lo)


def _sigmoid(x):
    return 1.0 / (1.0 + jnp.exp(-x))


def _layer_norm(x, g, b):
    mu = jnp.mean(x, axis=-1, keepdims=True)
    xc = x - mu
    var = jnp.mean(xc * xc, axis=-1, keepdims=True)
    return xc * lax.rsqrt(var + LN_EPS) * g + b


def _cparams(n_grid):
    return pltpu.CompilerParams(dimension_semantics=("arbitrary",) * n_grid, vmem_limit_bytes=VMEM_LIMIT)


def _full(shape):
    nd = len(shape)
    return pl.BlockSpec(shape, lambda *_: (0,) * nd)


def _rwkv_prep(rw, w0, a0, k_k, k_a, r_k, wl_pad, al_pad, g_pad, bd, aw):
    r = rw[:, 0:aw]
    k = rw[:, aw:2 * aw]
    v = rw[:, 2 * aw:3 * aw]
    lo = rw[:, 3 * aw:3 * aw + 128]
    glo = rw[:, 3 * aw + 128:]
    z = -(w0 + _dot(jnp.tanh(lo), wl_pad))
    softplus = jnp.maximum(z, 0.0) + jnp.log(1.0 + jnp.exp(-jnp.abs(z)))
    lw = -jnp.exp(-softplus - 0.5)
    a = _sigmoid(a0 + _dot(lo, al_pad))
    g = _dot(_sigmoid(glo), g_pad)
    kk = k * k_k
    norm = jnp.sqrt(_dot(kk * kk, bd))
    kkn = kk / jnp.maximum(norm, 1e-12)
    k2 = k * (1.0 + (a - 1.0) * k_a)
    bonus = _dot_const(r * k2 * r_k, bd) * v
    return r, lw, k2, v, -kkn, kkn * a, g, bonus


def _inproj_seq_kernel(x_ref, w_ref, mu_ref, w0_ref, a0_ref, kk_ref, ka_ref, rk_ref, wl_ref, al_ref, gp_ref,
                       bd_ref, q_ref, k_ref, v_ref, kt_ref, vt_ref, r_o, lw_o, k2_o, vr_o, al_o, be_o, g_o, bo_o,
                       carry_ref, *, aw):
    @pl.when(pl.program_id(1) == 0)
    def _():
        carry_ref[...] = jnp.zeros_like(carry_ref)

    p = _dot(x_ref[0], w_ref[...])
    q_ref[0] = p[:, 0:aw]
    k = p[:, aw:2 * aw]
    v = p[:, 2 * aw:3 * aw]
    k_ref[0] = k
    v_ref[0] = v
    kt_ref[0] = k.T
    vt_ref[0] = v.T
    prw = p[:, 3 * aw:]
    tm = prw.shape[0]
    row = lax.broadcasted_iota(jnp.int32, prw.shape, 0)
    prev = jnp.where(row == 0, carry_ref[0:1, :], pltpu.roll(prw, 1, axis=0))
    carry_ref[0:1, :] = prw[tm - 1:tm, :]
    rw = prw + (prev - prw) * mu_ref[...]
    outs = _rwkv_prep(rw, w0_ref[...], a0_ref[...], kk_ref[...], ka_ref[...], rk_ref[...], wl_ref[...],
                      al_ref[...], gp_ref[...], bd_ref[...], aw)
    for o_ref, val in zip((r_o, lw_o, k2_o, vr_o, al_o, be_o, g_o, bo_o), outs):
        o_ref[0] = val


def _inproj_step_kernel(x_ref, w_ref, mu_ref, w0_ref, a0_ref, kk_ref, ka_ref, rk_ref, wl_ref, al_ref, gp_ref,
                        bd_ref, q_ref, k_ref, v_ref, r_o, lw_o, k2_o, vr_o, al_o, be_o, g_o, bo_o, *, aw, nb):
    p = _dot(x_ref[...], w_ref[...])
    q_ref[...] = p[0:nb, 0:aw]
    k_ref[...] = p[0:nb, aw:2 * aw]
    v_ref[...] = p[0:nb, 2 * aw:3 * aw]
    prw = p[0:nb, 3 * aw:]
    prev = p[nb:2 * nb, 3 * aw:]
    rw = prw + (prev - prw) * mu_ref[...]
    outs = _rwkv_prep(rw, w0_ref[...], a0_ref[...], kk_ref[...], ka_ref[...], rk_ref[...], wl_ref[...],
                      al_ref[...], gp_ref[...], bd_ref[...], aw)
    for o_ref, val in zip((r_o, lw_o, k2_o, vr_o, al_o, be_o, g_o, bo_o), outs):
        o_ref[...] = val


_PREP_PARAMS = ("mu", "w0", "a0", "k_k", "k_a", "r_k", "wl_pad", "al_pad", "g_pad", "bd")


def _inproj_seq(x, prm, tm):
    B, T, D = x.shape
    aw = prm["aw"]
    nrw = prm["w_in"].shape[1] - 3 * aw
    small = [prm[n] for n in _PREP_PARAMS]
    row_spec = pl.BlockSpec((1, tm, aw), lambda b, t: (b, t, 0))
    col_spec = pl.BlockSpec((1, aw, tm), lambda b, t: (b, 0, t))
    row_sds = jax.ShapeDtypeStruct((B, T, aw), F32)
    col_sds = jax.ShapeDtypeStruct((B, aw, T), F32)
    return pl.pallas_call(
        functools.partial(_inproj_seq_kernel, aw=aw),
        grid=(B, T // tm),
        in_specs=[pl.BlockSpec((1, tm, D), lambda b, t: (b, t, 0)), _full(prm["w_in"].shape)]
        + [_full(s.shape) for s in small],
        out_specs=[row_spec] * 3 + [col_spec] * 2 + [row_spec] * 8,
        out_shape=[row_sds] * 3 + [col_sds] * 2 + [row_sds] * 8,
        scratch_shapes=[pltpu.VMEM((8, nrw), F32)],
        compiler_params=_cparams(2),
        name="inproj_seq",
    )(x, prm["w_in"], *small)


def _inproj_step(x2, prm):
    nb = x2.shape[0] // 2
    aw = prm["aw"]
    small = [prm[n] for n in _PREP_PARAMS]
    out_sds = jax.ShapeDtypeStruct((nb, aw), F32)
    return pl.pallas_call(
        functools.partial(_inproj_step_kernel, aw=aw, nb=nb),
        grid=(1,),
        in_specs=[_full(x2.shape), _full(prm["w_in"].shape)] + [_full(s.shape) for s in small],
        out_specs=[_full((nb, aw))] * 11,
        out_shape=[out_sds] * 11,
        compiler_params=_cparams(1),
        name="inproj_step",
    )(x2, prm["w_in"], *small)


def _prompt_attn_kernel(q_ref, k_ref, v_ref, sl_ref, o_ref, acc_s, m_s, l_s, *, T):
    hp = pl.program_id(1)
    scale = HEAD_DIM ** -0.5
    lane = lax.broadcasted_iota(jnp.int32, (1, LANES), 1)
    head_masks = ((lane < HEAD_DIM).astype(F32), (lane >= HEAD_DIM).astype(F32))
    qi = lax.broadcasted_iota(jnp.int32, (Q_TILE, Q_TILE), 0)
    ki = lax.broadcasted_iota(jnp.int32, (Q_TILE, Q_TILE), 1)
    steps_cur = (qi - ki).astype(F32)
    steps_prev = (qi - ki + Q_TILE).astype(F32)
    ok_cur = ki <= qi
    ok_prev = ki >= qi
    first_lanes = lax.broadcasted_iota(jnp.int32, (Q_TILE, LANES), 1) < HEAD_DIM
    slopes = [sl_ref[pl.ds(2 * hp + j, 1), :] for j in range(2)]

    def tile_group(tiles, bias_c, bias_p, first_branch):
        cases = [(t, j) for t in range(len(tiles)) for j in range(2)]
        q = [q_ref[0, rows, :] * scale for rows, _ in tiles]
        kc = [k_ref[0, rows, :].astype(BF16) for rows, _ in tiles]
        vc = [v_ref[0, rows, :].astype(BF16) for rows, _ in tiles]
        kp = [None if prev is None else k_ref[0, prev, :].astype(BF16) for _, prev in tiles]
        vp = [None if prev is None else v_ref[0, prev, :].astype(BF16) for _, prev in tiles]
        qm = {(t, j): (q[t] * head_masks[j]).astype(BF16) for t, j in cases}
        s_c = {(t, j): _dot(qm[t, j], kc[t], "NT") + bias_c[j] for t, j in cases}
        s_p = {(t, j): _dot(qm[t, j], kp[t], "NT") + bias_p[j] for t, j in cases if kp[t] is not None}
        m = {c: jnp.max(jnp.maximum(s_c[c], s_p[c]) if c in s_p else s_c[c], axis=-1, keepdims=True)
             for c in cases}
        p_c = {c: jnp.exp(s_c[c] - m[c]) for c in cases}
        p_p = {c: jnp.exp(s_p[c] - m[c]) for c in s_p}
        l = {c: jnp.sum(p_c[c] + p_p[c] if c in p_p else p_c[c], axis=-1, keepdims=True) for c in cases}
        acc = {(t, j): _dot(p_c[t, j], vc[t]) for t, j in cases}
        for t, j in s_p:
            acc[t, j] = acc[t, j] + _dot(p_p[t, j], vp[t])
        for t, (rows, _) in enumerate(tiles):
            acc_n = jnp.where(first_lanes, acc[t, 0], acc[t, 1])
            m_n = jnp.where(first_lanes, m[t, 0], m[t, 1])
            l_n = jnp.where(first_lanes, l[t, 0], l[t, 1])
            if first_branch:
                acc_s[rows, :] = acc_n
                m_s[rows, :] = m_n
                l_s[rows, :] = l_n
            else:
                m_o = m_s[rows, :]
                m_new = jnp.maximum(m_o, m_n)
                e_o = jnp.exp(m_o - m_new)
                e_n = jnp.exp(m_n - m_new)
                acc_s[rows, :] = acc_s[rows, :] * e_o + acc_n * e_n
                l_s[rows, :] = l_s[rows, :] * e_o + l_n * e_n
                m_s[rows, :] = m_new

    for bi, (win, dil) in enumerate(DIL_PATTERNS):
        assert win // dil == WIN_STEPS
        nb = T // dil // Q_TILE
        bias_c = [jnp.where(ok_cur, -(slopes[j] * float(dil)) * steps_cur, NEG) for j in range(2)]
        bias_p = [jnp.where(ok_prev, -(slopes[j] * float(dil)) * steps_prev, NEG) for j in range(2)]

        def rows_of(s, jb, dil=dil):
            if isinstance(jb, int) and jb < 0:
                return None
            if dil == 1:
                return pl.ds(pl.multiple_of(jb * Q_TILE, Q_TILE), Q_TILE)
            return pl.ds(s + dil * Q_TILE * jb, Q_TILE, stride=dil)

        if nb >= GROUP:
            assert nb % GROUP == 0

            def stream(s, carry, nb=nb, bi=bi, rows_of=rows_of, bias_c=bias_c, bias_p=bias_p):
                tile_group([(rows_of(s, jb), rows_of(s, jb - 1)) for jb in range(GROUP)], bias_c, bias_p, bi == 0)

                def later(g, c2):
                    tile_group([(rows_of(s, g * GROUP + i), rows_of(s, g * GROUP + i - 1)) for i in range(GROUP)],
                               bias_c, bias_p, bi == 0)
                    return c2

                if nb > GROUP:
                    lax.fori_loop(1, nb // GROUP, later, 0)
                return carry

            if dil == 1:
                stream(0, 0)
            else:
                lax.fori_loop(0, dil, stream, 0)
        else:
            assert dil % GROUP == 0

            def streams(g, carry, nb=nb, bi=bi, rows_of=rows_of, bias_c=bias_c, bias_p=bias_p):
                for jb in range(nb):
                    tile_group([(rows_of(g * GROUP + i, jb), rows_of(g * GROUP + i, jb - 1)) for i in range(GROUP)],
                               bias_c, bias_p, bi == 0)
                return carry

            lax.fori_loop(0, dil // GROUP, streams, 0)
    o_ref[0] = acc_s[...] / l_s[...]


def _prompt_attn(q, k, v, slopes_tab):
    B, T, W = q.shape
    spec = pl.BlockSpec((1, T, LANES), lambda b, hp: (b, 0, hp))
    return pl.pallas_call(
        functools.partial(_prompt_attn_kernel, T=T),
        grid=(B, W // LANES),
        in_specs=[spec, spec, spec, _full(slopes_tab.shape)],
        out_specs=spec,
        out_shape=jax.ShapeDtypeStruct((B, T, W), F32),
        scratch_shapes=[pltpu.VMEM((T, LANES), F32)] * 3,
        compiler_params=_cparams(2),
        name="prompt_attn",
    )(q, k, v, slopes_tab)


def _rwkv_chunk_kernel(r_ref, lw_ref, k_ref, v_ref, al_ref, be_ref, g_ref, bo_ref, gnw_ref, gnb_ref,
                       y_ref, s_ref, st_scr, *, Tb, npair, nbb):
    C = CHUNK
    n2 = 2 * C
    t = pl.program_id(1)

    @pl.when(t == 0)
    def _():
        st_scr[...] = jnp.zeros_like(st_scr)

    lane = lax.broadcasted_iota(jnp.int32, (1, LANES), 1)
    m0 = (lane < HEAD_DIM).astype(F32)
    m1 = 1.0 - m0
    ri = lax.broadcasted_iota(jnp.int32, (n2, n2), 0)
    ci = lax.broadcasted_iota(jnp.int32, (n2, n2), 1)
    same = (ri < C) == (ci < C)
    strict = same & ((ri & (C - 1)) > (ci & (C - 1)))
    incl = same & ((ri & (C - 1)) >= (ci & (C - 1)))
    tri = (lax.broadcasted_iota(jnp.int32, (C, C), 0) >= lax.broadcasted_iota(jnp.int32, (C, C), 1)).astype(BF16)
    li = lax.broadcasted_iota(jnp.int32, (LANES, LANES), 0)
    lj = lax.broadcasted_iota(jnp.int32, (LANES, LANES), 1)
    head_mean = jnp.where((li < HEAD_DIM) == (lj < HEAD_DIM), 1.0 / HEAD_DIM, 0.0).astype(BF16)

    def stack(x):
        return jnp.concatenate([x * m0, x * m1], axis=0)

    def twice(x):
        return jnp.concatenate([x, x], axis=0)

    def body(c, carry):
        rows = pl.ds(pl.multiple_of(c * C, C), C)
        chains = [(bb, slice(hp * LANES, (hp + 1) * LANES)) for bb in range(nbb) for hp in range(npair)]
        ids = range(len(chains))
        S = [st_scr[i] for i in ids]
        lw_all = jnp.concatenate([lw_ref[bb, rows, :] for bb in range(nbb)], axis=1)
        cl_all = _dot_const_l(tri, lw_all)
        e_pos_all = jnp.exp(cl_all)
        e_neg_all = jnp.exp(-cl_all)
        e_exc_all = jnp.exp(cl_all - lw_all)
        wide = [slice(i * LANES, (i + 1) * LANES) for i in ids]
        g_end = [e_pos_all[C - 1:C, wide[i]] for i in ids]
        rt = [r_ref[bb, rows, cs] * e_pos_all[:, wide[i]] for i, (bb, cs) in enumerate(chains)]
        kt = [k_ref[bb, rows, cs] * e_neg_all[:, wide[i]] for i, (bb, cs) in enumerate(chains)]
        bt = [be_ref[bb, rows, cs] * e_neg_all[:, wide[i]] for i, (bb, cs) in enumerate(chains)]
        at = [al_ref[bb, rows, cs] * e_exc_all[:, wide[i]] for i, (bb, cs) in enumerate(chains)]
        ar_st = [jnp.concatenate([stack(at[i]), stack(rt[i])], axis=0) for i in ids]
        kb2 = [jnp.concatenate([twice(kt[i]), twice(bt[i])], axis=0) for i in ids]
        v_st = [stack(v_ref[bb, rows, cs]) for bb, cs in chains]
        gram = [_dot(ar_st[i], kb2[i], "NT") for i in ids]
        l_ab = [jnp.where(strict, gram[i][0:n2, n2:], 0.0) for i in ids]
        l_kk = [jnp.concatenate([jnp.where(strict, gram[i][0:n2, 0:n2], 0.0),
                                 jnp.where(incl, gram[i][n2:, 0:n2], 0.0)], axis=0) for i in ids]
        l_rb = [jnp.where(incl, gram[i][n2:, n2:], 0.0) for i in ids]
        from_state = [_dot(ar_st[i], S[i], "NT") for i in ids]
        from_v = [_dot(l_kk[i], v_st[i]) for i in ids]
        x = l_ab
        u = [from_state[i][0:n2] + from_v[i][0:n2] for i in ids]
        for _ in range(C.bit_length() - 2):
            xu = [_dot(x[i], jnp.concatenate([x[i], u[i]], axis=1)) for i in ids]
            x = [xu[i][:, 0:n2] for i in ids]
            u = [u[i] + xu[i][:, n2:] for i in ids]
        u = [u[i] + _dot(x[i], u[i]) for i in ids]
        y_st = [from_state[i][n2:] + from_v[i][n2:] + _dot(l_rb[i], u[i]) for i in ids]
        for i in ids:
            vu = jnp.concatenate([v_st[i], u[i]], axis=0)
            kb_end = jnp.concatenate([stack(kt[i] * g_end[i]), stack(bt[i] * g_end[i])], axis=0)
            st_scr[i] = S[i] * g_end[i] + _dot(vu, kb_end, "TN")
        y = jnp.concatenate([y_st[i][0:C, :] + y_st[i][C:n2, :] for i in ids], axis=0)
        yc = y - _dot_const(y, head_mean)
        yn = yc * lax.rsqrt(_dot_const(yc * yc, head_mean) + GN_EPS)
        for i, (bb, cs) in enumerate(chains):
            yn_i = yn[i * C:(i + 1) * C, :] * gnw_ref[:, cs] + gnb_ref[:, cs]
            y_ref[bb, rows, cs] = (yn_i + bo_ref[bb, rows, cs]) * g_ref[bb, rows, cs]
        return carry

    lax.fori_loop(0, Tb // C, body, 0)

    @pl.when(t == pl.num_programs(1) - 1)
    def _():
        s_ref[...] = st_scr[...].reshape(s_ref.shape)


def _rwkv_chunked(r, lw, k2, v, al, be, g, bonus, gn_w, gn_b, tb):
    B, T, W = r.shape
    npair = W // LANES
    nbb = 2 if B % 2 == 0 else 1
    spec = pl.BlockSpec((nbb, tb, W), lambda b, t: (b, t, 0))
    st_spec = pl.BlockSpec((nbb, npair, LANES, LANES), lambda b, t: (b, 0, 0, 0))
    y, s = pl.pallas_call(
        functools.partial(_rwkv_chunk_kernel, Tb=tb, npair=npair, nbb=nbb),
        grid=(B // nbb, T // tb),
        in_specs=[spec] * 8 + [_full(gn_w.shape), _full(gn_b.shape)],
        out_specs=[spec, st_spec],
        out_shape=[jax.ShapeDtypeStruct((B, T, W), F32), jax.ShapeDtypeStruct((B, npair, LANES, LANES), F32)],
        scratch_shapes=[pltpu.VMEM((nbb * npair, LANES, LANES), F32)],
        compiler_params=_cparams(2),
        name="rwkv_chunk",
    )(r, lw, k2, v, al, be, g, bonus, gn_w, gn_b)
    s = s.reshape(B, npair, 2, HEAD_DIM, 2, HEAD_DIM)
    s = jnp.stack([s[:, :, 0, :, 0, :], s[:, :, 1, :, 1, :]], axis=2)
    return y, s.reshape(B, 2 * npair, HEAD_DIM, HEAD_DIM)


def _rwkv_step_kernel(s_ref, r_ref, lw_ref, k_ref, al_ref, be_ref, v_ref, g_ref, bo_ref, gnw_ref, gnb_ref,
                      y_ref, so_ref):
    S = s_ref[0]
    sa = jnp.sum(S * al_ref[0], axis=-1, keepdims=True)
    s_new = S * jnp.exp(lw_ref[0]) + sa * be_ref[0] + v_ref[0] * k_ref[0]
    so_ref[0] = s_new
    y = jnp.sum(s_new * r_ref[0], axis=-1, keepdims=True)
    mu = jnp.mean(y, axis=1, keepdims=True)
    yc = y - mu
    var = jnp.mean(yc * yc, axis=1, keepdims=True)
    yn = yc * lax.rsqrt(var + GN_EPS) * gnw_ref[...] + gnb_ref[...]
    y_ref[0] = (yn + bo_ref[0]) * g_ref[0]


def _rwkv_step(state, r, lw, k2, v, al, be, g, bonus, gn_w, gn_b):
    nb, H = state.shape[0], state.shape[1]
    rowv = lambda a: a.reshape(nb, H, 1, HEAD_DIM)
    colv = lambda a: a.reshape(nb, H, HEAD_DIM, 1)
    s_spec = pl.BlockSpec((1, H, HEAD_DIM, HEAD_DIM), lambda b: (b, 0, 0, 0))
    r_spec = pl.BlockSpec((1, H, 1, HEAD_DIM), lambda b: (b, 0, 0, 0))
    c_spec = pl.BlockSpec((1, H, HEAD_DIM, 1), lambda b: (b, 0, 0, 0))
    p_spec = _full((H, HEAD_DIM, 1))
    y, s_new = pl.pallas_call(
        _rwkv_step_kernel,
        grid=(nb,),
        in_specs=[s_spec] + [r_spec] * 5 + [c_spec] * 3 + [p_spec] * 2,
        out_specs=[c_spec, s_spec],
        out_shape=[jax.ShapeDtypeStruct((nb, H, HEAD_DIM, 1), F32), jax.ShapeDtypeStruct(state.shape, F32)],
        compiler_params=_cparams(1),
        name="rwkv_step",
    )(state, rowv(r), rowv(lw), rowv(k2), rowv(al), rowv(be), colv(v), colv(g), colv(bonus),
      gn_w.reshape(H, HEAD_DIM, 1), gn_b.reshape(H, HEAD_DIM, 1))
    return y.reshape(nb, H * HEAD_DIM), s_new


def _attn_step_kernel(q_ref, kn_ref, vn_ref, kt_ref, vt_ref, sl_ref, o_ref, *, P):
    scale = HEAD_DIM ** -0.5
    q = q_ref[0]
    dist = P - lax.broadcasted_iota(jnp.int32, (1, 1, P), 2)
    count = jnp.zeros((1, 1, P), F32)
    for win, dil in DIL_PATTERNS:
        assert dil & (dil - 1) == 0
        count = count + ((dist <= win) & ((dist & (dil - 1)) == 0)).astype(F32)
    s = jnp.sum(kt_ref[0] * q, axis=1, keepdims=True) * scale - sl_ref[...] * dist.astype(F32)
    s = jnp.where(count > 0.0, s, NEG)
    s0 = jnp.sum(kn_ref[0] * q, axis=1, keepdims=True) * scale
    m = jnp.maximum(jnp.max(s, axis=-1, keepdims=True), s0)
    p = count * jnp.exp(s - m)
    e0 = len(DIL_PATTERNS) * jnp.exp(s0 - m)
    l = jnp.sum(p, axis=-1, keepdims=True) + e0
    acc = jnp.sum(vt_ref[0] * p, axis=-1, keepdims=True) + e0 * vn_ref[0]
    o_ref[0] = acc / l


def _attn_step(q, k_new, v_new, k_cache, v_cache, slopes):
    nb, P, H, _ = k_cache.shape
    col = lambda a: a.reshape(nb, H, HEAD_DIM, 1)
    c_spec = pl.BlockSpec((1, H, HEAD_DIM, 1), lambda b: (b, 0, 0, 0))
    t_spec = pl.BlockSpec((1, H, HEAD_DIM, P), lambda b: (b, 0, 0, 0))
    out = pl.pallas_call(
        functools.partial(_attn_step_kernel, P=P),
        grid=(nb,),
        in_specs=[c_spec, c_spec, c_spec, t_spec, t_spec, _full((H, 1, 1))],
        out_specs=c_spec,
        out_shape=jax.ShapeDtypeStruct((nb, H, HEAD_DIM, 1), F32),
        compiler_params=_cparams(1),
        name="attn_step",
    )(col(q), col(k_new), col(v_new), jnp.transpose(k_cache, (0, 2, 3, 1)), jnp.transpose(v_cache, (0, 2, 3, 1)),
      slopes.reshape(H, 1, 1))
    return out.reshape(nb, H * HEAD_DIM)


def _outproj_kernel(att_ref, y_ref, x_ref, w_ref, g_ref, b_ref, o_ref, *, aw, alpha):
    h = _dot(att_ref[...], w_ref[0:aw, :]) + _dot(y_ref[...], w_ref[aw:, :])
    o_ref[...] = _layer_norm(alpha * x_ref[...] + h, g_ref[...], b_ref[...])


def _outproj(att, y_rw, x, w_out, ln_g, ln_b, tm, alpha):
    rows, D = x.shape
    aw = y_rw.shape[1]
    half = pl.BlockSpec((tm, aw), lambda i: (i, 0))
    return pl.pallas_call(
        functools.partial(_outproj_kernel, aw=aw, alpha=alpha),
        grid=(rows // tm,),
        in_specs=[half, half, pl.BlockSpec((tm, D), lambda i: (i, 0)), _full(w_out.shape), _full(ln_g.shape),
                  _full(ln_b.shape)],
        out_specs=pl.BlockSpec((tm, D), lambda i: (i, 0)),
        out_shape=jax.ShapeDtypeStruct((rows, D), F32),
        compiler_params=_cparams(1),
        name="outproj_ln",
    )(att, y_rw, x, w_out, ln_g, ln_b)


def _ffn_tail(x1, gate_pre, prev1, prev2, up, cw_ref, cb_ref, w2_ref, g_ref, b_ref, alpha):
    c = cw_ref[0:1, :] * prev2 + cw_ref[1:2, :] * prev1 + cw_ref[2:3, :] * gate_pre + cb_ref[...]
    h = c * _sigmoid(c) * up
    return _layer_norm(alpha * x1 + _dot(h, w2_ref[...]), g_ref[...], b_ref[...])


def _ffn_seq_kernel(x_ref, w1_ref, cw_ref, cb_ref, w2_ref, g_ref, b_ref, o_ref, tail_ref, carry_ref, *, dff, alpha):
    @pl.when(pl.program_id(1) == 0)
    def _():
        carry_ref[...] = jnp.zeros_like(carry_ref)

    x1 = x_ref[0]
    hu = _dot(x1, w1_ref[...])
    gate_pre = hu[:, 0:dff]
    up = hu[:, dff:]
    tm = gate_pre.shape[0]
    row = lax.broadcasted_iota(jnp.int32, gate_pre.shape, 0)
    c0 = carry_ref[0:1, :]
    c1 = carry_ref[1:2, :]
    prev1 = jnp.where(row == 0, c1, pltpu.roll(gate_pre, 1, axis=0))
    prev2 = jnp.where(row == 0, c0, jnp.where(row == 1, c1, pltpu.roll(gate_pre, 2, axis=0)))
    last2 = gate_pre[tm - 2:tm, :]
    carry_ref[0:2, :] = last2
    tail_ref[0] = last2
    o_ref[0] = _ffn_tail(x1, gate_pre, prev1, prev2, up, cw_ref, cb_ref, w2_ref, g_ref, b_ref, alpha)


def _ffn_step_kernel(x_ref, p2_ref, p1_ref, w1_ref, cw_ref, cb_ref, w2_ref, g_ref, b_ref, o_ref, gate_ref, *,
                     dff, alpha):
    x1 = x_ref[...]
    hu = _dot(x1, w1_ref[...])
    gate_pre = hu[:, 0:dff]
    gate_ref[...] = gate_pre
    o_ref[...] = _ffn_tail(x1, gate_pre, p1_ref[...], p2_ref[...], hu[:, dff:], cw_ref, cb_ref, w2_ref, g_ref,
                           b_ref, alpha)


def _resident(shape):
    nd = len(shape)
    return pl.BlockSpec(shape, lambda *_: (0,) * nd, pipeline_mode=pl.Buffered(1))


def _ffn_seq(x1, w1, conv_w, conv_b, w2, ln_g, ln_b, tm, alpha):
    B, T, D = x1.shape
    dff = w2.shape[0]
    return pl.pallas_call(
        functools.partial(_ffn_seq_kernel, dff=dff, alpha=alpha),
        grid=(B, T // tm),
        in_specs=[pl.BlockSpec((1, tm, D), lambda b, t: (b, t, 0)), _resident(w1.shape), _full(conv_w.shape),
                  _full(conv_b.shape), _resident(w2.shape), _full(ln_g.shape), _full(ln_b.shape)],
        out_specs=[pl.BlockSpec((1, tm, D), lambda b, t: (b, t, 0)),
                   pl.BlockSpec((1, 2, dff), lambda b, t: (b, 0, 0))],
        out_shape=[jax.ShapeDtypeStruct((B, T, D), F32), jax.ShapeDtypeStruct((B, 2, dff), F32)],
        scratch_shapes=[pltpu.VMEM((8, dff), F32)],
        compiler_params=_cparams(2),
        name="ffn_seq",
    )(x1, w1, conv_w, conv_b, w2, ln_g, ln_b)


def _ffn_step(x1, prev2, prev1, w1, conv_w, conv_b, w2, ln_g, ln_b, alpha):
    nb, D = x1.shape
    dff = w2.shape[0]
    return pl.pallas_call(
        functools.partial(_ffn_step_kernel, dff=dff, alpha=alpha),
        grid=(1,),
        in_specs=[_full(x1.shape), _full(prev2.shape), _full(prev1.shape), _resident(w1.shape),
                  _full(conv_w.shape), _full(conv_b.shape), _resident(w2.shape), _full(ln_g.shape),
                  _full(ln_b.shape)],
        out_specs=[_full((nb, D)), _full((nb, dff))],
        out_shape=[jax.ShapeDtypeStruct((nb, D), F32), jax.ShapeDtypeStruct((nb, dff), F32)],
        compiler_params=_cparams(1),
        name="ffn_step",
    )(x1, prev2, prev1, w1, conv_w, conv_b, w2, ln_g, ln_b)


def _layer_params(w_in, mu_shift, w0, w_lora_up, a0, a_lora_up, g_lora_up, k_k, k_a, r_k, aw):
    D, ncol = w_in.shape
    n_w, n_a, n_g = w_lora_up.shape[0], a_lora_up.shape[0], g_lora_up.shape[0]
    assert n_w + n_a == LANES and 3 * aw + 3 * aw + n_w + n_a + n_g == ncol
    g_cols = -(-n_g // LANES) * LANES
    pad = g_cols - n_g
    row = lambda a: a.reshape(1, -1).astype(F32)
    lane = jnp.arange(aw)
    bd = (lane[:, None] // HEAD_DIM == lane[None, :] // HEAD_DIM).astype(BF16)
    return dict(
        aw=aw,
        w_in=jnp.pad(w_in, ((0, 0), (0, pad))).astype(BF16),
        mu=jnp.pad(row(mu_shift), ((0, 0), (0, pad))),
        w0=row(w0), a0=row(a0), k_k=row(k_k), k_a=row(k_a), r_k=row(r_k),
        wl_pad=jnp.pad(w_lora_up, ((0, n_a), (0, 0))).astype(BF16),
        al_pad=jnp.pad(a_lora_up, ((n_w, 0), (0, 0))).astype(BF16),
        g_pad=jnp.pad(g_lora_up, ((0, pad), (0, 0))).astype(BF16),
        bd=bd,
    )


def kernel(x_prompt, x_sample, cache_k_win, cache_v_win, state_shift, state_wkv, state_conv, w_in, mu_shift, w0, w_lora_up, a0, a_lora_up, g_lora_up, k_k, k_a, r_k, gn_w, gn_b, w_out, ln1_g, ln1_b, w_ffn_in, conv_w, conv_b, w_ffn_out, ln2_g, ln2_b):
    depth = w_in.shape[0]
    alpha = (2.0 * depth) ** 0.25
    B, T, D = x_prompt.shape
    nb = x_sample.shape[0]
    assert x_sample.shape[1] == 1
    n_att = cache_k_win.shape[3]
    n_rw = state_wkv.shape[2]
    aw = n_att * HEAD_DIM
    assert n_rw * HEAD_DIM == aw and T % (Q_TILE * DIL_PATTERNS[-1][1]) == 0
    tm = min(256, T)
    tb = min(256, T)
    heads = jnp.arange(1, n_att + 1, dtype=F32)
    slopes = jnp.exp2(-8.0 * heads / n_att)
    slopes_tab = jnp.broadcast_to(slopes[:, None], (n_att, LANES))
    row = lambda a: a.reshape(1, -1)

    hp, hs = x_prompt, x_sample.reshape(nb, D)
    outs = [[] for _ in range(10)]
    for l in range(depth):
        prm = _layer_params(w_in[l], mu_shift[l], w0[l], w_lora_up[l], a0[l], a_lora_up[l], g_lora_up[l],
                            k_k[l], k_a[l], r_k[l], aw)
        w_out_b = w_out[l].astype(BF16)
        w1_b = w_ffn_in[l].astype(BF16)
        w2_b = w_ffn_out[l].astype(BF16)
        gnw, gnb = row(gn_w[l]), row(gn_b[l])

        q, k, v, kt, vt, r, lw, k2, vr, al, be, g, bonus = _inproj_seq(hp, prm, tm)
        att = _prompt_attn(q, k, v, slopes_tab)
        y_rw, wkv_p = _rwkv_chunked(r, lw, k2, vr, al, be, g, bonus, gnw, gnb, tb)
        flat = lambda a: a.reshape(B * T, a.shape[-1])
        x1 = _outproj(flat(att), flat(y_rw), flat(hp), w_out_b, row(ln1_g[l]), row(ln1_b[l]), tm, alpha)
        shift_p = hp[:, -1, :]
        hp, conv_p = _ffn_seq(x1.reshape(B, T, D), w1_b, conv_w[l], row(conv_b[l]), w2_b, row(ln2_g[l]),
                              row(ln2_b[l]), tm, alpha)

        x2 = jnp.concatenate([hs, state_shift[l]], axis=0)
        qs, ks, vs, r, lw, k2, vr, al, be, g, bonus = _inproj_step(x2, prm)
        att_s = _attn_step(qs, ks, vs, cache_k_win[l], cache_v_win[l], slopes)
        y_rw_s, wkv_s = _rwkv_step(state_wkv[l], r, lw, k2, vr, al, be, g, bonus, gn_w[l], gn_b[l])
        x1s = _outproj(att_s, y_rw_s, hs, w_out_b, row(ln1_g[l]), row(ln1_b[l]), nb, alpha)
        shift_s = hs
        hs, gate_s = _ffn_step(x1s, state_conv[l][:, 0, :], state_conv[l][:, 1, :], w1_b, conv_w[l],
                               row(conv_b[l]), w2_b, row(ln2_g[l]), row(ln2_b[l]), alpha)
        conv_s = jnp.stack([state_conv[l][:, 1, :], gate_s], axis=1)

        n_keep = min(DIL_PATTERNS[-1][0], T)
        win = lambda a: jnp.transpose(a.reshape(B, n_att, HEAD_DIM, T), (0, 3, 1, 2))[:, T - n_keep:]
        vals = (win(kt), win(vt), ks.reshape(nb, 1, n_att, HEAD_DIM), vs.reshape(nb, 1, n_att, HEAD_DIM),
                shift_p, shift_s, wkv_p, wkv_s, conv_p, conv_s)
        for lst, val in zip(outs, vals):
            lst.append(val)
    return (hp, hs.reshape(nb, 1, D)) + tuple(jnp.stack(lst) for lst in outs)
```

```python
import functools

import jax
import jax.numpy as jnp
from jax import lax
from jax.experimental import pallas as pl
from jax.experimental.pallas import tpu as pltpu

F32 = jnp.float32
BF16 = jnp.bfloat16

HEAD_DIM = 64
LANES = 128
DIL_PATTERNS = ((128, 1), (512, 4), (2048, 16))
WIN_STEPS = 128
Q_TILE = 128
GROUP = 4
CHUNK = 64
LN_EPS = 1e-5
GN_EPS = 64e-5
NEG = -1e30
VMEM_LIMIT = 56 * 1024 * 1024

_DN = {"NN": (((1,), (0,)), ((), ())), "NT": (((1,), (1,)), ((), ())), "TN": (((0,), (0,)), ((), ()))}


def _dot(a, b, dims="NN"):
    return lax.dot_general(a.astype(BF16), b.astype(BF16), _DN[dims], preferred_element_type=F32)


def _split2(x):
    hi = x.astype(BF16)
    lo = (x - hi.astype(F32)).astype(BF16)
    return hi, lo


def _dot_const(x, c):
    hi, lo = _split2(x)
    return _dot(hi, c) + _dot(lo, c)


def _dot_const_l(c, x):
    hi = x.astype(BF16)
    r1 = x - hi.astype(F32)
    mid = r1.astype(BF16)
    lo = (r1 - mid.astype(F32)).astype(BF16)
    return _dot(c, hi) + _dot(c, mid) + _dot(c, lo)


def _sigmoid(x):
    return 1.0 / (1.0 + jnp.exp(-x))


def _layer_norm(x, g, b):
    mu = jnp.mean(x, axis=-1, keepdims=True)
    xc = x - mu
    var = jnp.mean(xc * xc, axis=-1, keepdims=True)
    return xc * lax.rsqrt(var + LN_EPS) * g + b


def _cparams(n_grid):
    return pltpu.CompilerParams(dimension_semantics=("arbitrary",) * n_grid, vmem_limit_bytes=VMEM_LIMIT)


def _full(shape):
    nd = len(shape)
    return pl.BlockSpec(shape, lambda *_: (0,) * nd)


def _rwkv_prep(rw, w0, a0, k_k, k_a, r_k, wl_pad, al_pad, g_pad, bd, aw):
    r = rw[:, 0:aw]
    k = rw[:, aw:2 * aw]
    v = rw[:, 2 * aw:3 * aw]
    lo = rw[:, 3 * aw:3 * aw + 128]
    glo = rw[:, 3 * aw + 128:]
    z = -(w0 + _dot(jnp.tanh(lo), wl_pad))
    softplus = jnp.maximum(z, 0.0) + jnp.log(1.0 + jnp.exp(-jnp.abs(z)))
    lw = -jnp.exp(-softplus - 0.5)
    a = _sigmoid(a0 + _dot(lo, al_pad))
    g = _dot(_sigmoid(glo), g_pad)
    kk = k * k_k
    norm = jnp.sqrt(_dot(kk * kk, bd))
    kkn = kk / jnp.maximum(norm, 1e-12)
    k2 = k * (1.0 + (a - 1.0) * k_a)
    bonus = _dot(r * k2 * r_k, bd) * v
    return r, lw, k2, v, -kkn, kkn * a, g, bonus


def _inproj_seq_kernel(x_ref, w_ref, mu_ref, w0_ref, a0_ref, kk_ref, ka_ref, rk_ref, wl_ref, al_ref, gp_ref,
                       bd_ref, q_ref, k_ref, v_ref, kt_ref, vt_ref, r_o, lw_o, k2_o, vr_o, al_o, be_o, g_o, bo_o,
                       carry_ref, *, aw):
    @pl.when(pl.program_id(1) == 0)
    def _():
        carry_ref[...] = jnp.zeros_like(carry_ref)

    p = _dot(x_ref[0], w_ref[...])
    q_ref[0] = p[:, 0:aw]
    k = p[:, aw:2 * aw]
    v = p[:, 2 * aw:3 * aw]
    k_ref[0] = k
    v_ref[0] = v
    kt_ref[0] = k.T
    vt_ref[0] = v.T
    prw = p[:, 3 * aw:]
    tm = prw.shape[0]
    row = lax.broadcasted_iota(jnp.int32, prw.shape, 0)
    prev = jnp.where(row == 0, carry_ref[0:1, :], pltpu.roll(prw, 1, axis=0))
    carry_ref[0:1, :] = prw[tm - 1:tm, :]
    rw = prw + (prev - prw) * mu_ref[...]
    outs = _rwkv_prep(rw, w0_ref[...], a0_ref[...], kk_ref[...], ka_ref[...], rk_ref[...], wl_ref[...],
                      al_ref[...], gp_ref[...], bd_ref[...], aw)
    for o_ref, val in zip((r_o, lw_o, k2_o, vr_o, al_o, be_o, g_o, bo_o), outs):
        o_ref[0] = val.astype(o_ref.dtype)


def _inproj_step_kernel(x_ref, w_ref, mu_ref, w0_ref, a0_ref, kk_ref, ka_ref, rk_ref, wl_ref, al_ref, gp_ref,
                        bd_ref, q_ref, k_ref, v_ref, r_o, lw_o, k2_o, vr_o, al_o, be_o, g_o, bo_o, *, aw, nb):
    p = _dot(x_ref[...], w_ref[...])
    q_ref[...] = p[0:nb, 0:aw]
    k_ref[...] = p[0:nb, aw:2 * aw]
    v_ref[...] = p[0:nb, 2 * aw:3 * aw]
    prw = p[0:nb, 3 * aw:]
    prev = p[nb:2 * nb, 3 * aw:]
    rw = prw + (prev - prw) * mu_ref[...]
    outs = _rwkv_prep(rw, w0_ref[...], a0_ref[...], kk_ref[...], ka_ref[...], rk_ref[...], wl_ref[...],
                      al_ref[...], gp_ref[...], bd_ref[...], aw)
    for o_ref, val in zip((r_o, lw_o, k2_o, vr_o, al_o, be_o, g_o, bo_o), outs):
        o_ref[...] = val


_PREP_PARAMS = ("mu", "w0", "a0", "k_k", "k_a", "r_k", "wl_pad", "al_pad", "g_pad", "bd")


def _inproj_seq(x, prm, tm):
    B, T, D = x.shape
    aw = prm["aw"]
    nrw = prm["w_in"].shape[1] - 3 * aw
    small = [prm[n] for n in _PREP_PARAMS]
    row_spec = pl.BlockSpec((1, tm, aw), lambda b, t: (b, t, 0))
    col_spec = pl.BlockSpec((1, aw, tm), lambda b, t: (b, 0, t))
    row_sds = jax.ShapeDtypeStruct((B, T, aw), F32)
    col_sds = jax.ShapeDtypeStruct((B, aw, T), F32)
    return pl.pallas_call(
        functools.partial(_inproj_seq_kernel, aw=aw),
        grid=(B, T // tm),
        in_specs=[pl.BlockSpec((1, tm, D), lambda b, t: (b, t, 0)), _full(prm["w_in"].shape)]
        + [_full(s.shape) for s in small],
        out_specs=[row_spec] * 3 + [col_spec] * 2 + [row_spec] * 8,
        out_shape=[row_sds] * 3 + [col_sds] * 2
        + [jax.ShapeDtypeStruct((B, T, aw), F32 if n == "lw" else BF16)
           for n in ("r", "lw", "k2", "v", "al", "be", "g", "bonus")],
        scratch_shapes=[pltpu.VMEM((8, nrw), F32)],
        compiler_params=_cparams(2),
        name="inproj_seq",
    )(x, prm["w_in"], *small)


def _inproj_step(x2, prm):
    nb = x2.shape[0] // 2
    aw = prm["aw"]
    small = [prm[n] for n in _PREP_PARAMS]
    out_sds = jax.ShapeDtypeStruct((nb, aw), F32)
    return pl.pallas_call(
        functools.partial(_inproj_step_kernel, aw=aw, nb=nb),
        grid=(1,),
        in_specs=[_full(x2.shape), _full(prm["w_in"].shape)] + [_full(s.shape) for s in small],
        out_specs=[_full((nb, aw))] * 11,
        out_shape=[out_sds] * 11,
        compiler_params=_cparams(1),
        name="inproj_step",
    )(x2, prm["w_in"], *small)


def _prompt_attn_kernel(q_ref, k_ref, v_ref, sl_ref, o_ref, acc_s, m_s, l_s, *, T):
    hp = pl.program_id(1)
    scale = HEAD_DIM ** -0.5
    lane = lax.broadcasted_iota(jnp.int32, (1, LANES), 1)
    head_masks = ((lane < HEAD_DIM).astype(F32), (lane >= HEAD_DIM).astype(F32))
    qi = lax.broadcasted_iota(jnp.int32, (Q_TILE, Q_TILE), 0)
    ki = lax.broadcasted_iota(jnp.int32, (Q_TILE, Q_TILE), 1)
    steps_cur = (qi - ki).astype(F32)
    steps_prev = (qi - ki + Q_TILE).astype(F32)
    ok_cur = ki <= qi
    ok_prev = ki >= qi
    first_lanes = lax.broadcasted_iota(jnp.int32, (Q_TILE, LANES), 1) < HEAD_DIM
    slopes = [sl_ref[pl.ds(2 * hp + j, 1), :] for j in range(2)]

    def tile_group(tiles, bias_c, bias_p, first_branch):
        cases = [(t, j) for t in range(len(tiles)) for j in range(2)]
        q = [q_ref[0, rows, :] * scale for rows, _ in tiles]
        kc = [k_ref[0, rows, :].astype(BF16) for rows, _ in tiles]
        vc = [v_ref[0, rows, :].astype(BF16) for rows, _ in tiles]
        kp = [None if prev is None else k_ref[0, prev, :].astype(BF16) for _, prev in tiles]
        vp = [None if prev is None else v_ref[0, prev, :].astype(BF16) for _, prev in tiles]
        qm = {(t, j): (q[t] * head_masks[j]).astype(BF16) for t, j in cases}
        s_c = {(t, j): _dot(qm[t, j], kc[t], "NT") + bias_c[j] for t, j in cases}
        s_p = {(t, j): _dot(qm[t, j], kp[t], "NT") + bias_p[j] for t, j in cases if kp[t] is not None}
        m = {c: jnp.max(jnp.maximum(s_c[c], s_p[c]) if c in s_p else s_c[c], axis=-1, keepdims=True)
             for c in cases}
        p_c = {c: jnp.exp(s_c[c] - m[c]) for c in cases}
        p_p = {c: jnp.exp(s_p[c] - m[c]) for c in s_p}
        l = {c: jnp.sum(p_c[c] + p_p[c] if c in p_p else p_c[c], axis=-1, keepdims=True) for c in cases}
        acc = {(t, j): _dot(p_c[t, j], vc[t]) for t, j in cases}
        for t, j in s_p:
            acc[t, j] = acc[t, j] + _dot(p_p[t, j], vp[t])
        for t, (rows, _) in enumerate(tiles):
            acc_n = jnp.where(first_lanes, acc[t, 0], acc[t, 1])
            m_n = jnp.where(first_lanes, m[t, 0], m[t, 1])
            l_n = jnp.where(first_lanes, l[t, 0], l[t, 1])
            if first_branch:
                acc_s[rows, :] = acc_n
                m_s[rows, :] = m_n
                l_s[rows, :] = l_n
            else:
                m_o = m_s[rows, :]
                m_new = jnp.maximum(m_o, m_n)
                e_o = jnp.exp(m_o - m_new)
                e_n = jnp.exp(m_n - m_new)
                acc_s[rows, :] = acc_s[rows, :] * e_o + acc_n * e_n
                l_s[rows, :] = l_s[rows, :] * e_o + l_n * e_n
                m_s[rows, :] = m_new

    for bi, (win, dil) in enumerate(DIL_PATTERNS):
        assert win // dil == WIN_STEPS
        nb = T // dil // Q_TILE
        bias_c = [jnp.where(ok_cur, -(slopes[j] * float(dil)) * steps_cur, NEG) for j in range(2)]
        bias_p = [jnp.where(ok_prev, -(slopes[j] * float(dil)) * steps_prev, NEG) for j in range(2)]

        def rows_of(s, jb, dil=dil):
            if isinstance(jb, int) and jb < 0:
                return None
            if dil == 1:
                return pl.ds(pl.multiple_of(jb * Q_TILE, Q_TILE), Q_TILE)
            return pl.ds(s + dil * Q_TILE * jb, Q_TILE, stride=dil)

        if nb >= GROUP:
            assert nb % GROUP == 0

            def stream(s, carry, nb=nb, bi=bi, rows_of=rows_of, bias_c=bias_c, bias_p=bias_p):
                tile_group([(rows_of(s, jb), rows_of(s, jb - 1)) for jb in range(GROUP)], bias_c, bias_p, bi == 0)

                def later(g, c2):
                    tile_group([(rows_of(s, g * GROUP + i), rows_of(s, g * GROUP + i - 1)) for i in range(GROUP)],
                               bias_c, bias_p, bi == 0)
                    return c2

                if nb > GROUP:
                    lax.fori_loop(1, nb // GROUP, later, 0)
                return carry

            if dil == 1:
                stream(0, 0)
            else:
                lax.fori_loop(0, dil, stream, 0)
        else:
            assert dil % GROUP == 0

            def streams(g, carry, nb=nb, bi=bi, rows_of=rows_of, bias_c=bias_c, bias_p=bias_p):
                for jb in range(nb):
                    tile_group([(rows_of(g * GROUP + i, jb), rows_of(g * GROUP + i, jb - 1)) for i in range(GROUP)],
                               bias_c, bias_p, bi == 0)
                return carry

            lax.fori_loop(0, dil // GROUP, streams, 0)
    o_ref[0] = (acc_s[...] / l_s[...]).astype(o_ref.dtype)


def _prompt_attn(q, k, v, slopes_tab):
    B, T, W = q.shape
    spec = pl.BlockSpec((1, T, LANES), lambda b, hp: (b, 0, hp))
    return pl.pallas_call(
        functools.partial(_prompt_attn_kernel, T=T),
        grid=(B, W // LANES),
        in_specs=[spec, spec, spec, _full(slopes_tab.shape)],
        out_specs=spec,
        out_shape=jax.ShapeDtypeStruct((B, T, W), BF16),
        scratch_shapes=[pltpu.VMEM((T, LANES), F32)] * 3,
        compiler_params=_cparams(2),
        name="prompt_attn",
    )(q, k, v, slopes_tab)


def _rwkv_chunk_kernel(r_ref, lw_ref, k_ref, v_ref, al_ref, be_ref, g_ref, bo_ref, gnw_ref, gnb_ref,
                       y_ref, s_ref, st_scr, *, Tb, npair, nbb):
    C = CHUNK
    n2 = 2 * C
    t = pl.program_id(1)

    @pl.when(t == 0)
    def _():
        st_scr[...] = jnp.zeros_like(st_scr)

    lane = lax.broadcasted_iota(jnp.int32, (1, LANES), 1)
    m0 = (lane < HEAD_DIM).astype(F32)
    m1 = 1.0 - m0
    ri = lax.broadcasted_iota(jnp.int32, (n2, n2), 0)
    ci = lax.broadcasted_iota(jnp.int32, (n2, n2), 1)
    same = (ri < C) == (ci < C)
    strict = same & ((ri & (C - 1)) > (ci & (C - 1)))
    incl = same & ((ri & (C - 1)) >= (ci & (C - 1)))
    tri = (lax.broadcasted_iota(jnp.int32, (C, C), 0) >= lax.broadcasted_iota(jnp.int32, (C, C), 1)).astype(BF16)
    li = lax.broadcasted_iota(jnp.int32, (LANES, LANES), 0)
    lj = lax.broadcasted_iota(jnp.int32, (LANES, LANES), 1)
    head_mean = jnp.where((li < HEAD_DIM) == (lj < HEAD_DIM), 1.0 / HEAD_DIM, 0.0).astype(BF16)

    def stack(x):
        return jnp.concatenate([x * m0, x * m1], axis=0)

    def twice(x):
        return jnp.concatenate([x, x], axis=0)

    def body(c, carry):
        rows = pl.ds(pl.multiple_of(c * C, C), C)
        chains = [(bb, slice(hp * LANES, (hp + 1) * LANES)) for bb in range(nbb) for hp in range(npair)]
        ids = range(len(chains))
        S = [st_scr[i] for i in ids]
        lw_all = jnp.concatenate([lw_ref[bb, rows, :] for bb in range(nbb)], axis=1)
        cl_all = _dot_const_l(tri, lw_all)
        e_pos_all = jnp.exp(cl_all)
        e_neg_all = jnp.exp(-cl_all)
        e_exc_all = jnp.exp(cl_all - lw_all)
        wide = [slice(i * LANES, (i + 1) * LANES) for i in ids]
        g_end = [e_pos_all[C - 1:C, wide[i]] for i in ids]
        rt = [r_ref[bb, rows, cs] * e_pos_all[:, wide[i]] for i, (bb, cs) in enumerate(chains)]
        kt = [k_ref[bb, rows, cs] * e_neg_all[:, wide[i]] for i, (bb, cs) in enumerate(chains)]
        bt = [be_ref[bb, rows, cs] * e_neg_all[:, wide[i]] for i, (bb, cs) in enumerate(chains)]
        at = [al_ref[bb, rows, cs] * e_exc_all[:, wide[i]] for i, (bb, cs) in enumerate(chains)]
        ar_st = [jnp.concatenate([stack(at[i]), stack(rt[i])], axis=0) for i in ids]
        kb2 = [jnp.concatenate([twice(kt[i]), twice(bt[i])], axis=0) for i in ids]
        v_st = [stack(v_ref[bb, rows, cs]) for bb, cs in chains]
        gram = [_dot(ar_st[i], kb2[i], "NT") for i in ids]
        l_ab = [jnp.where(strict, gram[i][0:n2, n2:], 0.0) for i in ids]
        l_kk = [jnp.concatenate([jnp.where(strict, gram[i][0:n2, 0:n2], 0.0),
                                 jnp.where(incl, gram[i][n2:, 0:n2], 0.0)], axis=0) for i in ids]
        l_rb = [jnp.where(incl, gram[i][n2:, n2:], 0.0) for i in ids]
        from_state = [_dot(ar_st[i], S[i], "NT") for i in ids]
        from_v = [_dot(l_kk[i], v_st[i]) for i in ids]
        x = l_ab
        u = [from_state[i][0:n2] + from_v[i][0:n2] for i in ids]
        for _ in range(C.bit_length() - 2):
            xu = [_dot(x[i], jnp.concatenate([x[i], u[i]], axis=1)) for i in ids]
            x = [xu[i][:, 0:n2] for i in ids]
            u = [u[i] + xu[i][:, n2:] for i in ids]
        u = [u[i] + _dot(x[i], u[i]) for i in ids]
        y_st = [from_state[i][n2:] + from_v[i][n2:] + _dot(l_rb[i], u[i]) for i in ids]
        for i in ids:
            vu = jnp.concatenate([v_st[i], u[i]], axis=0)
            kb_end = jnp.concatenate([stack(kt[i] * g_end[i]), stack(bt[i] * g_end[i])], axis=0)
            st_scr[i] = S[i] * g_end[i] + _dot(vu, kb_end, "TN")
        y = jnp.concatenate([y_st[i][0:C, :] + y_st[i][C:n2, :] for i in ids], axis=0)
        yc = y - _dot_const(y, head_mean)
        yn = yc * lax.rsqrt(_dot_const(yc * yc, head_mean) + GN_EPS)
        for i, (bb, cs) in enumerate(chains):
            yn_i = yn[i * C:(i + 1) * C, :] * gnw_ref[:, cs] + gnb_ref[:, cs]
            y_ref[bb, rows, cs] = ((yn_i + bo_ref[bb, rows, cs]) * g_ref[bb, rows, cs]).astype(y_ref.dtype)
        return carry

    lax.fori_loop(0, Tb // C, body, 0)

    @pl.when(t == pl.num_programs(1) - 1)
    def _():
        s_ref[...] = st_scr[...].reshape(s_ref.shape)


def _rwkv_chunked(r, lw, k2, v, al, be, g, bonus, gn_w, gn_b, tb):
    B, T, W = r.shape
    npair = W // LANES
    nbb = 2 if B % 2 == 0 else 1
    spec = pl.BlockSpec((nbb, tb, W), lambda b, t: (b, t, 0))
    st_spec = pl.BlockSpec((nbb, npair, LANES, LANES), lambda b, t: (b, 0, 0, 0))
    y, s = pl.pallas_call(
        functools.partial(_rwkv_chunk_kernel, Tb=tb, npair=npair, nbb=nbb),
        grid=(B // nbb, T // tb),
        in_specs=[spec] * 8 + [_full(gn_w.shape), _full(gn_b.shape)],
        out_specs=[spec, st_spec],
        out_shape=[jax.ShapeDtypeStruct((B, T, W), BF16), jax.ShapeDtypeStruct((B, npair, LANES, LANES), F32)],
        scratch_shapes=[pltpu.VMEM((nbb * npair, LANES, LANES), F32)],
        compiler_params=_cparams(2),
        name="rwkv_chunk",
    )(r, lw, k2, v, al, be, g, bonus, gn_w, gn_b)
    s = s.reshape(B, npair, 2, HEAD_DIM, 2, HEAD_DIM)
    s = jnp.stack([s[:, :, 0, :, 0, :], s[:, :, 1, :, 1, :]], axis=2)
    return y, s.reshape(B, 2 * npair, HEAD_DIM, HEAD_DIM)


def _rwkv_step_kernel(s_ref, r_ref, lw_ref, k_ref, al_ref, be_ref, v_ref, g_ref, bo_ref, gnw_ref, gnb_ref,
                      y_ref, so_ref):
    S = s_ref[0]
    sa = jnp.sum(S * al_ref[0], axis=-1, keepdims=True)
    s_new = S * jnp.exp(lw_ref[0]) + sa * be_ref[0] + v_ref[0] * k_ref[0]
    so_ref[0] = s_new
    y = jnp.sum(s_new * r_ref[0], axis=-1, keepdims=True)
    mu = jnp.mean(y, axis=1, keepdims=True)
    yc = y - mu
    var = jnp.mean(yc * yc, axis=1, keepdims=True)
    yn = yc * lax.rsqrt(var + GN_EPS) * gnw_ref[...] + gnb_ref[...]
    y_ref[0] = (yn + bo_ref[0]) * g_ref[0]


def _rwkv_step(state, r, lw, k2, v, al, be, g, bonus, gn_w, gn_b):
    nb, H = state.shape[0], state.shape[1]
    rowv = lambda a: a.reshape(nb, H, 1, HEAD_DIM)
    colv = lambda a: a.reshape(nb, H, HEAD_DIM, 1)
    s_spec = pl.BlockSpec((1, H, HEAD_DIM, HEAD_DIM), lambda b: (b, 0, 0, 0))
    r_spec = pl.BlockSpec((1, H, 1, HEAD_DIM), lambda b: (b, 0, 0, 0))
    c_spec = pl.BlockSpec((1, H, HEAD_DIM, 1), lambda b: (b, 0, 0, 0))
    p_spec = _full((H, HEAD_DIM, 1))
    y, s_new = pl.pallas_call(
        _rwkv_step_kernel,
        grid=(nb,),
        in_specs=[s_spec] + [r_spec] * 5 + [c_spec] * 3 + [p_spec] * 2,
        out_specs=[c_spec, s_spec],
        out_shape=[jax.ShapeDtypeStruct((nb, H, HEAD_DIM, 1), F32), jax.ShapeDtypeStruct(state.shape, F32)],
        compiler_params=_cparams(1),
        name="rwkv_step",
    )(state, rowv(r), rowv(lw), rowv(k2), rowv(al), rowv(be), colv(v), colv(g), colv(bonus),
      gn_w.reshape(H, HEAD_DIM, 1), gn_b.reshape(H, HEAD_DIM, 1))
    return y.reshape(nb, H * HEAD_DIM), s_new


def _attn_step_kernel(q_ref, kn_ref, vn_ref, kt_ref, vt_ref, sl_ref, o_ref, *, P):
    scale = HEAD_DIM ** -0.5
    q = q_ref[0]
    dist = P - lax.broadcasted_iota(jnp.int32, (1, 1, P), 2)
    count = jnp.zeros((1, 1, P), F32)
    for win, dil in DIL_PATTERNS:
        assert dil & (dil - 1) == 0
        count = count + ((dist <= win) & ((dist & (dil - 1)) == 0)).astype(F32)
    s = jnp.sum(kt_ref[0] * q, axis=1, keepdims=True) * scale - sl_ref[...] * dist.astype(F32)
    s = jnp.where(count > 0.0, s, NEG)
    s0 = jnp.sum(kn_ref[0] * q, axis=1, keepdims=True) * scale
    m = jnp.maximum(jnp.max(s, axis=-1, keepdims=True), s0)
    p = count * jnp.exp(s - m)
    e0 = len(DIL_PATTERNS) * jnp.exp(s0 - m)
    l = jnp.sum(p, axis=-1, keepdims=True) + e0
    acc = jnp.sum(vt_ref[0] * p, axis=-1, keepdims=True) + e0 * vn_ref[0]
    o_ref[0] = acc / l


def _attn_step(q, k_new, v_new, k_cache, v_cache, slopes):
    nb, P, H, _ = k_cache.shape
    col = lambda a: a.reshape(nb, H, HEAD_DIM, 1)
    c_spec = pl.BlockSpec((1, H, HEAD_DIM, 1), lambda b: (b, 0, 0, 0))
    t_spec = pl.BlockSpec((1, H, HEAD_DIM, P), lambda b: (b, 0, 0, 0))
    out = pl.pallas_call(
        functools.partial(_attn_step_kernel, P=P),
        grid=(nb,),
        in_specs=[c_spec, c_spec, c_spec, t_spec, t_spec, _full((H, 1, 1))],
        out_specs=c_spec,
        out_shape=jax.ShapeDtypeStruct((nb, H, HEAD_DIM, 1), F32),
        compiler_params=_cparams(1),
        name="attn_step",
    )(col(q), col(k_new), col(v_new), jnp.transpose(k_cache, (0, 2, 3, 1)), jnp.transpose(v_cache, (0, 2, 3, 1)),
      slopes.reshape(H, 1, 1))
    return out.reshape(nb, H * HEAD_DIM)


def _outproj_kernel(att_ref, y_ref, x_ref, w_ref, g_ref, b_ref, o_ref, *, aw, alpha):
    h = _dot(att_ref[...], w_ref[0:aw, :]) + _dot(y_ref[...], w_ref[aw:, :])
    o_ref[...] = _layer_norm(alpha * x_ref[...] + h, g_ref[...], b_ref[...])


def _outproj(att, y_rw, x, w_out, ln_g, ln_b, tm, alpha):
    rows, D = x.shape
    aw = y_rw.shape[1]
    half = pl.BlockSpec((tm, aw), lambda i: (i, 0))
    return pl.pallas_call(
        functools.partial(_outproj_kernel, aw=aw, alpha=alpha),
        grid=(rows // tm,),
        in_specs=[half, half, pl.BlockSpec((tm, D), lambda i: (i, 0)), _full(w_out.shape), _full(ln_g.shape),
                  _full(ln_b.shape)],
        out_specs=pl.BlockSpec((tm, D), lambda i: (i, 0)),
        out_shape=jax.ShapeDtypeStruct((rows, D), F32),
        compiler_params=_cparams(1),
        name="outproj_ln",
    )(att, y_rw, x, w_out, ln_g, ln_b)


def _ffn_tail(x1, gate_pre, prev1, prev2, up, cw_ref, cb_ref, w2_ref, g_ref, b_ref, alpha):
    c = cw_ref[0:1, :] * prev2 + cw_ref[1:2, :] * prev1 + cw_ref[2:3, :] * gate_pre + cb_ref[...]
    h = c * _sigmoid(c) * up
    return _layer_norm(alpha * x1 + _dot(h, w2_ref[...]), g_ref[...], b_ref[...])


def _ffn_seq_kernel(att_ref, y_ref, x_ref, wo_ref, g1_ref, b1_ref, w1_ref, cw_ref, cb_ref, w2_ref, g_ref, b_ref,
                    o_ref, tail_ref, carry_ref, *, aw, dff, alpha):
    @pl.when(pl.program_id(1) == 0)
    def _():
        carry_ref[...] = jnp.zeros_like(carry_ref)

    h = _dot(att_ref[0], wo_ref[0:aw, :]) + _dot(y_ref[0], wo_ref[aw:, :])
    x1 = _layer_norm(alpha * x_ref[0] + h, g1_ref[...], b1_ref[...])
    hu = _dot(x1, w1_ref[...])
    gate_pre = hu[:, 0:dff]
    up = hu[:, dff:]
    tm = gate_pre.shape[0]
    row = lax.broadcasted_iota(jnp.int32, gate_pre.shape, 0)
    c0 = carry_ref[0:1, :]
    c1 = carry_ref[1:2, :]
    prev1 = jnp.where(row == 0, c1, pltpu.roll(gate_pre, 1, axis=0))
    prev2 = jnp.where(row == 0, c0, jnp.where(row == 1, c1, pltpu.roll(gate_pre, 2, axis=0)))
    last2 = gate_pre[tm - 2:tm, :]
    carry_ref[0:2, :] = last2
    tail_ref[0] = last2
    o_ref[0] = _ffn_tail(x1, gate_pre, prev1, prev2, up, cw_ref, cb_ref, w2_ref, g_ref, b_ref, alpha)


def _ffn_step_kernel(x_ref, p2_ref, p1_ref, w1_ref, cw_ref, cb_ref, w2_ref, g_ref, b_ref, o_ref, gate_ref, *,
                     dff, alpha):
    x1 = x_ref[...]
    hu = _dot(x1, w1_ref[...])
    gate_pre = hu[:, 0:dff]
    gate_ref[...] = gate_pre
    o_ref[...] = _ffn_tail(x1, gate_pre, p1_ref[...], p2_ref[...], hu[:, dff:], cw_ref, cb_ref, w2_ref, g_ref,
                           b_ref, alpha)


def _resident(shape):
    nd = len(shape)
    return pl.BlockSpec(shape, lambda *_: (0,) * nd, pipeline_mode=pl.Buffered(1))


def _ffn_seq(att, y_rw, x, w_out, ln1_g, ln1_b, w1, conv_w, conv_b, w2, ln2_g, ln2_b, tm, alpha):
    B, T, D = x.shape
    aw = att.shape[-1]
    dff = w2.shape[0]
    half = pl.BlockSpec((1, tm, aw), lambda b, t: (b, t, 0))
    rows = pl.BlockSpec((1, tm, D), lambda b, t: (b, t, 0))
    return pl.pallas_call(
        functools.partial(_ffn_seq_kernel, aw=aw, dff=dff, alpha=alpha),
        grid=(B, T // tm),
        in_specs=[half, half, rows, _resident(w_out.shape), _full(ln1_g.shape), _full(ln1_b.shape),
                  _resident(w1.shape), _full(conv_w.shape), _full(conv_b.shape), _resident(w2.shape),
                  _full(ln2_g.shape), _full(ln2_b.shape)],
        out_specs=[rows, pl.BlockSpec((1, 2, dff), lambda b, t: (b, 0, 0))],
        out_shape=[jax.ShapeDtypeStruct((B, T, D), F32), jax.ShapeDtypeStruct((B, 2, dff), F32)],
        scratch_shapes=[pltpu.VMEM((8, dff), F32)],
        compiler_params=_cparams(2),
        name="ffn_seq",
    )(att, y_rw, x, w_out, ln1_g, ln1_b, w1, conv_w, conv_b, w2, ln2_g, ln2_b)


def _ffn_step(x1, prev2, prev1, w1, conv_w, conv_b, w2, ln_g, ln_b, alpha):
    nb, D = x1.shape
    dff = w2.shape[0]
    return pl.pallas_call(
        functools.partial(_ffn_step_kernel, dff=dff, alpha=alpha),
        grid=(1,),
        in_specs=[_full(x1.shape), _full(prev2.shape), _full(prev1.shape), _resident(w1.shape),
                  _full(conv_w.shape), _full(conv_b.shape), _resident(w2.shape), _full(ln_g.shape),
                  _full(ln_b.shape)],
        out_specs=[_full((nb, D)), _full((nb, dff))],
        out_shape=[jax.ShapeDtypeStruct((nb, D), F32), jax.ShapeDtypeStruct((nb, dff), F32)],
        compiler_params=_cparams(1),
        name="ffn_step",
    )(x1, prev2, prev1, w1, conv_w, conv_b, w2, ln_g, ln_b)


def _layer_params(w_in, mu_shift, w0, w_lora_up, a0, a_lora_up, g_lora_up, k_k, k_a, r_k, aw):
    D, ncol = w_in.shape
    n_w, n_a, n_g = w_lora_up.shape[0], a_lora_up.shape[0], g_lora_up.shape[0]
    assert n_w + n_a == LANES and 3 * aw + 3 * aw + n_w + n_a + n_g == ncol
    g_cols = -(-n_g // LANES) * LANES
    pad = g_cols - n_g
    row = lambda a: a.reshape(1, -1).astype(F32)
    lane = jnp.arange(aw)
    bd = (lane[:, None] // HEAD_DIM == lane[None, :] // HEAD_DIM).astype(BF16)
    return dict(
        aw=aw,
        w_in=jnp.pad(w_in, ((0, 0), (0, pad))).astype(BF16),
        mu=jnp.pad(row(mu_shift), ((0, 0), (0, pad))),
        w0=row(w0), a0=row(a0), k_k=row(k_k), k_a=row(k_a), r_k=row(r_k),
        wl_pad=jnp.pad(w_lora_up, ((0, n_a), (0, 0))).astype(BF16),
        al_pad=jnp.pad(a_lora_up, ((n_w, 0), (0, 0))).astype(BF16),
        g_pad=jnp.pad(g_lora_up, ((0, pad), (0, 0))).astype(BF16),
        bd=bd,
    )


def kernel(x_prompt, x_sample, cache_k_win, cache_v_win, state_shift, state_wkv, state_conv, w_in, mu_shift, w0, w_lora_up, a0, a_lora_up, g_lora_up, k_k, k_a, r_k, gn_w, gn_b, w_out, ln1_g, ln1_b, w_ffn_in, conv_w, conv_b, w_ffn_out, ln2_g, ln2_b):
    depth = w_in.shape[0]
    alpha = (2.0 * depth) ** 0.25
    B, T, D = x_prompt.shape
    nb = x_sample.shape[0]
    assert x_sample.shape[1] == 1
    n_att = cache_k_win.shape[3]
    n_rw = state_wkv.shape[2]
    aw = n_att * HEAD_DIM
    assert n_rw * HEAD_DIM == aw and T % (Q_TILE * DIL_PATTERNS[-1][1]) == 0
    tm = min(256, T)
    tb = min(256, T)
    heads = jnp.arange(1, n_att + 1, dtype=F32)
    slopes = jnp.exp2(-8.0 * heads / n_att)
    slopes_tab = jnp.broadcast_to(slopes[:, None], (n_att, LANES))
    row = lambda a: a.reshape(1, -1)

    hp, hs = x_prompt, x_sample.reshape(nb, D)
    outs = [[] for _ in range(10)]
    for l in range(depth):
        prm = _layer_params(w_in[l], mu_shift[l], w0[l], w_lora_up[l], a0[l], a_lora_up[l], g_lora_up[l],
                            k_k[l], k_a[l], r_k[l], aw)
        w_out_b = w_out[l].astype(BF16)
        w1_b = w_ffn_in[l].astype(BF16)
        w2_b = w_ffn_out[l].astype(BF16)
        gnw, gnb = row(gn_w[l]), row(gn_b[l])

        q, k, v, kt, vt, r, lw, k2, vr, al, be, g, bonus = _inproj_seq(hp, prm, tm)
        att = _prompt_attn(q, k, v, slopes_tab)
        y_rw, wkv_p = _rwkv_chunked(r, lw, k2, vr, al, be, g, bonus, gnw, gnb, tb)
        shift_p = hp[:, -1, :]
        hp, conv_p = _ffn_seq(att, y_rw, hp, w_out_b, row(ln1_g[l]), row(ln1_b[l]), w1_b, conv_w[l],
                              row(conv_b[l]), w2_b, row(ln2_g[l]), row(ln2_b[l]), tm, alpha)

        x2 = jnp.concatenate([hs, state_shift[l]], axis=0)
        qs, ks, vs, r, lw, k2, vr, al, be, g, bonus = _inproj_step(x2, prm)
        att_s = _attn_step(qs, ks, vs, cache_k_win[l], cache_v_win[l], slopes)
        y_rw_s, wkv_s = _rwkv_step(state_wkv[l], r, lw, k2, vr, al, be, g, bonus, gn_w[l], gn_b[l])
        x1s = _outproj(att_s, y_rw_s, hs, w_out_b, row(ln1_g[l]), row(ln1_b[l]), nb, alpha)
        shift_s = hs
        hs, gate_s = _ffn_step(x1s, state_conv[l][:, 0, :], state_conv[l][:, 1, :], w1_b, conv_w[l],
                               row(conv_b[l]), w2_b, row(ln2_g[l]), row(ln2_b[l]), alpha)
        conv_s = jnp.stack([state_conv[l][:, 1, :], gate_s], axis=1)

        n_keep = min(DIL_PATTERNS[-1][0], T)
        win = lambda a: jnp.transpose(a.reshape(B, n_att, HEAD_DIM, T), (0, 3, 1, 2))[:, T - n_keep:]
        vals = (win(kt), win(vt), ks.reshape(nb, 1, n_att, HEAD_DIM), vs.reshape(nb, 1, n_att, HEAD_DIM),
                shift_p, shift_s, wkv_p, wkv_s, conv_p, conv_s)
        for lst, val in zip(outs, vals):
            lst.append(val)
    return (hp, hs.reshape(nb, 1, D)) + tuple(jnp.stack(lst) for lst in outs)
```

```python
import functools

import jax
import jax.numpy as jnp
from jax import lax
from jax.experimental import pallas as pl
from jax.experimental.pallas import tpu as pltpu

F32 = jnp.float32
BF16 = jnp.bfloat16

HEAD_DIM = 64
LANES = 128
DIL_PATTERNS = ((128, 1), (512, 4), (2048, 16))
WIN_STEPS = 128
Q_TILE = 128
GROUP = 4
CHUNK = 64
LN_EPS = 1e-5
GN_EPS = 64e-5
NEG = -1e30
VMEM_LIMIT = 56 * 1024 * 1024

_DN = {"NN": (((1,), (0,)), ((), ())), "NT": (((1,), (1,)), ((), ())), "TN": (((0,), (0,)), ((), ()))}


def _dot(a, b, dims="NN"):
    return lax.dot_general(a.astype(BF16), b.astype(BF16), _DN[dims], preferred_element_type=F32)


def _split2(x):
    hi = x.astype(BF16)
    lo = (x - hi.astype(F32)).astype(BF16)
    return hi, lo


def _dot_const(x, c):
    hi, lo = _split2(x)
    return _dot(hi, c) + _dot(lo, c)


def _dot_const_l(c, x):
    hi = x.astype(BF16)
    r1 = x - hi.astype(F32)
    mid = r1.astype(BF16)
    lo = (r1 - mid.astype(F32)).astype(BF16)
    return _dot(c, hi) + _dot(c, mid) + _dot(c, lo)


def _sigmoid(x):
    return 1.0 / (1.0 + jnp.exp(-x))


def _layer_norm(x, g, b):
    mu = jnp.mean(x, axis=-1, keepdims=True)
    xc = x - mu
    var = jnp.mean(xc * xc, axis=-1, keepdims=True)
    return xc * lax.rsqrt(var + LN_EPS) * g + b


def _cparams(n_grid):
    return pltpu.CompilerParams(dimension_semantics=("arbitrary",) * n_grid, vmem_limit_bytes=VMEM_LIMIT)


def _full(shape):
    nd = len(shape)
    return pl.BlockSpec(shape, lambda *_: (0,) * nd)


def _rwkv_prep(rw, w0, a0, k_k, k_a, r_k, wl_pad, al_pad, g_pad, bd, aw):
    r = rw[:, 0:aw]
    k = rw[:, aw:2 * aw]
    v = rw[:, 2 * aw:3 * aw]
    lo = rw[:, 3 * aw:3 * aw + 128]
    glo = rw[:, 3 * aw + 128:]
    z = -(w0 + _dot(jnp.tanh(lo), wl_pad))
    softplus = jnp.maximum(z, 0.0) + jnp.log(1.0 + jnp.exp(-jnp.abs(z)))
    lw = -jnp.exp(-softplus - 0.5)
    a = _sigmoid(a0 + _dot(lo, al_pad))
    g = _dot(_sigmoid(glo), g_pad)
    kk = k * k_k
    norm = jnp.sqrt(_dot(kk * kk, bd))
    kkn = kk / jnp.maximum(norm, 1e-12)
    k2 = k * (1.0 + (a - 1.0) * k_a)
    bonus = _dot(r * k2 * r_k, bd) * v
    return r, lw, k2, v, -kkn, kkn * a, g, bonus


def _inproj_seq_kernel(x_ref, w_ref, mu_ref, w0_ref, a0_ref, kk_ref, ka_ref, rk_ref, wl_ref, al_ref, gp_ref,
                       bd_ref, q_ref, k_ref, v_ref, kt_ref, vt_ref, r_o, lw_o, k2_o, vr_o, al_o, be_o, g_o, bo_o,
                       carry_ref, *, aw):
    @pl.when(pl.program_id(1) == 0)
    def _():
        carry_ref[...] = jnp.zeros_like(carry_ref)

    p = _dot(x_ref[0], w_ref[...])
    q_ref[0] = p[:, 0:aw]
    k = p[:, aw:2 * aw]
    v = p[:, 2 * aw:3 * aw]
    k_ref[0] = k
    v_ref[0] = v
    kt_ref[0] = k.T
    vt_ref[0] = v.T
    prw = p[:, 3 * aw:]
    tm = prw.shape[0]
    row = lax.broadcasted_iota(jnp.int32, prw.shape, 0)
    prev = jnp.where(row == 0, carry_ref[0:1, :], pltpu.roll(prw, 1, axis=0))
    carry_ref[0:1, :] = prw[tm - 1:tm, :]
    rw = prw + (prev - prw) * mu_ref[...]
    outs = _rwkv_prep(rw, w0_ref[...], a0_ref[...], kk_ref[...], ka_ref[...], rk_ref[...], wl_ref[...],
                      al_ref[...], gp_ref[...], bd_ref[...], aw)
    for o_ref, val in zip((r_o, lw_o, k2_o, vr_o, al_o, be_o, g_o, bo_o), outs):
        o_ref[0] = val.astype(o_ref.dtype)


def _inproj_step_kernel(x_ref, w_ref, mu_ref, w0_ref, a0_ref, kk_ref, ka_ref, rk_ref, wl_ref, al_ref, gp_ref,
                        bd_ref, q_ref, k_ref, v_ref, r_o, lw_o, k2_o, vr_o, al_o, be_o, g_o, bo_o, *, aw, nb):
    p = _dot(x_ref[...], w_ref[...])
    q_ref[...] = p[0:nb, 0:aw]
    k_ref[...] = p[0:nb, aw:2 * aw]
    v_ref[...] = p[0:nb, 2 * aw:3 * aw]
    prw = p[0:nb, 3 * aw:]
    prev = p[nb:2 * nb, 3 * aw:]
    rw = prw + (prev - prw) * mu_ref[...]
    outs = _rwkv_prep(rw, w0_ref[...], a0_ref[...], kk_ref[...], ka_ref[...], rk_ref[...], wl_ref[...],
                      al_ref[...], gp_ref[...], bd_ref[...], aw)
    for o_ref, val in zip((r_o, lw_o, k2_o, vr_o, al_o, be_o, g_o, bo_o), outs):
        o_ref[...] = val


_PREP_PARAMS = ("mu", "w0", "a0", "k_k", "k_a", "r_k", "wl_pad", "al_pad", "g_pad", "bd")


def _inproj_seq(x, prm, tm):
    B, T, D = x.shape
    aw = prm["aw"]
    nrw = prm["w_in"].shape[1] - 3 * aw
    small = [prm[n] for n in _PREP_PARAMS]
    row_spec = pl.BlockSpec((1, tm, aw), lambda b, t: (b, t, 0))
    col_spec = pl.BlockSpec((1, aw, tm), lambda b, t: (b, 0, t))
    row_sds = jax.ShapeDtypeStruct((B, T, aw), F32)
    col_sds = jax.ShapeDtypeStruct((B, aw, T), F32)
    return pl.pallas_call(
        functools.partial(_inproj_seq_kernel, aw=aw),
        grid=(B, T // tm),
        in_specs=[pl.BlockSpec((1, tm, D), lambda b, t: (b, t, 0)), _full(prm["w_in"].shape)]
        + [_full(s.shape) for s in small],
        out_specs=[row_spec] * 3 + [col_spec] * 2 + [row_spec] * 8,
        out_shape=[row_sds] * 3 + [col_sds] * 2
        + [jax.ShapeDtypeStruct((B, T, aw), F32 if n == "lw" else BF16)
           for n in ("r", "lw", "k2", "v", "al", "be", "g", "bonus")],
        scratch_shapes=[pltpu.VMEM((8, nrw), F32)],
        compiler_params=_cparams(2),
        name="inproj_seq",
    )(x, prm["w_in"], *small)


def _inproj_step(x2, prm):
    nb = x2.shape[0] // 2
    aw = prm["aw"]
    small = [prm[n] for n in _PREP_PARAMS]
    out_sds = jax.ShapeDtypeStruct((nb, aw), F32)
    return pl.pallas_call(
        functools.partial(_inproj_step_kernel, aw=aw, nb=nb),
        grid=(1,),
        in_specs=[_full(x2.shape), _full(prm["w_in"].shape)] + [_full(s.shape) for s in small],
        out_specs=[_full((nb, aw))] * 11,
        out_shape=[out_sds] * 11,
        compiler_params=_cparams(1),
        name="inproj_step",
    )(x2, prm["w_in"], *small)


def _prompt_attn_kernel(q_ref, k_ref, v_ref, sl_ref, o_ref, acc_s, m_s, l_s, yq, yk, yv, yacc, ym, yl, *, T):
    hp = pl.program_id(1)
    scale = HEAD_DIM ** -0.5
    lane = lax.broadcasted_iota(jnp.int32, (1, LANES), 1)
    head_masks = ((lane < HEAD_DIM).astype(F32), (lane >= HEAD_DIM).astype(F32))
    qi = lax.broadcasted_iota(jnp.int32, (Q_TILE, Q_TILE), 0)
    ki = lax.broadcasted_iota(jnp.int32, (Q_TILE, Q_TILE), 1)
    steps_cur = (qi - ki).astype(F32)
    steps_prev = (qi - ki + Q_TILE).astype(F32)
    ok_cur = ki <= qi
    ok_prev = ki >= qi
    first_lanes = lax.broadcasted_iota(jnp.int32, (Q_TILE, LANES), 1) < HEAD_DIM
    slopes = [sl_ref[pl.ds(2 * hp + j, 1), :] for j in range(2)]

    def tile_group(src, tiles, bias, stats, first_branch):
        rq, rk, rv = src
        bias_c, bias_pc = bias
        acc_s, m_s, l_s = stats
        ids = range(len(tiles))
        q = [rq(rows) * scale for rows, _ in tiles]
        kk = [rk(rows) if prev is None else jnp.concatenate([rk(prev), rk(rows)], axis=0) for rows, prev in tiles]
        vv = [rv(rows) if prev is None else jnp.concatenate([rv(prev), rv(rows)], axis=0) for rows, prev in tiles]
        qq = [jnp.concatenate([q[t] * head_masks[0], q[t] * head_masks[1]], axis=0) for t in ids]
        s = [_dot(qq[t], kk[t], "NT") + (bias_c if tiles[t][1] is None else bias_pc) for t in ids]
        m = [jnp.max(s[t], axis=-1, keepdims=True) for t in ids]
        p = [jnp.exp(s[t] - m[t]) for t in ids]
        l = [jnp.sum(p[t], axis=-1, keepdims=True) for t in ids]
        acc = [_dot(p[t], vv[t]) for t in ids]
        for t, (rows, _) in enumerate(tiles):
            acc_n = jnp.where(first_lanes, acc[t][0:Q_TILE], acc[t][Q_TILE:])
            m_n = jnp.where(first_lanes, m[t][0:Q_TILE], m[t][Q_TILE:])
            l_n = jnp.where(first_lanes, l[t][0:Q_TILE], l[t][Q_TILE:])
            if first_branch:
                acc_s[rows, :] = acc_n
                m_s[rows, :] = m_n
                l_s[rows, :] = l_n
            else:
                m_o = m_s[rows, :]
                m_new = jnp.maximum(m_o, m_n)
                e_o = jnp.exp(m_o - m_new)
                e_n = jnp.exp(m_n - m_new)
                acc_s[rows, :] = acc_s[rows, :] * e_o + acc_n * e_n
                l_s[rows, :] = l_s[rows, :] * e_o + l_n * e_n
                m_s[rows, :] = m_new

    dils = [dil for _, dil in DIL_PATTERNS]
    R = dils[1]
    assert dils == [1, R, R * R] and all(win // dil == WIN_STEPS for win, dil in DIL_PATTERNS)

    def bias_of(dil):
        b_cur = [jnp.where(ok_cur, -(slopes[j] * float(dil)) * steps_cur, NEG) for j in range(2)]
        b_prev = [jnp.where(ok_prev, -(slopes[j] * float(dil)) * steps_prev, NEG) for j in range(2)]
        return (jnp.concatenate(b_cur, axis=0),
                jnp.concatenate([jnp.concatenate([b_prev[j], b_cur[j]], axis=1) for j in range(2)], axis=0))

    def groups(tiles):
        return [tiles[i:i + GROUP] for i in range(0, len(tiles), GROUP)]

    natural = tuple((lambda rows, ref=ref: ref[0, rows, :]) for ref in (q_ref, k_ref, v_ref))
    scratch = tuple((lambda rows, ref=ref: ref[rows, :]) for ref in (yq, yk, yv))
    nat_stats = (acc_s, m_s, l_s)
    y_stats = (yacc, ym, yl)

    nb1 = T // Q_TILE
    assert nb1 % GROUP == 0
    bias1 = bias_of(dils[0])
    tile_group(natural, [(pl.ds(jb * Q_TILE, Q_TILE), pl.ds((jb - 1) * Q_TILE, Q_TILE) if jb else None)
                         for jb in range(GROUP)], bias1, nat_stats, True)

    def later(g, carry):
        start = pl.multiple_of(g * (GROUP * Q_TILE), GROUP * Q_TILE)
        tile_group(natural, [(pl.ds(start + i * Q_TILE, Q_TILE), pl.ds(start + (i - 1) * Q_TILE, Q_TILE))
                             for i in range(GROUP)], bias1, nat_stats, True)
        return carry

    lax.fori_loop(1, nb1 // GROUP, later, 0)

    LR = T // R
    bias_r = bias_of(dils[1])
    bias_rr = bias_of(dils[2])

    def stream(s, carry):
        own = pl.ds(s, LR, stride=R)
        yq[...] = q_ref[0, own, :]
        yk[...] = k_ref[0, own, :]
        yv[...] = v_ref[0, own, :]
        for grp in groups([(pl.ds(jb * Q_TILE, Q_TILE), pl.ds((jb - 1) * Q_TILE, Q_TILE) if jb else None)
                           for jb in range(LR // Q_TILE)]):
            tile_group(scratch, grp, bias_r, y_stats, True)
        for grp in groups([(pl.ds(c + R * Q_TILE * jb, Q_TILE, stride=R),
                            pl.ds(c + R * Q_TILE * (jb - 1), Q_TILE, stride=R) if jb else None)
                           for c in range(R) for jb in range(LR // R // Q_TILE)]):
            tile_group(scratch, grp, bias_rr, y_stats, False)
        m_a, m_b = m_s[own, :], ym[...]
        m_new = jnp.maximum(m_a, m_b)
        e_a = jnp.exp(m_a - m_new)
        e_b = jnp.exp(m_b - m_new)
        acc_s[own, :] = (acc_s[own, :] * e_a + yacc[...] * e_b) / (l_s[own, :] * e_a + yl[...] * e_b)
        return carry

    lax.fori_loop(0, R, stream, 0)
    o_ref[0] = acc_s[...].astype(o_ref.dtype)


def _prompt_attn(q, k, v, slopes_tab):
    B, T, W = q.shape
    spec = pl.BlockSpec((1, T, LANES), lambda b, hp: (b, 0, hp))
    return pl.pallas_call(
        functools.partial(_prompt_attn_kernel, T=T),
        grid=(B, W // LANES),
        in_specs=[spec, spec, spec, _full(slopes_tab.shape)],
        out_specs=spec,
        out_shape=jax.ShapeDtypeStruct((B, T, W), BF16),
        scratch_shapes=[pltpu.VMEM((T, LANES), F32)] * 3 + [pltpu.VMEM((T // DIL_PATTERNS[1][1], LANES), F32)] * 6,
        compiler_params=_cparams(2),
        name="prompt_attn",
    )(q, k, v, slopes_tab)


def _rwkv_chunk_kernel(r_ref, lw_ref, k_ref, v_ref, al_ref, be_ref, g_ref, bo_ref, gnw_ref, gnb_ref,
                       y_ref, s_ref, st_scr, *, Tb, npair, nbb):
    C = CHUNK
    n2 = 2 * C
    t = pl.program_id(1)

    @pl.when(t == 0)
    def _():
        st_scr[...] = jnp.zeros_like(st_scr)

    lane = lax.broadcasted_iota(jnp.int32, (1, LANES), 1)
    m0 = (lane < HEAD_DIM).astype(F32)
    m1 = 1.0 - m0
    ri = lax.broadcasted_iota(jnp.int32, (n2, n2), 0)
    ci = lax.broadcasted_iota(jnp.int32, (n2, n2), 1)
    same = (ri < C) == (ci < C)
    strict = same & ((ri & (C - 1)) > (ci & (C - 1)))
    incl = same & ((ri & (C - 1)) >= (ci & (C - 1)))
    tri = (lax.broadcasted_iota(jnp.int32, (C, C), 0) >= lax.broadcasted_iota(jnp.int32, (C, C), 1)).astype(BF16)
    li = lax.broadcasted_iota(jnp.int32, (LANES, LANES), 0)
    lj = lax.broadcasted_iota(jnp.int32, (LANES, LANES), 1)
    head_mean = jnp.where((li < HEAD_DIM) == (lj < HEAD_DIM), 1.0 / HEAD_DIM, 0.0).astype(BF16)

    def stack(x):
        return jnp.concatenate([x * m0, x * m1], axis=0)

    def twice(x):
        return jnp.concatenate([x, x], axis=0)

    def body(c, carry):
        rows = pl.ds(pl.multiple_of(c * C, C), C)
        chains = [(bb, slice(hp * LANES, (hp + 1) * LANES)) for bb in range(nbb) for hp in range(npair)]
        ids = range(len(chains))
        S = [st_scr[i] for i in ids]
        lw_all = jnp.concatenate([lw_ref[bb, rows, :] for bb in range(nbb)], axis=1)
        cl_all = _dot_const_l(tri, lw_all)
        e_pos_all = jnp.exp(cl_all)
        e_neg_all = jnp.exp(-cl_all)
        e_exc_all = jnp.exp(cl_all - lw_all)
        wide = [slice(i * LANES, (i + 1) * LANES) for i in ids]
        g_end = [e_pos_all[C - 1:C, wide[i]] for i in ids]
        rt = [r_ref[bb, rows, cs] * e_pos_all[:, wide[i]] for i, (bb, cs) in enumerate(chains)]
        kt = [k_ref[bb, rows, cs] * e_neg_all[:, wide[i]] for i, (bb, cs) in enumerate(chains)]
        bt = [be_ref[bb, rows, cs] * e_neg_all[:, wide[i]] for i, (bb, cs) in enumerate(chains)]
        at = [al_ref[bb, rows, cs] * e_exc_all[:, wide[i]] for i, (bb, cs) in enumerate(chains)]
        ar_st = [jnp.concatenate([stack(at[i]), stack(rt[i])], axis=0) for i in ids]
        kb2 = [jnp.concatenate([twice(kt[i]), twice(bt[i])], axis=0) for i in ids]
        v_st = [stack(v_ref[bb, rows, cs]) for bb, cs in chains]
        gram = [_dot(ar_st[i], kb2[i], "NT") for i in ids]
        l_ab = [jnp.where(strict, gram[i][0:n2, n2:], 0.0) for i in ids]
        l_kk = [jnp.concatenate([jnp.where(strict, gram[i][0:n2, 0:n2], 0.0),
                                 jnp.where(incl, gram[i][n2:, 0:n2], 0.0)], axis=0) for i in ids]
        l_rb = [jnp.where(incl, gram[i][n2:, n2:], 0.0) for i in ids]
        from_state = [_dot(ar_st[i], S[i], "NT") for i in ids]
        from_v = [_dot(l_kk[i], v_st[i]) for i in ids]
        x = l_ab
        u = [from_state[i][0:n2] + from_v[i][0:n2] for i in ids]
        for _ in range(C.bit_length() - 2):
            xu = [_dot(x[i], jnp.concatenate([x[i], u[i]], axis=1)) for i in ids]
            x = [xu[i][:, 0:n2] for i in ids]
            u = [u[i] + xu[i][:, n2:] for i in ids]
        u = [u[i] + _dot(x[i], u[i]) for i in ids]
        y_st = [from_state[i][n2:] + from_v[i][n2:] + _dot(l_rb[i], u[i]) for i in ids]
        for i in ids:
            vu = jnp.concatenate([v_st[i], u[i]], axis=0)
            kb_end = jnp.concatenate([stack(kt[i] * g_end[i]), stack(bt[i] * g_end[i])], axis=0)
            st_scr[i] = S[i] * g_end[i] + _dot(vu, kb_end, "TN")
        y = jnp.concatenate([y_st[i][0:C, :] + y_st[i][C:n2, :] for i in ids], axis=0)
        yc = y - _dot_const(y, head_mean)
        yn = yc * lax.rsqrt(_dot_const(yc * yc, head_mean) + GN_EPS)
        for i, (bb, cs) in enumerate(chains):
            yn_i = yn[i * C:(i + 1) * C, :] * gnw_ref[:, cs] + gnb_ref[:, cs]
            y_ref[bb, rows, cs] = ((yn_i + bo_ref[bb, rows, cs]) * g_ref[bb, rows, cs]).astype(y_ref.dtype)
        return carry

    lax.fori_loop(0, Tb // C, body, 0)

    @pl.when(t == pl.num_programs(1) - 1)
    def _():
        s_ref[...] = st_scr[...].reshape(s_ref.shape)


def _rwkv_chunked(r, lw, k2, v, al, be, g, bonus, gn_w, gn_b, tb):
    B, T, W = r.shape
    npair = W // LANES
    nbb = 2 if B % 2 == 0 else 1
    spec = pl.BlockSpec((nbb, tb, W), lambda b, t: (b, t, 0))
    st_spec = pl.BlockSpec((nbb, npair, LANES, LANES), lambda b, t: (b, 0, 0, 0))
    y, s = pl.pallas_call(
        functools.partial(_rwkv_chunk_kernel, Tb=tb, npair=npair, nbb=nbb),
        grid=(B // nbb, T // tb),
        in_specs=[spec] * 8 + [_full(gn_w.shape), _full(gn_b.shape)],
        out_specs=[spec, st_spec],
        out_shape=[jax.ShapeDtypeStruct((B, T, W), BF16), jax.ShapeDtypeStruct((B, npair, LANES, LANES), F32)],
        scratch_shapes=[pltpu.VMEM((nbb * npair, LANES, LANES), F32)],
        compiler_params=_cparams(2),
        name="rwkv_chunk",
    )(r, lw, k2, v, al, be, g, bonus, gn_w, gn_b)
    s = s.reshape(B, npair, 2, HEAD_DIM, 2, HEAD_DIM)
    s = jnp.stack([s[:, :, 0, :, 0, :], s[:, :, 1, :, 1, :]], axis=2)
    return y, s.reshape(B, 2 * npair, HEAD_DIM, HEAD_DIM)


def _rwkv_step_kernel(s_ref, r_ref, lw_ref, k_ref, al_ref, be_ref, v_ref, g_ref, bo_ref, gnw_ref, gnb_ref,
                      y_ref, so_ref):
    S = s_ref[0]
    sa = jnp.sum(S * al_ref[0], axis=-1, keepdims=True)
    s_new = S * jnp.exp(lw_ref[0]) + sa * be_ref[0] + v_ref[0] * k_ref[0]
    so_ref[0] = s_new
    y = jnp.sum(s_new * r_ref[0], axis=-1, keepdims=True)
    mu = jnp.mean(y, axis=1, keepdims=True)
    yc = y - mu
    var = jnp.mean(yc * yc, axis=1, keepdims=True)
    yn = yc * lax.rsqrt(var + GN_EPS) * gnw_ref[...] + gnb_ref[...]
    y_ref[0] = (yn + bo_ref[0]) * g_ref[0]


def _rwkv_step(state, r, lw, k2, v, al, be, g, bonus, gn_w, gn_b):
    nb, H = state.shape[0], state.shape[1]
    rowv = lambda a: a.reshape(nb, H, 1, HEAD_DIM)
    colv = lambda a: a.reshape(nb, H, HEAD_DIM, 1)
    s_spec = pl.BlockSpec((1, H, HEAD_DIM, HEAD_DIM), lambda b: (b, 0, 0, 0))
    r_spec = pl.BlockSpec((1, H, 1, HEAD_DIM), lambda b: (b, 0, 0, 0))
    c_spec = pl.BlockSpec((1, H, HEAD_DIM, 1), lambda b: (b, 0, 0, 0))
    p_spec = _full((H, HEAD_DIM, 1))
    y, s_new = pl.pallas_call(
        _rwkv_step_kernel,
        grid=(nb,),
        in_specs=[s_spec] + [r_spec] * 5 + [c_spec] * 3 + [p_spec] * 2,
        out_specs=[c_spec, s_spec],
        out_shape=[jax.ShapeDtypeStruct((nb, H, HEAD_DIM, 1), F32), jax.ShapeDtypeStruct(state.shape, F32)],
        compiler_params=_cparams(1),
        name="rwkv_step",
    )(state, rowv(r), rowv(lw), rowv(k2), rowv(al), rowv(be), colv(v), colv(g), colv(bonus),
      gn_w.reshape(H, HEAD_DIM, 1), gn_b.reshape(H, HEAD_DIM, 1))
    return y.reshape(nb, H * HEAD_DIM), s_new


def _attn_step_kernel(q_ref, kn_ref, vn_ref, kt_ref, vt_ref, sl_ref, o_ref, *, P):
    scale = HEAD_DIM ** -0.5
    q = q_ref[0]
    dist = P - lax.broadcasted_iota(jnp.int32, (1, 1, P), 2)
    count = jnp.zeros((1, 1, P), F32)
    for win, dil in DIL_PATTERNS:
        assert dil & (dil - 1) == 0
        count = count + ((dist <= win) & ((dist & (dil - 1)) == 0)).astype(F32)
    s = jnp.sum(kt_ref[0] * q, axis=1, keepdims=True) * scale - sl_ref[...] * dist.astype(F32)
    s = jnp.where(count > 0.0, s, NEG)
    s0 = jnp.sum(kn_ref[0] * q, axis=1, keepdims=True) * scale
    m = jnp.maximum(jnp.max(s, axis=-1, keepdims=True), s0)
    p = count * jnp.exp(s - m)
    e0 = len(DIL_PATTERNS) * jnp.exp(s0 - m)
    l = jnp.sum(p, axis=-1, keepdims=True) + e0
    acc = jnp.sum(vt_ref[0] * p, axis=-1, keepdims=True) + e0 * vn_ref[0]
    o_ref[0] = acc / l


def _attn_step(q, k_new, v_new, k_cache, v_cache, slopes):
    nb, P, H, _ = k_cache.shape
    col = lambda a: a.reshape(nb, H, HEAD_DIM, 1)
    c_spec = pl.BlockSpec((1, H, HEAD_DIM, 1), lambda b: (b, 0, 0, 0))
    t_spec = pl.BlockSpec((1, H, HEAD_DIM, P), lambda b: (b, 0, 0, 0))
    out = pl.pallas_call(
        functools.partial(_attn_step_kernel, P=P),
        grid=(nb,),
        in_specs=[c_spec, c_spec, c_spec, t_spec, t_spec, _full((H, 1, 1))],
        out_specs=c_spec,
        out_shape=jax.ShapeDtypeStruct((nb, H, HEAD_DIM, 1), F32),
        compiler_params=_cparams(1),
        name="attn_step",
    )(col(q), col(k_new), col(v_new), jnp.transpose(k_cache, (0, 2, 3, 1)), jnp.transpose(v_cache, (0, 2, 3, 1)),
      slopes.reshape(H, 1, 1))
    return out.reshape(nb, H * HEAD_DIM)


def _outproj_kernel(att_ref, y_ref, x_ref, w_ref, g_ref, b_ref, o_ref, *, aw, alpha):
    h = _dot(att_ref[...], w_ref[0:aw, :]) + _dot(y_ref[...], w_ref[aw:, :])
    o_ref[...] = _layer_norm(alpha * x_ref[...] + h, g_ref[...], b_ref[...])


def _outproj(att, y_rw, x, w_out, ln_g, ln_b, tm, alpha):
    rows, D = x.shape
    aw = y_rw.shape[1]
    half = pl.BlockSpec((tm, aw), lambda i: (i, 0))
    return pl.pallas_call(
        functools.partial(_outproj_kernel, aw=aw, alpha=alpha),
        grid=(rows // tm,),
        in_specs=[half, half, pl.BlockSpec((tm, D), lambda i: (i, 0)), _full(w_out.shape), _full(ln_g.shape),
                  _full(ln_b.shape)],
        out_specs=pl.BlockSpec((tm, D), lambda i: (i, 0)),
        out_shape=jax.ShapeDtypeStruct((rows, D), F32),
        compiler_params=_cparams(1),
        name="outproj_ln",
    )(att, y_rw, x, w_out, ln_g, ln_b)


def _ffn_tail(x1, gate_pre, prev1, prev2, up, cw_ref, cb_ref, w2_ref, g_ref, b_ref, alpha):
    c = cw_ref[0:1, :] * prev2 + cw_ref[1:2, :] * prev1 + cw_ref[2:3, :] * gate_pre + cb_ref[...]
    h = c * _sigmoid(c) * up
    return _layer_norm(alpha * x1 + _dot(h, w2_ref[...]), g_ref[...], b_ref[...])


def _ffn_seq_kernel(att_ref, y_ref, x_ref, wo_ref, g1_ref, b1_ref, w1_ref, cw_ref, cb_ref, w2_ref, g_ref, b_ref,
                    o_ref, tail_ref, carry_ref, *, aw, dff, alpha):
    @pl.when(pl.program_id(1) == 0)
    def _():
        carry_ref[...] = jnp.zeros_like(carry_ref)

    h = _dot(att_ref[0], wo_ref[0:aw, :]) + _dot(y_ref[0], wo_ref[aw:, :])
    x1 = _layer_norm(alpha * x_ref[0] + h, g1_ref[...], b1_ref[...])
    hu = _dot(x1, w1_ref[...])
    gate_pre = hu[:, 0:dff]
    up = hu[:, dff:]
    tm = gate_pre.shape[0]
    row = lax.broadcasted_iota(jnp.int32, gate_pre.shape, 0)
    c0 = carry_ref[0:1, :]
    c1 = carry_ref[1:2, :]
    prev1 = jnp.where(row == 0, c1, pltpu.roll(gate_pre, 1, axis=0))
    prev2 = jnp.where(row == 0, c0, jnp.where(row == 1, c1, pltpu.roll(gate_pre, 2, axis=0)))
    last2 = gate_pre[tm - 2:tm, :]
    carry_ref[0:2, :] = last2
    tail_ref[0] = last2
    o_ref[0] = _ffn_tail(x1, gate_pre, prev1, prev2, up, cw_ref, cb_ref, w2_ref, g_ref, b_ref, alpha)


def _ffn_step_kernel(x_ref, p2_ref, p1_ref, w1_ref, cw_ref, cb_ref, w2_ref, g_ref, b_ref, o_ref, gate_ref, *,
                     dff, alpha):
    x1 = x_ref[...]
    hu = _dot(x1, w1_ref[...])
    gate_pre = hu[:, 0:dff]
    gate_ref[...] = gate_pre
    o_ref[...] = _ffn_tail(x1, gate_pre, p1_ref[...], p2_ref[...], hu[:, dff:], cw_ref, cb_ref, w2_ref, g_ref,
                           b_ref, alpha)


def _resident(shape):
    nd = len(shape)
    return pl.BlockSpec(shape, lambda *_: (0,) * nd, pipeline_mode=pl.Buffered(1))


def _ffn_seq(att, y_rw, x, w_out, ln1_g, ln1_b, w1, conv_w, conv_b, w2, ln2_g, ln2_b, tm, alpha):
    B, T, D = x.shape
    aw = att.shape[-1]
    dff = w2.shape[0]
    half = pl.BlockSpec((1, tm, aw), lambda b, t: (b, t, 0))
    rows = pl.BlockSpec((1, tm, D), lambda b, t: (b, t, 0))
    return pl.pallas_call(
        functools.partial(_ffn_seq_kernel, aw=aw, dff=dff, alpha=alpha),
        grid=(B, T // tm),
        in_specs=[half, half, rows, _resident(w_out.shape), _full(ln1_g.shape), _full(ln1_b.shape),
                  _resident(w1.shape), _full(conv_w.shape), _full(conv_b.shape), _resident(w2.shape),
                  _full(ln2_g.shape), _full(ln2_b.shape)],
        out_specs=[rows, pl.BlockSpec((1, 2, dff), lambda b, t: (b, 0, 0))],
        out_shape=[jax.ShapeDtypeStruct((B, T, D), F32), jax.ShapeDtypeStruct((B, 2, dff), F32)],
        scratch_shapes=[pltpu.VMEM((8, dff), F32)],
        compiler_params=_cparams(2),
        name="ffn_seq",
    )(att, y_rw, x, w_out, ln1_g, ln1_b, w1, conv_w, conv_b, w2, ln2_g, ln2_b)


def _ffn_step(x1, prev2, prev1, w1, conv_w, conv_b, w2, ln_g, ln_b, alpha):
    nb, D = x1.shape
    dff = w2.shape[0]
    return pl.pallas_call(
        functools.partial(_ffn_step_kernel, dff=dff, alpha=alpha),
        grid=(1,),
        in_specs=[_full(x1.shape), _full(prev2.shape), _full(prev1.shape), _resident(w1.shape),
                  _full(conv_w.shape), _full(conv_b.shape), _resident(w2.shape), _full(ln_g.shape),
                  _full(ln_b.shape)],
        out_specs=[_full((nb, D)), _full((nb, dff))],
        out_shape=[jax.ShapeDtypeStruct((nb, D), F32), jax.ShapeDtypeStruct((nb, dff), F32)],
        compiler_params=_cparams(1),
        name="ffn_step",
    )(x1, prev2, prev1, w1, conv_w, conv_b, w2, ln_g, ln_b)


def _layer_params(w_in, mu_shift, w0, w_lora_up, a0, a_lora_up, g_lora_up, k_k, k_a, r_k, aw):
    D, ncol = w_in.shape
    n_w, n_a, n_g = w_lora_up.shape[0], a_lora_up.shape[0], g_lora_up.shape[0]
    assert n_w + n_a == LANES and 3 * aw + 3 * aw + n_w + n_a + n_g == ncol
    g_cols = -(-n_g // LANES) * LANES
    pad = g_cols - n_g
    row = lambda a: a.reshape(1, -1).astype(F32)
    lane = jnp.arange(aw)
    bd = (lane[:, None] // HEAD_DIM == lane[None, :] // HEAD_DIM).astype(BF16)
    return dict(
        aw=aw,
        w_in=jnp.pad(w_in, ((0, 0), (0, pad))).astype(BF16),
        mu=jnp.pad(row(mu_shift), ((0, 0), (0, pad))),
        w0=row(w0), a0=row(a0), k_k=row(k_k), k_a=row(k_a), r_k=row(r_k),
        wl_pad=jnp.pad(w_lora_up, ((0, n_a), (0, 0))).astype(BF16),
        al_pad=jnp.pad(a_lora_up, ((n_w, 0), (0, 0))).astype(BF16),
        g_pad=jnp.pad(g_lora_up, ((0, pad), (0, 0))).astype(BF16),
        bd=bd,
    )


def kernel(x_prompt, x_sample, cache_k_win, cache_v_win, state_shift, state_wkv, state_conv, w_in, mu_shift, w0, w_lora_up, a0, a_lora_up, g_lora_up, k_k, k_a, r_k, gn_w, gn_b, w_out, ln1_g, ln1_b, w_ffn_in, conv_w, conv_b, w_ffn_out, ln2_g, ln2_b):
    depth = w_in.shape[0]
    alpha = (2.0 * depth) ** 0.25
    B, T, D = x_prompt.shape
    nb = x_sample.shape[0]
    assert x_sample.shape[1] == 1
    n_att = cache_k_win.shape[3]
    n_rw = state_wkv.shape[2]
    aw = n_att * HEAD_DIM
    assert n_rw * HEAD_DIM == aw and T % (Q_TILE * DIL_PATTERNS[-1][1]) == 0
    tm = min(256, T)
    tb = min(256, T)
    heads = jnp.arange(1, n_att + 1, dtype=F32)
    slopes = jnp.exp2(-8.0 * heads / n_att)
    slopes_tab = jnp.broadcast_to(slopes[:, None], (n_att, LANES))
    row = lambda a: a.reshape(1, -1)

    hp, hs = x_prompt, x_sample.reshape(nb, D)
    outs = [[] for _ in range(10)]
    for l in range(depth):
        prm = _layer_params(w_in[l], mu_shift[l], w0[l], w_lora_up[l], a0[l], a_lora_up[l], g_lora_up[l],
                            k_k[l], k_a[l], r_k[l], aw)
        w_out_b = w_out[l].astype(BF16)
        w1_b = w_ffn_in[l].astype(BF16)
        w2_b = w_ffn_out[l].astype(BF16)
        gnw, gnb = row(gn_w[l]), row(gn_b[l])

        q, k, v, kt, vt, r, lw, k2, vr, al, be, g, bonus = _inproj_seq(hp, prm, tm)
        att = _prompt_attn(q, k, v, slopes_tab)
        y_rw, wkv_p = _rwkv_chunked(r, lw, k2, vr, al, be, g, bonus, gnw, gnb, tb)
        shift_p = hp[:, -1, :]
        hp, conv_p = _ffn_seq(att, y_rw, hp, w_out_b, row(ln1_g[l]), row(ln1_b[l]), w1_b, conv_w[l],
                              row(conv_b[l]), w2_b, row(ln2_g[l]), row(ln2_b[l]), tm, alpha)

        x2 = jnp.concatenate([hs, state_shift[l]], axis=0)
        qs, ks, vs, r, lw, k2, vr, al, be, g, bonus = _inproj_step(x2, prm)
        att_s = _attn_step(qs, ks, vs, cache_k_win[l], cache_v_win[l], slopes)
        y_rw_s, wkv_s = _rwkv_step(state_wkv[l], r, lw, k2, vr, al, be, g, bonus, gn_w[l], gn_b[l])
        x1s = _outproj(att_s, y_rw_s, hs, w_out_b, row(ln1_g[l]), row(ln1_b[l]), nb, alpha)
        shift_s = hs
        hs, gate_s = _ffn_step(x1s, state_conv[l][:, 0, :], state_conv[l][:, 1, :], w1_b, conv_w[l],
                               row(conv_b[l]), w2_b, row(ln2_g[l]), row(ln2_b[l]), alpha)
        conv_s = jnp.stack([state_conv[l][:, 1, :], gate_s], axis=1)

        n_keep = min(DIL_PATTERNS[-1][0], T)
        win = lambda a: jnp.transpose(a.reshape(B, n_att, HEAD_DIM, T), (0, 3, 1, 2))[:, T - n_keep:]
        vals = (win(kt), win(vt), ks.reshape(nb, 1, n_att, HEAD_DIM), vs.reshape(nb, 1, n_att, HEAD_DIM),
                shift_p, shift_s, wkv_p, wkv_s, conv_p, conv_s)
        for lst, val in zip(outs, vals):
            lst.append(val)
    return (hp, hs.reshape(nb, 1, D)) + tuple(jnp.stack(lst) for lst in outs)
```

```python
import functools
import math

import jax
import jax.numpy as jnp
from jax import lax
from jax.experimental import pallas as pl
from jax.experimental.pallas import tpu as pltpu

F32 = jnp.float32
BF16 = jnp.bfloat16

HEAD_DIM = 64
LANES = 128
DIL_PATTERNS = ((128, 1), (512, 4), (2048, 16))
WIN_STEPS = 128
Q_TILE = 128
GROUP = 4
SUB_TILES = 4
CHUNK = 64
LN_EPS = 1e-5
GN_EPS = 64e-5
NEG = -1e30
VMEM_LIMIT = 56 * 1024 * 1024

_DN = {"NN": (((1,), (0,)), ((), ())), "NT": (((1,), (1,)), ((), ())), "TN": (((0,), (0,)), ((), ()))}


def _dot(a, b, dims="NN"):
    return lax.dot_general(a.astype(BF16), b.astype(BF16), _DN[dims], preferred_element_type=F32)


def _split2(x):
    hi = x.astype(BF16)
    lo = (x - hi.astype(F32)).astype(BF16)
    return hi, lo


def _dot_const(x, c):
    hi, lo = _split2(x)
    return _dot(hi, c) + _dot(lo, c)


def _dot_const_l(c, x):
    hi = x.astype(BF16)
    r1 = x - hi.astype(F32)
    mid = r1.astype(BF16)
    lo = (r1 - mid.astype(F32)).astype(BF16)
    return _dot(c, hi) + _dot(c, mid) + _dot(c, lo)


def _sigmoid(x):
    return 0.5 * jnp.tanh(0.5 * x) + 0.5


def _layer_norm(x, g, b):
    mu = jnp.mean(x, axis=-1, keepdims=True)
    xc = x - mu
    var = jnp.mean(xc * xc, axis=-1, keepdims=True)
    return xc * lax.rsqrt(var + LN_EPS) * g + b


def _cparams(n_grid):
    return pltpu.CompilerParams(dimension_semantics=("arbitrary",) * n_grid, vmem_limit_bytes=VMEM_LIMIT)


def _full(shape):
    nd = len(shape)
    return pl.BlockSpec(shape, lambda *_: (0,) * nd)


def _rwkv_prep_tiles(rws, w0, a0, k_k, k_a, r_k, wl_pad, al_pad, g_pad, bd, aw):
    n = range(len(rws))
    r = [rw[:, 0:aw] for rw in rws]
    k = [rw[:, aw:2 * aw] for rw in rws]
    v = [rw[:, 2 * aw:3 * aw] for rw in rws]
    lo = [rw[:, 3 * aw:3 * aw + 128] for rw in rws]
    glo = [rw[:, 3 * aw + 128:] for rw in rws]
    u = [w0 + _dot(jnp.tanh(lo[i]), wl_pad) for i in n]
    a = [_sigmoid(a0 + _dot(lo[i], al_pad)) for i in n]
    g = [_dot(_sigmoid(glo[i]), g_pad) for i in n]
    kk = [k[i] * k_k for i in n]
    sumsq = [_dot(kk[i] * kk[i], bd) for i in n]
    lw = [-math.exp(-0.5) * _sigmoid(u[i]) for i in n]
    k2 = [k[i] * (1.0 + (a[i] - 1.0) * k_a) for i in n]
    bonus = [_dot(r[i] * k2[i] * r_k, bd) * v[i] for i in n]
    kkn = [kk[i] * lax.rsqrt(jnp.maximum(sumsq[i], 1e-24)) for i in n]
    return [(r[i], lw[i], k2[i], v[i], -kkn[i], kkn[i] * a[i], g[i], bonus[i]) for i in n]


def _rwkv_prep(rw, *params):
    return _rwkv_prep_tiles([rw], *params)[0]


def _inproj_seq_kernel(x_ref, w_ref, mu_ref, w0_ref, a0_ref, kk_ref, ka_ref, rk_ref, wl_ref, al_ref, gp_ref,
                       bd_ref, q_ref, k_ref, v_ref, kt_ref, vt_ref, r_o, lw_o, k2_o, vr_o, al_o, be_o, g_o, bo_o,
                       carry_ref, *, aw):
    @pl.when(pl.program_id(1) == 0)
    def _():
        carry_ref[...] = jnp.zeros_like(carry_ref)

    tm = x_ref.shape[1]
    hr = tm // SUB_TILES
    spans = [slice(h * hr, (h + 1) * hr) for h in range(SUB_TILES)]
    params = (w0_ref[...], a0_ref[...], kk_ref[...], ka_ref[...], rk_ref[...], wl_ref[...], al_ref[...],
              gp_ref[...], bd_ref[...], aw)
    row = lax.broadcasted_iota(jnp.int32, (hr, w_ref.shape[1] - 3 * aw), 0)

    def finish(p, sp, last):
        q_ref[0, sp, :] = p[:, 0:aw]
        k = p[:, aw:2 * aw]
        v = p[:, 2 * aw:3 * aw]
        k_ref[0, sp, :] = k
        v_ref[0, sp, :] = v
        kt_ref[0, :, sp] = k.T
        vt_ref[0, :, sp] = v.T
        prw = p[:, 3 * aw:]
        prev = jnp.where(row == 0, last, pltpu.roll(prw, 1, axis=0))
        outs = _rwkv_prep(prw + (prev - prw) * mu_ref[...], *params)
        for o_ref, val in zip((r_o, lw_o, k2_o, vr_o, al_o, be_o, g_o, bo_o), outs):
            o_ref[0, sp, :] = val.astype(o_ref.dtype)
        return prw[hr - 1:hr, :]

    last = carry_ref[0:1, :]
    p = _dot(x_ref[0, spans[0], :], w_ref[...])
    for h in range(1, SUB_TILES):
        p_next = _dot(x_ref[0, spans[h], :], w_ref[...])
        last = finish(p, spans[h - 1], last)
        p = p_next
    carry_ref[0:1, :] = finish(p, spans[-1], last)


def _inproj_step_kernel(x_ref, w_ref, mu_ref, w0_ref, a0_ref, kk_ref, ka_ref, rk_ref, wl_ref, al_ref, gp_ref,
                        bd_ref, k_ref, v_ref, r_o, lw_o, k2_o, al_o, be_o, col_o, *, aw, nb):
    p = _dot(x_ref[...], w_ref[...])
    q = p[0:nb, 0:aw]
    k = p[0:nb, aw:2 * aw]
    v = p[0:nb, 2 * aw:3 * aw]
    k_ref[...] = k
    v_ref[...] = v
    prw = p[0:nb, 3 * aw:]
    prev = p[nb:2 * nb, 3 * aw:]
    rw = prw + (prev - prw) * mu_ref[...]
    r, lw, k2, vr, al, be, g, bonus = _rwkv_prep(rw, w0_ref[...], a0_ref[...], kk_ref[...], ka_ref[...],
                                                 rk_ref[...], wl_ref[...], al_ref[...], gp_ref[...], bd_ref[...], aw)
    for o_ref, val in zip((r_o, lw_o, k2_o, al_o, be_o), (r, lw, k2, al, be)):
        o_ref[...] = val
    pad = jnp.zeros((LANES - nb, aw), F32)
    for i, val in enumerate((q, k, v, vr, g, bonus)):
        col_o[i * aw:(i + 1) * aw, :] = jnp.concatenate([val, pad], axis=0).T


_PREP_PARAMS = ("mu", "w0", "a0", "k_k", "k_a", "r_k", "wl_pad", "al_pad", "g_pad", "bd")


def _inproj_seq(x, prm, tm):
    B, T, D = x.shape
    aw = prm["aw"]
    nrw = prm["w_in"].shape[1] - 3 * aw
    small = [prm[n] for n in _PREP_PARAMS]
    row_spec = pl.BlockSpec((1, tm, aw), lambda b, t: (b, t, 0))
    col_spec = pl.BlockSpec((1, aw, tm), lambda b, t: (b, 0, t))
    row_sds = jax.ShapeDtypeStruct((B, T, aw), F32)
    col_sds = jax.ShapeDtypeStruct((B, aw, T), F32)
    return pl.pallas_call(
        functools.partial(_inproj_seq_kernel, aw=aw),
        grid=(B, T // tm),
        in_specs=[pl.BlockSpec((1, tm, D), lambda b, t: (b, t, 0)), _resident(prm["w_in"].shape)]
        + [_full(s.shape) for s in small],
        out_specs=[row_spec] * 3 + [col_spec] * 2 + [row_spec] * 8,
        out_shape=[row_sds] * 3 + [col_sds] * 2
        + [jax.ShapeDtypeStruct((B, T, aw), F32 if n == "lw" else BF16)
           for n in ("r", "lw", "k2", "v", "al", "be", "g", "bonus")],
        scratch_shapes=[pltpu.VMEM((8, nrw), F32)],
        compiler_params=_cparams(2),
        name="inproj_seq",
    )(x, prm["w_in"], *small)


def _inproj_step(x2, prm):
    nb = x2.shape[0] // 2
    aw = prm["aw"]
    small = [prm[n] for n in _PREP_PARAMS]
    assert nb <= LANES
    out_sds = jax.ShapeDtypeStruct((nb, aw), F32)
    return pl.pallas_call(
        functools.partial(_inproj_step_kernel, aw=aw, nb=nb),
        grid=(1,),
        in_specs=[_full(x2.shape), _full(prm["w_in"].shape)] + [_full(s.shape) for s in small],
        out_specs=[_full((nb, aw))] * 7 + [_full((6 * aw, LANES))],
        out_shape=[out_sds] * 7 + [jax.ShapeDtypeStruct((6 * aw, LANES), F32)],
        compiler_params=_cparams(1),
        name="inproj_step",
    )(x2, prm["w_in"], *small)


def _prompt_attn_kernel(q_ref, k_ref, v_ref, sl_ref, o_ref, acc_s, m_s, l_s, yq, yk, yv, yacc, ym, yl, *, T):
    hp = pl.program_id(1)
    scale = HEAD_DIM ** -0.5
    lane = lax.broadcasted_iota(jnp.int32, (1, LANES), 1)
    head_masks = ((lane < HEAD_DIM).astype(F32), (lane >= HEAD_DIM).astype(F32))
    qi = lax.broadcasted_iota(jnp.int32, (Q_TILE, Q_TILE), 0)
    ki = lax.broadcasted_iota(jnp.int32, (Q_TILE, Q_TILE), 1)
    steps_cur = (qi - ki).astype(F32)
    steps_prev = (qi - ki + Q_TILE).astype(F32)
    ok_cur = ki <= qi
    ok_prev = ki >= qi
    first_lanes = lax.broadcasted_iota(jnp.int32, (Q_TILE, LANES), 1) < HEAD_DIM
    slopes = [sl_ref[pl.ds(2 * hp + j, 1), :] for j in range(2)]

    def tile_group(src, tiles, bias, stats, first_branch):
        rq, rk, rv = src
        bias_c, bias_pc = bias
        acc_s, m_s, l_s = stats
        ids = range(len(tiles))
        q = [rq(rows) * scale for rows, _ in tiles]
        kk = [rk(rows) if prev is None else jnp.concatenate([rk(prev), rk(rows)], axis=0) for rows, prev in tiles]
        vv = [rv(rows) if prev is None else jnp.concatenate([rv(prev), rv(rows)], axis=0) for rows, prev in tiles]
        qq = [jnp.concatenate([q[t] * head_masks[0], q[t] * head_masks[1]], axis=0) for t in ids]
        s = [_dot(qq[t], kk[t], "NT") + (bias_c if tiles[t][1] is None else bias_pc) for t in ids]
        m = [jnp.max(s[t], axis=-1, keepdims=True) for t in ids]
        p = [jnp.exp(s[t] - m[t]) for t in ids]
        l = [jnp.sum(p[t], axis=-1, keepdims=True) for t in ids]
        acc = [_dot(p[t], vv[t]) for t in ids]
        for t, (rows, _) in enumerate(tiles):
            acc_n = jnp.where(first_lanes, acc[t][0:Q_TILE], acc[t][Q_TILE:])
            m_n = jnp.where(first_lanes, m[t][0:Q_TILE], m[t][Q_TILE:])
            l_n = jnp.where(first_lanes, l[t][0:Q_TILE], l[t][Q_TILE:])
            if first_branch:
                acc_s[rows, :] = acc_n
                m_s[rows, :] = m_n
                l_s[rows, :] = l_n
            else:
                m_o = m_s[rows, :]
                m_new = jnp.maximum(m_o, m_n)
                e_o = jnp.exp(m_o - m_new)
                e_n = jnp.exp(m_n - m_new)
                acc_s[rows, :] = acc_s[rows, :] * e_o + acc_n * e_n
                l_s[rows, :] = l_s[rows, :] * e_o + l_n * e_n
                m_s[rows, :] = m_new

    dils = [dil for _, dil in DIL_PATTERNS]
    R = dils[1]
    assert dils == [1, R, R * R] and all(win // dil == WIN_STEPS for win, dil in DIL_PATTERNS)

    def bias_of(dil):
        b_cur = [jnp.where(ok_cur, -(slopes[j] * float(dil)) * steps_cur, NEG) for j in range(2)]
        b_prev = [jnp.where(ok_prev, -(slopes[j] * float(dil)) * steps_prev, NEG) for j in range(2)]
        return (jnp.concatenate(b_cur, axis=0),
                jnp.concatenate([jnp.concatenate([b_prev[j], b_cur[j]], axis=1) for j in range(2)], axis=0))

    def groups(tiles):
        return [tiles[i:i + GROUP] for i in range(0, len(tiles), GROUP)]

    natural = tuple((lambda rows, ref=ref: ref[0, rows, :]) for ref in (q_ref, k_ref, v_ref))
    scratch = tuple((lambda rows, ref=ref: ref[rows, :]) for ref in (yq, yk, yv))
    nat_stats = (acc_s, m_s, l_s)
    y_stats = (yacc, ym, yl)

    nb1 = T // Q_TILE
    assert nb1 % GROUP == 0
    bias1 = bias_of(dils[0])
    tile_group(natural, [(pl.ds(jb * Q_TILE, Q_TILE), pl.ds((jb - 1) * Q_TILE, Q_TILE) if jb else None)
                         for jb in range(GROUP)], bias1, nat_stats, True)

    def later(g, carry):
        start = pl.multiple_of(g * (GROUP * Q_TILE), GROUP * Q_TILE)
        tile_group(natural, [(pl.ds(start + i * Q_TILE, Q_TILE), pl.ds(start + (i - 1) * Q_TILE, Q_TILE))
                             for i in range(GROUP)], bias1, nat_stats, True)
        return carry

    lax.fori_loop(1, nb1 // GROUP, later, 0)

    LR = T // R
    bias_r = bias_of(dils[1])
    bias_rr = bias_of(dils[2])

    def stream(s, carry):
        own = pl.ds(s, LR, stride=R)
        yq[...] = q_ref[0, own, :]
        yk[...] = k_ref[0, own, :]
        yv[...] = v_ref[0, own, :]
        for grp in groups([(pl.ds(jb * Q_TILE, Q_TILE), pl.ds((jb - 1) * Q_TILE, Q_TILE) if jb else None)
                           for jb in range(LR // Q_TILE)]):
            tile_group(scratch, grp, bias_r, y_stats, True)
        for grp in groups([(pl.ds(c + R * Q_TILE * jb, Q_TILE, stride=R),
                            pl.ds(c + R * Q_TILE * (jb - 1), Q_TILE, stride=R) if jb else None)
                           for c in range(R) for jb in range(LR // R // Q_TILE)]):
            tile_group(scratch, grp, bias_rr, y_stats, False)
        m_a, m_b = m_s[own, :], ym[...]
        m_new = jnp.maximum(m_a, m_b)
        e_a = jnp.exp(m_a - m_new)
        e_b = jnp.exp(m_b - m_new)
        acc_s[own, :] = (acc_s[own, :] * e_a + yacc[...] * e_b) / (l_s[own, :] * e_a + yl[...] * e_b)
        return carry

    lax.fori_loop(0, R, stream, 0)
    o_ref[0] = acc_s[...].astype(o_ref.dtype)


def _prompt_attn(q, k, v, slopes_tab):
    B, T, W = q.shape
    spec = pl.BlockSpec((1, T, LANES), lambda b, hp: (b, 0, hp))
    return pl.pallas_call(
        functools.partial(_prompt_attn_kernel, T=T),
        grid=(B, W // LANES),
        in_specs=[spec, spec, spec, _full(slopes_tab.shape)],
        out_specs=spec,
        out_shape=jax.ShapeDtypeStruct((B, T, W), BF16),
        scratch_shapes=[pltpu.VMEM((T, LANES), F32)] * 3 + [pltpu.VMEM((T // DIL_PATTERNS[1][1], LANES), F32)] * 6,
        compiler_params=_cparams(2),
        name="prompt_attn",
    )(q, k, v, slopes_tab)


def _rwkv_chunk_kernel(r_ref, lw_ref, k_ref, v_ref, al_ref, be_ref, g_ref, bo_ref, gnw_ref, gnb_ref,
                       y_ref, s_ref, st_scr, *, Tb, npair, nbb):
    C = CHUNK
    n2 = 2 * C
    t = pl.program_id(1)

    @pl.when(t == 0)
    def _():
        st_scr[...] = jnp.zeros_like(st_scr)

    lane = lax.broadcasted_iota(jnp.int32, (1, LANES), 1)
    m0 = (lane < HEAD_DIM).astype(F32)
    m1 = 1.0 - m0
    ri = lax.broadcasted_iota(jnp.int32, (n2, n2), 0)
    ci = lax.broadcasted_iota(jnp.int32, (n2, n2), 1)
    same = (ri < C) == (ci < C)
    strict = same & ((ri & (C - 1)) > (ci & (C - 1)))
    incl = same & ((ri & (C - 1)) >= (ci & (C - 1)))
    tri = (lax.broadcasted_iota(jnp.int32, (C, C), 0) >= lax.broadcasted_iota(jnp.int32, (C, C), 1)).astype(BF16)
    head_rows = ((lax.broadcasted_iota(jnp.int32, (n2, LANES), 0) < C)
                 == (lax.broadcasted_iota(jnp.int32, (n2, LANES), 1) < HEAD_DIM)).astype(F32)

    def stack(x):
        return jnp.concatenate([x * m0, x * m1], axis=0)

    def twice(x):
        return jnp.concatenate([x, x], axis=0)

    def body(c, carry):
        rows = pl.ds(pl.multiple_of(c * C, C), C)
        chains = [(bb, slice(hp * LANES, (hp + 1) * LANES)) for bb in range(nbb) for hp in range(npair)]
        ids = range(len(chains))
        S = [st_scr[i] for i in ids]
        lw_all = jnp.concatenate([lw_ref[bb, rows, :] for bb in range(nbb)], axis=1)
        cl_all = _dot_const_l(tri, lw_all)
        e_pos_all = jnp.exp(cl_all)
        e_neg_all = jnp.exp(-cl_all)
        e_exc_all = jnp.exp(cl_all - lw_all)
        wide = [slice(i * LANES, (i + 1) * LANES) for i in ids]
        g_end = [e_pos_all[C - 1:C, wide[i]] for i in ids]
        rt = [r_ref[bb, rows, cs] * e_pos_all[:, wide[i]] for i, (bb, cs) in enumerate(chains)]
        kt = [k_ref[bb, rows, cs] * e_neg_all[:, wide[i]] for i, (bb, cs) in enumerate(chains)]
        bt = [be_ref[bb, rows, cs] * e_neg_all[:, wide[i]] for i, (bb, cs) in enumerate(chains)]
        at = [al_ref[bb, rows, cs] * e_exc_all[:, wide[i]] for i, (bb, cs) in enumerate(chains)]
        ar_st = [jnp.concatenate([stack(at[i]), stack(rt[i])], axis=0) for i in ids]
        kb2 = [jnp.concatenate([twice(kt[i]), twice(bt[i])], axis=0) for i in ids]
        v_st = [stack(v_ref[bb, rows, cs]) for bb, cs in chains]
        gram = [_dot(ar_st[i], kb2[i], "NT") for i in ids]
        l_ab = [jnp.where(strict, gram[i][0:n2, n2:], 0.0) for i in ids]
        l_kk = [jnp.concatenate([jnp.where(strict, gram[i][0:n2, 0:n2], 0.0),
                                 jnp.where(incl, gram[i][n2:, 0:n2], 0.0)], axis=0) for i in ids]
        l_rb = [jnp.where(incl, gram[i][n2:, n2:], 0.0) for i in ids]
        from_state = [_dot(ar_st[i], S[i], "NT") for i in ids]
        from_v = [_dot(l_kk[i], v_st[i]) for i in ids]
        x = l_ab
        u = [from_state[i][0:n2] + from_v[i][0:n2] for i in ids]
        for _ in range(C.bit_length() - 2):
            xu = [_dot(x[i], jnp.concatenate([x[i], u[i]], axis=1)) for i in ids]
            x = [xu[i][:, 0:n2] for i in ids]
            u = [u[i] + xu[i][:, n2:] for i in ids]
        u = [u[i] + _dot(x[i], u[i]) for i in ids]
        y_st = [from_state[i][n2:] + from_v[i][n2:] + _dot(l_rb[i], u[i]) for i in ids]
        for i in ids:
            vu = jnp.concatenate([v_st[i], u[i]], axis=0)
            kb_end = jnp.concatenate([stack(kt[i] * g_end[i]), stack(bt[i] * g_end[i])], axis=0)
            st_scr[i] = S[i] * g_end[i] + _dot(vu, kb_end, "TN")
        for i, (bb, cs) in enumerate(chains):
            mu = jnp.sum(y_st[i], axis=-1, keepdims=True) * (1.0 / HEAD_DIM)
            yc = (y_st[i] - mu) * head_rows
            var = jnp.sum(yc * yc, axis=-1, keepdims=True) * (1.0 / HEAD_DIM)
            yn_st = yc * lax.rsqrt(var + GN_EPS)
            yn_i = (yn_st[0:C, :] + yn_st[C:n2, :]) * gnw_ref[:, cs] + gnb_ref[:, cs]
            y_ref[bb, rows, cs] = ((yn_i + bo_ref[bb, rows, cs]) * g_ref[bb, rows, cs]).astype(y_ref.dtype)
        return carry

    lax.fori_loop(0, Tb // C, body, 0)

    @pl.when(t == pl.num_programs(1) - 1)
    def _():
        s_ref[...] = st_scr[...].reshape(s_ref.shape)


def _rwkv_chunked(r, lw, k2, v, al, be, g, bonus, gn_w, gn_b, tb):
    B, T, W = r.shape
    npair = W // LANES
    nbb = 2 if B % 2 == 0 else 1
    spec = pl.BlockSpec((nbb, tb, W), lambda b, t: (b, t, 0))
    st_spec = pl.BlockSpec((nbb, npair, LANES, LANES), lambda b, t: (b, 0, 0, 0))
    y, s = pl.pallas_call(
        functools.partial(_rwkv_chunk_kernel, Tb=tb, npair=npair, nbb=nbb),
        grid=(B // nbb, T // tb),
        in_specs=[spec] * 8 + [_full(gn_w.shape), _full(gn_b.shape)],
        out_specs=[spec, st_spec],
        out_shape=[jax.ShapeDtypeStruct((B, T, W), BF16), jax.ShapeDtypeStruct((B, npair, LANES, LANES), F32)],
        scratch_shapes=[pltpu.VMEM((nbb * npair, LANES, LANES), F32)],
        compiler_params=_cparams(2),
        name="rwkv_chunk",
    )(r, lw, k2, v, al, be, g, bonus, gn_w, gn_b)
    s = s.reshape(B, npair, 2, HEAD_DIM, 2, HEAD_DIM)
    s = jnp.stack([s[:, :, 0, :, 0, :], s[:, :, 1, :, 1, :]], axis=2)
    return y, s.reshape(B, 2 * npair, HEAD_DIM, HEAD_DIM)


def _rwkv_one_step(S, r, lw, k, al, be, v, g, bonus, gnw, gnb):
    sa = jnp.sum(S * al, axis=-1, keepdims=True)
    s_new = S * jnp.exp(lw) + sa * be + v * k
    y = jnp.sum(s_new * r, axis=-1, keepdims=True)
    mu = jnp.mean(y, axis=1, keepdims=True)
    yc = y - mu
    var = jnp.mean(yc * yc, axis=1, keepdims=True)
    yn = yc * lax.rsqrt(var + GN_EPS) * gnw + gnb
    return (yn + bonus) * g, s_new


def _attn_one_step(q, k_new, v_new, kt, vt, slopes, P):
    scale = HEAD_DIM ** -0.5
    dist = P - lax.broadcasted_iota(jnp.int32, (1, 1, P), 2)
    count = jnp.zeros((1, 1, P), F32)
    for win, dil in DIL_PATTERNS:
        assert dil & (dil - 1) == 0
        count = count + ((dist <= win) & ((dist & (dil - 1)) == 0)).astype(F32)
    s = jnp.sum(kt * q, axis=1, keepdims=True) * scale - slopes * dist.astype(F32)
    s = jnp.where(count > 0.0, s, NEG)
    s0 = jnp.sum(k_new * q, axis=1, keepdims=True) * scale
    m = jnp.maximum(jnp.max(s, axis=-1, keepdims=True), s0)
    p = count * jnp.exp(s - m)
    e0 = len(DIL_PATTERNS) * jnp.exp(s0 - m)
    l = jnp.sum(p, axis=-1, keepdims=True) + e0
    acc = jnp.sum(vt * p, axis=-1, keepdims=True) + e0 * v_new
    return acc / l


def _mixer_step_kernel(col_ref, kt_ref, vt_ref, sl_ref, s_ref, r_ref, lw_ref, k_ref, al_ref, be_ref, gnw_ref,
                       gnb_ref, mix_ref, so_ref, *, P, aw):
    b = pl.program_id(0)
    H = aw // HEAD_DIM
    own = (lax.broadcasted_iota(jnp.int32, (1, LANES), 1) == b).astype(F32)

    def column(i):
        return jnp.sum(col_ref[i * aw:(i + 1) * aw, :] * own, axis=-1, keepdims=True).reshape(H, HEAD_DIM, 1)

    @pl.when(b == 0)
    def _():
        mix_ref[...] = jnp.zeros_like(mix_ref)

    att = _attn_one_step(column(0), column(1), column(2), kt_ref[0], vt_ref[0], sl_ref[...], P)
    y, s_new = _rwkv_one_step(s_ref[0], r_ref[0], lw_ref[0], k_ref[0], al_ref[0], be_ref[0], column(3), column(4),
                              column(5), gnw_ref[...], gnb_ref[...])
    so_ref[0] = s_new
    mix_ref[0:aw, :] += att.reshape(aw, 1) * own
    mix_ref[aw:, :] += y.reshape(aw, 1) * own


def _mixer_step(col, k_cache, v_cache, slopes, state, r, lw, k2, al, be, gn_w, gn_b):
    nb, P, H, _ = k_cache.shape
    aw = H * HEAD_DIM
    rowv = lambda a: a.reshape(nb, H, 1, HEAD_DIM)
    t_spec = pl.BlockSpec((1, H, HEAD_DIM, P), lambda b: (b, 0, 0, 0))
    s_spec = pl.BlockSpec((1, H, HEAD_DIM, HEAD_DIM), lambda b: (b, 0, 0, 0))
    r_spec = pl.BlockSpec((1, H, 1, HEAD_DIM), lambda b: (b, 0, 0, 0))
    p_spec = _full((H, HEAD_DIM, 1))
    return pl.pallas_call(
        functools.partial(_mixer_step_kernel, P=P, aw=aw),
        grid=(nb,),
        in_specs=[_full(col.shape), t_spec, t_spec, _full((H, 1, 1)), s_spec] + [r_spec] * 5 + [p_spec] * 2,
        out_specs=[_full((2 * aw, LANES)), s_spec],
        out_shape=[jax.ShapeDtypeStruct((2 * aw, LANES), F32), jax.ShapeDtypeStruct(state.shape, F32)],
        compiler_params=_cparams(1),
        name="mixer_step",
    )(col, jnp.transpose(k_cache, (0, 2, 3, 1)), jnp.transpose(v_cache, (0, 2, 3, 1)), slopes.reshape(H, 1, 1),
      state, rowv(r), rowv(lw), rowv(k2), rowv(al), rowv(be), gn_w.reshape(H, HEAD_DIM, 1),
      gn_b.reshape(H, HEAD_DIM, 1))


def _ffn_tail(x1, gate_pre, prev1, prev2, up, cw_ref, cb_ref, w2_ref, g_ref, b_ref, alpha):
    c = cw_ref[0:1, :] * prev2 + cw_ref[1:2, :] * prev1 + cw_ref[2:3, :] * gate_pre + cb_ref[...]
    h = c * _sigmoid(c) * up
    return _layer_norm(alpha * x1 + _dot(h, w2_ref[...]), g_ref[...], b_ref[...])


def _ffn_seq_kernel(att_ref, y_ref, x_ref, wo_ref, g1_ref, b1_ref, w1_ref, cw_ref, cb_ref, w2_ref, g_ref, b_ref,
                    o_ref, tail_ref, carry_ref, *, aw, dff, alpha):
    @pl.when(pl.program_id(1) == 0)
    def _():
        carry_ref[...] = jnp.zeros_like(carry_ref)

    tm = x_ref.shape[1]
    hr = tm // SUB_TILES
    spans = [slice(h * hr, (h + 1) * hr) for h in range(SUB_TILES)]
    n = range(SUB_TILES)
    row = lax.broadcasted_iota(jnp.int32, (hr, dff), 0)
    mixed = [_dot(att_ref[0, sp, :], wo_ref[0:aw, :]) + _dot(y_ref[0, sp, :], wo_ref[aw:, :]) for sp in spans]
    x1 = [_layer_norm(alpha * x_ref[0, spans[i], :] + mixed[i], g1_ref[...], b1_ref[...]) for i in n]
    hu = [_dot(x1[0], w1_ref[...])]
    last2 = carry_ref[0:2, :]
    gated, f = [], []
    for i in n:
        if i + 1 < SUB_TILES:
            hu.append(_dot(x1[i + 1], w1_ref[...]))
        gate_pre = hu[i][:, 0:dff]
        c0, c1 = last2[0:1, :], last2[1:2, :]
        prev1 = jnp.where(row == 0, c1, pltpu.roll(gate_pre, 1, axis=0))
        prev2 = jnp.where(row == 0, c0, jnp.where(row == 1, c1, pltpu.roll(gate_pre, 2, axis=0)))
        last2 = gate_pre[hr - 2:hr, :]
        c = cw_ref[0:1, :] * prev2 + cw_ref[1:2, :] * prev1 + cw_ref[2:3, :] * gate_pre + cb_ref[...]
        gated.append(c * _sigmoid(c) * hu[i][:, dff:])
        if i > 0:
            f.append(_dot(gated[i - 1], w2_ref[...]))
    f.append(_dot(gated[-1], w2_ref[...]))
    carry_ref[0:2, :] = last2
    tail_ref[0] = last2
    for i in n:
        o_ref[0, spans[i], :] = _layer_norm(alpha * x1[i] + f[i], g_ref[...], b_ref[...])


def _ffn_step_kernel(mix_ref, x_ref, wo_ref, g1_ref, b1_ref, p2_ref, p1_ref, w1_ref, cw_ref, cb_ref, w2_ref, g_ref,
                     b_ref, o_ref, gate_ref, *, dff, alpha):
    nb = x_ref.shape[0]
    mixed = mix_ref[...].T[0:nb, :]
    x1 = _layer_norm(alpha * x_ref[...] + _dot(mixed, wo_ref[...]), g1_ref[...], b1_ref[...])
    hu = _dot(x1, w1_ref[...])
    gate_pre = hu[:, 0:dff]
    gate_ref[...] = gate_pre
    o_ref[...] = _ffn_tail(x1, gate_pre, p1_ref[...], p2_ref[...], hu[:, dff:], cw_ref, cb_ref, w2_ref, g_ref,
                           b_ref, alpha)


def _resident(shape):
    nd = len(shape)
    return pl.BlockSpec(shape, lambda *_: (0,) * nd, pipeline_mode=pl.Buffered(1))


def _ffn_seq(att, y_rw, x, w_out, ln1_g, ln1_b, w1, conv_w, conv_b, w2, ln2_g, ln2_b, tm, alpha):
    B, T, D = x.shape
    aw = att.shape[-1]
    dff = w2.shape[0]
    half = pl.BlockSpec((1, tm, aw), lambda b, t: (b, t, 0))
    rows = pl.BlockSpec((1, tm, D), lambda b, t: (b, t, 0))
    return pl.pallas_call(
        functools.partial(_ffn_seq_kernel, aw=aw, dff=dff, alpha=alpha),
        grid=(B, T // tm),
        in_specs=[half, half, rows, _resident(w_out.shape), _full(ln1_g.shape), _full(ln1_b.shape),
                  _resident(w1.shape), _full(conv_w.shape), _full(conv_b.shape), _resident(w2.shape),
                  _full(ln2_g.shape), _full(ln2_b.shape)],
        out_specs=[rows, pl.BlockSpec((1, 2, dff), lambda b, t: (b, 0, 0))],
        out_shape=[jax.ShapeDtypeStruct((B, T, D), F32), jax.ShapeDtypeStruct((B, 2, dff), F32)],
        scratch_shapes=[pltpu.VMEM((8, dff), F32)],
        compiler_params=_cparams(2),
        name="ffn_seq",
    )(att, y_rw, x, w_out, ln1_g, ln1_b, w1, conv_w, conv_b, w2, ln2_g, ln2_b)


def _ffn_step(mix, x, w_out, ln1_g, ln1_b, prev2, prev1, w1, conv_w, conv_b, w2, ln2_g, ln2_b, alpha):
    nb, D = x.shape
    dff = w2.shape[0]
    return pl.pallas_call(
        functools.partial(_ffn_step_kernel, dff=dff, alpha=alpha),
        grid=(1,),
        in_specs=[_full(mix.shape), _full(x.shape), _resident(w_out.shape), _full(ln1_g.shape), _full(ln1_b.shape),
                  _full(prev2.shape), _full(prev1.shape), _resident(w1.shape), _full(conv_w.shape),
                  _full(conv_b.shape), _resident(w2.shape), _full(ln2_g.shape), _full(ln2_b.shape)],
        out_specs=[_full((nb, D)), _full((nb, dff))],
        out_shape=[jax.ShapeDtypeStruct((nb, D), F32), jax.ShapeDtypeStruct((nb, dff), F32)],
        compiler_params=_cparams(1),
        name="ffn_step",
    )(mix, x, w_out, ln1_g, ln1_b, prev2, prev1, w1, conv_w, conv_b, w2, ln2_g, ln2_b)


def _layer_params(w_in, mu_shift, w0, w_lora_up, a0, a_lora_up, g_lora_up, k_k, k_a, r_k, aw):
    D, ncol = w_in.shape
    n_w, n_a, n_g = w_lora_up.shape[0], a_lora_up.shape[0], g_lora_up.shape[0]
    assert n_w + n_a == LANES and 3 * aw + 3 * aw + n_w + n_a + n_g == ncol
    g_cols = -(-n_g // LANES) * LANES
    pad = g_cols - n_g
    row = lambda a: a.reshape(1, -1).astype(F32)
    lane = jnp.arange(aw)
    bd = (lane[:, None] // HEAD_DIM == lane[None, :] // HEAD_DIM).astype(BF16)
    return dict(
        aw=aw,
        w_in=jnp.pad(w_in, ((0, 0), (0, pad))).astype(BF16),
        mu=jnp.pad(row(mu_shift), ((0, 0), (0, pad))),
        w0=row(w0), a0=row(a0), k_k=row(k_k), k_a=row(k_a), r_k=row(r_k),
        wl_pad=jnp.pad(w_lora_up, ((0, n_a), (0, 0))).astype(BF16),
        al_pad=jnp.pad(a_lora_up, ((n_w, 0), (0, 0))).astype(BF16),
        g_pad=jnp.pad(g_lora_up, ((0, pad), (0, 0))).astype(BF16),
        bd=bd,
    )


def kernel(x_prompt, x_sample, cache_k_win, cache_v_win, state_shift, state_wkv, state_conv, w_in, mu_shift, w0, w_lora_up, a0, a_lora_up, g_lora_up, k_k, k_a, r_k, gn_w, gn_b, w_out, ln1_g, ln1_b, w_ffn_in, conv_w, conv_b, w_ffn_out, ln2_g, ln2_b):
    depth = w_in.shape[0]
    alpha = (2.0 * depth) ** 0.25
    B, T, D = x_prompt.shape
    nb = x_sample.shape[0]
    assert x_sample.shape[1] == 1
    n_att = cache_k_win.shape[3]
    n_rw = state_wkv.shape[2]
    aw = n_att * HEAD_DIM
    assert n_rw * HEAD_DIM == aw and T % (Q_TILE * DIL_PATTERNS[-1][1]) == 0
    tm = min(256, T)
    tb = min(256, T)
    heads = jnp.arange(1, n_att + 1, dtype=F32)
    slopes = jnp.exp2(-8.0 * heads / n_att)
    slopes_tab = jnp.broadcast_to(slopes[:, None], (n_att, LANES))
    row = lambda a: a.reshape(1, -1)

    hp, hs = x_prompt, x_sample.reshape(nb, D)
    outs = [[] for _ in range(10)]
    for l in range(depth):
        prm = _layer_params(w_in[l], mu_shift[l], w0[l], w_lora_up[l], a0[l], a_lora_up[l], g_lora_up[l],
                            k_k[l], k_a[l], r_k[l], aw)
        w_out_b = w_out[l].astype(BF16)
        w1_b = w_ffn_in[l].astype(BF16)
        w2_b = w_ffn_out[l].astype(BF16)
        gnw, gnb = row(gn_w[l]), row(gn_b[l])

        q, k, v, kt, vt, r, lw, k2, vr, al, be, g, bonus = _inproj_seq(hp, prm, min(2 * tm, T))
        att = _prompt_attn(q, k, v, slopes_tab)
        y_rw, wkv_p = _rwkv_chunked(r, lw, k2, vr, al, be, g, bonus, gnw, gnb, tb)
        shift_p = hp[:, -1, :]
        hp, conv_p = _ffn_seq(att, y_rw, hp, w_out_b, row(ln1_g[l]), row(ln1_b[l]), w1_b, conv_w[l],
                              row(conv_b[l]), w2_b, row(ln2_g[l]), row(ln2_b[l]), min(2 * tm, T), alpha)

        x2 = jnp.concatenate([hs, state_shift[l]], axis=0)
        ks, vs, r, lw, k2, al, be, col = _inproj_step(x2, prm)
        mix_s, wkv_s = _mixer_step(col, cache_k_win[l], cache_v_win[l], slopes, state_wkv[l], r, lw, k2, al, be,
                                   gn_w[l], gn_b[l])
        shift_s = hs
        hs, gate_s = _ffn_step(mix_s, hs, w_out_b, row(ln1_g[l]), row(ln1_b[l]), state_conv[l][:, 0, :],
                               state_conv[l][:, 1, :], w1_b, conv_w[l], row(conv_b[l]), w2_b, row(ln2_g[l]),
                               row(ln2_b[l]), alpha)
        conv_s = jnp.stack([state_conv[l][:, 1, :], gate_s], axis=1)

        n_keep = min(DIL_PATTERNS[-1][0], T)
        win = lambda a: jnp.transpose(a.reshape(B, n_att, HEAD_DIM, T), (0, 3, 1, 2))[:, T - n_keep:]
        vals = (win(kt), win(vt), ks.reshape(nb, 1, n_att, HEAD_DIM), vs.reshape(nb, 1, n_att, HEAD_DIM),
                shift_p, shift_s, wkv_p, wkv_s, conv_p, conv_s)
        for lst, val in zip(outs, vals):
            lst.append(val)
    return (hp, hs.reshape(nb, 1, D)) + tuple(jnp.stack(lst) for lst in outs)
```

```python
import functools
import math

import jax
import jax.numpy as jnp
from jax import lax
from jax.experimental import pallas as pl
from jax.experimental.pallas import tpu as pltpu

F32 = jnp.float32
BF16 = jnp.bfloat16

HEAD_DIM = 64
LANES = 128
DIL_PATTERNS = ((128, 1), (512, 4), (2048, 16))
WIN_STEPS = 128
Q_TILE = 128
GROUP = 4
SUB_TILES = 4
CHUNK = 64
LN_EPS = 1e-5
GN_EPS = 64e-5
NEG = -1e30
VMEM_LIMIT = 56 * 1024 * 1024

_DN = {"NN": (((1,), (0,)), ((), ())), "NT": (((1,), (1,)), ((), ())), "TN": (((0,), (0,)), ((), ()))}


def _dot(a, b, dims="NN"):
    return lax.dot_general(a.astype(BF16), b.astype(BF16), _DN[dims], preferred_element_type=F32)


def _split2(x):
    hi = x.astype(BF16)
    lo = (x - hi.astype(F32)).astype(BF16)
    return hi, lo


def _dot_const(x, c):
    hi, lo = _split2(x)
    return _dot(hi, c) + _dot(lo, c)


def _dot_const_l(c, x):
    hi = x.astype(BF16)
    r1 = x - hi.astype(F32)
    mid = r1.astype(BF16)
    lo = (r1 - mid.astype(F32)).astype(BF16)
    return _dot(c, hi) + _dot(c, mid) + _dot(c, lo)


def _sigmoid(x):
    return 0.5 * jnp.tanh(0.5 * x) + 0.5


def _layer_norm(x, g, b):
    mu = jnp.mean(x, axis=-1, keepdims=True)
    xc = x - mu
    var = jnp.mean(xc * xc, axis=-1, keepdims=True)
    return xc * lax.rsqrt(var + LN_EPS) * g + b


def _cparams(n_grid):
    return pltpu.CompilerParams(dimension_semantics=("arbitrary",) * n_grid, vmem_limit_bytes=VMEM_LIMIT)


def _full(shape):
    nd = len(shape)
    return pl.BlockSpec(shape, lambda *_: (0,) * nd)


def _rwkv_prep_tiles(rws, w0, a0, k_k, k_a, r_k, wl_pad, al_pad, g_pad, bd, aw):
    n = range(len(rws))
    r = [rw[:, 0:aw] for rw in rws]
    k = [rw[:, aw:2 * aw] for rw in rws]
    v = [rw[:, 2 * aw:3 * aw] for rw in rws]
    lo = [rw[:, 3 * aw:3 * aw + 128] for rw in rws]
    glo = [rw[:, 3 * aw + 128:] for rw in rws]
    u = [w0 + _dot(jnp.tanh(lo[i]), wl_pad) for i in n]
    a = [_sigmoid(a0 + _dot(lo[i], al_pad)) for i in n]
    g = [_dot(_sigmoid(glo[i]), g_pad) for i in n]
    kk = [k[i] * k_k for i in n]
    sumsq = [_dot(kk[i] * kk[i], bd) for i in n]
    lw = [-math.exp(-0.5) * _sigmoid(u[i]) for i in n]
    k2 = [k[i] * (1.0 + (a[i] - 1.0) * k_a) for i in n]
    bonus = [_dot(r[i] * k2[i] * r_k, bd) * v[i] for i in n]
    kkn = [kk[i] * lax.rsqrt(jnp.maximum(sumsq[i], 1e-24)) for i in n]
    return [(r[i], lw[i], k2[i], v[i], -kkn[i], kkn[i] * a[i], g[i], bonus[i]) for i in n]


def _rwkv_prep(rw, *params):
    return _rwkv_prep_tiles([rw], *params)[0]


def _inproj_seq_kernel(x_ref, w_ref, mu_ref, w0_ref, a0_ref, kk_ref, ka_ref, rk_ref, wl_ref, al_ref, gp_ref,
                       bd_ref, q_ref, k_ref, v_ref, kt_ref, vt_ref, r_o, lw_o, k2_o, vr_o, al_o, be_o, g_o, bo_o,
                       carry_ref, *, aw):
    @pl.when(pl.program_id(1) == 0)
    def _():
        carry_ref[...] = jnp.zeros_like(carry_ref)

    tm = x_ref.shape[1]
    hr = tm // SUB_TILES
    spans = [slice(h * hr, (h + 1) * hr) for h in range(SUB_TILES)]
    params = (w0_ref[...], a0_ref[...], kk_ref[...], ka_ref[...], rk_ref[...], wl_ref[...], al_ref[...],
              gp_ref[...], bd_ref[...], aw)
    row = lax.broadcasted_iota(jnp.int32, (hr, w_ref.shape[1] - 3 * aw), 0)

    def finish(p, sp, last):
        q_ref[0, sp, :] = p[:, 0:aw]
        k = p[:, aw:2 * aw]
        v = p[:, 2 * aw:3 * aw]
        k_ref[0, sp, :] = k
        v_ref[0, sp, :] = v
        kt_ref[0, :, sp] = k.T
        vt_ref[0, :, sp] = v.T
        prw = p[:, 3 * aw:]
        prev = jnp.where(row == 0, last, pltpu.roll(prw, 1, axis=0))
        outs = _rwkv_prep(prw + (prev - prw) * mu_ref[...], *params)
        for o_ref, val in zip((r_o, lw_o, k2_o, vr_o, al_o, be_o, g_o, bo_o), outs):
            o_ref[0, sp, :] = val.astype(o_ref.dtype)
        return prw[hr - 1:hr, :]

    last = carry_ref[0:1, :]
    p = _dot(x_ref[0, spans[0], :], w_ref[...])
    for h in range(1, SUB_TILES):
        p_next = _dot(x_ref[0, spans[h], :], w_ref[...])
        last = finish(p, spans[h - 1], last)
        p = p_next
    carry_ref[0:1, :] = finish(p, spans[-1], last)


def _inproj_step_kernel(x_ref, w_ref, mu_ref, w0_ref, a0_ref, kk_ref, ka_ref, rk_ref, wl_ref, al_ref, gp_ref,
                        bd_ref, k_ref, v_ref, r_o, lw_o, k2_o, al_o, be_o, col_o, *, aw, nb):
    p = _dot(x_ref[...], w_ref[...])
    q = p[0:nb, 0:aw]
    k = p[0:nb, aw:2 * aw]
    v = p[0:nb, 2 * aw:3 * aw]
    k_ref[...] = k
    v_ref[...] = v
    prw = p[0:nb, 3 * aw:]
    prev = p[nb:2 * nb, 3 * aw:]
    rw = prw + (prev - prw) * mu_ref[...]
    r, lw, k2, vr, al, be, g, bonus = _rwkv_prep(rw, w0_ref[...], a0_ref[...], kk_ref[...], ka_ref[...],
                                                 rk_ref[...], wl_ref[...], al_ref[...], gp_ref[...], bd_ref[...], aw)
    for o_ref, val in zip((r_o, lw_o, k2_o, al_o, be_o), (r, lw, k2, al, be)):
        o_ref[...] = val
    pad = jnp.zeros((LANES - nb, aw), F32)
    for i, val in enumerate((q, k, v, vr, g, bonus)):
        col_o[i * aw:(i + 1) * aw, :] = jnp.concatenate([val, pad], axis=0).T


_PREP_PARAMS = ("mu", "w0", "a0", "k_k", "k_a", "r_k", "wl_pad", "al_pad", "g_pad", "bd")


def _inproj_seq(x, prm, tm):
    B, T, D = x.shape
    aw = prm["aw"]
    nrw = prm["w_in"].shape[1] - 3 * aw
    small = [prm[n] for n in _PREP_PARAMS]
    row_spec = pl.BlockSpec((1, tm, aw), lambda b, t: (b, t, 0))
    col_spec = pl.BlockSpec((1, aw, tm), lambda b, t: (b, 0, t))
    row_sds = jax.ShapeDtypeStruct((B, T, aw), F32)
    col_sds = jax.ShapeDtypeStruct((B, aw, T), F32)
    return pl.pallas_call(
        functools.partial(_inproj_seq_kernel, aw=aw),
        grid=(B, T // tm),
        in_specs=[pl.BlockSpec((1, tm, D), lambda b, t: (b, t, 0)), _resident(prm["w_in"].shape)]
        + [_full(s.shape) for s in small],
        out_specs=[row_spec] * 3 + [col_spec] * 2 + [row_spec] * 8,
        out_shape=[row_sds] * 3 + [col_sds] * 2
        + [jax.ShapeDtypeStruct((B, T, aw), F32 if n == "lw" else BF16)
           for n in ("r", "lw", "k2", "v", "al", "be", "g", "bonus")],
        scratch_shapes=[pltpu.VMEM((8, nrw), F32)],
        compiler_params=_cparams(2),
        name="inproj_seq",
    )(x, prm["w_in"], *small)


def _inproj_step(x2, prm):
    nb = x2.shape[0] // 2
    aw = prm["aw"]
    small = [prm[n] for n in _PREP_PARAMS]
    assert nb <= LANES
    out_sds = jax.ShapeDtypeStruct((nb, aw), F32)
    return pl.pallas_call(
        functools.partial(_inproj_step_kernel, aw=aw, nb=nb),
        grid=(1,),
        in_specs=[_full(x2.shape), _full(prm["w_in"].shape)] + [_full(s.shape) for s in small],
        out_specs=[_full((nb, aw))] * 7 + [_full((6 * aw, LANES))],
        out_shape=[out_sds] * 7 + [jax.ShapeDtypeStruct((6 * aw, LANES), F32)],
        compiler_params=_cparams(1),
        name="inproj_step",
    )(x2, prm["w_in"], *small)


def _prompt_attn_kernel(q_ref, k_ref, v_ref, sl_ref, o_ref, acc_s, m_s, l_s, yq, yk, yv, yacc, ym, yl, *, T):
    hp = pl.program_id(1)
    scale = HEAD_DIM ** -0.5
    lane = lax.broadcasted_iota(jnp.int32, (1, LANES), 1)
    head_masks = ((lane < HEAD_DIM).astype(F32), (lane >= HEAD_DIM).astype(F32))
    qi = lax.broadcasted_iota(jnp.int32, (Q_TILE, Q_TILE), 0)
    ki = lax.broadcasted_iota(jnp.int32, (Q_TILE, Q_TILE), 1)
    steps_cur = (qi - ki).astype(F32)
    steps_prev = (qi - ki + Q_TILE).astype(F32)
    ok_cur = ki <= qi
    ok_prev = ki >= qi
    first_lanes = lax.broadcasted_iota(jnp.int32, (Q_TILE, LANES), 1) < HEAD_DIM
    slopes = [sl_ref[pl.ds(2 * hp + j, 1), :] for j in range(2)]

    def tile_group(src, tiles, bias, stats, first_branch):
        rq, rk, rv = src
        bias_c, bias_pc = bias
        acc_s, m_s, l_s = stats
        ids = range(len(tiles))
        q = [rq(rows) * scale for rows, _ in tiles]
        kk = [rk(rows) if prev is None else jnp.concatenate([rk(prev), rk(rows)], axis=0) for rows, prev in tiles]
        vv = [rv(rows) if prev is None else jnp.concatenate([rv(prev), rv(rows)], axis=0) for rows, prev in tiles]
        qq = [jnp.concatenate([q[t] * head_masks[0], q[t] * head_masks[1]], axis=0) for t in ids]
        s = [_dot(qq[t], kk[t], "NT") + (bias_c if tiles[t][1] is None else bias_pc) for t in ids]
        m = [jnp.max(s[t], axis=-1, keepdims=True) for t in ids]
        p = [jnp.exp(s[t] - m[t]) for t in ids]
        l = [jnp.sum(p[t], axis=-1, keepdims=True) for t in ids]
        acc = [_dot(p[t], vv[t]) for t in ids]
        for t, (rows, _) in enumerate(tiles):
            acc_n = jnp.where(first_lanes, acc[t][0:Q_TILE], acc[t][Q_TILE:])
            m_n = jnp.where(first_lanes, m[t][0:Q_TILE], m[t][Q_TILE:])
            l_n = jnp.where(first_lanes, l[t][0:Q_TILE], l[t][Q_TILE:])
            if first_branch:
                acc_s[rows, :] = acc_n
                m_s[rows, :] = m_n
                l_s[rows, :] = l_n
            else:
                m_o = m_s[rows, :]
                m_new = jnp.maximum(m_o, m_n)
                e_o = jnp.exp(m_o - m_new)
                e_n = jnp.exp(m_n - m_new)
                acc_s[rows, :] = acc_s[rows, :] * e_o + acc_n * e_n
                l_s[rows, :] = l_s[rows, :] * e_o + l_n * e_n
                m_s[rows, :] = m_new

    dils = [dil for _, dil in DIL_PATTERNS]
    R = dils[1]
    assert dils == [1, R, R * R] and all(win // dil == WIN_STEPS for win, dil in DIL_PATTERNS)

    def bias_of(dil):
        b_cur = [jnp.where(ok_cur, -(slopes[j] * float(dil)) * steps_cur, NEG) for j in range(2)]
        b_prev = [jnp.where(ok_prev, -(slopes[j] * float(dil)) * steps_prev, NEG) for j in range(2)]
        return (jnp.concatenate(b_cur, axis=0),
                jnp.concatenate([jnp.concatenate([b_prev[j], b_cur[j]], axis=1) for j in range(2)], axis=0))

    def groups(tiles):
        return [tiles[i:i + GROUP] for i in range(0, len(tiles), GROUP)]

    natural = tuple((lambda rows, ref=ref: ref[0, rows, :]) for ref in (q_ref, k_ref, v_ref))
    scratch = tuple((lambda rows, ref=ref: ref[rows, :]) for ref in (yq, yk, yv))
    nat_stats = (acc_s, m_s, l_s)
    y_stats = (yacc, ym, yl)

    nb1 = T // Q_TILE
    assert nb1 % GROUP == 0
    bias1 = bias_of(dils[0])
    tile_group(natural, [(pl.ds(jb * Q_TILE, Q_TILE), pl.ds((jb - 1) * Q_TILE, Q_TILE) if jb else None)
                         for jb in range(GROUP)], bias1, nat_stats, True)

    def later(g, carry):
        start = pl.multiple_of(g * (GROUP * Q_TILE), GROUP * Q_TILE)
        tile_group(natural, [(pl.ds(start + i * Q_TILE, Q_TILE), pl.ds(start + (i - 1) * Q_TILE, Q_TILE))
                             for i in range(GROUP)], bias1, nat_stats, True)
        return carry

    lax.fori_loop(1, nb1 // GROUP, later, 0)

    LR = T // R
    bias_r = bias_of(dils[1])
    bias_rr = bias_of(dils[2])

    def stream(s, carry):
        own = pl.ds(s, LR, stride=R)
        yq[...] = q_ref[0, own, :]
        yk[...] = k_ref[0, own, :]
        yv[...] = v_ref[0, own, :]
        for grp in groups([(pl.ds(jb * Q_TILE, Q_TILE), pl.ds((jb - 1) * Q_TILE, Q_TILE) if jb else None)
                           for jb in range(LR // Q_TILE)]):
            tile_group(scratch, grp, bias_r, y_stats, True)
        for grp in groups([(pl.ds(c + R * Q_TILE * jb, Q_TILE, stride=R),
                            pl.ds(c + R * Q_TILE * (jb - 1), Q_TILE, stride=R) if jb else None)
                           for c in range(R) for jb in range(LR // R // Q_TILE)]):
            tile_group(scratch, grp, bias_rr, y_stats, False)
        m_a, m_b = m_s[own, :], ym[...]
        m_new = jnp.maximum(m_a, m_b)
        e_a = jnp.exp(m_a - m_new)
        e_b = jnp.exp(m_b - m_new)
        acc_s[own, :] = (acc_s[own, :] * e_a + yacc[...] * e_b) / (l_s[own, :] * e_a + yl[...] * e_b)
        return carry

    lax.fori_loop(0, R, stream, 0)
    o_ref[0] = acc_s[...].astype(o_ref.dtype)


def _prompt_attn(q, k, v, slopes_tab):
    B, T, W = q.shape
    spec = pl.BlockSpec((1, T, LANES), lambda b, hp: (b, 0, hp))
    return pl.pallas_call(
        functools.partial(_prompt_attn_kernel, T=T),
        grid=(B, W // LANES),
        in_specs=[spec, spec, spec, _full(slopes_tab.shape)],
        out_specs=spec,
        out_shape=jax.ShapeDtypeStruct((B, T, W), BF16),
        scratch_shapes=[pltpu.VMEM((T, LANES), F32)] * 3 + [pltpu.VMEM((T // DIL_PATTERNS[1][1], LANES), F32)] * 6,
        compiler_params=_cparams(2),
        name="prompt_attn",
    )(q, k, v, slopes_tab)


def _rwkv_chunk_kernel(r_ref, lw_ref, k_ref, v_ref, al_ref, be_ref, g_ref, bo_ref, gnw_ref, gnb_ref,
                       y_ref, s_ref, st_scr, *, Tb, npair, nbb):
    C = CHUNK
    n2 = 2 * C
    t = pl.program_id(1)

    @pl.when(t == 0)
    def _():
        st_scr[...] = jnp.zeros_like(st_scr)

    lane = lax.broadcasted_iota(jnp.int32, (1, LANES), 1)
    m0 = (lane < HEAD_DIM).astype(F32)
    m1 = 1.0 - m0
    ri = lax.broadcasted_iota(jnp.int32, (n2, n2), 0)
    ci = lax.broadcasted_iota(jnp.int32, (n2, n2), 1)
    same = (ri < C) == (ci < C)
    strict = same & ((ri & (C - 1)) > (ci & (C - 1)))
    incl = same & ((ri & (C - 1)) >= (ci & (C - 1)))
    tri = (lax.broadcasted_iota(jnp.int32, (C, C), 0) >= lax.broadcasted_iota(jnp.int32, (C, C), 1)).astype(BF16)
    head_rows = ((lax.broadcasted_iota(jnp.int32, (n2, LANES), 0) < C)
                 == (lax.broadcasted_iota(jnp.int32, (n2, LANES), 1) < HEAD_DIM)).astype(F32)

    def stack(x):
        return jnp.concatenate([x * m0, x * m1], axis=0)

    def twice(x):
        return jnp.concatenate([x, x], axis=0)

    def body(c, carry):
        rows = pl.ds(pl.multiple_of(c * C, C), C)
        chains = [(bb, slice(hp * LANES, (hp + 1) * LANES)) for bb in range(nbb) for hp in range(npair)]
        ids = range(len(chains))
        S = [st_scr[i] for i in ids]
        lw_all = jnp.concatenate([lw_ref[bb, rows, :] for bb in range(nbb)], axis=1)
        cl_all = _dot_const_l(tri, lw_all)
        e_pos_all = jnp.exp(cl_all)
        e_neg_all = jnp.exp(-cl_all)
        e_exc_all = jnp.exp(cl_all - lw_all)
        wide = [slice(i * LANES, (i + 1) * LANES) for i in ids]
        g_end = [e_pos_all[C - 1:C, wide[i]] for i in ids]
        rt = [r_ref[bb, rows, cs] * e_pos_all[:, wide[i]] for i, (bb, cs) in enumerate(chains)]
        kt = [k_ref[bb, rows, cs] * e_neg_all[:, wide[i]] for i, (bb, cs) in enumerate(chains)]
        bt = [be_ref[bb, rows, cs] * e_neg_all[:, wide[i]] for i, (bb, cs) in enumerate(chains)]
        at = [al_ref[bb, rows, cs] * e_exc_all[:, wide[i]] for i, (bb, cs) in enumerate(chains)]
        ar_st = [jnp.concatenate([stack(at[i]), stack(rt[i])], axis=0) for i in ids]
        kb2 = [jnp.concatenate([twice(kt[i]), twice(bt[i])], axis=0) for i in ids]
        v_st = [stack(v_ref[bb, rows, cs]) for bb, cs in chains]
        gram = [_dot(ar_st[i], kb2[i], "NT") for i in ids]
        l_ab = [jnp.where(strict, gram[i][0:n2, n2:], 0.0) for i in ids]
        l_kk = [jnp.concatenate([jnp.where(strict, gram[i][0:n2, 0:n2], 0.0),
                                 jnp.where(incl, gram[i][n2:, 0:n2], 0.0)], axis=0) for i in ids]
        l_rb = [jnp.where(incl, gram[i][n2:, n2:], 0.0) for i in ids]
        from_state = [_dot(jnp.concatenate([at[i], rt[i]], axis=0), S[i], "NT") for i in ids]
        from_state = [jnp.concatenate([stack(from_state[i][0:C]), stack(from_state[i][C:])], axis=0) for i in ids]
        from_v = [_dot(l_kk[i], v_st[i]) for i in ids]
        x = l_ab
        u = [from_state[i][0:n2] + from_v[i][0:n2] for i in ids]
        for _ in range(C.bit_length() - 2):
            xu = [_dot(x[i], jnp.concatenate([x[i], u[i]], axis=1)) for i in ids]
            x = [xu[i][:, 0:n2] for i in ids]
            u = [u[i] + xu[i][:, n2:] for i in ids]
        u = [u[i] + _dot(x[i], u[i]) for i in ids]
        y_st = [from_state[i][n2:] + from_v[i][n2:] + _dot(l_rb[i], u[i]) for i in ids]
        for i in ids:
            vu = jnp.concatenate([v_st[i], u[i]], axis=0)
            kb_end = jnp.concatenate([stack(kt[i] * g_end[i]), stack(bt[i] * g_end[i])], axis=0)
            st_scr[i] = S[i] * g_end[i] + _dot(vu, kb_end, "TN")
        for i, (bb, cs) in enumerate(chains):
            mu = jnp.sum(y_st[i], axis=-1, keepdims=True) * (1.0 / HEAD_DIM)
            yc = (y_st[i] - mu) * head_rows
            var = jnp.sum(yc * yc, axis=-1, keepdims=True) * (1.0 / HEAD_DIM)
            yn_st = yc * lax.rsqrt(var + GN_EPS)
            yn_i = (yn_st[0:C, :] + yn_st[C:n2, :]) * gnw_ref[:, cs] + gnb_ref[:, cs]
            y_ref[bb, rows, cs] = ((yn_i + bo_ref[bb, rows, cs]) * g_ref[bb, rows, cs]).astype(y_ref.dtype)
        return carry

    lax.fori_loop(0, Tb // C, body, 0)

    @pl.when(t == pl.num_programs(1) - 1)
    def _():
        for i in range(nbb * npair):
            bb, hp = divmod(i, npair)
            pair = st_scr[i]
            s_ref[bb, 2 * hp] = pair[0:HEAD_DIM, 0:HEAD_DIM]
            s_ref[bb, 2 * hp + 1] = pltpu.roll(pair, HEAD_DIM, axis=1)[HEAD_DIM:, 0:HEAD_DIM]


def _rwkv_chunked(r, lw, k2, v, al, be, g, bonus, gn_w, gn_b, tb):
    B, T, W = r.shape
    npair = W // LANES
    nbb = next(n for n in (4, 2, 1) if B % n == 0)
    spec = pl.BlockSpec((nbb, tb, W), lambda b, t: (b, t, 0))
    st_spec = pl.BlockSpec((nbb, 2 * npair, HEAD_DIM, HEAD_DIM), lambda b, t: (b, 0, 0, 0))
    return pl.pallas_call(
        functools.partial(_rwkv_chunk_kernel, Tb=tb, npair=npair, nbb=nbb),
        grid=(B // nbb, T // tb),
        in_specs=[spec] * 8 + [_full(gn_w.shape), _full(gn_b.shape)],
        out_specs=[spec, st_spec],
        out_shape=[jax.ShapeDtypeStruct((B, T, W), BF16),
                   jax.ShapeDtypeStruct((B, 2 * npair, HEAD_DIM, HEAD_DIM), F32)],
        scratch_shapes=[pltpu.VMEM((nbb * npair, LANES, LANES), F32)],
        compiler_params=_cparams(2),
        name="rwkv_chunk",
    )(r, lw, k2, v, al, be, g, bonus, gn_w, gn_b)


def _rwkv_one_step(S, r, lw, k, al, be, v, g, bonus, gnw, gnb):
    sa = jnp.sum(S * al, axis=-1, keepdims=True)
    s_new = S * jnp.exp(lw) + sa * be + v * k
    y = jnp.sum(s_new * r, axis=-1, keepdims=True)
    mu = jnp.mean(y, axis=1, keepdims=True)
    yc = y - mu
    var = jnp.mean(yc * yc, axis=1, keepdims=True)
    yn = yc * lax.rsqrt(var + GN_EPS) * gnw + gnb
    return (yn + bonus) * g, s_new


def _attn_one_step(q, k_new, v_new, kt, vt, slopes, P):
    scale = HEAD_DIM ** -0.5
    dist = P - lax.broadcasted_iota(jnp.int32, (1, 1, P), 2)
    count = jnp.zeros((1, 1, P), F32)
    for win, dil in DIL_PATTERNS:
        assert dil & (dil - 1) == 0
        count = count + ((dist <= win) & ((dist & (dil - 1)) == 0)).astype(F32)
    s = jnp.sum(kt * q, axis=1, keepdims=True) * scale - slopes * dist.astype(F32)
    s = jnp.where(count > 0.0, s, NEG)
    s0 = jnp.sum(k_new * q, axis=1, keepdims=True) * scale
    m = jnp.maximum(jnp.max(s, axis=-1, keepdims=True), s0)
    p = count * jnp.exp(s - m)
    e0 = len(DIL_PATTERNS) * jnp.exp(s0 - m)
    l = jnp.sum(p, axis=-1, keepdims=True) + e0
    acc = jnp.sum(vt * p, axis=-1, keepdims=True) + e0 * v_new
    return acc / l


def _mixer_step_kernel(col_ref, kt_ref, vt_ref, sl_ref, s_ref, r_ref, lw_ref, k_ref, al_ref, be_ref, gnw_ref,
                       gnb_ref, mix_ref, so_ref, *, P, aw):
    b = pl.program_id(0)
    H = aw // HEAD_DIM
    own = (lax.broadcasted_iota(jnp.int32, (1, LANES), 1) == b).astype(F32)

    def column(i):
        return jnp.sum(col_ref[i * aw:(i + 1) * aw, :] * own, axis=-1, keepdims=True).reshape(H, HEAD_DIM, 1)

    @pl.when(b == 0)
    def _():
        mix_ref[...] = jnp.zeros_like(mix_ref)

    att = _attn_one_step(column(0), column(1), column(2), kt_ref[0], vt_ref[0], sl_ref[...], P)
    y, s_new = _rwkv_one_step(s_ref[0], r_ref[0], lw_ref[0], k_ref[0], al_ref[0], be_ref[0], column(3), column(4),
                              column(5), gnw_ref[...], gnb_ref[...])
    so_ref[0] = s_new
    mix_ref[0:aw, :] += att.reshape(aw, 1) * own
    mix_ref[aw:, :] += y.reshape(aw, 1) * own


def _mixer_step(col, k_cache, v_cache, slopes, state, r, lw, k2, al, be, gn_w, gn_b):
    nb, P, H, _ = k_cache.shape
    aw = H * HEAD_DIM
    rowv = lambda a: a.reshape(nb, H, 1, HEAD_DIM)
    t_spec = pl.BlockSpec((1, H, HEAD_DIM, P), lambda b: (b, 0, 0, 0))
    s_spec = pl.BlockSpec((1, H, HEAD_DIM, HEAD_DIM), lambda b: (b, 0, 0, 0))
    r_spec = pl.BlockSpec((1, H, 1, HEAD_DIM), lambda b: (b, 0, 0, 0))
    p_spec = _full((H, HEAD_DIM, 1))
    return pl.pallas_call(
        functools.partial(_mixer_step_kernel, P=P, aw=aw),
        grid=(nb,),
        in_specs=[_full(col.shape), t_spec, t_spec, _full((H, 1, 1)), s_spec] + [r_spec] * 5 + [p_spec] * 2,
        out_specs=[_full((2 * aw, LANES)), s_spec],
        out_shape=[jax.ShapeDtypeStruct((2 * aw, LANES), F32), jax.ShapeDtypeStruct(state.shape, F32)],
        compiler_params=_cparams(1),
        name="mixer_step",
    )(col, jnp.transpose(k_cache, (0, 2, 3, 1)), jnp.transpose(v_cache, (0, 2, 3, 1)), slopes.reshape(H, 1, 1),
      state, rowv(r), rowv(lw), rowv(k2), rowv(al), rowv(be), gn_w.reshape(H, HEAD_DIM, 1),
      gn_b.reshape(H, HEAD_DIM, 1))


def _ffn_tail(x1, gate_pre, prev1, prev2, up, cw_ref, cb_ref, w2_ref, g_ref, b_ref, alpha):
    c = cw_ref[0:1, :] * prev2 + cw_ref[1:2, :] * prev1 + cw_ref[2:3, :] * gate_pre + cb_ref[...]
    h = c * _sigmoid(c) * up
    return _layer_norm(alpha * x1 + _dot(h, w2_ref[...]), g_ref[...], b_ref[...])


def _ffn_seq_kernel(att_ref, y_ref, x_ref, wo_ref, g1_ref, b1_ref, w1_ref, cw_ref, cb_ref, w2_ref, g_ref, b_ref,
                    o_ref, tail_ref, carry_ref, *, aw, dff, alpha):
    @pl.when(pl.program_id(1) == 0)
    def _():
        carry_ref[...] = jnp.zeros_like(carry_ref)

    tm = x_ref.shape[1]
    hr = tm // SUB_TILES
    spans = [slice(h * hr, (h + 1) * hr) for h in range(SUB_TILES)]
    n = range(SUB_TILES)
    row = lax.broadcasted_iota(jnp.int32, (hr, dff), 0)
    mixed = [_dot(att_ref[0, sp, :], wo_ref[0:aw, :]) + _dot(y_ref[0, sp, :], wo_ref[aw:, :]) for sp in spans]
    x1 = [_layer_norm(alpha * x_ref[0, spans[i], :] + mixed[i], g1_ref[...], b1_ref[...]) for i in n]
    hu = [_dot(x1[0], w1_ref[...])]
    last2 = carry_ref[0:2, :]
    gated, f = [], []
    for i in n:
        if i + 1 < SUB_TILES:
            hu.append(_dot(x1[i + 1], w1_ref[...]))
        gate_pre = hu[i][:, 0:dff]
        c0, c1 = last2[0:1, :], last2[1:2, :]
        prev1 = jnp.where(row == 0, c1, pltpu.roll(gate_pre, 1, axis=0))
        prev2 = jnp.where(row == 0, c0, jnp.where(row == 1, c1, pltpu.roll(gate_pre, 2, axis=0)))
        last2 = gate_pre[hr - 2:hr, :]
        c = cw_ref[0:1, :] * prev2 + cw_ref[1:2, :] * prev1 + cw_ref[2:3, :] * gate_pre + cb_ref[...]
        gated.append(c * _sigmoid(c) * hu[i][:, dff:])
        if i > 0:
            f.append(_dot(gated[i - 1], w2_ref[...]))
    f.append(_dot(gated[-1], w2_ref[...]))
    carry_ref[0:2, :] = last2
    tail_ref[0] = last2
    for i in n:
        o_ref[0, spans[i], :] = _layer_norm(alpha * x1[i] + f[i], g_ref[...], b_ref[...])


def _ffn_step_kernel(mix_ref, x_ref, wo_ref, g1_ref, b1_ref, p2_ref, p1_ref, w1_ref, cw_ref, cb_ref, w2_ref, g_ref,
                     b_ref, o_ref, gate_ref, *, dff, alpha):
    nb = x_ref.shape[0]
    mixed = mix_ref[...].T[0:nb, :]
    x1 = _layer_norm(alpha * x_ref[...] + _dot(mixed, wo_ref[...]), g1_ref[...], b1_ref[...])
    hu = _dot(x1, w1_ref[...])
    gate_pre = hu[:, 0:dff]
    gate_ref[...] = gate_pre
    o_ref[...] = _ffn_tail(x1, gate_pre, p1_ref[...], p2_ref[...], hu[:, dff:], cw_ref, cb_ref, w2_ref, g_ref,
                           b_ref, alpha)


def _resident(shape):
    nd = len(shape)
    return pl.BlockSpec(shape, lambda *_: (0,) * nd, pipeline_mode=pl.Buffered(1))


def _ffn_seq(att, y_rw, x, w_out, ln1_g, ln1_b, w1, conv_w, conv_b, w2, ln2_g, ln2_b, tm, alpha):
    B, T, D = x.shape
    aw = att.shape[-1]
    dff = w2.shape[0]
    half = pl.BlockSpec((1, tm, aw), lambda b, t: (b, t, 0))
    rows = pl.BlockSpec((1, tm, D), lambda b, t: (b, t, 0))
    return pl.pallas_call(
        functools.partial(_ffn_seq_kernel, aw=aw, dff=dff, alpha=alpha),
        grid=(B, T // tm),
        in_specs=[half, half, rows, _resident(w_out.shape), _full(ln1_g.shape), _full(ln1_b.shape),
                  _resident(w1.shape), _full(conv_w.shape), _full(conv_b.shape), _resident(w2.shape),
                  _full(ln2_g.shape), _full(ln2_b.shape)],
        out_specs=[rows, pl.BlockSpec((1, 2, dff), lambda b, t: (b, 0, 0))],
        out_shape=[jax.ShapeDtypeStruct((B, T, D), F32), jax.ShapeDtypeStruct((B, 2, dff), F32)],
        scratch_shapes=[pltpu.VMEM((8, dff), F32)],
        compiler_params=_cparams(2),
        name="ffn_seq",
    )(att, y_rw, x, w_out, ln1_g, ln1_b, w1, conv_w, conv_b, w2, ln2_g, ln2_b)


def _ffn_step(mix, x, w_out, ln1_g, ln1_b, prev2, prev1, w1, conv_w, conv_b, w2, ln2_g, ln2_b, alpha):
    nb, D = x.shape
    dff = w2.shape[0]
    return pl.pallas_call(
        functools.partial(_ffn_step_kernel, dff=dff, alpha=alpha),
        grid=(1,),
        in_specs=[_full(mix.shape), _full(x.shape), _resident(w_out.shape), _full(ln1_g.shape), _full(ln1_b.shape),
                  _full(prev2.shape), _full(prev1.shape), _resident(w1.shape), _full(conv_w.shape),
                  _full(conv_b.shape), _resident(w2.shape), _full(ln2_g.shape), _full(ln2_b.shape)],
        out_specs=[_full((nb, D)), _full((nb, dff))],
        out_shape=[jax.ShapeDtypeStruct((nb, D), F32), jax.ShapeDtypeStruct((nb, dff), F32)],
        compiler_params=_cparams(1),
        name="ffn_step",
    )(mix, x, w_out, ln1_g, ln1_b, prev2, prev1, w1, conv_w, conv_b, w2, ln2_g, ln2_b)


def _layer_params(w_in, mu_shift, w0, w_lora_up, a0, a_lora_up, g_lora_up, k_k, k_a, r_k, aw):
    D, ncol = w_in.shape
    n_w, n_a, n_g = w_lora_up.shape[0], a_lora_up.shape[0], g_lora_up.shape[0]
    assert n_w + n_a == LANES and 3 * aw + 3 * aw + n_w + n_a + n_g == ncol
    g_cols = -(-n_g // LANES) * LANES
    pad = g_cols - n_g
    row = lambda a: a.reshape(1, -1).astype(F32)
    lane = jnp.arange(aw)
    bd = (lane[:, None] // HEAD_DIM == lane[None, :] // HEAD_DIM).astype(BF16)
    return dict(
        aw=aw,
        w_in=jnp.pad(w_in, ((0, 0), (0, pad))).astype(BF16),
        mu=jnp.pad(row(mu_shift), ((0, 0), (0, pad))),
        w0=row(w0), a0=row(a0), k_k=row(k_k), k_a=row(k_a), r_k=row(r_k),
        wl_pad=jnp.pad(w_lora_up, ((0, n_a), (0, 0))).astype(BF16),
        al_pad=jnp.pad(a_lora_up, ((n_w, 0), (0, 0))).astype(BF16),
        g_pad=jnp.pad(g_lora_up, ((0, pad), (0, 0))).astype(BF16),
        bd=bd,
    )


def kernel(x_prompt, x_sample, cache_k_win, cache_v_win, state_shift, state_wkv, state_conv, w_in, mu_shift, w0, w_lora_up, a0, a_lora_up, g_lora_up, k_k, k_a, r_k, gn_w, gn_b, w_out, ln1_g, ln1_b, w_ffn_in, conv_w, conv_b, w_ffn_out, ln2_g, ln2_b):
    depth = w_in.shape[0]
    alpha = (2.0 * depth) ** 0.25
    B, T, D = x_prompt.shape
    nb = x_sample.shape[0]
    assert x_sample.shape[1] == 1
    n_att = cache_k_win.shape[3]
    n_rw = state_wkv.shape[2]
    aw = n_att * HEAD_DIM
    assert n_rw * HEAD_DIM == aw and T % (Q_TILE * DIL_PATTERNS[-1][1]) == 0
    tm = min(256, T)
    tb = min(256, T)
    heads = jnp.arange(1, n_att + 1, dtype=F32)
    slopes = jnp.exp2(-8.0 * heads / n_att)
    slopes_tab = jnp.broadcast_to(slopes[:, None], (n_att, LANES))
    row = lambda a: a.reshape(1, -1)

    hp, hs = x_prompt, x_sample.reshape(nb, D)
    outs = [[] for _ in range(10)]
    for l in range(depth):
        prm = _layer_params(w_in[l], mu_shift[l], w0[l], w_lora_up[l], a0[l], a_lora_up[l], g_lora_up[l],
                            k_k[l], k_a[l], r_k[l], aw)
        w_out_b = w_out[l].astype(BF16)
        w1_b = w_ffn_in[l].astype(BF16)
        w2_b = w_ffn_out[l].astype(BF16)
        gnw, gnb = row(gn_w[l]), row(gn_b[l])

        q, k, v, kt, vt, r, lw, k2, vr, al, be, g, bonus = _inproj_seq(hp, prm, min(2 * tm, T))
        att = _prompt_attn(q, k, v, slopes_tab)
        y_rw, wkv_p = _rwkv_chunked(r, lw, k2, vr, al, be, g, bonus, gnw, gnb, tb)
        shift_p = hp[:, -1, :]
        hp, conv_p = _ffn_seq(att, y_rw, hp, w_out_b, row(ln1_g[l]), row(ln1_b[l]), w1_b, conv_w[l],
                              row(conv_b[l]), w2_b, row(ln2_g[l]), row(ln2_b[l]), min(2 * tm, T), alpha)

        x2 = jnp.concatenate([hs, state_shift[l]], axis=0)
        ks, vs, r, lw, k2, al, be, col = _inproj_step(x2, prm)
        mix_s, wkv_s = _mixer_step(col, cache_k_win[l], cache_v_win[l], slopes, state_wkv[l], r, lw, k2, al, be,
                                   gn_w[l], gn_b[l])
        shift_s = hs
        hs, gate_s = _ffn_step(mix_s, hs, w_out_b, row(ln1_g[l]), row(ln1_b[l]), state_conv[l][:, 0, :],
                               state_conv[l][:, 1, :], w1_b, conv_w[l], row(conv_b[l]), w2_b, row(ln2_g[l]),
                               row(ln2_b[l]), alpha)
        conv_s = jnp.stack([state_conv[l][:, 1, :], gate_s], axis=1)

        n_keep = min(DIL_PATTERNS[-1][0], T)
        win = lambda a: jnp.transpose(a.reshape(B, n_att, HEAD_DIM, T), (0, 3, 1, 2))[:, T - n_keep:]
        vals = (win(kt), win(vt), ks.reshape(nb, 1, n_att, HEAD_DIM), vs.reshape(nb, 1, n_att, HEAD_DIM),
                shift_p, shift_s, wkv_p, wkv_s, conv_p, conv_s)
        for lst, val in zip(outs, vals):
            lst.append(val)
    return (hp, hs.reshape(nb, 1, D)) + tuple(jnp.stack(lst) for lst in outs)
```

```python
import functools
import math

import jax
import jax.numpy as jnp
from jax import lax
from jax.experimental import pallas as pl
from jax.experimental.pallas import tpu as pltpu

F32 = jnp.float32
BF16 = jnp.bfloat16

HEAD_DIM = 64
LANES = 128
DIL_PATTERNS = ((128, 1), (512, 4), (2048, 16))
WIN_STEPS = 128
Q_TILE = 128
GROUP = 4
SUB_TILES = 4
CHUNK = 64
LN_EPS = 1e-5
GN_EPS = 64e-5
NEG = -1e30
VMEM_LIMIT = 56 * 1024 * 1024

_DN = {"NN": (((1,), (0,)), ((), ())), "NT": (((1,), (1,)), ((), ())), "TN": (((0,), (0,)), ((), ()))}


def _dot(a, b, dims="NN"):
    return lax.dot_general(a.astype(BF16), b.astype(BF16), _DN[dims], preferred_element_type=F32)


def _split2(x):
    hi = x.astype(BF16)
    lo = (x - hi.astype(F32)).astype(BF16)
    return hi, lo


def _dot_const(x, c):
    hi, lo = _split2(x)
    return _dot(hi, c) + _dot(lo, c)


def _dot_const_l(c, x):
    hi = x.astype(BF16)
    r1 = x - hi.astype(F32)
    mid = r1.astype(BF16)
    lo = (r1 - mid.astype(F32)).astype(BF16)
    return _dot(c, hi) + _dot(c, mid) + _dot(c, lo)


def _sigmoid(x):
    return 0.5 * jnp.tanh(0.5 * x) + 0.5


def _layer_norm(x, g, b):
    mu = jnp.mean(x, axis=-1, keepdims=True)
    xc = x - mu
    var = jnp.mean(xc * xc, axis=-1, keepdims=True)
    return xc * lax.rsqrt(var + LN_EPS) * g + b


def _cparams(n_grid):
    return pltpu.CompilerParams(dimension_semantics=("arbitrary",) * n_grid, vmem_limit_bytes=VMEM_LIMIT)


def _full(shape):
    nd = len(shape)
    return pl.BlockSpec(shape, lambda *_: (0,) * nd)


def _rwkv_prep_tiles(rws, w0, a0, k_k, k_a, r_k, wl_pad, al_pad, g_pad, bd, aw):
    n = range(len(rws))
    r = [rw[:, 0:aw] for rw in rws]
    k = [rw[:, aw:2 * aw] for rw in rws]
    v = [rw[:, 2 * aw:3 * aw] for rw in rws]
    lo = [rw[:, 3 * aw:3 * aw + 128] for rw in rws]
    glo = [rw[:, 3 * aw + 128:] for rw in rws]
    u = [w0 + _dot(jnp.tanh(lo[i]), wl_pad) for i in n]
    a = [_sigmoid(a0 + _dot(lo[i], al_pad)) for i in n]
    g = [_dot(_sigmoid(glo[i]), g_pad) for i in n]
    kk = [k[i] * k_k for i in n]
    sumsq = [_dot(kk[i] * kk[i], bd) for i in n]
    lw = [-math.exp(-0.5) * _sigmoid(u[i]) for i in n]
    k2 = [k[i] * (1.0 + (a[i] - 1.0) * k_a) for i in n]
    bonus = [_dot(r[i] * k2[i] * r_k, bd) * v[i] for i in n]
    kkn = [kk[i] * lax.rsqrt(jnp.maximum(sumsq[i], 1e-24)) for i in n]
    return [(r[i], lw[i], k2[i], v[i], -kkn[i], kkn[i] * a[i], g[i], bonus[i]) for i in n]


def _rwkv_prep(rw, *params):
    return _rwkv_prep_tiles([rw], *params)[0]


def _inproj_seq_kernel(x_ref, w_ref, mu_ref, w0_ref, a0_ref, kk_ref, ka_ref, rk_ref, wl_ref, al_ref, gp_ref,
                       bd_ref, q_ref, k_ref, v_ref, kt_ref, vt_ref, r_o, lw_o, k2_o, vr_o, al_o, be_o, g_o, bo_o,
                       carry_ref, *, aw):
    @pl.when(pl.program_id(1) == 0)
    def _():
        carry_ref[...] = jnp.zeros_like(carry_ref)

    tm = x_ref.shape[1]
    hr = tm // SUB_TILES
    spans = [slice(h * hr, (h + 1) * hr) for h in range(SUB_TILES)]
    params = (w0_ref[...], a0_ref[...], kk_ref[...], ka_ref[...], rk_ref[...], wl_ref[...], al_ref[...],
              gp_ref[...], bd_ref[...], aw)
    row = lax.broadcasted_iota(jnp.int32, (hr, w_ref.shape[1] - 3 * aw), 0)

    def finish(p, sp, last):
        q_ref[0, sp, :] = p[:, 0:aw]
        k = p[:, aw:2 * aw]
        v = p[:, 2 * aw:3 * aw]
        k_ref[0, sp, :] = k
        v_ref[0, sp, :] = v
        kt_ref[0, :, sp] = k.T
        vt_ref[0, :, sp] = v.T
        prw = p[:, 3 * aw:]
        prev = jnp.where(row == 0, last, pltpu.roll(prw, 1, axis=0))
        outs = _rwkv_prep(prw + (prev - prw) * mu_ref[...], *params)
        for o_ref, val in zip((r_o, lw_o, k2_o, vr_o, al_o, be_o, g_o, bo_o), outs):
            o_ref[0, sp, :] = val.astype(o_ref.dtype)
        return prw[hr - 1:hr, :]

    last = carry_ref[0:1, :]
    p = _dot(x_ref[0, spans[0], :], w_ref[...])
    for h in range(1, SUB_TILES):
        p_next = _dot(x_ref[0, spans[h], :], w_ref[...])
        last = finish(p, spans[h - 1], last)
        p = p_next
    carry_ref[0:1, :] = finish(p, spans[-1], last)


def _inproj_step_kernel(x_ref, w_ref, mu_ref, w0_ref, a0_ref, kk_ref, ka_ref, rk_ref, wl_ref, al_ref, gp_ref,
                        bd_ref, k_ref, v_ref, r_o, lw_o, k2_o, al_o, be_o, col_o, *, aw, nb):
    p = _dot(x_ref[...], w_ref[...])
    q = p[0:nb, 0:aw]
    k = p[0:nb, aw:2 * aw]
    v = p[0:nb, 2 * aw:3 * aw]
    k_ref[...] = k
    v_ref[...] = v
    prw = p[0:nb, 3 * aw:]
    prev = p[nb:2 * nb, 3 * aw:]
    rw = prw + (prev - prw) * mu_ref[...]
    r, lw, k2, vr, al, be, g, bonus = _rwkv_prep(rw, w0_ref[...], a0_ref[...], kk_ref[...], ka_ref[...],
                                                 rk_ref[...], wl_ref[...], al_ref[...], gp_ref[...], bd_ref[...], aw)
    for o_ref, val in zip((r_o, lw_o, k2_o, al_o, be_o), (r, lw, k2, al, be)):
        o_ref[...] = val
    pad = jnp.zeros((LANES - nb, aw), F32)
    for i, val in enumerate((q, k, v, vr, g, bonus)):
        col_o[i * aw:(i + 1) * aw, :] = jnp.concatenate([val, pad], axis=0).T


_PREP_PARAMS = ("mu", "w0", "a0", "k_k", "k_a", "r_k", "wl_pad", "al_pad", "g_pad", "bd")


def _inproj_seq(x, prm, tm):
    B, T, D = x.shape
    aw = prm["aw"]
    nrw = prm["w_in"].shape[1] - 3 * aw
    small = [prm[n] for n in _PREP_PARAMS]
    row_spec = pl.BlockSpec((1, tm, aw), lambda b, t: (b, t, 0))
    col_spec = pl.BlockSpec((1, aw, tm), lambda b, t: (b, 0, t))
    row_sds = jax.ShapeDtypeStruct((B, T, aw), F32)
    col_sds = jax.ShapeDtypeStruct((B, aw, T), F32)
    return pl.pallas_call(
        functools.partial(_inproj_seq_kernel, aw=aw),
        grid=(B, T // tm),
        in_specs=[pl.BlockSpec((1, tm, D), lambda b, t: (b, t, 0)), _resident(prm["w_in"].shape)]
        + [_full(s.shape) for s in small],
        out_specs=[row_spec] * 3 + [col_spec] * 2 + [row_spec] * 8,
        out_shape=[row_sds] * 3 + [col_sds] * 2
        + [jax.ShapeDtypeStruct((B, T, aw), F32 if n == "lw" else BF16)
           for n in ("r", "lw", "k2", "v", "al", "be", "g", "bonus")],
        scratch_shapes=[pltpu.VMEM((8, nrw), F32)],
        compiler_params=_cparams(2),
        name="inproj_seq",
    )(x, prm["w_in"], *small)


def _inproj_step(x2, prm):
    nb = x2.shape[0] // 2
    aw = prm["aw"]
    small = [prm[n] for n in _PREP_PARAMS]
    assert nb <= LANES
    out_sds = jax.ShapeDtypeStruct((nb, aw), F32)
    return pl.pallas_call(
        functools.partial(_inproj_step_kernel, aw=aw, nb=nb),
        grid=(1,),
        in_specs=[_full(x2.shape), _full(prm["w_in"].shape)] + [_full(s.shape) for s in small],
        out_specs=[_full((nb, aw))] * 7 + [_full((6 * aw, LANES))],
        out_shape=[out_sds] * 7 + [jax.ShapeDtypeStruct((6 * aw, LANES), F32)],
        compiler_params=_cparams(1),
        name="inproj_step",
    )(x2, prm["w_in"], *small)


def _prompt_attn_kernel(q_ref, k_ref, v_ref, sl_ref, o_ref, acc_s, m_s, l_s, yq, yk, yv, yacc, ym, yl, *, T):
    hp = pl.program_id(1)
    scale = HEAD_DIM ** -0.5
    lane = lax.broadcasted_iota(jnp.int32, (1, LANES), 1)
    head_masks = ((lane < HEAD_DIM).astype(F32), (lane >= HEAD_DIM).astype(F32))
    qi = lax.broadcasted_iota(jnp.int32, (Q_TILE, Q_TILE), 0)
    ki = lax.broadcasted_iota(jnp.int32, (Q_TILE, Q_TILE), 1)
    steps_cur = (qi - ki).astype(F32)
    steps_prev = (qi - ki + Q_TILE).astype(F32)
    ok_cur = ki <= qi
    ok_prev = ki >= qi
    first_lanes = lax.broadcasted_iota(jnp.int32, (Q_TILE, LANES), 1) < HEAD_DIM
    slopes = [sl_ref[pl.ds(2 * hp + j, 1), :] for j in range(2)]

    def tile_group(src, tiles, bias, stats, first_branch):
        rq, rk, rv = src
        bias_c, bias_pc = bias
        acc_s, m_s, l_s = stats
        ids = range(len(tiles))
        q = [rq(rows) * scale for rows, _ in tiles]
        kk = [rk(rows) if prev is None else jnp.concatenate([rk(prev), rk(rows)], axis=0) for rows, prev in tiles]
        vv = [rv(rows) if prev is None else jnp.concatenate([rv(prev), rv(rows)], axis=0) for rows, prev in tiles]
        qq = [jnp.concatenate([q[t] * head_masks[0], q[t] * head_masks[1]], axis=0) for t in ids]
        s = [_dot(qq[t], kk[t], "NT") + (bias_c if tiles[t][1] is None else bias_pc) for t in ids]
        m = [jnp.max(s[t], axis=-1, keepdims=True) for t in ids]
        p = [jnp.exp(s[t] - m[t]) for t in ids]
        l = [jnp.sum(p[t], axis=-1, keepdims=True) for t in ids]
        acc = [_dot(p[t], vv[t]) for t in ids]
        for t, (rows, _) in enumerate(tiles):
            acc_n = jnp.where(first_lanes, acc[t][0:Q_TILE], acc[t][Q_TILE:])
            m_n = jnp.where(first_lanes, m[t][0:Q_TILE], m[t][Q_TILE:])
            l_n = jnp.where(first_lanes, l[t][0:Q_TILE], l[t][Q_TILE:])
            if first_branch:
                acc_s[rows, :] = acc_n
                m_s[rows, :] = m_n
                l_s[rows, :] = l_n
            else:
                m_o = m_s[rows, :]
                m_new = jnp.maximum(m_o, m_n)
                e_o = jnp.exp(m_o - m_new)
                e_n = jnp.exp(m_n - m_new)
                acc_s[rows, :] = acc_s[rows, :] * e_o + acc_n * e_n
                l_s[rows, :] = l_s[rows, :] * e_o + l_n * e_n
                m_s[rows, :] = m_new

    dils = [dil for _, dil in DIL_PATTERNS]
    R = dils[1]
    assert dils == [1, R, R * R] and all(win // dil == WIN_STEPS for win, dil in DIL_PATTERNS)

    def bias_of(dil):
        b_cur = [jnp.where(ok_cur, -(slopes[j] * float(dil)) * steps_cur, NEG) for j in range(2)]
        b_prev = [jnp.where(ok_prev, -(slopes[j] * float(dil)) * steps_prev, NEG) for j in range(2)]
        return (jnp.concatenate(b_cur, axis=0),
                jnp.concatenate([jnp.concatenate([b_prev[j], b_cur[j]], axis=1) for j in range(2)], axis=0))

    def groups(tiles):
        return [tiles[i:i + GROUP] for i in range(0, len(tiles), GROUP)]

    natural = tuple((lambda rows, ref=ref: ref[0, rows, :]) for ref in (q_ref, k_ref, v_ref))
    scratch = tuple((lambda rows, ref=ref: ref[rows, :]) for ref in (yq, yk, yv))
    nat_stats = (acc_s, m_s, l_s)
    y_stats = (yacc, ym, yl)

    nb1 = T // Q_TILE
    assert nb1 % GROUP == 0
    bias1 = bias_of(dils[0])
    tile_group(natural, [(pl.ds(jb * Q_TILE, Q_TILE), pl.ds((jb - 1) * Q_TILE, Q_TILE) if jb else None)
                         for jb in range(GROUP)], bias1, nat_stats, True)

    def later(g, carry):
        start = pl.multiple_of(g * (GROUP * Q_TILE), GROUP * Q_TILE)
        tile_group(natural, [(pl.ds(start + i * Q_TILE, Q_TILE), pl.ds(start + (i - 1) * Q_TILE, Q_TILE))
                             for i in range(GROUP)], bias1, nat_stats, True)
        return carry

    lax.fori_loop(1, nb1 // GROUP, later, 0)

    LR = T // R
    bias_r = bias_of(dils[1])
    bias_rr = bias_of(dils[2])

    def stream(s, carry):
        own = pl.ds(s, LR, stride=R)
        yq[...] = q_ref[0, own, :]
        yk[...] = k_ref[0, own, :]
        yv[...] = v_ref[0, own, :]
        for grp in groups([(pl.ds(jb * Q_TILE, Q_TILE), pl.ds((jb - 1) * Q_TILE, Q_TILE) if jb else None)
                           for jb in range(LR // Q_TILE)]):
            tile_group(scratch, grp, bias_r, y_stats, True)
        for grp in groups([(pl.ds(c + R * Q_TILE * jb, Q_TILE, stride=R),
                            pl.ds(c + R * Q_TILE * (jb - 1), Q_TILE, stride=R) if jb else None)
                           for c in range(R) for jb in range(LR // R // Q_TILE)]):
            tile_group(scratch, grp, bias_rr, y_stats, False)
        m_a, m_b = m_s[own, :], ym[...]
        m_new = jnp.maximum(m_a, m_b)
        e_a = jnp.exp(m_a - m_new)
        e_b = jnp.exp(m_b - m_new)
        acc_s[own, :] = (acc_s[own, :] * e_a + yacc[...] * e_b) / (l_s[own, :] * e_a + yl[...] * e_b)
        return carry

    lax.fori_loop(0, R, stream, 0)
    o_ref[0] = acc_s[...].astype(o_ref.dtype)


def _prompt_attn_scratch(T):
    return [pltpu.VMEM((T, LANES), F32)] * 3 + [pltpu.VMEM((T // DIL_PATTERNS[1][1], LANES), F32)] * 6


def _prompt_attn(q, k, v, slopes_tab):
    B, T, W = q.shape
    spec = pl.BlockSpec((1, T, LANES), lambda b, hp: (b, 0, hp))
    return pl.pallas_call(
        functools.partial(_prompt_attn_kernel, T=T),
        grid=(B, W // LANES),
        in_specs=[spec, spec, spec, _full(slopes_tab.shape)],
        out_specs=spec,
        out_shape=jax.ShapeDtypeStruct((B, T, W), BF16),
        scratch_shapes=_prompt_attn_scratch(T),
        compiler_params=_cparams(2),
        name="prompt_attn",
    )(q, k, v, slopes_tab)


def _rwkv_chunk_kernel(r_ref, lw_ref, k_ref, v_ref, al_ref, be_ref, g_ref, bo_ref, gnw_ref, gnb_ref,
                       y_ref, s_ref, st_scr, *, Tb, npair, nbb):
    C = CHUNK
    n2 = 2 * C
    t = pl.program_id(1)

    @pl.when(t == 0)
    def _():
        st_scr[...] = jnp.zeros_like(st_scr)

    lane = lax.broadcasted_iota(jnp.int32, (1, LANES), 1)
    m0 = (lane < HEAD_DIM).astype(F32)
    m1 = 1.0 - m0
    ri = lax.broadcasted_iota(jnp.int32, (n2, n2), 0)
    ci = lax.broadcasted_iota(jnp.int32, (n2, n2), 1)
    same = (ri < C) == (ci < C)
    strict = same & ((ri & (C - 1)) > (ci & (C - 1)))
    incl = same & ((ri & (C - 1)) >= (ci & (C - 1)))
    tri = (lax.broadcasted_iota(jnp.int32, (C, C), 0) >= lax.broadcasted_iota(jnp.int32, (C, C), 1)).astype(BF16)
    head_rows = ((lax.broadcasted_iota(jnp.int32, (n2, LANES), 0) < C)
                 == (lax.broadcasted_iota(jnp.int32, (n2, LANES), 1) < HEAD_DIM)).astype(F32)

    def stack(x):
        return jnp.concatenate([x * m0, x * m1], axis=0)

    def twice(x):
        return jnp.concatenate([x, x], axis=0)

    def body(c, carry):
        rows = pl.ds(pl.multiple_of(c * C, C), C)
        chains = [(bb, slice(hp * LANES, (hp + 1) * LANES)) for bb in range(nbb) for hp in range(npair)]
        ids = range(len(chains))
        S = [st_scr[i] for i in ids]
        lw_all = jnp.concatenate([lw_ref[bb, rows, :] for bb in range(nbb)], axis=1)
        cl_all = _dot_const_l(tri, lw_all)
        e_pos_all = jnp.exp(cl_all)
        e_neg_all = jnp.exp(-cl_all)
        e_exc_all = jnp.exp(cl_all - lw_all)
        wide = [slice(i * LANES, (i + 1) * LANES) for i in ids]
        g_end = [e_pos_all[C - 1:C, wide[i]] for i in ids]
        rt = [r_ref[bb, rows, cs] * e_pos_all[:, wide[i]] for i, (bb, cs) in enumerate(chains)]
        kt = [k_ref[bb, rows, cs] * e_neg_all[:, wide[i]] for i, (bb, cs) in enumerate(chains)]
        bt = [be_ref[bb, rows, cs] * e_neg_all[:, wide[i]] for i, (bb, cs) in enumerate(chains)]
        at = [al_ref[bb, rows, cs] * e_exc_all[:, wide[i]] for i, (bb, cs) in enumerate(chains)]
        ar_st = [jnp.concatenate([stack(at[i]), stack(rt[i])], axis=0) for i in ids]
        kb2 = [jnp.concatenate([twice(kt[i]), twice(bt[i])], axis=0) for i in ids]
        v_st = [stack(v_ref[bb, rows, cs]) for bb, cs in chains]
        gram = [_dot(ar_st[i], kb2[i], "NT") for i in ids]
        l_ab = [jnp.where(strict, gram[i][0:n2, n2:], 0.0) for i in ids]
        l_kk = [jnp.concatenate([jnp.where(strict, gram[i][0:n2, 0:n2], 0.0),
                                 jnp.where(incl, gram[i][n2:, 0:n2], 0.0)], axis=0) for i in ids]
        l_rb = [jnp.where(incl, gram[i][n2:, n2:], 0.0) for i in ids]
        from_state = [_dot(jnp.concatenate([at[i], rt[i]], axis=0), S[i], "NT") for i in ids]
        from_state = [jnp.concatenate([stack(from_state[i][0:C]), stack(from_state[i][C:])], axis=0) for i in ids]
        from_v = [_dot(l_kk[i], v_st[i]) for i in ids]
        x = l_ab
        u = [from_state[i][0:n2] + from_v[i][0:n2] for i in ids]
        for _ in range(C.bit_length() - 2):
            xu = [_dot(x[i], jnp.concatenate([x[i], u[i]], axis=1)) for i in ids]
            x = [xu[i][:, 0:n2] for i in ids]
            u = [u[i] + xu[i][:, n2:] for i in ids]
        u = [u[i] + _dot(x[i], u[i]) for i in ids]
        y_st = [from_state[i][n2:] + from_v[i][n2:] + _dot(l_rb[i], u[i]) for i in ids]
        for i in ids:
            vu = jnp.concatenate([v_st[i], u[i]], axis=0)
            kb_end = jnp.concatenate([stack(kt[i] * g_end[i]), stack(bt[i] * g_end[i])], axis=0)
            st_scr[i] = S[i] * g_end[i] + _dot(vu, kb_end, "TN")
        for i, (bb, cs) in enumerate(chains):
            mu = jnp.sum(y_st[i], axis=-1, keepdims=True) * (1.0 / HEAD_DIM)
            yc = (y_st[i] - mu) * head_rows
            var = jnp.sum(yc * yc, axis=-1, keepdims=True) * (1.0 / HEAD_DIM)
            yn_st = yc * lax.rsqrt(var + GN_EPS)
            yn_i = (yn_st[0:C, :] + yn_st[C:n2, :]) * gnw_ref[:, cs] + gnb_ref[:, cs]
            y_ref[bb, rows, cs] = ((yn_i + bo_ref[bb, rows, cs]) * g_ref[bb, rows, cs]).astype(y_ref.dtype)
        return carry

    lax.fori_loop(0, Tb // C, body, 0)

    @pl.when(t == pl.num_programs(1) - 1)
    def _():
        for i in range(nbb * npair):
            bb, hp = divmod(i, npair)
            pair = st_scr[i]
            s_ref[bb, 2 * hp] = pair[0:HEAD_DIM, 0:HEAD_DIM]
            s_ref[bb, 2 * hp + 1] = pltpu.roll(pair, HEAD_DIM, axis=1)[HEAD_DIM:, 0:HEAD_DIM]


def _rwkv_chunked(r, lw, k2, v, al, be, g, bonus, gn_w, gn_b, tb):
    B, T, W = r.shape
    npair = W // LANES
    nbb = next(n for n in (4, 2, 1) if B % n == 0)
    spec = pl.BlockSpec((nbb, tb, W), lambda b, t: (b, t, 0))
    st_spec = pl.BlockSpec((nbb, 2 * npair, HEAD_DIM, HEAD_DIM), lambda b, t: (b, 0, 0, 0))
    return pl.pallas_call(
        functools.partial(_rwkv_chunk_kernel, Tb=tb, npair=npair, nbb=nbb),
        grid=(B // nbb, T // tb),
        in_specs=[spec] * 8 + [_full(gn_w.shape), _full(gn_b.shape)],
        out_specs=[spec, st_spec],
        out_shape=[jax.ShapeDtypeStruct((B, T, W), BF16),
                   jax.ShapeDtypeStruct((B, 2 * npair, HEAD_DIM, HEAD_DIM), F32)],
        scratch_shapes=[pltpu.VMEM((nbb * npair, LANES, LANES), F32)],
        compiler_params=_cparams(2),
        name="rwkv_chunk",
    )(r, lw, k2, v, al, be, g, bonus, gn_w, gn_b)


def _rwkv_one_step(S, r, lw, k, al, be, v, g, bonus, gnw, gnb):
    sa = jnp.sum(S * al, axis=-1, keepdims=True)
    s_new = S * jnp.exp(lw) + sa * be + v * k
    y = jnp.sum(s_new * r, axis=-1, keepdims=True)
    mu = jnp.mean(y, axis=1, keepdims=True)
    yc = y - mu
    var = jnp.mean(yc * yc, axis=1, keepdims=True)
    yn = yc * lax.rsqrt(var + GN_EPS) * gnw + gnb
    return (yn + bonus) * g, s_new


def _attn_one_step(q, k_new, v_new, kt, vt, slopes, P):
    scale = HEAD_DIM ** -0.5
    dist = P - lax.broadcasted_iota(jnp.int32, (1, 1, P), 2)
    count = jnp.zeros((1, 1, P), F32)
    for win, dil in DIL_PATTERNS:
        assert dil & (dil - 1) == 0
        count = count + ((dist <= win) & ((dist & (dil - 1)) == 0)).astype(F32)
    s = jnp.sum(kt * q, axis=1, keepdims=True) * scale - slopes * dist.astype(F32)
    s = jnp.where(count > 0.0, s, NEG)
    s0 = jnp.sum(k_new * q, axis=1, keepdims=True) * scale
    m = jnp.maximum(jnp.max(s, axis=-1, keepdims=True), s0)
    p = count * jnp.exp(s - m)
    e0 = len(DIL_PATTERNS) * jnp.exp(s0 - m)
    l = jnp.sum(p, axis=-1, keepdims=True) + e0
    acc = jnp.sum(vt * p, axis=-1, keepdims=True) + e0 * v_new
    return acc / l


def _mixer_step_kernel(*refs, P, aw):
    _mixer_step_body(pl.program_id(0), *refs, P=P, aw=aw)


def _mixer_step_body(b, col_ref, kt_ref, vt_ref, sl_ref, s_ref, r_ref, lw_ref, k_ref, al_ref, be_ref, gnw_ref,
                     gnb_ref, mix_ref, so_ref, *, P, aw):
    H = aw // HEAD_DIM
    own = (lax.broadcasted_iota(jnp.int32, (1, LANES), 1) == b).astype(F32)

    def column(i):
        return jnp.sum(col_ref[i * aw:(i + 1) * aw, :] * own, axis=-1, keepdims=True).reshape(H, HEAD_DIM, 1)

    @pl.when(b == 0)
    def _():
        mix_ref[...] = jnp.zeros_like(mix_ref)

    att = _attn_one_step(column(0), column(1), column(2), kt_ref[0], vt_ref[0], sl_ref[...], P)
    y, s_new = _rwkv_one_step(s_ref[0], r_ref[0], lw_ref[0], k_ref[0], al_ref[0], be_ref[0], column(3), column(4),
                              column(5), gnw_ref[...], gnb_ref[...])
    so_ref[0] = s_new
    mix_ref[0:aw, :] += att.reshape(aw, 1) * own
    mix_ref[aw:, :] += y.reshape(aw, 1) * own


def _mixer_step_operands(seq_of, col, k_cache, v_cache, slopes, state, r, lw, k2, al, be, gn_w, gn_b):
    nb, P, H, _ = k_cache.shape
    aw = H * HEAD_DIM
    rowv = lambda a: a.reshape(nb, H, 1, HEAD_DIM)
    per_seq = lambda shape: pl.BlockSpec((1,) + shape, lambda *g: (seq_of(*g), 0, 0, 0))
    t_spec, s_spec, r_spec = per_seq((H, HEAD_DIM, P)), per_seq((H, HEAD_DIM, HEAD_DIM)), per_seq((H, 1, HEAD_DIM))
    p_spec = _full((H, HEAD_DIM, 1))
    args = (col, jnp.transpose(k_cache, (0, 2, 3, 1)), jnp.transpose(v_cache, (0, 2, 3, 1)), slopes.reshape(H, 1, 1),
            state, rowv(r), rowv(lw), rowv(k2), rowv(al), rowv(be), gn_w.reshape(H, HEAD_DIM, 1),
            gn_b.reshape(H, HEAD_DIM, 1))
    in_specs = [_full(col.shape), t_spec, t_spec, _full((H, 1, 1)), s_spec] + [r_spec] * 5 + [p_spec] * 2
    out_specs = [_full((2 * aw, LANES)), s_spec]
    out_shape = [jax.ShapeDtypeStruct((2 * aw, LANES), F32), jax.ShapeDtypeStruct(state.shape, F32)]
    return args, in_specs, out_specs, out_shape


def _mixer_step(*operands):
    args, in_specs, out_specs, out_shape = _mixer_step_operands(lambda b: b, *operands)
    nb, P, H, _ = operands[1].shape
    return pl.pallas_call(
        functools.partial(_mixer_step_kernel, P=P, aw=H * HEAD_DIM),
        grid=(nb,),
        in_specs=in_specs,
        out_specs=out_specs,
        out_shape=out_shape,
        compiler_params=_cparams(1),
        name="mixer_step",
    )(*args)


def _attn_mixer_kernel(q_ref, k_ref, v_ref, sl_ref, *refs, T, P, aw, npair):
    mixer_in, (o_ref, mix_ref, so_ref), scratch = refs[:12], refs[12:15], refs[15:]
    _mixer_step_body(pl.program_id(0) * npair + pl.program_id(1), *mixer_in, mix_ref, so_ref, P=P, aw=aw)
    _prompt_attn_kernel(q_ref, k_ref, v_ref, sl_ref, o_ref, *scratch, T=T)


def _prompt_attn_and_mixer_step(q, k, v, slopes_tab, *mixer_operands):
    B, T, W = q.shape
    npair = W // LANES
    args, in_specs, out_specs, out_shape = _mixer_step_operands(lambda b, hp: b * npair + hp, *mixer_operands)
    nb, P, H, _ = mixer_operands[1].shape
    assert nb == B * npair
    spec = pl.BlockSpec((1, T, LANES), lambda b, hp: (b, 0, hp))
    return pl.pallas_call(
        functools.partial(_attn_mixer_kernel, T=T, P=P, aw=H * HEAD_DIM, npair=npair),
        grid=(B, npair),
        in_specs=[spec, spec, spec, _full(slopes_tab.shape)] + in_specs,
        out_specs=[spec] + out_specs,
        out_shape=[jax.ShapeDtypeStruct((B, T, W), BF16)] + out_shape,
        scratch_shapes=_prompt_attn_scratch(T),
        compiler_params=_cparams(2),
        name="prompt_attn_mixer_step",
    )(q, k, v, slopes_tab, *args)


def _ffn_tail(x1, gate_pre, prev1, prev2, up, cw_ref, cb_ref, w2_ref, g_ref, b_ref, alpha):
    c = cw_ref[0:1, :] * prev2 + cw_ref[1:2, :] * prev1 + cw_ref[2:3, :] * gate_pre + cb_ref[...]
    h = c * _sigmoid(c) * up
    return _layer_norm(alpha * x1 + _dot(h, w2_ref[...]), g_ref[...], b_ref[...])


def _ffn_seq_kernel(att_ref, y_ref, x_ref, wo_ref, g1_ref, b1_ref, w1_ref, cw_ref, cb_ref, w2_ref, g_ref, b_ref,
                    o_ref, tail_ref, carry_ref, *, aw, dff, alpha):
    @pl.when(pl.program_id(1) == 0)
    def _():
        carry_ref[...] = jnp.zeros_like(carry_ref)

    tm = x_ref.shape[1]
    hr = tm // SUB_TILES
    spans = [slice(h * hr, (h + 1) * hr) for h in range(SUB_TILES)]
    n = range(SUB_TILES)
    row = lax.broadcasted_iota(jnp.int32, (hr, dff), 0)
    mixed = [_dot(att_ref[0, sp, :], wo_ref[0:aw, :]) + _dot(y_ref[0, sp, :], wo_ref[aw:, :]) for sp in spans]
    x1 = [_layer_norm(alpha * x_ref[0, spans[i], :] + mixed[i], g1_ref[...], b1_ref[...]) for i in n]
    hu = [_dot(x1[0], w1_ref[...])]
    last2 = carry_ref[0:2, :]
    gated, f = [], []
    for i in n:
        if i + 1 < SUB_TILES:
            hu.append(_dot(x1[i + 1], w1_ref[...]))
        gate_pre = hu[i][:, 0:dff]
        c0, c1 = last2[0:1, :], last2[1:2, :]
        prev1 = jnp.where(row == 0, c1, pltpu.roll(gate_pre, 1, axis=0))
        prev2 = jnp.where(row == 0, c0, jnp.where(row == 1, c1, pltpu.roll(gate_pre, 2, axis=0)))
        last2 = gate_pre[hr - 2:hr, :]
        c = cw_ref[0:1, :] * prev2 + cw_ref[1:2, :] * prev1 + cw_ref[2:3, :] * gate_pre + cb_ref[...]
        gated.append(c * _sigmoid(c) * hu[i][:, dff:])
        if i > 0:
            f.append(_dot(gated[i - 1], w2_ref[...]))
    f.append(_dot(gated[-1], w2_ref[...]))
    carry_ref[0:2, :] = last2
    tail_ref[0] = last2
    for i in n:
        o_ref[0, spans[i], :] = _layer_norm(alpha * x1[i] + f[i], g_ref[...], b_ref[...])


def _ffn_step_kernel(mix_ref, x_ref, wo_ref, g1_ref, b1_ref, p2_ref, p1_ref, w1_ref, cw_ref, cb_ref, w2_ref, g_ref,
                     b_ref, o_ref, gate_ref, *, dff, alpha):
    nb = x_ref.shape[0]
    mixed = mix_ref[...].T[0:nb, :]
    x1 = _layer_norm(alpha * x_ref[...] + _dot(mixed, wo_ref[...]), g1_ref[...], b1_ref[...])
    hu = _dot(x1, w1_ref[...])
    gate_pre = hu[:, 0:dff]
    gate_ref[...] = gate_pre
    o_ref[...] = _ffn_tail(x1, gate_pre, p1_ref[...], p2_ref[...], hu[:, dff:], cw_ref, cb_ref, w2_ref, g_ref,
                           b_ref, alpha)


def _resident(shape):
    nd = len(shape)
    return pl.BlockSpec(shape, lambda *_: (0,) * nd, pipeline_mode=pl.Buffered(1))


def _ffn_seq(att, y_rw, x, w_out, ln1_g, ln1_b, w1, conv_w, conv_b, w2, ln2_g, ln2_b, tm, alpha):
    B, T, D = x.shape
    aw = att.shape[-1]
    dff = w2.shape[0]
    half = pl.BlockSpec((1, tm, aw), lambda b, t: (b, t, 0))
    rows = pl.BlockSpec((1, tm, D), lambda b, t: (b, t, 0))
    return pl.pallas_call(
        functools.partial(_ffn_seq_kernel, aw=aw, dff=dff, alpha=alpha),
        grid=(B, T // tm),
        in_specs=[half, half, rows, _resident(w_out.shape), _full(ln1_g.shape), _full(ln1_b.shape),
                  _resident(w1.shape), _full(conv_w.shape), _full(conv_b.shape), _resident(w2.shape),
                  _full(ln2_g.shape), _full(ln2_b.shape)],
        out_specs=[rows, pl.BlockSpec((1, 2, dff), lambda b, t: (b, 0, 0))],
        out_shape=[jax.ShapeDtypeStruct((B, T, D), F32), jax.ShapeDtypeStruct((B, 2, dff), F32)],
        scratch_shapes=[pltpu.VMEM((8, dff), F32)],
        compiler_params=_cparams(2),
        name="ffn_seq",
    )(att, y_rw, x, w_out, ln1_g, ln1_b, w1, conv_w, conv_b, w2, ln2_g, ln2_b)


def _ffn_step(mix, x, w_out, ln1_g, ln1_b, prev2, prev1, w1, conv_w, conv_b, w2, ln2_g, ln2_b, alpha):
    nb, D = x.shape
    dff = w2.shape[0]
    return pl.pallas_call(
        functools.partial(_ffn_step_kernel, dff=dff, alpha=alpha),
        grid=(1,),
        in_specs=[_full(mix.shape), _full(x.shape), _resident(w_out.shape), _full(ln1_g.shape), _full(ln1_b.shape),
                  _full(prev2.shape), _full(prev1.shape), _resident(w1.shape), _full(conv_w.shape),
                  _full(conv_b.shape), _resident(w2.shape), _full(ln2_g.shape), _full(ln2_b.shape)],
        out_specs=[_full((nb, D)), _full((nb, dff))],
        out_shape=[jax.ShapeDtypeStruct((nb, D), F32), jax.ShapeDtypeStruct((nb, dff), F32)],
        compiler_params=_cparams(1),
        name="ffn_step",
    )(mix, x, w_out, ln1_g, ln1_b, prev2, prev1, w1, conv_w, conv_b, w2, ln2_g, ln2_b)


def _layer_params(w_in, mu_shift, w0, w_lora_up, a0, a_lora_up, g_lora_up, k_k, k_a, r_k, aw):
    D, ncol = w_in.shape
    n_w, n_a, n_g = w_lora_up.shape[0], a_lora_up.shape[0], g_lora_up.shape[0]
    assert n_w + n_a == LANES and 3 * aw + 3 * aw + n_w + n_a + n_g == ncol
    g_cols = -(-n_g // LANES) * LANES
    pad = g_cols - n_g
    row = lambda a: a.reshape(1, -1).astype(F32)
    lane = jnp.arange(aw)
    bd = (lane[:, None] // HEAD_DIM == lane[None, :] // HEAD_DIM).astype(BF16)
    return dict(
        aw=aw,
        w_in=jnp.pad(w_in, ((0, 0), (0, pad))).astype(BF16),
        mu=jnp.pad(row(mu_shift), ((0, 0), (0, pad))),
        w0=row(w0), a0=row(a0), k_k=row(k_k), k_a=row(k_a), r_k=row(r_k),
        wl_pad=jnp.pad(w_lora_up, ((0, n_a), (0, 0))).astype(BF16),
        al_pad=jnp.pad(a_lora_up, ((n_w, 0), (0, 0))).astype(BF16),
        g_pad=jnp.pad(g_lora_up, ((0, pad), (0, 0))).astype(BF16),
        bd=bd,
    )


def kernel(x_prompt, x_sample, cache_k_win, cache_v_win, state_shift, state_wkv, state_conv, w_in, mu_shift, w0, w_lora_up, a0, a_lora_up, g_lora_up, k_k, k_a, r_k, gn_w, gn_b, w_out, ln1_g, ln1_b, w_ffn_in, conv_w, conv_b, w_ffn_out, ln2_g, ln2_b):
    depth = w_in.shape[0]
    alpha = (2.0 * depth) ** 0.25
    B, T, D = x_prompt.shape
    nb = x_sample.shape[0]
    assert x_sample.shape[1] == 1
    n_att = cache_k_win.shape[3]
    n_rw = state_wkv.shape[2]
    aw = n_att * HEAD_DIM
    assert n_rw * HEAD_DIM == aw and T % (Q_TILE * DIL_PATTERNS[-1][1]) == 0
    tm = min(256, T)
    tb = min(256, T)
    heads = jnp.arange(1, n_att + 1, dtype=F32)
    slopes = jnp.exp2(-8.0 * heads / n_att)
    slopes_tab = jnp.broadcast_to(slopes[:, None], (n_att, LANES))
    row = lambda a: a.reshape(1, -1)

    hp, hs = x_prompt, x_sample.reshape(nb, D)
    outs = [[] for _ in range(10)]
    for l in range(depth):
        prm = _layer_params(w_in[l], mu_shift[l], w0[l], w_lora_up[l], a0[l], a_lora_up[l], g_lora_up[l],
                            k_k[l], k_a[l], r_k[l], aw)
        w_out_b = w_out[l].astype(BF16)
        w1_b = w_ffn_in[l].astype(BF16)
        w2_b = w_ffn_out[l].astype(BF16)
        gnw, gnb = row(gn_w[l]), row(gn_b[l])

        q, k, v, kt, vt, r, lw, k2, vr, al, be, g, bonus = _inproj_seq(hp, prm, min(2 * tm, T))
        x2 = jnp.concatenate([hs, state_shift[l]], axis=0)
        ks, vs, r_s, lw_s, k2_s, al_s, be_s, col = _inproj_step(x2, prm)
        mixer_operands = (col, cache_k_win[l], cache_v_win[l], slopes, state_wkv[l], r_s, lw_s, k2_s, al_s, be_s,
                          gn_w[l], gn_b[l])

        if nb == B * (aw // LANES):
            att, mix_s, wkv_s = _prompt_attn_and_mixer_step(q, k, v, slopes_tab, *mixer_operands)
        else:
            att = _prompt_attn(q, k, v, slopes_tab)
            mix_s, wkv_s = _mixer_step(*mixer_operands)
        y_rw, wkv_p = _rwkv_chunked(r, lw, k2, vr, al, be, g, bonus, gnw, gnb, tb)

        shift_p = hp[:, -1, :]
        hp, conv_p = _ffn_seq(att, y_rw, hp, w_out_b, row(ln1_g[l]), row(ln1_b[l]), w1_b, conv_w[l],
                              row(conv_b[l]), w2_b, row(ln2_g[l]), row(ln2_b[l]), min(2 * tm, T), alpha)
        shift_s = hs
        hs, gate_s = _ffn_step(mix_s, hs, w_out_b, row(ln1_g[l]), row(ln1_b[l]), state_conv[l][:, 0, :],
                               state_conv[l][:, 1, :], w1_b, conv_w[l], row(conv_b[l]), w2_b, row(ln2_g[l]),
                               row(ln2_b[l]), alpha)
        conv_s = jnp.stack([state_conv[l][:, 1, :], gate_s], axis=1)

        n_keep = min(DIL_PATTERNS[-1][0], T)
        win = lambda a: jnp.transpose(a.reshape(B, n_att, HEAD_DIM, T), (0, 3, 1, 2))[:, T - n_keep:]
        vals = (win(kt), win(vt), ks.reshape(nb, 1, n_att, HEAD_DIM), vs.reshape(nb, 1, n_att, HEAD_DIM),
                shift_p, shift_s, wkv_p, wkv_s, conv_p, conv_s)
        for lst, val in zip(outs, vals):
            lst.append(val)
    return (hp, hs.reshape(nb, 1, D)) + tuple(jnp.stack(lst) for lst in outs)
```

```python
import functools
import math

import jax
import jax.numpy as jnp
from jax import lax
from jax.experimental import pallas as pl
from jax.experimental.pallas import tpu as pltpu

F32 = jnp.float32
BF16 = jnp.bfloat16

HEAD_DIM = 64
LANES = 128
DIL_PATTERNS = ((128, 1), (512, 4), (2048, 16))
WIN_STEPS = 128
Q_TILE = 128
GROUP = 4
SUB_TILES = 4
CHUNK = 64
LN_EPS = 1e-5
GN_EPS = 64e-5
NEG = -1e30
VMEM_LIMIT = 56 * 1024 * 1024

_DN = {"NN": (((1,), (0,)), ((), ())), "NT": (((1,), (1,)), ((), ())), "TN": (((0,), (0,)), ((), ()))}


def _dot(a, b, dims="NN"):
    return lax.dot_general(a.astype(BF16), b.astype(BF16), _DN[dims], preferred_element_type=F32)


def _dot_const_l(c, x):
    hi = x.astype(BF16)
    r1 = x - hi.astype(F32)
    mid = r1.astype(BF16)
    lo = (r1 - mid.astype(F32)).astype(BF16)
    return _dot(c, hi) + _dot(c, mid) + _dot(c, lo)


def _sigmoid(x):
    return 0.5 * jnp.tanh(0.5 * x) + 0.5


def _layer_norm(x, g, b):
    mu = jnp.mean(x, axis=-1, keepdims=True)
    xc = x - mu
    var = jnp.mean(xc * xc, axis=-1, keepdims=True)
    return xc * lax.rsqrt(var + LN_EPS) * g + b


def _cparams(n_grid):
    return pltpu.CompilerParams(dimension_semantics=("arbitrary",) * n_grid, vmem_limit_bytes=VMEM_LIMIT)


def _full(shape):
    nd = len(shape)
    return pl.BlockSpec(shape, lambda *_: (0,) * nd)


def _rwkv_prep(rw, w0, a0, k_k, k_a, r_k, wl_pad, al_pad, g_pad, bd, aw):
    r = rw[:, 0:aw]
    k = rw[:, aw:2 * aw]
    v = rw[:, 2 * aw:3 * aw]
    lo = rw[:, 3 * aw:3 * aw + 128]
    glo = rw[:, 3 * aw + 128:]
    u = w0 + _dot(jnp.tanh(lo), wl_pad)
    a = _sigmoid(a0 + _dot(lo, al_pad))
    g = _dot(_sigmoid(glo), g_pad)
    kk = k * k_k
    sumsq = _dot(kk * kk, bd)
    lw = -math.exp(-0.5) * _sigmoid(u)
    k2 = k * (1.0 + (a - 1.0) * k_a)
    bonus = _dot(r * k2 * r_k, bd) * v
    kkn = kk * lax.rsqrt(jnp.maximum(sumsq, 1e-24))
    return r, lw, k2, v, -kkn, kkn * a, g, bonus


def _inproj_seq_kernel(x_ref, w_ref, mu_ref, w0_ref, a0_ref, kk_ref, ka_ref, rk_ref, wl_ref, al_ref, gp_ref,
                       bd_ref, q_ref, k_ref, v_ref, kt_ref, vt_ref, r_o, lw_o, k2_o, vr_o, al_o, be_o, g_o, bo_o,
                       carry_ref, *, aw):
    @pl.when(pl.program_id(1) == 0)
    def _():
        carry_ref[...] = jnp.zeros_like(carry_ref)

    tm = x_ref.shape[1]
    hr = tm // SUB_TILES
    spans = [slice(h * hr, (h + 1) * hr) for h in range(SUB_TILES)]
    params = (w0_ref[...], a0_ref[...], kk_ref[...], ka_ref[...], rk_ref[...], wl_ref[...], al_ref[...],
              gp_ref[...], bd_ref[...], aw)
    row = lax.broadcasted_iota(jnp.int32, (hr, w_ref.shape[1] - 3 * aw), 0)

    def finish(p, sp, last):
        q_ref[0, sp, :] = p[:, 0:aw]
        k = p[:, aw:2 * aw]
        v = p[:, 2 * aw:3 * aw]
        k_ref[0, sp, :] = k
        v_ref[0, sp, :] = v
        kt_ref[0, :, sp] = k.T
        vt_ref[0, :, sp] = v.T
        prw = p[:, 3 * aw:]
        prev = jnp.where(row == 0, last, pltpu.roll(prw, 1, axis=0))
        outs = _rwkv_prep(prw + (prev - prw) * mu_ref[...], *params)
        for o_ref, val in zip((r_o, lw_o, k2_o, vr_o, al_o, be_o, g_o, bo_o), outs):
            o_ref[0, sp, :] = val.astype(o_ref.dtype)
        return prw[hr - 1:hr, :]

    last = carry_ref[0:1, :]
    p = _dot(x_ref[0, spans[0], :], w_ref[...])
    for h in range(1, SUB_TILES):
        p_next = _dot(x_ref[0, spans[h], :], w_ref[...])
        last = finish(p, spans[h - 1], last)
        p = p_next
    carry_ref[0:1, :] = finish(p, spans[-1], last)


def _inproj_step_kernel(x_ref, w_ref, mu_ref, w0_ref, a0_ref, kk_ref, ka_ref, rk_ref, wl_ref, al_ref, gp_ref,
                        bd_ref, q_ref, k_ref, v_ref, r_o, lw_o, k2_o, al_o, be_o, col_o, *, aw, nb):
    p = _dot(x_ref[...], w_ref[...])
    q_ref[...] = p[0:nb, 0:aw]
    k_ref[...] = p[0:nb, aw:2 * aw]
    v_ref[...] = p[0:nb, 2 * aw:3 * aw]
    prw = p[0:nb, 3 * aw:]
    prev = p[nb:2 * nb, 3 * aw:]
    rw = prw + (prev - prw) * mu_ref[...]
    r, lw, k2, vr, al, be, g, bonus = _rwkv_prep(rw, w0_ref[...], a0_ref[...], kk_ref[...], ka_ref[...],
                                                 rk_ref[...], wl_ref[...], al_ref[...], gp_ref[...], bd_ref[...], aw)
    for o_ref, val in zip((r_o, lw_o, k2_o, al_o, be_o), (r, lw, k2, al, be)):
        o_ref[...] = val
    pad = jnp.zeros((LANES - nb, aw), F32)
    for i, val in enumerate((vr, g, bonus)):
        col_o[i * aw:(i + 1) * aw, :] = jnp.concatenate([val, pad], axis=0).T


_PREP_PARAMS = ("mu", "w0", "a0", "k_k", "k_a", "r_k", "wl_pad", "al_pad", "g_pad", "bd")


def _inproj_seq(x, prm, tm):
    B, T, D = x.shape
    aw = prm["aw"]
    nrw = prm["w_in"].shape[1] - 3 * aw
    small = [prm[n] for n in _PREP_PARAMS]
    row_spec = pl.BlockSpec((1, tm, aw), lambda b, t: (b, t, 0))
    col_spec = pl.BlockSpec((1, aw, tm), lambda b, t: (b, 0, t))
    row_sds = jax.ShapeDtypeStruct((B, T, aw), F32)
    col_sds = jax.ShapeDtypeStruct((B, aw, T), F32)
    return pl.pallas_call(
        functools.partial(_inproj_seq_kernel, aw=aw),
        grid=(B, T // tm),
        in_specs=[pl.BlockSpec((1, tm, D), lambda b, t: (b, t, 0)), _resident(prm["w_in"].shape)]
        + [_full(s.shape) for s in small],
        out_specs=[row_spec] * 3 + [col_spec] * 2 + [row_spec] * 8,
        out_shape=[row_sds] * 3 + [col_sds] * 2
        + [jax.ShapeDtypeStruct((B, T, aw), F32 if n == "lw" else BF16)
           for n in ("r", "lw", "k2", "v", "al", "be", "g", "bonus")],
        scratch_shapes=[pltpu.VMEM((8, nrw), F32)],
        compiler_params=_cparams(2),
        name="inproj_seq",
    )(x, prm["w_in"], *small)


def _inproj_step(x2, prm):
    nb = x2.shape[0] // 2
    aw = prm["aw"]
    small = [prm[n] for n in _PREP_PARAMS]
    assert nb <= LANES
    out_sds = jax.ShapeDtypeStruct((nb, aw), F32)
    return pl.pallas_call(
        functools.partial(_inproj_step_kernel, aw=aw, nb=nb),
        grid=(1,),
        in_specs=[_full(x2.shape), _full(prm["w_in"].shape)] + [_full(s.shape) for s in small],
        out_specs=[_full((nb, aw))] * 8 + [_full((3 * aw, LANES))],
        out_shape=[out_sds] * 8 + [jax.ShapeDtypeStruct((3 * aw, LANES), F32)],
        compiler_params=_cparams(1),
        name="inproj_step",
    )(x2, prm["w_in"], *small)


def _prompt_attn_kernel(q_ref, k_ref, v_ref, sl_ref, o_ref, acc_s, m_s, l_s, yq, yk, yv, yacc, ym, yl, *, T):
    hp = pl.program_id(1)
    scale = HEAD_DIM ** -0.5
    lane = lax.broadcasted_iota(jnp.int32, (1, LANES), 1)
    head_masks = ((lane < HEAD_DIM).astype(F32), (lane >= HEAD_DIM).astype(F32))
    qi = lax.broadcasted_iota(jnp.int32, (Q_TILE, Q_TILE), 0)
    ki = lax.broadcasted_iota(jnp.int32, (Q_TILE, Q_TILE), 1)
    steps_cur = (qi - ki).astype(F32)
    steps_prev = (qi - ki + Q_TILE).astype(F32)
    ok_cur = ki <= qi
    ok_prev = ki >= qi
    first_lanes = lax.broadcasted_iota(jnp.int32, (Q_TILE, LANES), 1) < HEAD_DIM
    slopes = [sl_ref[pl.ds(2 * hp + j, 1), :] for j in range(2)]

    def tile_group(src, tiles, bias, stats, first_branch):
        rq, rk, rv = src
        bias_c, bias_pc = bias
        acc_s, m_s, l_s = stats
        ids = range(len(tiles))
        q = [rq(rows) * scale for rows, _ in tiles]
        kk = [rk(rows) if prev is None else jnp.concatenate([rk(prev), rk(rows)], axis=0) for rows, prev in tiles]
        vv = [rv(rows) if prev is None else jnp.concatenate([rv(prev), rv(rows)], axis=0) for rows, prev in tiles]
        qq = [jnp.concatenate([q[t] * head_masks[0], q[t] * head_masks[1]], axis=0) for t in ids]
        s = [_dot(qq[t], kk[t], "NT") + (bias_c if tiles[t][1] is None else bias_pc) for t in ids]
        m = [jnp.max(s[t], axis=-1, keepdims=True) for t in ids]
        p = [jnp.exp(s[t] - m[t]) for t in ids]
        l = [jnp.sum(p[t], axis=-1, keepdims=True) for t in ids]
        acc = [_dot(p[t], vv[t]) for t in ids]
        for t, (rows, _) in enumerate(tiles):
            acc_n = jnp.where(first_lanes, acc[t][0:Q_TILE], acc[t][Q_TILE:])
            m_n = jnp.where(first_lanes, m[t][0:Q_TILE], m[t][Q_TILE:])
            l_n = jnp.where(first_lanes, l[t][0:Q_TILE], l[t][Q_TILE:])
            if first_branch:
                acc_s[rows, :] = acc_n
                m_s[rows, :] = m_n
                l_s[rows, :] = l_n
            else:
                m_o = m_s[rows, :]
                m_new = jnp.maximum(m_o, m_n)
                e_o = jnp.exp(m_o - m_new)
                e_n = jnp.exp(m_n - m_new)
                acc_s[rows, :] = acc_s[rows, :] * e_o + acc_n * e_n
                l_s[rows, :] = l_s[rows, :] * e_o + l_n * e_n
                m_s[rows, :] = m_new

    dils = [dil for _, dil in DIL_PATTERNS]
    R = dils[1]
    assert dils == [1, R, R * R] and all(win // dil == WIN_STEPS for win, dil in DIL_PATTERNS)

    def bias_of(dil):
        b_cur = [jnp.where(ok_cur, -(slopes[j] * float(dil)) * steps_cur, NEG) for j in range(2)]
        b_prev = [jnp.where(ok_prev, -(slopes[j] * float(dil)) * steps_prev, NEG) for j in range(2)]
        return (jnp.concatenate(b_cur, axis=0),
                jnp.concatenate([jnp.concatenate([b_prev[j], b_cur[j]], axis=1) for j in range(2)], axis=0))

    def groups(tiles):
        return [tiles[i:i + GROUP] for i in range(0, len(tiles), GROUP)]

    natural = tuple((lambda rows, ref=ref: ref[0, rows, :]) for ref in (q_ref, k_ref, v_ref))
    scratch = tuple((lambda rows, ref=ref: ref[rows, :]) for ref in (yq, yk, yv))
    nat_stats = (acc_s, m_s, l_s)
    y_stats = (yacc, ym, yl)

    nb1 = T // Q_TILE
    assert nb1 % GROUP == 0
    bias1 = bias_of(dils[0])
    tile_group(natural, [(pl.ds(jb * Q_TILE, Q_TILE), pl.ds((jb - 1) * Q_TILE, Q_TILE) if jb else None)
                         for jb in range(GROUP)], bias1, nat_stats, True)

    def later(g, carry):
        start = pl.multiple_of(g * (GROUP * Q_TILE), GROUP * Q_TILE)
        tile_group(natural, [(pl.ds(start + i * Q_TILE, Q_TILE), pl.ds(start + (i - 1) * Q_TILE, Q_TILE))
                             for i in range(GROUP)], bias1, nat_stats, True)
        return carry

    lax.fori_loop(1, nb1 // GROUP, later, 0)

    LR = T // R
    bias_r = bias_of(dils[1])
    bias_rr = bias_of(dils[2])

    def stream(s, carry):
        own = pl.ds(s, LR, stride=R)
        yq[...] = q_ref[0, own, :]
        yk[...] = k_ref[0, own, :]
        yv[...] = v_ref[0, own, :]
        for grp in groups([(pl.ds(jb * Q_TILE, Q_TILE), pl.ds((jb - 1) * Q_TILE, Q_TILE) if jb else None)
                           for jb in range(LR // Q_TILE)]):
            tile_group(scratch, grp, bias_r, y_stats, True)
        for grp in groups([(pl.ds(c + R * Q_TILE * jb, Q_TILE, stride=R),
                            pl.ds(c + R * Q_TILE * (jb - 1), Q_TILE, stride=R) if jb else None)
                           for c in range(R) for jb in range(LR // R // Q_TILE)]):
            tile_group(scratch, grp, bias_rr, y_stats, False)
        m_a, m_b = m_s[own, :], ym[...]
        m_new = jnp.maximum(m_a, m_b)
        e_a = jnp.exp(m_a - m_new)
        e_b = jnp.exp(m_b - m_new)
        acc_s[own, :] = (acc_s[own, :] * e_a + yacc[...] * e_b) / (l_s[own, :] * e_a + yl[...] * e_b)
        return carry

    lax.fori_loop(0, R, stream, 0)
    o_ref[0] = acc_s[...].astype(o_ref.dtype)


def _prompt_attn_scratch(T):
    return [pltpu.VMEM((T, LANES), F32)] * 3 + [pltpu.VMEM((T // DIL_PATTERNS[1][1], LANES), F32)] * 6


def _prompt_attn(q, k, v, slopes_tab):
    B, T, W = q.shape
    spec = pl.BlockSpec((1, T, LANES), lambda b, hp: (b, 0, hp))
    return pl.pallas_call(
        functools.partial(_prompt_attn_kernel, T=T),
        grid=(B, W // LANES),
        in_specs=[spec, spec, spec, _full(slopes_tab.shape)],
        out_specs=spec,
        out_shape=jax.ShapeDtypeStruct((B, T, W), BF16),
        scratch_shapes=_prompt_attn_scratch(T),
        compiler_params=_cparams(2),
        name="prompt_attn",
    )(q, k, v, slopes_tab)


def _rwkv_chunk_kernel(r_ref, lw_ref, k_ref, v_ref, al_ref, be_ref, g_ref, bo_ref, gnw_ref, gnb_ref,
                       y_ref, s_ref, st_scr, *, Tb, npair, nbb):
    C = CHUNK
    n2 = 2 * C
    t = pl.program_id(1)

    @pl.when(t == 0)
    def _():
        st_scr[...] = jnp.zeros_like(st_scr)

    lane = lax.broadcasted_iota(jnp.int32, (1, LANES), 1)
    m0 = (lane < HEAD_DIM).astype(F32)
    m1 = 1.0 - m0
    ri = lax.broadcasted_iota(jnp.int32, (n2, n2), 0)
    ci = lax.broadcasted_iota(jnp.int32, (n2, n2), 1)
    same = (ri < C) == (ci < C)
    strict = same & ((ri & (C - 1)) > (ci & (C - 1)))
    incl = same & ((ri & (C - 1)) >= (ci & (C - 1)))
    tri = (lax.broadcasted_iota(jnp.int32, (C, C), 0) >= lax.broadcasted_iota(jnp.int32, (C, C), 1)).astype(BF16)
    head_rows = ((lax.broadcasted_iota(jnp.int32, (n2, LANES), 0) < C)
                 == (lax.broadcasted_iota(jnp.int32, (n2, LANES), 1) < HEAD_DIM)).astype(F32)

    def stack(x):
        return jnp.concatenate([x * m0, x * m1], axis=0)

    def twice(x):
        return jnp.concatenate([x, x], axis=0)

    def body(c, carry):
        rows = pl.ds(pl.multiple_of(c * C, C), C)
        chains = [(bb, slice(hp * LANES, (hp + 1) * LANES)) for bb in range(nbb) for hp in range(npair)]
        ids = range(len(chains))
        S = [st_scr[i] for i in ids]
        lw_all = jnp.concatenate([lw_ref[bb, rows, :] for bb in range(nbb)], axis=1)
        cl_all = _dot_const_l(tri, lw_all)
        e_pos_all = jnp.exp(cl_all)
        e_neg_all = jnp.exp(-cl_all)
        e_exc_all = jnp.exp(cl_all - lw_all)
        wide = [slice(i * LANES, (i + 1) * LANES) for i in ids]
        g_end = [e_pos_all[C - 1:C, wide[i]] for i in ids]
        rt = [r_ref[bb, rows, cs] * e_pos_all[:, wide[i]] for i, (bb, cs) in enumerate(chains)]
        kt = [k_ref[bb, rows, cs] * e_neg_all[:, wide[i]] for i, (bb, cs) in enumerate(chains)]
        bt = [be_ref[bb, rows, cs] * e_neg_all[:, wide[i]] for i, (bb, cs) in enumerate(chains)]
        at = [al_ref[bb, rows, cs] * e_exc_all[:, wide[i]] for i, (bb, cs) in enumerate(chains)]
        ar_st = [jnp.concatenate([stack(at[i]), stack(rt[i])], axis=0) for i in ids]
        kb2 = [jnp.concatenate([twice(kt[i]), twice(bt[i])], axis=0) for i in ids]
        v_st = [stack(v_ref[bb, rows, cs]) for bb, cs in chains]
        gram = [_dot(ar_st[i], kb2[i], "NT") for i in ids]
        l_ab = [jnp.where(strict, gram[i][0:n2, n2:], 0.0) for i in ids]
        l_kk = [jnp.concatenate([jnp.where(strict, gram[i][0:n2, 0:n2], 0.0),
                                 jnp.where(incl, gram[i][n2:, 0:n2], 0.0)], axis=0) for i in ids]
        l_rb = [jnp.where(incl, gram[i][n2:, n2:], 0.0) for i in ids]
        from_state = [_dot(jnp.concatenate([at[i], rt[i]], axis=0), S[i], "NT") for i in ids]
        from_state = [jnp.concatenate([stack(from_state[i][0:C]), stack(from_state[i][C:])], axis=0) for i in ids]
        from_v = [_dot(l_kk[i], v_st[i]) for i in ids]
        x = l_ab
        u = [from_state[i][0:n2] + from_v[i][0:n2] for i in ids]
        for _ in range(C.bit_length() - 2):
            xu = [_dot(x[i], jnp.concatenate([x[i], u[i]], axis=1)) for i in ids]
            x = [xu[i][:, 0:n2] for i in ids]
            u = [u[i] + xu[i][:, n2:] for i in ids]
        u = [u[i] + _dot(x[i], u[i]) for i in ids]
        y_st = [from_state[i][n2:] + from_v[i][n2:] + _dot(l_rb[i], u[i]) for i in ids]
        for i in ids:
            vu = jnp.concatenate([v_st[i], u[i]], axis=0)
            kb_end = jnp.concatenate([stack(kt[i] * g_end[i]), stack(bt[i] * g_end[i])], axis=0)
            st_scr[i] = S[i] * g_end[i] + _dot(vu, kb_end, "TN")
        for i, (bb, cs) in enumerate(chains):
            mu = jnp.sum(y_st[i], axis=-1, keepdims=True) * (1.0 / HEAD_DIM)
            yc = (y_st[i] - mu) * head_rows
            var = jnp.sum(yc * yc, axis=-1, keepdims=True) * (1.0 / HEAD_DIM)
            yn_st = yc * lax.rsqrt(var + GN_EPS)
            yn_i = (yn_st[0:C, :] + yn_st[C:n2, :]) * gnw_ref[:, cs] + gnb_ref[:, cs]
            y_ref[bb, rows, cs] = ((yn_i + bo_ref[bb, rows, cs]) * g_ref[bb, rows, cs]).astype(y_ref.dtype)
        return carry

    lax.fori_loop(0, Tb // C, body, 0)

    @pl.when(t == pl.num_programs(1) - 1)
    def _():
        for i in range(nbb * npair):
            bb, hp = divmod(i, npair)
            pair = st_scr[i]
            s_ref[bb, 2 * hp] = pair[0:HEAD_DIM, 0:HEAD_DIM]
            s_ref[bb, 2 * hp + 1] = pltpu.roll(pair, HEAD_DIM, axis=1)[HEAD_DIM:, 0:HEAD_DIM]


def _rwkv_chunked(r, lw, k2, v, al, be, g, bonus, gn_w, gn_b, tb):
    B, T, W = r.shape
    npair = W // LANES
    nbb = next(n for n in (4, 2, 1) if B % n == 0)
    spec = pl.BlockSpec((nbb, tb, W), lambda b, t: (b, t, 0))
    st_spec = pl.BlockSpec((nbb, 2 * npair, HEAD_DIM, HEAD_DIM), lambda b, t: (b, 0, 0, 0))
    return pl.pallas_call(
        functools.partial(_rwkv_chunk_kernel, Tb=tb, npair=npair, nbb=nbb),
        grid=(B // nbb, T // tb),
        in_specs=[spec] * 8 + [_full(gn_w.shape), _full(gn_b.shape)],
        out_specs=[spec, st_spec],
        out_shape=[jax.ShapeDtypeStruct((B, T, W), BF16),
                   jax.ShapeDtypeStruct((B, 2 * npair, HEAD_DIM, HEAD_DIM), F32)],
        scratch_shapes=[pltpu.VMEM((nbb * npair, LANES, LANES), F32)],
        compiler_params=_cparams(2),
        name="rwkv_chunk",
    )(r, lw, k2, v, al, be, g, bonus, gn_w, gn_b)


def _rwkv_one_step(S, r, lw, k, al, be, v, g, bonus, gnw, gnb):
    sa = jnp.sum(S * al, axis=-1, keepdims=True)
    s_new = S * jnp.exp(lw) + sa * be + v * k
    y = jnp.sum(s_new * r, axis=-1, keepdims=True)
    mu = jnp.mean(y, axis=1, keepdims=True)
    yc = y - mu
    var = jnp.mean(yc * yc, axis=1, keepdims=True)
    yn = yc * lax.rsqrt(var + GN_EPS) * gnw + gnb
    return (yn + bonus) * g, s_new


def _attn_one_step(q, k_new, v_new, kt, vt, slopes, P):
    aw = q.shape[1]
    H = aw // HEAD_DIM
    scale = HEAD_DIM ** -0.5
    dist = P - lax.broadcasted_iota(jnp.int32, (1, P), 1)
    count = jnp.zeros((1, P), F32)
    for win, dil in DIL_PATTERNS:
        assert dil & (dil - 1) == 0
        count = count + ((dist <= win) & ((dist & (dil - 1)) == 0)).astype(F32)
    own_head = (lax.broadcasted_iota(jnp.int32, (H, aw), 1) // HEAD_DIM
                == lax.broadcasted_iota(jnp.int32, (H, aw), 0)).astype(F32)
    q_heads = q * own_head
    s = _dot(q_heads, kt) * scale - slopes * dist.astype(F32)
    s = jnp.where(count > 0.0, s, NEG)
    s0 = jnp.sum(q_heads * k_new, axis=-1, keepdims=True) * scale
    m = jnp.maximum(jnp.max(s, axis=-1, keepdims=True), s0)
    p = count * jnp.exp(s - m)
    e0 = len(DIL_PATTERNS) * jnp.exp(s0 - m)
    l = jnp.sum(p, axis=-1, keepdims=True) + e0
    acc = _dot(p, vt, "NT") + e0 * v_new
    return jnp.sum(acc * own_head, axis=0, keepdims=True) / jnp.sum(l * own_head, axis=0, keepdims=True)


def _mixer_step_kernel(*refs, P, aw):
    _mixer_step_body(pl.program_id(0), *refs, P=P, aw=aw)


N_MIXER_IN = 15


def _mixer_step_body(b, q_ref, kn_ref, vn_ref, col_ref, kt_ref, vt_ref, sl_ref, s_ref, r_ref, lw_ref, k_ref, al_ref,
                     be_ref, gnw_ref, gnb_ref, att_ref, y_ref, so_ref, *, P, aw):
    H = aw // HEAD_DIM
    own = (lax.broadcasted_iota(jnp.int32, (1, LANES), 1) == b).astype(F32)

    def column(i):
        return jnp.sum(col_ref[i * aw:(i + 1) * aw, :] * own, axis=-1, keepdims=True).reshape(H, HEAD_DIM, 1)

    @pl.when(b == 0)
    def _():
        y_ref[...] = jnp.zeros_like(y_ref)

    att_ref[0] = _attn_one_step(q_ref[0], kn_ref[0], vn_ref[0], kt_ref[0].reshape(aw, P), vt_ref[0].reshape(aw, P),
                                sl_ref[...], P)
    y, s_new = _rwkv_one_step(s_ref[0], r_ref[0], lw_ref[0], k_ref[0], al_ref[0], be_ref[0], column(0), column(1),
                              column(2), gnw_ref[...], gnb_ref[...])
    so_ref[0] = s_new
    y_ref[...] += y.reshape(aw, 1) * own


def _mixer_step_operands(seq_of, q, k_new, v_new, col, k_cache, v_cache, slopes, state, r, lw, k2, al, be, gn_w,
                         gn_b):
    nb, P, H, _ = k_cache.shape
    aw = H * HEAD_DIM
    rowv = lambda a: a.reshape(nb, H, 1, HEAD_DIM)
    vec = lambda a: a.reshape(nb, 1, aw)
    per_seq = lambda shape: pl.BlockSpec((1,) + shape, lambda *g: (seq_of(*g),) + (0,) * len(shape))
    t_spec, s_spec, r_spec = per_seq((H, HEAD_DIM, P)), per_seq((H, HEAD_DIM, HEAD_DIM)), per_seq((H, 1, HEAD_DIM))
    v_spec = per_seq((1, aw))
    p_spec = _full((H, HEAD_DIM, 1))
    args = (vec(q), vec(k_new), vec(v_new), col, jnp.transpose(k_cache, (0, 2, 3, 1)),
            jnp.transpose(v_cache, (0, 2, 3, 1)), slopes.reshape(H, 1), state, rowv(r), rowv(lw), rowv(k2), rowv(al),
            rowv(be), gn_w.reshape(H, HEAD_DIM, 1), gn_b.reshape(H, HEAD_DIM, 1))
    assert len(args) == N_MIXER_IN
    in_specs = ([v_spec] * 3 + [_full(col.shape), t_spec, t_spec, _full((H, 1)), s_spec] + [r_spec] * 5
                + [p_spec] * 2)
    out_specs = [v_spec, _full((aw, LANES)), s_spec]
    out_shape = [jax.ShapeDtypeStruct((nb, 1, aw), F32), jax.ShapeDtypeStruct((aw, LANES), F32),
                 jax.ShapeDtypeStruct(state.shape, F32)]
    return args, in_specs, out_specs, out_shape


def _mixer_step(*operands):
    args, in_specs, out_specs, out_shape = _mixer_step_operands(lambda b: b, *operands)
    nb, P, H, _ = operands[4].shape
    return pl.pallas_call(
        functools.partial(_mixer_step_kernel, P=P, aw=H * HEAD_DIM),
        grid=(nb,),
        in_specs=in_specs,
        out_specs=out_specs,
        out_shape=out_shape,
        compiler_params=_cparams(1),
        name="mixer_step",
    )(*args)


def _attn_mixer_kernel(q_ref, k_ref, v_ref, sl_ref, *refs, T, P, aw, npair):
    mixer_in, (o_ref, *mixer_out), scratch = refs[:N_MIXER_IN], refs[N_MIXER_IN:N_MIXER_IN + 4], refs[N_MIXER_IN + 4:]
    _mixer_step_body(pl.program_id(0) * npair + pl.program_id(1), *mixer_in, *mixer_out, P=P, aw=aw)
    _prompt_attn_kernel(q_ref, k_ref, v_ref, sl_ref, o_ref, *scratch, T=T)


def _prompt_attn_and_mixer_step(q, k, v, slopes_tab, *mixer_operands):
    B, T, W = q.shape
    npair = W // LANES
    args, in_specs, out_specs, out_shape = _mixer_step_operands(lambda b, hp: b * npair + hp, *mixer_operands)
    nb, P, H, _ = mixer_operands[4].shape
    assert nb == B * npair
    spec = pl.BlockSpec((1, T, LANES), lambda b, hp: (b, 0, hp))
    return pl.pallas_call(
        functools.partial(_attn_mixer_kernel, T=T, P=P, aw=H * HEAD_DIM, npair=npair),
        grid=(B, npair),
        in_specs=[spec, spec, spec, _full(slopes_tab.shape)] + in_specs,
        out_specs=[spec] + out_specs,
        out_shape=[jax.ShapeDtypeStruct((B, T, W), BF16)] + out_shape,
        scratch_shapes=_prompt_attn_scratch(T),
        compiler_params=_cparams(2),
        name="prompt_attn_mixer_step",
    )(q, k, v, slopes_tab, *args)


def _ffn_tail(x1, gate_pre, prev1, prev2, up, cw_ref, cb_ref, w2_ref, g_ref, b_ref, alpha):
    c = cw_ref[0:1, :] * prev2 + cw_ref[1:2, :] * prev1 + cw_ref[2:3, :] * gate_pre + cb_ref[...]
    h = c * _sigmoid(c) * up
    return _layer_norm(alpha * x1 + _dot(h, w2_ref[...]), g_ref[...], b_ref[...])


def _ffn_seq_kernel(att_ref, y_ref, x_ref, wo_ref, g1_ref, b1_ref, w1_ref, cw_ref, cb_ref, w2_ref, g_ref, b_ref,
                    o_ref, tail_ref, carry_ref, *, aw, dff, alpha):
    @pl.when(pl.program_id(1) == 0)
    def _():
        carry_ref[...] = jnp.zeros_like(carry_ref)

    tm = x_ref.shape[1]
    hr = tm // SUB_TILES
    spans = [slice(h * hr, (h + 1) * hr) for h in range(SUB_TILES)]
    n = range(SUB_TILES)
    row = lax.broadcasted_iota(jnp.int32, (hr, dff), 0)
    mixed = [_dot(att_ref[0, sp, :], wo_ref[0:aw, :]) + _dot(y_ref[0, sp, :], wo_ref[aw:, :]) for sp in spans]
    x1 = [_layer_norm(alpha * x_ref[0, spans[i], :] + mixed[i], g1_ref[...], b1_ref[...]) for i in n]
    hu = [_dot(x1[0], w1_ref[...])]
    last2 = carry_ref[0:2, :]
    gated, f = [], []
    for i in n:
        if i + 1 < SUB_TILES:
            hu.append(_dot(x1[i + 1], w1_ref[...]))
        gate_pre = hu[i][:, 0:dff]
        c0, c1 = last2[0:1, :], last2[1:2, :]
        prev1 = jnp.where(row == 0, c1, pltpu.roll(gate_pre, 1, axis=0))
        prev2 = jnp.where(row == 0, c0, jnp.where(row == 1, c1, pltpu.roll(gate_pre, 2, axis=0)))
        last2 = gate_pre[hr - 2:hr, :]
        c = cw_ref[0:1, :] * prev2 + cw_ref[1:2, :] * prev1 + cw_ref[2:3, :] * gate_pre + cb_ref[...]
        gated.append(c * _sigmoid(c) * hu[i][:, dff:])
        if i > 0:
            f.append(_dot(gated[i - 1], w2_ref[...]))
    f.append(_dot(gated[-1], w2_ref[...]))
    carry_ref[0:2, :] = last2
    tail_ref[0] = last2
    for i in n:
        o_ref[0, spans[i], :] = _layer_norm(alpha * x1[i] + f[i], g_ref[...], b_ref[...])


def _ffn_step_kernel(att_ref, y_ref, x_ref, wo_ref, g1_ref, b1_ref, p2_ref, p1_ref, w1_ref, cw_ref, cb_ref, w2_ref,
                     g_ref, b_ref, o_ref, gate_ref, *, dff, alpha):
    nb, aw = att_ref.shape
    h = _dot(att_ref[...], wo_ref[0:aw, :]) + _dot(y_ref[...].T[0:nb, :], wo_ref[aw:, :])
    x1 = _layer_norm(alpha * x_ref[...] + h, g1_ref[...], b1_ref[...])
    hu = _dot(x1, w1_ref[...])
    gate_pre = hu[:, 0:dff]
    gate_ref[...] = gate_pre
    o_ref[...] = _ffn_tail(x1, gate_pre, p1_ref[...], p2_ref[...], hu[:, dff:], cw_ref, cb_ref, w2_ref, g_ref,
                           b_ref, alpha)


def _resident(shape):
    nd = len(shape)
    return pl.BlockSpec(shape, lambda *_: (0,) * nd, pipeline_mode=pl.Buffered(1))


def _ffn_seq(att, y_rw, x, w_out, ln1_g, ln1_b, w1, conv_w, conv_b, w2, ln2_g, ln2_b, tm, alpha):
    B, T, D = x.shape
    aw = att.shape[-1]
    dff = w2.shape[0]
    half = pl.BlockSpec((1, tm, aw), lambda b, t: (b, t, 0))
    rows = pl.BlockSpec((1, tm, D), lambda b, t: (b, t, 0))
    return pl.pallas_call(
        functools.partial(_ffn_seq_kernel, aw=aw, dff=dff, alpha=alpha),
        grid=(B, T // tm),
        in_specs=[half, half, rows, _resident(w_out.shape), _full(ln1_g.shape), _full(ln1_b.shape),
                  _resident(w1.shape), _full(conv_w.shape), _full(conv_b.shape), _resident(w2.shape),
                  _full(ln2_g.shape), _full(ln2_b.shape)],
        out_specs=[rows, pl.BlockSpec((1, 2, dff), lambda b, t: (b, 0, 0))],
        out_shape=[jax.ShapeDtypeStruct((B, T, D), F32), jax.ShapeDtypeStruct((B, 2, dff), F32)],
        scratch_shapes=[pltpu.VMEM((8, dff), F32)],
        compiler_params=_cparams(2),
        name="ffn_seq",
    )(att, y_rw, x, w_out, ln1_g, ln1_b, w1, conv_w, conv_b, w2, ln2_g, ln2_b)


def _ffn_step(att, y_cols, x, w_out, ln1_g, ln1_b, prev2, prev1, w1, conv_w, conv_b, w2, ln2_g, ln2_b, alpha):
    nb, D = x.shape
    dff = w2.shape[0]
    return pl.pallas_call(
        functools.partial(_ffn_step_kernel, dff=dff, alpha=alpha),
        grid=(1,),
        in_specs=[_full(att.shape), _full(y_cols.shape), _full(x.shape), _resident(w_out.shape), _full(ln1_g.shape),
                  _full(ln1_b.shape),
                  _full(prev2.shape), _full(prev1.shape), _resident(w1.shape), _full(conv_w.shape),
                  _full(conv_b.shape), _resident(w2.shape), _full(ln2_g.shape), _full(ln2_b.shape)],
        out_specs=[_full((nb, D)), _full((nb, dff))],
        out_shape=[jax.ShapeDtypeStruct((nb, D), F32), jax.ShapeDtypeStruct((nb, dff), F32)],
        compiler_params=_cparams(1),
        name="ffn_step",
    )(att, y_cols, x, w_out, ln1_g, ln1_b, prev2, prev1, w1, conv_w, conv_b, w2, ln2_g, ln2_b)


def _layer_params(w_in, mu_shift, w0, w_lora_up, a0, a_lora_up, g_lora_up, k_k, k_a, r_k, aw):
    D, ncol = w_in.shape
    n_w, n_a, n_g = w_lora_up.shape[0], a_lora_up.shape[0], g_lora_up.shape[0]
    assert n_w + n_a == LANES and 3 * aw + 3 * aw + n_w + n_a + n_g == ncol
    g_cols = -(-n_g // LANES) * LANES
    pad = g_cols - n_g
    row = lambda a: a.reshape(1, -1).astype(F32)
    lane = jnp.arange(aw)
    bd = (lane[:, None] // HEAD_DIM == lane[None, :] // HEAD_DIM).astype(BF16)
    return dict(
        aw=aw,
        w_in=jnp.pad(w_in, ((0, 0), (0, pad))).astype(BF16),
        mu=jnp.pad(row(mu_shift), ((0, 0), (0, pad))),
        w0=row(w0), a0=row(a0), k_k=row(k_k), k_a=row(k_a), r_k=row(r_k),
        wl_pad=jnp.pad(w_lora_up, ((0, n_a), (0, 0))).astype(BF16),
        al_pad=jnp.pad(a_lora_up, ((n_w, 0), (0, 0))).astype(BF16),
        g_pad=jnp.pad(g_lora_up, ((0, pad), (0, 0))).astype(BF16),
        bd=bd,
    )


def kernel(x_prompt, x_sample, cache_k_win, cache_v_win, state_shift, state_wkv, state_conv, w_in, mu_shift, w0, w_lora_up, a0, a_lora_up, g_lora_up, k_k, k_a, r_k, gn_w, gn_b, w_out, ln1_g, ln1_b, w_ffn_in, conv_w, conv_b, w_ffn_out, ln2_g, ln2_b):
    depth = w_in.shape[0]
    alpha = (2.0 * depth) ** 0.25
    B, T, D = x_prompt.shape
    nb = x_sample.shape[0]
    assert x_sample.shape[1] == 1
    n_att = cache_k_win.shape[3]
    n_rw = state_wkv.shape[2]
    aw = n_att * HEAD_DIM
    assert n_rw * HEAD_DIM == aw and T % (Q_TILE * DIL_PATTERNS[-1][1]) == 0
    tm = min(256, T)
    tb = min(256, T)
    heads = jnp.arange(1, n_att + 1, dtype=F32)
    slopes = jnp.exp2(-8.0 * heads / n_att)
    slopes_tab = jnp.broadcast_to(slopes[:, None], (n_att, LANES))
    row = lambda a: a.reshape(1, -1)

    hp, hs = x_prompt, x_sample.reshape(nb, D)
    outs = [[] for _ in range(10)]
    for l in range(depth):
        prm = _layer_params(w_in[l], mu_shift[l], w0[l], w_lora_up[l], a0[l], a_lora_up[l], g_lora_up[l],
                            k_k[l], k_a[l], r_k[l], aw)
        w_out_b = w_out[l].astype(BF16)
        w1_b = w_ffn_in[l].astype(BF16)
        w2_b = w_ffn_out[l].astype(BF16)
        gnw, gnb = row(gn_w[l]), row(gn_b[l])

        q, k, v, kt, vt, r, lw, k2, vr, al, be, g, bonus = _inproj_seq(hp, prm, min(2 * tm, T))
        x2 = jnp.concatenate([hs, state_shift[l]], axis=0)
        qs, ks, vs, r_s, lw_s, k2_s, al_s, be_s, col = _inproj_step(x2, prm)
        mixer_operands = (qs, ks, vs, col, cache_k_win[l], cache_v_win[l], slopes, state_wkv[l], r_s, lw_s, k2_s,
                          al_s, be_s, gn_w[l], gn_b[l])

        if nb == B * (aw // LANES):
            att, att_s, y_s, wkv_s = _prompt_attn_and_mixer_step(q, k, v, slopes_tab, *mixer_operands)
        else:
            att = _prompt_attn(q, k, v, slopes_tab)
            att_s, y_s, wkv_s = _mixer_step(*mixer_operands)
        y_rw, wkv_p = _rwkv_chunked(r, lw, k2, vr, al, be, g, bonus, gnw, gnb, tb)

        shift_p = hp[:, -1, :]
        hp, conv_p = _ffn_seq(att, y_rw, hp, w_out_b, row(ln1_g[l]), row(ln1_b[l]), w1_b, conv_w[l],
                              row(conv_b[l]), w2_b, row(ln2_g[l]), row(ln2_b[l]), min(2 * tm, T), alpha)
        shift_s = hs
        hs, gate_s = _ffn_step(att_s.reshape(nb, aw), y_s, hs, w_out_b, row(ln1_g[l]), row(ln1_b[l]),
                               state_conv[l][:, 0, :],
                               state_conv[l][:, 1, :], w1_b, conv_w[l], row(conv_b[l]), w2_b, row(ln2_g[l]),
                               row(ln2_b[l]), alpha)
        conv_s = jnp.stack([state_conv[l][:, 1, :], gate_s], axis=1)

        n_keep = min(DIL_PATTERNS[-1][0], T)
        win = lambda a: jnp.transpose(a.reshape(B, n_att, HEAD_DIM, T), (0, 3, 1, 2))[:, T - n_keep:]
        vals = (win(kt), win(vt), ks.reshape(nb, 1, n_att, HEAD_DIM), vs.reshape(nb, 1, n_att, HEAD_DIM),
                shift_p, shift_s, wkv_p, wkv_s, conv_p, conv_s)
        for lst, val in zip(outs, vals):
            lst.append(val)
    return (hp, hs.reshape(nb, 1, D)) + tuple(jnp.stack(lst) for lst in outs)
```

```python
import functools
import math

import jax
import jax.numpy as jnp
from jax import lax
from jax.experimental import pallas as pl
from jax.experimental.pallas import tpu as pltpu

F32 = jnp.float32
BF16 = jnp.bfloat16

HEAD_DIM = 64
LANES = 128
DIL_PATTERNS = ((128, 1), (512, 4), (2048, 16))
WIN_STEPS = 128
Q_TILE = 128
GROUP = 4
SUB_TILES = 4
CHUNK = 64
LN_EPS = 1e-5
GN_EPS = 64e-5
NEG = -1e30
LOG2E = math.log2(math.e)
VMEM_LIMIT = 56 * 1024 * 1024

_DN = {"NN": (((1,), (0,)), ((), ())), "NT": (((1,), (1,)), ((), ())), "TN": (((0,), (0,)), ((), ()))}


def _dot(a, b, dims="NN"):
    return lax.dot_general(a.astype(BF16), b.astype(BF16), _DN[dims], preferred_element_type=F32)


def _dot_const_l(c, x):
    hi = x.astype(BF16)
    r1 = x - hi.astype(F32)
    mid = r1.astype(BF16)
    lo = (r1 - mid.astype(F32)).astype(BF16)
    return _dot(c, hi) + _dot(c, mid) + _dot(c, lo)


def _sigmoid(x):
    return 0.5 * jnp.tanh(0.5 * x) + 0.5


def _layer_norm(x, g, b):
    mu = jnp.mean(x, axis=-1, keepdims=True)
    xc = x - mu
    var = jnp.mean(xc * xc, axis=-1, keepdims=True)
    return xc * lax.rsqrt(var + LN_EPS) * g + b


def _cparams(n_grid):
    return pltpu.CompilerParams(dimension_semantics=("arbitrary",) * n_grid, vmem_limit_bytes=VMEM_LIMIT)


def _full(shape):
    nd = len(shape)
    return pl.BlockSpec(shape, lambda *_: (0,) * nd)


def _rwkv_prep(rw, w0, a0, k_k, k_a, r_k, wl_pad, al_pad, g_pad, bd, aw):
    r = rw[:, 0:aw]
    k = rw[:, aw:2 * aw]
    v = rw[:, 2 * aw:3 * aw]
    lo = rw[:, 3 * aw:3 * aw + 128]
    glo = rw[:, 3 * aw + 128:]
    u = w0 + _dot(jnp.tanh(lo), wl_pad)
    a = _sigmoid(a0 + _dot(lo, al_pad))
    g = _dot(_sigmoid(glo), g_pad)
    kk = k * k_k
    sumsq = _dot(kk * kk, bd)
    lw = -math.exp(-0.5) * _sigmoid(u)
    k2 = k * (1.0 + (a - 1.0) * k_a)
    bonus = _dot(r * k2 * r_k, bd) * v
    kkn = kk * lax.rsqrt(jnp.maximum(sumsq, 1e-24))
    return r, lw, k2, v, -kkn, kkn * a, g, bonus


def _inproj_seq_kernel(x_ref, w_ref, mu_ref, w0_ref, a0_ref, kk_ref, ka_ref, rk_ref, wl_ref, al_ref, gp_ref,
                       bd_ref, q_ref, k_ref, v_ref, kt_ref, vt_ref, r_o, lw_o, k2_o, vr_o, al_o, be_o, g_o, bo_o,
                       carry_ref, *, aw):
    @pl.when(pl.program_id(1) == 0)
    def _():
        carry_ref[...] = jnp.zeros_like(carry_ref)

    tm = x_ref.shape[1]
    hr = tm // SUB_TILES
    spans = [slice(h * hr, (h + 1) * hr) for h in range(SUB_TILES)]
    params = (w0_ref[...], a0_ref[...], kk_ref[...], ka_ref[...], rk_ref[...], wl_ref[...], al_ref[...],
              gp_ref[...], bd_ref[...], aw)
    row = lax.broadcasted_iota(jnp.int32, (hr, w_ref.shape[1] - 3 * aw), 0)

    def finish(p, sp, last):
        q_ref[0, sp, :] = p[:, 0:aw]
        k = p[:, aw:2 * aw]
        v = p[:, 2 * aw:3 * aw]
        k_ref[0, sp, :] = k
        v_ref[0, sp, :] = v
        kt_ref[0, :, sp] = k.T
        vt_ref[0, :, sp] = v.T
        prw = p[:, 3 * aw:]
        prev = jnp.where(row == 0, last, pltpu.roll(prw, 1, axis=0))
        outs = _rwkv_prep(prw + (prev - prw) * mu_ref[...], *params)
        for o_ref, val in zip((r_o, lw_o, k2_o, vr_o, al_o, be_o, g_o, bo_o), outs):
            o_ref[0, sp, :] = val.astype(o_ref.dtype)
        return prw[hr - 1:hr, :]

    last = carry_ref[0:1, :]
    p = _dot(x_ref[0, spans[0], :], w_ref[...])
    for h in range(1, SUB_TILES):
        p_next = _dot(x_ref[0, spans[h], :], w_ref[...])
        last = finish(p, spans[h - 1], last)
        p = p_next
    carry_ref[0:1, :] = finish(p, spans[-1], last)


def _inproj_step_kernel(x_ref, w_ref, mu_ref, w0_ref, a0_ref, kk_ref, ka_ref, rk_ref, wl_ref, al_ref, gp_ref,
                        bd_ref, q_ref, k_ref, v_ref, r_o, lw_o, k2_o, al_o, be_o, col_o, *, aw, nb):
    p = _dot(x_ref[...], w_ref[...])
    q_ref[...] = p[0:nb, 0:aw]
    k_ref[...] = p[0:nb, aw:2 * aw]
    v_ref[...] = p[0:nb, 2 * aw:3 * aw]
    prw = p[0:nb, 3 * aw:]
    prev = p[nb:2 * nb, 3 * aw:]
    rw = prw + (prev - prw) * mu_ref[...]
    r, lw, k2, vr, al, be, g, bonus = _rwkv_prep(rw, w0_ref[...], a0_ref[...], kk_ref[...], ka_ref[...],
                                                 rk_ref[...], wl_ref[...], al_ref[...], gp_ref[...], bd_ref[...], aw)
    for o_ref, val in zip((r_o, lw_o, k2_o, al_o, be_o), (r, lw, k2, al, be)):
        o_ref[...] = val
    pad = jnp.zeros((LANES - nb, aw), F32)
    for i, val in enumerate((vr, g, bonus)):
        col_o[i * aw:(i + 1) * aw, :] = jnp.concatenate([val, pad], axis=0).T


_PREP_PARAMS = ("mu", "w0", "a0", "k_k", "k_a", "r_k", "wl_pad", "al_pad", "g_pad", "bd")


def _inproj_seq(x, prm, tm):
    B, T, D = x.shape
    aw = prm["aw"]
    nrw = prm["w_in"].shape[1] - 3 * aw
    small = [prm[n] for n in _PREP_PARAMS]
    row_spec = pl.BlockSpec((1, tm, aw), lambda b, t: (b, t, 0))
    col_spec = pl.BlockSpec((1, aw, tm), lambda b, t: (b, 0, t))
    row_sds = jax.ShapeDtypeStruct((B, T, aw), F32)
    col_sds = jax.ShapeDtypeStruct((B, aw, T), F32)
    return pl.pallas_call(
        functools.partial(_inproj_seq_kernel, aw=aw),
        grid=(B, T // tm),
        in_specs=[pl.BlockSpec((1, tm, D), lambda b, t: (b, t, 0)), _resident(prm["w_in"].shape)]
        + [_full(s.shape) for s in small],
        out_specs=[row_spec] * 3 + [col_spec] * 2 + [row_spec] * 8,
        out_shape=[row_sds] * 3 + [col_sds] * 2
        + [jax.ShapeDtypeStruct((B, T, aw), F32 if n == "lw" else BF16)
           for n in ("r", "lw", "k2", "v", "al", "be", "g", "bonus")],
        scratch_shapes=[pltpu.VMEM((8, nrw), F32)],
        compiler_params=_cparams(2),
        name="inproj_seq",
    )(x, prm["w_in"], *small)


def _inproj_step(x2, prm):
    nb = x2.shape[0] // 2
    aw = prm["aw"]
    small = [prm[n] for n in _PREP_PARAMS]
    assert nb <= LANES
    out_sds = jax.ShapeDtypeStruct((nb, aw), F32)
    return pl.pallas_call(
        functools.partial(_inproj_step_kernel, aw=aw, nb=nb),
        grid=(1,),
        in_specs=[_full(x2.shape), _full(prm["w_in"].shape)] + [_full(s.shape) for s in small],
        out_specs=[_full((nb, aw))] * 8 + [_full((3 * aw, LANES))],
        out_shape=[out_sds] * 8 + [jax.ShapeDtypeStruct((3 * aw, LANES), F32)],
        compiler_params=_cparams(1),
        name="inproj_step",
    )(x2, prm["w_in"], *small)


def _prompt_attn_kernel(q_ref, k_ref, v_ref, sl_ref, o_ref, acc_s, m_s, l_s, yq, yk, yv, yacc, ym, yl, *, T):
    hp = pl.program_id(1)
    scale = HEAD_DIM ** -0.5 * LOG2E
    lane = lax.broadcasted_iota(jnp.int32, (1, LANES), 1)
    head_masks = ((lane < HEAD_DIM).astype(F32), (lane >= HEAD_DIM).astype(F32))
    qi = lax.broadcasted_iota(jnp.int32, (Q_TILE, Q_TILE), 0)
    ki = lax.broadcasted_iota(jnp.int32, (Q_TILE, Q_TILE), 1)
    steps_cur = (qi - ki).astype(F32)
    steps_prev = (qi - ki + Q_TILE).astype(F32)
    ok_cur = ki <= qi
    ok_prev = ki >= qi
    first_lanes = lax.broadcasted_iota(jnp.int32, (Q_TILE, LANES), 1) < HEAD_DIM
    slopes = [sl_ref[pl.ds(2 * hp + j, 1), :] * LOG2E for j in range(2)]

    def tile_group(src, tiles, bias, stats, first_branch):
        rq, rk, rv = src
        bias_c, bias_pc = bias
        acc_s, m_s, l_s = stats
        ids = range(len(tiles))
        q = [rq(rows) * scale for rows, _ in tiles]
        kk = [rk(rows) if prev is None else jnp.concatenate([rk(prev), rk(rows)], axis=0) for rows, prev in tiles]
        vv = [rv(rows) if prev is None else jnp.concatenate([rv(prev), rv(rows)], axis=0) for rows, prev in tiles]
        qq = [jnp.concatenate([q[t] * head_masks[0], q[t] * head_masks[1]], axis=0) for t in ids]
        s = [_dot(qq[t], kk[t], "NT") + (bias_c if tiles[t][1] is None else bias_pc) for t in ids]
        m = [jnp.max(s[t], axis=-1, keepdims=True) for t in ids]
        p = [jnp.exp2(s[t] - m[t]) for t in ids]
        l = [jnp.sum(p[t], axis=-1, keepdims=True) for t in ids]
        acc = [_dot(p[t], vv[t]) for t in ids]
        for t, (rows, _) in enumerate(tiles):
            acc_n = jnp.where(first_lanes, acc[t][0:Q_TILE], acc[t][Q_TILE:])
            m_n = jnp.where(first_lanes, m[t][0:Q_TILE], m[t][Q_TILE:])
            l_n = jnp.where(first_lanes, l[t][0:Q_TILE], l[t][Q_TILE:])
            if first_branch:
                acc_s[rows, :] = acc_n
                m_s[rows, :] = m_n
                l_s[rows, :] = l_n
            else:
                m_o = m_s[rows, :]
                m_new = jnp.maximum(m_o, m_n)
                e_o = jnp.exp2(m_o - m_new)
                e_n = jnp.exp2(m_n - m_new)
                acc_s[rows, :] = acc_s[rows, :] * e_o + acc_n * e_n
                l_s[rows, :] = l_s[rows, :] * e_o + l_n * e_n
                m_s[rows, :] = m_new

    dils = [dil for _, dil in DIL_PATTERNS]
    R = dils[1]
    assert dils == [1, R, R * R] and all(win // dil == WIN_STEPS for win, dil in DIL_PATTERNS)

    def bias_of(dil):
        b_cur = [jnp.where(ok_cur, -(slopes[j] * float(dil)) * steps_cur, NEG) for j in range(2)]
        b_prev = [jnp.where(ok_prev, -(slopes[j] * float(dil)) * steps_prev, NEG) for j in range(2)]
        return (jnp.concatenate(b_cur, axis=0),
                jnp.concatenate([jnp.concatenate([b_prev[j], b_cur[j]], axis=1) for j in range(2)], axis=0))

    def groups(tiles):
        return [tiles[i:i + GROUP] for i in range(0, len(tiles), GROUP)]

    natural = tuple((lambda rows, ref=ref: ref[0, rows, :]) for ref in (q_ref, k_ref, v_ref))
    scratch = tuple((lambda rows, ref=ref: ref[rows, :]) for ref in (yq, yk, yv))
    nat_stats = (acc_s, m_s, l_s)
    y_stats = (yacc, ym, yl)

    nb1 = T // Q_TILE
    assert nb1 % GROUP == 0
    bias1 = bias_of(dils[0])
    tile_group(natural, [(pl.ds(jb * Q_TILE, Q_TILE), pl.ds((jb - 1) * Q_TILE, Q_TILE) if jb else None)
                         for jb in range(GROUP)], bias1, nat_stats, True)

    def later(g, carry):
        start = pl.multiple_of(g * (GROUP * Q_TILE), GROUP * Q_TILE)
        tile_group(natural, [(pl.ds(start + i * Q_TILE, Q_TILE), pl.ds(start + (i - 1) * Q_TILE, Q_TILE))
                             for i in range(GROUP)], bias1, nat_stats, True)
        return carry

    lax.fori_loop(1, nb1 // GROUP, later, 0)

    LR = T // R
    bias_r = bias_of(dils[1])
    bias_rr = bias_of(dils[2])

    def stream(s, carry):
        own = pl.ds(s, LR, stride=R)
        yq[...] = q_ref[0, own, :]
        yk[...] = k_ref[0, own, :]
        yv[...] = v_ref[0, own, :]
        for grp in groups([(pl.ds(jb * Q_TILE, Q_TILE), pl.ds((jb - 1) * Q_TILE, Q_TILE) if jb else None)
                           for jb in range(LR // Q_TILE)]):
            tile_group(scratch, grp, bias_r, y_stats, True)
        for grp in groups([(pl.ds(c + R * Q_TILE * jb, Q_TILE, stride=R),
                            pl.ds(c + R * Q_TILE * (jb - 1), Q_TILE, stride=R) if jb else None)
                           for c in range(R) for jb in range(LR // R // Q_TILE)]):
            tile_group(scratch, grp, bias_rr, y_stats, False)
        m_a, m_b = m_s[own, :], ym[...]
        m_new = jnp.maximum(m_a, m_b)
        e_a = jnp.exp2(m_a - m_new)
        e_b = jnp.exp2(m_b - m_new)
        acc_s[own, :] = (acc_s[own, :] * e_a + yacc[...] * e_b) / (l_s[own, :] * e_a + yl[...] * e_b)
        return carry

    lax.fori_loop(0, R, stream, 0)
    o_ref[0] = acc_s[...].astype(o_ref.dtype)


def _prompt_attn_scratch(T):
    return [pltpu.VMEM((T, LANES), F32)] * 3 + [pltpu.VMEM((T // DIL_PATTERNS[1][1], LANES), F32)] * 6


def _prompt_attn(q, k, v, slopes_tab):
    B, T, W = q.shape
    spec = pl.BlockSpec((1, T, LANES), lambda b, hp: (b, 0, hp))
    return pl.pallas_call(
        functools.partial(_prompt_attn_kernel, T=T),
        grid=(B, W // LANES),
        in_specs=[spec, spec, spec, _full(slopes_tab.shape)],
        out_specs=spec,
        out_shape=jax.ShapeDtypeStruct((B, T, W), BF16),
        scratch_shapes=_prompt_attn_scratch(T),
        compiler_params=_cparams(2),
        name="prompt_attn",
    )(q, k, v, slopes_tab)


def _rwkv_chunk_kernel(r_ref, lw_ref, k_ref, v_ref, al_ref, be_ref, g_ref, bo_ref, gnw_ref, gnb_ref,
                       y_ref, s_ref, st_scr, *, Tb, npair, nbb):
    C = CHUNK
    n2 = 2 * C
    t = pl.program_id(1)

    @pl.when(t == 0)
    def _():
        st_scr[...] = jnp.zeros_like(st_scr)

    lane = lax.broadcasted_iota(jnp.int32, (1, LANES), 1)
    m0 = (lane < HEAD_DIM).astype(F32)
    m1 = 1.0 - m0
    ri = lax.broadcasted_iota(jnp.int32, (n2, n2), 0)
    ci = lax.broadcasted_iota(jnp.int32, (n2, n2), 1)
    same = (ri < C) == (ci < C)
    strict = same & ((ri & (C - 1)) > (ci & (C - 1)))
    incl = same & ((ri & (C - 1)) >= (ci & (C - 1)))
    tri = (lax.broadcasted_iota(jnp.int32, (C, C), 0) >= lax.broadcasted_iota(jnp.int32, (C, C), 1)).astype(BF16)
    head_rows = ((lax.broadcasted_iota(jnp.int32, (n2, LANES), 0) < C)
                 == (lax.broadcasted_iota(jnp.int32, (n2, LANES), 1) < HEAD_DIM)).astype(F32)

    def stack(x):
        return jnp.concatenate([x * m0, x * m1], axis=0)

    def twice(x):
        return jnp.concatenate([x, x], axis=0)

    def body(c, carry):
        rows = pl.ds(pl.multiple_of(c * C, C), C)
        chains = [(bb, slice(hp * LANES, (hp + 1) * LANES)) for bb in range(nbb) for hp in range(npair)]
        ids = range(len(chains))
        S = [st_scr[i] for i in ids]
        lw_all = jnp.concatenate([lw_ref[bb, rows, :] for bb in range(nbb)], axis=1)
        cl_all = _dot_const_l(tri, lw_all)
        e_pos_all = jnp.exp(cl_all)
        e_neg_all = jnp.exp(-cl_all)
        e_exc_all = jnp.exp(cl_all - lw_all)
        wide = [slice(i * LANES, (i + 1) * LANES) for i in ids]
        g_end = [e_pos_all[C - 1:C, wide[i]] for i in ids]
        rt = [r_ref[bb, rows, cs] * e_pos_all[:, wide[i]] for i, (bb, cs) in enumerate(chains)]
        kt = [k_ref[bb, rows, cs] * e_neg_all[:, wide[i]] for i, (bb, cs) in enumerate(chains)]
        bt = [be_ref[bb, rows, cs] * e_neg_all[:, wide[i]] for i, (bb, cs) in enumerate(chains)]
        at = [al_ref[bb, rows, cs] * e_exc_all[:, wide[i]] for i, (bb, cs) in enumerate(chains)]
        ar_st = [jnp.concatenate([stack(at[i]), stack(rt[i])], axis=0) for i in ids]
        kb2 = [jnp.concatenate([twice(kt[i]), twice(bt[i])], axis=0) for i in ids]
        v_st = [stack(v_ref[bb, rows, cs]) for bb, cs in chains]
        gram = [_dot(ar_st[i], kb2[i], "NT") for i in ids]
        l_ab = [jnp.where(strict, gram[i][0:n2, n2:], 0.0) for i in ids]
        l_kk = [jnp.concatenate([jnp.where(strict, gram[i][0:n2, 0:n2], 0.0),
                                 jnp.where(incl, gram[i][n2:, 0:n2], 0.0)], axis=0) for i in ids]
        l_rb = [jnp.where(incl, gram[i][n2:, n2:], 0.0) for i in ids]
        from_state = [_dot(jnp.concatenate([at[i], rt[i]], axis=0), S[i], "NT") for i in ids]
        from_state = [jnp.concatenate([stack(from_state[i][0:C]), stack(from_state[i][C:])], axis=0) for i in ids]
        from_v = [_dot(l_kk[i], v_st[i]) for i in ids]
        x = l_ab
        u = [from_state[i][0:n2] + from_v[i][0:n2] for i in ids]
        for _ in range(C.bit_length() - 2):
            xu = [_dot(x[i], jnp.concatenate([x[i], u[i]], axis=1)) for i in ids]
            x = [xu[i][:, 0:n2] for i in ids]
            u = [u[i] + xu[i][:, n2:] for i in ids]
        u = [u[i] + _dot(x[i], u[i]) for i in ids]
        y_st = [from_state[i][n2:] + from_v[i][n2:] + _dot(l_rb[i], u[i]) for i in ids]
        for i in ids:
            vu = jnp.concatenate([v_st[i], u[i]], axis=0)
            kb_end = jnp.concatenate([stack(kt[i] * g_end[i]), stack(bt[i] * g_end[i])], axis=0)
            st_scr[i] = S[i] * g_end[i] + _dot(vu, kb_end, "TN")
        for i, (bb, cs) in enumerate(chains):
            mu = jnp.sum(y_st[i], axis=-1, keepdims=True) * (1.0 / HEAD_DIM)
            yc = (y_st[i] - mu) * head_rows
            var = jnp.sum(yc * yc, axis=-1, keepdims=True) * (1.0 / HEAD_DIM)
            yn_st = yc * lax.rsqrt(var + GN_EPS)
            yn_i = (yn_st[0:C, :] + yn_st[C:n2, :]) * gnw_ref[:, cs] + gnb_ref[:, cs]
            y_ref[bb, rows, cs] = ((yn_i + bo_ref[bb, rows, cs]) * g_ref[bb, rows, cs]).astype(y_ref.dtype)
        return carry

    lax.fori_loop(0, Tb // C, body, 0)

    @pl.when(t == pl.num_programs(1) - 1)
    def _():
        for i in range(nbb * npair):
            bb, hp = divmod(i, npair)
            pair = st_scr[i]
            s_ref[bb, 2 * hp] = pair[0:HEAD_DIM, 0:HEAD_DIM]
            s_ref[bb, 2 * hp + 1] = pltpu.roll(pair, HEAD_DIM, axis=1)[HEAD_DIM:, 0:HEAD_DIM]


def _rwkv_chunked(r, lw, k2, v, al, be, g, bonus, gn_w, gn_b, tb):
    B, T, W = r.shape
    npair = W // LANES
    nbb = next(n for n in (4, 2, 1) if B % n == 0)
    spec = pl.BlockSpec((nbb, tb, W), lambda b, t: (b, t, 0))
    st_spec = pl.BlockSpec((nbb, 2 * npair, HEAD_DIM, HEAD_DIM), lambda b, t: (b, 0, 0, 0))
    return pl.pallas_call(
        functools.partial(_rwkv_chunk_kernel, Tb=tb, npair=npair, nbb=nbb),
        grid=(B // nbb, T // tb),
        in_specs=[spec] * 8 + [_full(gn_w.shape), _full(gn_b.shape)],
        out_specs=[spec, st_spec],
        out_shape=[jax.ShapeDtypeStruct((B, T, W), BF16),
                   jax.ShapeDtypeStruct((B, 2 * npair, HEAD_DIM, HEAD_DIM), F32)],
        scratch_shapes=[pltpu.VMEM((nbb * npair, LANES, LANES), F32)],
        compiler_params=_cparams(2),
        name="rwkv_chunk",
    )(r, lw, k2, v, al, be, g, bonus, gn_w, gn_b)


def _rwkv_one_step(S, r, lw, k, al, be, v, g, bonus, gnw, gnb):
    sa = jnp.sum(S * al, axis=-1, keepdims=True)
    s_new = S * jnp.exp(lw) + sa * be + v * k
    y = jnp.sum(s_new * r, axis=-1, keepdims=True)
    mu = jnp.mean(y, axis=1, keepdims=True)
    yc = y - mu
    var = jnp.mean(yc * yc, axis=1, keepdims=True)
    yn = yc * lax.rsqrt(var + GN_EPS) * gnw + gnb
    return (yn + bonus) * g, s_new


def _attn_one_step(q, k_new, v_new, kt, vt, slopes, P):
    aw = q.shape[1]
    H = aw // HEAD_DIM
    scale = HEAD_DIM ** -0.5
    dist = P - lax.broadcasted_iota(jnp.int32, (1, P), 1)
    count = jnp.zeros((1, P), F32)
    for win, dil in DIL_PATTERNS:
        assert dil & (dil - 1) == 0
        count = count + ((dist <= win) & ((dist & (dil - 1)) == 0)).astype(F32)
    own_head = (lax.broadcasted_iota(jnp.int32, (H, aw), 1) // HEAD_DIM
                == lax.broadcasted_iota(jnp.int32, (H, aw), 0)).astype(F32)
    q_heads = q * own_head
    s = _dot(q_heads, kt) * scale - slopes * dist.astype(F32)
    s = jnp.where(count > 0.0, s, NEG)
    s0 = jnp.sum(q_heads * k_new, axis=-1, keepdims=True) * scale
    m = jnp.maximum(jnp.max(s, axis=-1, keepdims=True), s0)
    p = count * jnp.exp(s - m)
    e0 = len(DIL_PATTERNS) * jnp.exp(s0 - m)
    l = jnp.sum(p, axis=-1, keepdims=True) + e0
    acc = _dot(p, vt, "NT") + e0 * v_new
    return jnp.sum(acc * own_head, axis=0, keepdims=True) / jnp.sum(l * own_head, axis=0, keepdims=True)


def _mixer_step_kernel(*refs, P, aw):
    _mixer_step_body(pl.program_id(0), *refs, P=P, aw=aw)


N_MIXER_IN = 15


def _mixer_step_body(b, q_ref, kn_ref, vn_ref, col_ref, kt_ref, vt_ref, sl_ref, s_ref, r_ref, lw_ref, k_ref, al_ref,
                     be_ref, gnw_ref, gnb_ref, att_ref, y_ref, so_ref, *, P, aw):
    H = aw // HEAD_DIM
    own = (lax.broadcasted_iota(jnp.int32, (1, LANES), 1) == b).astype(F32)

    def column(i):
        return jnp.sum(col_ref[i * aw:(i + 1) * aw, :] * own, axis=-1, keepdims=True).reshape(H, HEAD_DIM, 1)

    @pl.when(b == 0)
    def _():
        y_ref[...] = jnp.zeros_like(y_ref)

    att_ref[0] = _attn_one_step(q_ref[0], kn_ref[0], vn_ref[0], kt_ref[0].reshape(aw, P), vt_ref[0].reshape(aw, P),
                                sl_ref[...], P)
    y, s_new = _rwkv_one_step(s_ref[0], r_ref[0], lw_ref[0], k_ref[0], al_ref[0], be_ref[0], column(0), column(1),
                              column(2), gnw_ref[...], gnb_ref[...])
    so_ref[0] = s_new
    y_ref[...] += y.reshape(aw, 1) * own


def _mixer_step_operands(seq_of, q, k_new, v_new, col, k_cache, v_cache, slopes, state, r, lw, k2, al, be, gn_w,
                         gn_b):
    nb, P, H, _ = k_cache.shape
    aw = H * HEAD_DIM
    rowv = lambda a: a.reshape(nb, H, 1, HEAD_DIM)
    vec = lambda a: a.reshape(nb, 1, aw)
    per_seq = lambda shape: pl.BlockSpec((1,) + shape, lambda *g: (seq_of(*g),) + (0,) * len(shape))
    t_spec, s_spec, r_spec = per_seq((H, HEAD_DIM, P)), per_seq((H, HEAD_DIM, HEAD_DIM)), per_seq((H, 1, HEAD_DIM))
    v_spec = per_seq((1, aw))
    p_spec = _full((H, HEAD_DIM, 1))
    args = (vec(q), vec(k_new), vec(v_new), col, jnp.transpose(k_cache, (0, 2, 3, 1)),
            jnp.transpose(v_cache, (0, 2, 3, 1)), slopes.reshape(H, 1), state, rowv(r), rowv(lw), rowv(k2), rowv(al),
            rowv(be), gn_w.reshape(H, HEAD_DIM, 1), gn_b.reshape(H, HEAD_DIM, 1))
    assert len(args) == N_MIXER_IN
    in_specs = ([v_spec] * 3 + [_full(col.shape), t_spec, t_spec, _full((H, 1)), s_spec] + [r_spec] * 5
                + [p_spec] * 2)
    out_specs = [v_spec, _full((aw, LANES)), s_spec]
    out_shape = [jax.ShapeDtypeStruct((nb, 1, aw), F32), jax.ShapeDtypeStruct((aw, LANES), F32),
                 jax.ShapeDtypeStruct(state.shape, F32)]
    return args, in_specs, out_specs, out_shape


def _mixer_step(*operands):
    args, in_specs, out_specs, out_shape = _mixer_step_operands(lambda b: b, *operands)
    nb, P, H, _ = operands[4].shape
    return pl.pallas_call(
        functools.partial(_mixer_step_kernel, P=P, aw=H * HEAD_DIM),
        grid=(nb,),
        in_specs=in_specs,
        out_specs=out_specs,
        out_shape=out_shape,
        compiler_params=_cparams(1),
        name="mixer_step",
    )(*args)


def _attn_mixer_kernel(q_ref, k_ref, v_ref, sl_ref, *refs, T, P, aw, npair):
    mixer_in, (o_ref, *mixer_out), scratch = refs[:N_MIXER_IN], refs[N_MIXER_IN:N_MIXER_IN + 4], refs[N_MIXER_IN + 4:]
    _mixer_step_body(pl.program_id(0) * npair + pl.program_id(1), *mixer_in, *mixer_out, P=P, aw=aw)
    _prompt_attn_kernel(q_ref, k_ref, v_ref, sl_ref, o_ref, *scratch, T=T)


def _prompt_attn_and_mixer_step(q, k, v, slopes_tab, *mixer_operands):
    B, T, W = q.shape
    npair = W // LANES
    args, in_specs, out_specs, out_shape = _mixer_step_operands(lambda b, hp: b * npair + hp, *mixer_operands)
    nb, P, H, _ = mixer_operands[4].shape
    assert nb == B * npair
    spec = pl.BlockSpec((1, T, LANES), lambda b, hp: (b, 0, hp))
    return pl.pallas_call(
        functools.partial(_attn_mixer_kernel, T=T, P=P, aw=H * HEAD_DIM, npair=npair),
        grid=(B, npair),
        in_specs=[spec, spec, spec, _full(slopes_tab.shape)] + in_specs,
        out_specs=[spec] + out_specs,
        out_shape=[jax.ShapeDtypeStruct((B, T, W), BF16)] + out_shape,
        scratch_shapes=_prompt_attn_scratch(T),
        compiler_params=_cparams(2),
        name="prompt_attn_mixer_step",
    )(q, k, v, slopes_tab, *args)


def _ffn_tail(x1, gate_pre, prev1, prev2, up, cw_ref, cb_ref, w2_ref, g_ref, b_ref, alpha):
    c = cw_ref[0:1, :] * prev2 + cw_ref[1:2, :] * prev1 + cw_ref[2:3, :] * gate_pre + cb_ref[...]
    h = c * _sigmoid(c) * up
    return _layer_norm(alpha * x1 + _dot(h, w2_ref[...]), g_ref[...], b_ref[...])


def _ffn_seq_kernel(att_ref, y_ref, x_ref, wo_ref, g1_ref, b1_ref, w1_ref, cw_ref, cb_ref, w2_ref, g_ref, b_ref,
                    o_ref, tail_ref, carry_ref, *, aw, dff, alpha):
    @pl.when(pl.program_id(1) == 0)
    def _():
        carry_ref[...] = jnp.zeros_like(carry_ref)

    tm = x_ref.shape[1]
    hr = tm // SUB_TILES
    spans = [slice(h * hr, (h + 1) * hr) for h in range(SUB_TILES)]
    n = range(SUB_TILES)
    row = lax.broadcasted_iota(jnp.int32, (hr, dff), 0)
    mixed = [_dot(att_ref[0, sp, :], wo_ref[0:aw, :]) + _dot(y_ref[0, sp, :], wo_ref[aw:, :]) for sp in spans]
    x1 = [_layer_norm(alpha * x_ref[0, spans[i], :] + mixed[i], g1_ref[...], b1_ref[...]) for i in n]
    hu = [_dot(x1[0], w1_ref[...])]
    last2 = carry_ref[0:2, :]
    gated, f = [], []
    for i in n:
        if i + 1 < SUB_TILES:
            hu.append(_dot(x1[i + 1], w1_ref[...]))
        gate_pre = hu[i][:, 0:dff]
        c0, c1 = last2[0:1, :], last2[1:2, :]
        prev1 = jnp.where(row == 0, c1, pltpu.roll(gate_pre, 1, axis=0))
        prev2 = jnp.where(row == 0, c0, jnp.where(row == 1, c1, pltpu.roll(gate_pre, 2, axis=0)))
        last2 = gate_pre[hr - 2:hr, :]
        c = cw_ref[0:1, :] * prev2 + cw_ref[1:2, :] * prev1 + cw_ref[2:3, :] * gate_pre + cb_ref[...]
        gated.append(c * _sigmoid(c) * hu[i][:, dff:])
        if i > 0:
            f.append(_dot(gated[i - 1], w2_ref[...]))
    f.append(_dot(gated[-1], w2_ref[...]))
    carry_ref[0:2, :] = last2
    tail_ref[0] = last2
    for i in n:
        o_ref[0, spans[i], :] = _layer_norm(alpha * x1[i] + f[i], g_ref[...], b_ref[...])


def _ffn_step_kernel(att_ref, y_ref, x_ref, wo_ref, g1_ref, b1_ref, p2_ref, p1_ref, w1_ref, cw_ref, cb_ref, w2_ref,
                     g_ref, b_ref, o_ref, gate_ref, *, dff, alpha):
    nb, aw = att_ref.shape
    h = _dot(att_ref[...], wo_ref[0:aw, :]) + _dot(y_ref[...].T[0:nb, :], wo_ref[aw:, :])
    x1 = _layer_norm(alpha * x_ref[...] + h, g1_ref[...], b1_ref[...])
    hu = _dot(x1, w1_ref[...])
    gate_pre = hu[:, 0:dff]
    gate_ref[...] = gate_pre
    o_ref[...] = _ffn_tail(x1, gate_pre, p1_ref[...], p2_ref[...], hu[:, dff:], cw_ref, cb_ref, w2_ref, g_ref,
                           b_ref, alpha)


def _resident(shape):
    nd = len(shape)
    return pl.BlockSpec(shape, lambda *_: (0,) * nd, pipeline_mode=pl.Buffered(1))


def _ffn_seq(att, y_rw, x, w_out, ln1_g, ln1_b, w1, conv_w, conv_b, w2, ln2_g, ln2_b, tm, alpha):
    B, T, D = x.shape
    aw = att.shape[-1]
    dff = w2.shape[0]
    half = pl.BlockSpec((1, tm, aw), lambda b, t: (b, t, 0))
    rows = pl.BlockSpec((1, tm, D), lambda b, t: (b, t, 0))
    return pl.pallas_call(
        functools.partial(_ffn_seq_kernel, aw=aw, dff=dff, alpha=alpha),
        grid=(B, T // tm),
        in_specs=[half, half, rows, _resident(w_out.shape), _full(ln1_g.shape), _full(ln1_b.shape),
                  _resident(w1.shape), _full(conv_w.shape), _full(conv_b.shape), _resident(w2.shape),
                  _full(ln2_g.shape), _full(ln2_b.shape)],
        out_specs=[rows, pl.BlockSpec((1, 2, dff), lambda b, t: (b, 0, 0))],
        out_shape=[jax.ShapeDtypeStruct((B, T, D), F32), jax.ShapeDtypeStruct((B, 2, dff), F32)],
        scratch_shapes=[pltpu.VMEM((8, dff), F32)],
        compiler_params=_cparams(2),
        name="ffn_seq",
    )(att, y_rw, x, w_out, ln1_g, ln1_b, w1, conv_w, conv_b, w2, ln2_g, ln2_b)


def _ffn_step(att, y_cols, x, w_out, ln1_g, ln1_b, prev2, prev1, w1, conv_w, conv_b, w2, ln2_g, ln2_b, alpha):
    nb, D = x.shape
    dff = w2.shape[0]
    return pl.pallas_call(
        functools.partial(_ffn_step_kernel, dff=dff, alpha=alpha),
        grid=(1,),
        in_specs=[_full(att.shape), _full(y_cols.shape), _full(x.shape), _resident(w_out.shape), _full(ln1_g.shape),
                  _full(ln1_b.shape),
                  _full(prev2.shape), _full(prev1.shape), _resident(w1.shape), _full(conv_w.shape),
                  _full(conv_b.shape), _resident(w2.shape), _full(ln2_g.shape), _full(ln2_b.shape)],
        out_specs=[_full((nb, D)), _full((nb, dff))],
        out_shape=[jax.ShapeDtypeStruct((nb, D), F32), jax.ShapeDtypeStruct((nb, dff), F32)],
        compiler_params=_cparams(1),
        name="ffn_step",
    )(att, y_cols, x, w_out, ln1_g, ln1_b, prev2, prev1, w1, conv_w, conv_b, w2, ln2_g, ln2_b)


def _layer_params(w_in, mu_shift, w0, w_lora_up, a0, a_lora_up, g_lora_up, k_k, k_a, r_k, aw):
    D, ncol = w_in.shape
    n_w, n_a, n_g = w_lora_up.shape[0], a_lora_up.shape[0], g_lora_up.shape[0]
    assert n_w + n_a == LANES and 3 * aw + 3 * aw + n_w + n_a + n_g == ncol
    g_cols = -(-n_g // LANES) * LANES
    pad = g_cols - n_g
    row = lambda a: a.reshape(1, -1).astype(F32)
    lane = jnp.arange(aw)
    bd = (lane[:, None] // HEAD_DIM == lane[None, :] // HEAD_DIM).astype(BF16)
    return dict(
        aw=aw,
        w_in=jnp.pad(w_in, ((0, 0), (0, pad))).astype(BF16),
        mu=jnp.pad(row(mu_shift), ((0, 0), (0, pad))),
        w0=row(w0), a0=row(a0), k_k=row(k_k), k_a=row(k_a), r_k=row(r_k),
        wl_pad=jnp.pad(w_lora_up, ((0, n_a), (0, 0))).astype(BF16),
        al_pad=jnp.pad(a_lora_up, ((n_w, 0), (0, 0))).astype(BF16),
        g_pad=jnp.pad(g_lora_up, ((0, pad), (0, 0))).astype(BF16),
        bd=bd,
    )


def kernel(x_prompt, x_sample, cache_k_win, cache_v_win, state_shift, state_wkv, state_conv, w_in, mu_shift, w0, w_lora_up, a0, a_lora_up, g_lora_up, k_k, k_a, r_k, gn_w, gn_b, w_out, ln1_g, ln1_b, w_ffn_in, conv_w, conv_b, w_ffn_out, ln2_g, ln2_b):
    depth = w_in.shape[0]
    alpha = (2.0 * depth) ** 0.25
    B, T, D = x_prompt.shape
    nb = x_sample.shape[0]
    assert x_sample.shape[1] == 1
    n_att = cache_k_win.shape[3]
    n_rw = state_wkv.shape[2]
    aw = n_att * HEAD_DIM
    assert n_rw * HEAD_DIM == aw and T % (Q_TILE * DIL_PATTERNS[-1][1]) == 0
    tm = min(256, T)
    tb = min(256, T)
    heads = jnp.arange(1, n_att + 1, dtype=F32)
    slopes = jnp.exp2(-8.0 * heads / n_att)
    slopes_tab = jnp.broadcast_to(slopes[:, None], (n_att, LANES))
    row = lambda a: a.reshape(1, -1)

    hp, hs = x_prompt, x_sample.reshape(nb, D)
    outs = [[] for _ in range(10)]
    for l in range(depth):
        prm = _layer_params(w_in[l], mu_shift[l], w0[l], w_lora_up[l], a0[l], a_lora_up[l], g_lora_up[l],
                            k_k[l], k_a[l], r_k[l], aw)
        w_out_b = w_out[l].astype(BF16)
        w1_b = w_ffn_in[l].astype(BF16)
        w2_b = w_ffn_out[l].astype(BF16)
        gnw, gnb = row(gn_w[l]), row(gn_b[l])

        q, k, v, kt, vt, r, lw, k2, vr, al, be, g, bonus = _inproj_seq(hp, prm, min(2 * tm, T))
        x2 = jnp.concatenate([hs, state_shift[l]], axis=0)
        qs, ks, vs, r_s, lw_s, k2_s, al_s, be_s, col = _inproj_step(x2, prm)
        mixer_operands = (qs, ks, vs, col, cache_k_win[l], cache_v_win[l], slopes, state_wkv[l], r_s, lw_s, k2_s,
                          al_s, be_s, gn_w[l], gn_b[l])

        if nb == B * (aw // LANES):
            att, att_s, y_s, wkv_s = _prompt_attn_and_mixer_step(q, k, v, slopes_tab, *mixer_operands)
        else:
            att = _prompt_attn(q, k, v, slopes_tab)
            att_s, y_s, wkv_s = _mixer_step(*mixer_operands)
        y_rw, wkv_p = _rwkv_chunked(r, lw, k2, vr, al, be, g, bonus, gnw, gnb, tb)

        shift_p = hp[:, -1, :]
        hp, conv_p = _ffn_seq(att, y_rw, hp, w_out_b, row(ln1_g[l]), row(ln1_b[l]), w1_b, conv_w[l],
                              row(conv_b[l]), w2_b, row(ln2_g[l]), row(ln2_b[l]), min(2 * tm, T), alpha)
        shift_s = hs
        hs, gate_s = _ffn_step(att_s.reshape(nb, aw), y_s, hs, w_out_b, row(ln1_g[l]), row(ln1_b[l]),
                               state_conv[l][:, 0, :],
                               state_conv[l][:, 1, :], w1_b, conv_w[l], row(conv_b[l]), w2_b, row(ln2_g[l]),
                               row(ln2_b[l]), alpha)
        conv_s = jnp.stack([state_conv[l][:, 1, :], gate_s], axis=1)

        n_keep = min(DIL_PATTERNS[-1][0], T)
        win = lambda a: jnp.transpose(a.reshape(B, n_att, HEAD_DIM, T), (0, 3, 1, 2))[:, T - n_keep:]
        vals = (win(kt), win(vt), ks.reshape(nb, 1, n_att, HEAD_DIM), vs.reshape(nb, 1, n_att, HEAD_DIM),
                shift_p, shift_s, wkv_p, wkv_s, conv_p, conv_s)
        for lst, val in zip(outs, vals):
            lst.append(val)
    return (hp, hs.reshape(nb, 1, D)) + tuple(jnp.stack(lst) for lst in outs)
```

```python
import functools
import math

import jax
import jax.numpy as jnp
from jax import lax
from jax.experimental import pallas as pl
from jax.experimental.pallas import tpu as pltpu

F32 = jnp.float32
BF16 = jnp.bfloat16

HEAD_DIM = 64
LANES = 128
DIL_PATTERNS = ((128, 1), (512, 4), (2048, 16))
WIN_STEPS = 128
Q_TILE = 128
GROUP = 4
SUB_TILES = 4
FFN_SUB_ROWS = 128
CHUNK = 64
LN_EPS = 1e-5
GN_EPS = 64e-5
NEG = -1e30
VMEM_LIMIT = 56 * 1024 * 1024

_DN = {"NN": (((1,), (0,)), ((), ())), "NT": (((1,), (1,)), ((), ())), "TN": (((0,), (0,)), ((), ()))}


def _dot(a, b, dims="NN"):
    return lax.dot_general(a.astype(BF16), b.astype(BF16), _DN[dims], preferred_element_type=F32)


def _dot_const_l(c, x):
    hi = x.astype(BF16)
    r1 = x - hi.astype(F32)
    mid = r1.astype(BF16)
    lo = (r1 - mid.astype(F32)).astype(BF16)
    return _dot(c, hi) + _dot(c, mid) + _dot(c, lo)


def _sigmoid(x):
    return 0.5 * jnp.tanh(0.5 * x) + 0.5


def _layer_norm(x, g, b):
    mu = jnp.mean(x, axis=-1, keepdims=True)
    xc = x - mu
    var = jnp.mean(xc * xc, axis=-1, keepdims=True)
    return xc * lax.rsqrt(var + LN_EPS) * g + b


def _cparams(n_grid):
    return pltpu.CompilerParams(dimension_semantics=("arbitrary",) * n_grid, vmem_limit_bytes=VMEM_LIMIT)


def _full(shape):
    nd = len(shape)
    return pl.BlockSpec(shape, lambda *_: (0,) * nd)


def _rwkv_prep(rw, w0, a0, k_k, k_a, r_k, wl_pad, al_pad, g_pad, bd, aw):
    r = rw[:, 0:aw]
    k = rw[:, aw:2 * aw]
    v = rw[:, 2 * aw:3 * aw]
    lo = rw[:, 3 * aw:3 * aw + 128]
    glo = rw[:, 3 * aw + 128:]
    u = w0 + _dot(jnp.tanh(lo), wl_pad)
    a = _sigmoid(a0 + _dot(lo, al_pad))
    g = _dot(_sigmoid(glo), g_pad)
    kk = k * k_k
    sumsq = _dot(kk * kk, bd)
    lw = -math.exp(-0.5) * _sigmoid(u)
    k2 = k * (1.0 + (a - 1.0) * k_a)
    bonus = _dot(r * k2 * r_k, bd) * v
    kkn = kk * lax.rsqrt(jnp.maximum(sumsq, 1e-24))
    return r, lw, k2, v, -kkn, kkn * a, g, bonus


def _inproj_seq_kernel(x_ref, w_ref, mu_ref, w0_ref, a0_ref, kk_ref, ka_ref, rk_ref, wl_ref, al_ref, gp_ref,
                       bd_ref, q_ref, k_ref, v_ref, kt_ref, vt_ref, r_o, lw_o, k2_o, vr_o, al_o, be_o, g_o, bo_o,
                       carry_ref, *, aw):
    @pl.when(pl.program_id(1) == 0)
    def _():
        carry_ref[...] = jnp.zeros_like(carry_ref)

    tm = x_ref.shape[1]
    hr = tm // SUB_TILES
    spans = [slice(h * hr, (h + 1) * hr) for h in range(SUB_TILES)]
    params = (w0_ref[...], a0_ref[...], kk_ref[...], ka_ref[...], rk_ref[...], wl_ref[...], al_ref[...],
              gp_ref[...], bd_ref[...], aw)
    row = lax.broadcasted_iota(jnp.int32, (hr, w_ref.shape[1] - 3 * aw), 0)

    def finish(p, sp, last):
        q_ref[0, sp, :] = p[:, 0:aw]
        k = p[:, aw:2 * aw]
        v = p[:, 2 * aw:3 * aw]
        k_ref[0, sp, :] = k
        v_ref[0, sp, :] = v
        kt_ref[0, :, sp] = k.T
        vt_ref[0, :, sp] = v.T
        prw = p[:, 3 * aw:]
        prev = jnp.where(row == 0, last, pltpu.roll(prw, 1, axis=0))
        outs = _rwkv_prep(prw + (prev - prw) * mu_ref[...], *params)
        for o_ref, val in zip((r_o, lw_o, k2_o, vr_o, al_o, be_o, g_o, bo_o), outs):
            o_ref[0, sp, :] = val.astype(o_ref.dtype)
        return prw[hr - 1:hr, :]

    last = carry_ref[0:1, :]
    p = _dot(x_ref[0, spans[0], :], w_ref[...])
    for h in range(1, SUB_TILES):
        p_next = _dot(x_ref[0, spans[h], :], w_ref[...])
        last = finish(p, spans[h - 1], last)
        p = p_next
    carry_ref[0:1, :] = finish(p, spans[-1], last)


def _inproj_step_kernel(x_ref, w_ref, mu_ref, w0_ref, a0_ref, kk_ref, ka_ref, rk_ref, wl_ref, al_ref, gp_ref,
                        bd_ref, q_ref, k_ref, v_ref, r_o, lw_o, k2_o, al_o, be_o, col_o, *, aw, nb):
    p = _dot(x_ref[...], w_ref[...])
    q_ref[...] = p[0:nb, 0:aw]
    k_ref[...] = p[0:nb, aw:2 * aw]
    v_ref[...] = p[0:nb, 2 * aw:3 * aw]
    prw = p[0:nb, 3 * aw:]
    prev = p[nb:2 * nb, 3 * aw:]
    rw = prw + (prev - prw) * mu_ref[...]
    r, lw, k2, vr, al, be, g, bonus = _rwkv_prep(rw, w0_ref[...], a0_ref[...], kk_ref[...], ka_ref[...],
                                                 rk_ref[...], wl_ref[...], al_ref[...], gp_ref[...], bd_ref[...], aw)
    for o_ref, val in zip((r_o, lw_o, k2_o, al_o, be_o), (r, lw, k2, al, be)):
        o_ref[...] = val
    pad = jnp.zeros((LANES - nb, aw), F32)
    for i, val in enumerate((vr, g, bonus)):
        col_o[i * aw:(i + 1) * aw, :] = jnp.concatenate([val, pad], axis=0).T


_PREP_PARAMS = ("mu", "w0", "a0", "k_k", "k_a", "r_k", "wl_pad", "al_pad", "g_pad", "bd")


def _inproj_seq(x, prm, tm):
    B, T, D = x.shape
    aw = prm["aw"]
    nrw = prm["w_in"].shape[1] - 3 * aw
    small = [prm[n] for n in _PREP_PARAMS]
    row_spec = pl.BlockSpec((1, tm, aw), lambda b, t: (b, t, 0))
    col_spec = pl.BlockSpec((1, aw, tm), lambda b, t: (b, 0, t))
    row_sds = jax.ShapeDtypeStruct((B, T, aw), F32)
    col_sds = jax.ShapeDtypeStruct((B, aw, T), F32)
    return pl.pallas_call(
        functools.partial(_inproj_seq_kernel, aw=aw),
        grid=(B, T // tm),
        in_specs=[pl.BlockSpec((1, tm, D), lambda b, t: (b, t, 0)), _resident(prm["w_in"].shape)]
        + [_full(s.shape) for s in small],
        out_specs=[row_spec] * 3 + [col_spec] * 2 + [row_spec] * 8,
        out_shape=[row_sds] * 3 + [col_sds] * 2
        + [jax.ShapeDtypeStruct((B, T, aw), F32 if n == "lw" else BF16)
           for n in ("r", "lw", "k2", "v", "al", "be", "g", "bonus")],
        scratch_shapes=[pltpu.VMEM((8, nrw), F32)],
        compiler_params=_cparams(2),
        name="inproj_seq",
    )(x, prm["w_in"], *small)


def _inproj_step(x2, prm):
    nb = x2.shape[0] // 2
    aw = prm["aw"]
    small = [prm[n] for n in _PREP_PARAMS]
    assert nb <= LANES
    out_sds = jax.ShapeDtypeStruct((nb, aw), F32)
    return pl.pallas_call(
        functools.partial(_inproj_step_kernel, aw=aw, nb=nb),
        grid=(1,),
        in_specs=[_full(x2.shape), _full(prm["w_in"].shape)] + [_full(s.shape) for s in small],
        out_specs=[_full((nb, aw))] * 8 + [_full((3 * aw, LANES))],
        out_shape=[out_sds] * 8 + [jax.ShapeDtypeStruct((3 * aw, LANES), F32)],
        compiler_params=_cparams(1),
        name="inproj_step",
    )(x2, prm["w_in"], *small)


def _prompt_attn_kernel(q_ref, k_ref, v_ref, sl_ref, o_ref, acc_s, m_s, l_s, yq, yk, yv, yacc, ym, yl, *, T):
    hp = pl.program_id(1)
    scale = HEAD_DIM ** -0.5
    lane = lax.broadcasted_iota(jnp.int32, (1, LANES), 1)
    head_masks = ((lane < HEAD_DIM).astype(F32), (lane >= HEAD_DIM).astype(F32))
    qi = lax.broadcasted_iota(jnp.int32, (Q_TILE, Q_TILE), 0)
    ki = lax.broadcasted_iota(jnp.int32, (Q_TILE, Q_TILE), 1)
    steps_cur = (qi - ki).astype(F32)
    steps_prev = (qi - ki + Q_TILE).astype(F32)
    ok_cur = ki <= qi
    ok_prev = ki >= qi
    first_lanes = lax.broadcasted_iota(jnp.int32, (Q_TILE, LANES), 1) < HEAD_DIM
    slopes = [sl_ref[pl.ds(2 * hp + j, 1), :] for j in range(2)]

    def tile_group(src, tiles, bias, stats, first_branch):
        rq, rk, rv = src
        bias_c, bias_pc = bias
        acc_s, m_s, l_s = stats
        ids = range(len(tiles))
        q = [rq(rows) * scale for rows, _ in tiles]
        kk = [rk(rows) if prev is None else jnp.concatenate([rk(prev), rk(rows)], axis=0) for rows, prev in tiles]
        vv = [rv(rows) if prev is None else jnp.concatenate([rv(prev), rv(rows)], axis=0) for rows, prev in tiles]
        qq = [jnp.concatenate([q[t] * head_masks[0], q[t] * head_masks[1]], axis=0) for t in ids]
        s = [_dot(qq[t], kk[t], "NT") + (bias_c if tiles[t][1] is None else bias_pc) for t in ids]
        m = [jnp.max(s[t], axis=-1, keepdims=True) for t in ids]
        p = [jnp.exp(s[t] - m[t]) for t in ids]
        l = [jnp.sum(p[t], axis=-1, keepdims=True) for t in ids]
        acc = [_dot(p[t], vv[t]) for t in ids]
        for t, (rows, _) in enumerate(tiles):
            acc_n = jnp.where(first_lanes, acc[t][0:Q_TILE], acc[t][Q_TILE:])
            m_n = jnp.where(first_lanes, m[t][0:Q_TILE], m[t][Q_TILE:])
            l_n = jnp.where(first_lanes, l[t][0:Q_TILE], l[t][Q_TILE:])
            if first_branch:
                acc_s[rows, :] = acc_n
                m_s[rows, :] = m_n
                l_s[rows, :] = l_n
            else:
                m_o = m_s[rows, :]
                m_new = jnp.maximum(m_o, m_n)
                e_o = jnp.exp(m_o - m_new)
                e_n = jnp.exp(m_n - m_new)
                acc_s[rows, :] = acc_s[rows, :] * e_o + acc_n * e_n
                l_s[rows, :] = l_s[rows, :] * e_o + l_n * e_n
                m_s[rows, :] = m_new

    dils = [dil for _, dil in DIL_PATTERNS]
    R = dils[1]
    assert dils == [1, R, R * R] and all(win // dil == WIN_STEPS for win, dil in DIL_PATTERNS)

    def bias_of(dil):
        b_cur = [jnp.where(ok_cur, -(slopes[j] * float(dil)) * steps_cur, NEG) for j in range(2)]
        b_prev = [jnp.where(ok_prev, -(slopes[j] * float(dil)) * steps_prev, NEG) for j in range(2)]
        return (jnp.concatenate(b_cur, axis=0),
                jnp.concatenate([jnp.concatenate([b_prev[j], b_cur[j]], axis=1) for j in range(2)], axis=0))

    def groups(tiles):
        return [tiles[i:i + GROUP] for i in range(0, len(tiles), GROUP)]

    natural = tuple((lambda rows, ref=ref: ref[0, rows, :]) for ref in (q_ref, k_ref, v_ref))
    scratch = tuple((lambda rows, ref=ref: ref[rows, :]) for ref in (yq, yk, yv))
    nat_stats = (acc_s, m_s, l_s)
    y_stats = (yacc, ym, yl)

    nb1 = T // Q_TILE
    assert nb1 % GROUP == 0
    bias1 = bias_of(dils[0])
    tile_group(natural, [(pl.ds(jb * Q_TILE, Q_TILE), pl.ds((jb - 1) * Q_TILE, Q_TILE) if jb else None)
                         for jb in range(GROUP)], bias1, nat_stats, True)

    def later(g, carry):
        start = pl.multiple_of(g * (GROUP * Q_TILE), GROUP * Q_TILE)
        tile_group(natural, [(pl.ds(start + i * Q_TILE, Q_TILE), pl.ds(start + (i - 1) * Q_TILE, Q_TILE))
                             for i in range(GROUP)], bias1, nat_stats, True)
        return carry

    lax.fori_loop(1, nb1 // GROUP, later, 0)

    LR = T // R
    bias_r = bias_of(dils[1])
    bias_rr = bias_of(dils[2])

    def stream(s, carry):
        own = pl.ds(s, LR, stride=R)
        yq[...] = q_ref[0, own, :]
        yk[...] = k_ref[0, own, :]
        yv[...] = v_ref[0, own, :]
        for grp in groups([(pl.ds(jb * Q_TILE, Q_TILE), pl.ds((jb - 1) * Q_TILE, Q_TILE) if jb else None)
                           for jb in range(LR // Q_TILE)]):
            tile_group(scratch, grp, bias_r, y_stats, True)
        for grp in groups([(pl.ds(c + R * Q_TILE * jb, Q_TILE, stride=R),
                            pl.ds(c + R * Q_TILE * (jb - 1), Q_TILE, stride=R) if jb else None)
                           for c in range(R) for jb in range(LR // R // Q_TILE)]):
            tile_group(scratch, grp, bias_rr, y_stats, False)
        m_a, m_b = m_s[own, :], ym[...]
        m_new = jnp.maximum(m_a, m_b)
        e_a = jnp.exp(m_a - m_new)
        e_b = jnp.exp(m_b - m_new)
        acc_s[own, :] = (acc_s[own, :] * e_a + yacc[...] * e_b) / (l_s[own, :] * e_a + yl[...] * e_b)
        return carry

    lax.fori_loop(0, R, stream, 0)
    o_ref[0] = acc_s[...].astype(o_ref.dtype)


def _prompt_attn_scratch(T):
    return [pltpu.VMEM((T, LANES), F32)] * 3 + [pltpu.VMEM((T // DIL_PATTERNS[1][1], LANES), F32)] * 6


def _prompt_attn(q, k, v, slopes_tab):
    B, T, W = q.shape
    spec = pl.BlockSpec((1, T, LANES), lambda b, hp: (b, 0, hp))
    return pl.pallas_call(
        functools.partial(_prompt_attn_kernel, T=T),
        grid=(B, W // LANES),
        in_specs=[spec, spec, spec, _full(slopes_tab.shape)],
        out_specs=spec,
        out_shape=jax.ShapeDtypeStruct((B, T, W), BF16),
        scratch_shapes=_prompt_attn_scratch(T),
        compiler_params=_cparams(2),
        name="prompt_attn",
    )(q, k, v, slopes_tab)


def _rwkv_chunk_kernel(r_ref, lw_ref, k_ref, v_ref, al_ref, be_ref, g_ref, bo_ref, gnw_ref, gnb_ref,
                       y_ref, s_ref, st_scr, *, Tb, npair, nbb):
    C = CHUNK
    n2 = 2 * C
    t = pl.program_id(1)

    @pl.when(t == 0)
    def _():
        st_scr[...] = jnp.zeros_like(st_scr)

    lane = lax.broadcasted_iota(jnp.int32, (1, LANES), 1)
    m0 = (lane < HEAD_DIM).astype(F32)
    m1 = 1.0 - m0
    ri = lax.broadcasted_iota(jnp.int32, (n2, n2), 0)
    ci = lax.broadcasted_iota(jnp.int32, (n2, n2), 1)
    same = (ri < C) == (ci < C)
    strict = same & ((ri & (C - 1)) > (ci & (C - 1)))
    incl = same & ((ri & (C - 1)) >= (ci & (C - 1)))
    tri = (lax.broadcasted_iota(jnp.int32, (C, C), 0) >= lax.broadcasted_iota(jnp.int32, (C, C), 1)).astype(BF16)
    head_rows = ((lax.broadcasted_iota(jnp.int32, (n2, LANES), 0) < C)
                 == (lax.broadcasted_iota(jnp.int32, (n2, LANES), 1) < HEAD_DIM)).astype(F32)

    def stack(x):
        return jnp.concatenate([x * m0, x * m1], axis=0)

    def twice(x):
        return jnp.concatenate([x, x], axis=0)

    def body(c, carry):
        rows = pl.ds(pl.multiple_of(c * C, C), C)
        chains = [(bb, slice(hp * LANES, (hp + 1) * LANES)) for bb in range(nbb) for hp in range(npair)]
        ids = range(len(chains))
        S = [st_scr[i] for i in ids]
        lw_all = jnp.concatenate([lw_ref[bb, rows, :] for bb in range(nbb)], axis=1)
        cl_all = _dot_const_l(tri, lw_all)
        e_pos_all = jnp.exp(cl_all)
        e_neg_all = jnp.exp(-cl_all)
        e_exc_all = jnp.exp(cl_all - lw_all)
        wide = [slice(i * LANES, (i + 1) * LANES) for i in ids]
        g_end = [e_pos_all[C - 1:C, wide[i]] for i in ids]
        rt = [r_ref[bb, rows, cs] * e_pos_all[:, wide[i]] for i, (bb, cs) in enumerate(chains)]
        kt = [k_ref[bb, rows, cs] * e_neg_all[:, wide[i]] for i, (bb, cs) in enumerate(chains)]
        bt = [be_ref[bb, rows, cs] * e_neg_all[:, wide[i]] for i, (bb, cs) in enumerate(chains)]
        at = [al_ref[bb, rows, cs] * e_exc_all[:, wide[i]] for i, (bb, cs) in enumerate(chains)]
        ar_st = [jnp.concatenate([stack(at[i]), stack(rt[i])], axis=0) for i in ids]
        kb2 = [jnp.concatenate([twice(kt[i]), twice(bt[i])], axis=0) for i in ids]
        v_st = [stack(v_ref[bb, rows, cs]) for bb, cs in chains]
        gram = [_dot(ar_st[i], kb2[i], "NT") for i in ids]
        l_ab = [jnp.where(strict, gram[i][0:n2, n2:], 0.0) for i in ids]
        l_kk = [jnp.concatenate([jnp.where(strict, gram[i][0:n2, 0:n2], 0.0),
                                 jnp.where(incl, gram[i][n2:, 0:n2], 0.0)], axis=0) for i in ids]
        l_rb = [jnp.where(incl, gram[i][n2:, n2:], 0.0) for i in ids]
        from_state = [_dot(jnp.concatenate([at[i], rt[i]], axis=0), S[i], "NT") for i in ids]
        from_state = [jnp.concatenate([stack(from_state[i][0:C]), stack(from_state[i][C:])], axis=0) for i in ids]
        from_v = [_dot(l_kk[i], v_st[i]) for i in ids]
        x = l_ab
        u = [from_state[i][0:n2] + from_v[i][0:n2] for i in ids]
        for _ in range(C.bit_length() - 2):
            xu = [_dot(x[i], jnp.concatenate([x[i], u[i]], axis=1)) for i in ids]
            x = [xu[i][:, 0:n2] for i in ids]
            u = [u[i] + xu[i][:, n2:] for i in ids]
        u = [u[i] + _dot(x[i], u[i]) for i in ids]
        y_st = [from_state[i][n2:] + from_v[i][n2:] + _dot(l_rb[i], u[i]) for i in ids]
        for i in ids:
            vu = jnp.concatenate([v_st[i], u[i]], axis=0)
            kb_end = jnp.concatenate([stack(kt[i] * g_end[i]), stack(bt[i] * g_end[i])], axis=0)
            st_scr[i] = S[i] * g_end[i] + _dot(vu, kb_end, "TN")
        for i, (bb, cs) in enumerate(chains):
            mu = jnp.sum(y_st[i], axis=-1, keepdims=True) * (1.0 / HEAD_DIM)
            yc = (y_st[i] - mu) * head_rows
            var = jnp.sum(yc * yc, axis=-1, keepdims=True) * (1.0 / HEAD_DIM)
            yn_st = yc * lax.rsqrt(var + GN_EPS)
            yn_i = (yn_st[0:C, :] + yn_st[C:n2, :]) * gnw_ref[:, cs] + gnb_ref[:, cs]
            y_ref[bb, rows, cs] = ((yn_i + bo_ref[bb, rows, cs]) * g_ref[bb, rows, cs]).astype(y_ref.dtype)
        return carry

    lax.fori_loop(0, Tb // C, body, 0)

    @pl.when(t == pl.num_programs(1) - 1)
    def _():
        for i in range(nbb * npair):
            bb, hp = divmod(i, npair)
            pair = st_scr[i]
            s_ref[bb, 2 * hp] = pair[0:HEAD_DIM, 0:HEAD_DIM]
            s_ref[bb, 2 * hp + 1] = pltpu.roll(pair, HEAD_DIM, axis=1)[HEAD_DIM:, 0:HEAD_DIM]


def _rwkv_chunked(r, lw, k2, v, al, be, g, bonus, gn_w, gn_b, tb):
    B, T, W = r.shape
    npair = W // LANES
    nbb = next(n for n in (4, 2, 1) if B % n == 0)
    spec = pl.BlockSpec((nbb, tb, W), lambda b, t: (b, t, 0))
    st_spec = pl.BlockSpec((nbb, 2 * npair, HEAD_DIM, HEAD_DIM), lambda b, t: (b, 0, 0, 0))
    return pl.pallas_call(
        functools.partial(_rwkv_chunk_kernel, Tb=tb, npair=npair, nbb=nbb),
        grid=(B // nbb, T // tb),
        in_specs=[spec] * 8 + [_full(gn_w.shape), _full(gn_b.shape)],
        out_specs=[spec, st_spec],
        out_shape=[jax.ShapeDtypeStruct((B, T, W), BF16),
                   jax.ShapeDtypeStruct((B, 2 * npair, HEAD_DIM, HEAD_DIM), F32)],
        scratch_shapes=[pltpu.VMEM((nbb * npair, LANES, LANES), F32)],
        compiler_params=_cparams(2),
        name="rwkv_chunk",
    )(r, lw, k2, v, al, be, g, bonus, gn_w, gn_b)


def _rwkv_one_step(S, r, lw, k, al, be, v, g, bonus, gnw, gnb):
    sa = jnp.sum(S * al, axis=-1, keepdims=True)
    s_new = S * jnp.exp(lw) + sa * be + v * k
    y = jnp.sum(s_new * r, axis=-1, keepdims=True)
    mu = jnp.mean(y, axis=1, keepdims=True)
    yc = y - mu
    var = jnp.mean(yc * yc, axis=1, keepdims=True)
    yn = yc * lax.rsqrt(var + GN_EPS) * gnw + gnb
    return (yn + bonus) * g, s_new


def _attn_one_step(q, k_new, v_new, kt, vt, slopes, P):
    aw = q.shape[1]
    H = aw // HEAD_DIM
    scale = HEAD_DIM ** -0.5
    dist = P - lax.broadcasted_iota(jnp.int32, (1, P), 1)
    count = jnp.zeros((1, P), F32)
    for win, dil in DIL_PATTERNS:
        assert dil & (dil - 1) == 0
        count = count + ((dist <= win) & ((dist & (dil - 1)) == 0)).astype(F32)
    own_head = (lax.broadcasted_iota(jnp.int32, (H, aw), 1) // HEAD_DIM
                == lax.broadcasted_iota(jnp.int32, (H, aw), 0)).astype(F32)
    q_heads = q * own_head
    s = _dot(q_heads, kt) * scale - slopes * dist.astype(F32)
    s = jnp.where(count > 0.0, s, NEG)
    s0 = jnp.sum(q_heads * k_new, axis=-1, keepdims=True) * scale
    m = jnp.maximum(jnp.max(s, axis=-1, keepdims=True), s0)
    p = count * jnp.exp(s - m)
    e0 = len(DIL_PATTERNS) * jnp.exp(s0 - m)
    l = jnp.sum(p, axis=-1, keepdims=True) + e0
    acc = _dot(p, vt, "NT") + e0 * v_new
    return jnp.sum(acc * own_head, axis=0, keepdims=True) / jnp.sum(l * own_head, axis=0, keepdims=True)


N_MIXER_IN = 15
N_MIXER_OUT = 3


def _mixer_step_body(seq, grp, q_ref, kn_ref, vn_ref, col_ref, kt_ref, vt_ref, sl_ref, s_ref, r_ref, lw_ref, k_ref,
                     al_ref, be_ref, gnw_ref, gnb_ref, att_ref, y_ref, so_ref, *, P, aw):
    hg = kt_ref.shape[1]
    gw = hg * HEAD_DIM
    own = (lax.broadcasted_iota(jnp.int32, (1, LANES), 1) == seq).astype(F32)
    aligned = lambda x, m: x if isinstance(x, int) else pl.multiple_of(x, m)
    heads = pl.ds(aligned(grp * hg, hg), hg)

    def chan(i):
        return pl.ds(aligned(i * aw + grp * gw, gw), gw)

    def column(i):
        return jnp.sum(col_ref[chan(i), :] * own, axis=-1, keepdims=True).reshape(hg, HEAD_DIM, 1)

    att_ref[0, 0] = _attn_one_step(q_ref[0, 0], kn_ref[0, 0], vn_ref[0, 0], kt_ref[0].reshape(gw, P),
                                   vt_ref[0].reshape(gw, P), sl_ref[heads, :], P)
    y, s_new = _rwkv_one_step(s_ref[0], r_ref[0], lw_ref[0], k_ref[0], al_ref[0], be_ref[0], column(0), column(1),
                              column(2), gnw_ref[heads], gnb_ref[heads])
    so_ref[0] = s_new
    y_ref[chan(0), :] += y.reshape(gw, 1) * own


def _mixer_step_kernel(*refs, P, aw):
    @pl.when(pl.program_id(0) == 0)
    def _():
        refs[N_MIXER_IN + 1][...] = jnp.zeros_like(refs[N_MIXER_IN + 1])

    _mixer_step_body(pl.program_id(0), 0, *refs, P=P, aw=aw)


def _mixer_step_operands(seq_grp_of, hg, q, k_new, v_new, col, k_cache, v_cache, slopes, state, r, lw, k2, al, be,
                         gn_w, gn_b):
    nb, P, H, _ = k_cache.shape
    aw = H * HEAD_DIM
    rowv = lambda a: a.reshape(nb, H, 1, HEAD_DIM)
    vec = lambda a: a.reshape(nb, H // hg, 1, hg * HEAD_DIM)
    per_grp = lambda shape: pl.BlockSpec((1,) + shape, lambda *g: seq_grp_of(*g) + (0,) * (len(shape) - 1))
    t_spec, s_spec, r_spec = per_grp((hg, HEAD_DIM, P)), per_grp((hg, HEAD_DIM, HEAD_DIM)), per_grp((hg, 1, HEAD_DIM))
    v_spec = per_grp((1, 1, hg * HEAD_DIM))
    p_spec = _full((H, HEAD_DIM, 1))
    args = (vec(q), vec(k_new), vec(v_new), col, jnp.transpose(k_cache, (0, 2, 3, 1)),
            jnp.transpose(v_cache, (0, 2, 3, 1)), slopes.reshape(H, 1), state, rowv(r), rowv(lw), rowv(k2), rowv(al),
            rowv(be), gn_w.reshape(H, HEAD_DIM, 1), gn_b.reshape(H, HEAD_DIM, 1))
    assert len(args) == N_MIXER_IN
    in_specs = ([v_spec] * 3 + [_full(col.shape), t_spec, t_spec, _full((H, 1)), s_spec] + [r_spec] * 5
                + [p_spec] * 2)
    out_specs = [v_spec, _full((aw, LANES)), s_spec]
    out_shape = [jax.ShapeDtypeStruct((nb, H // hg, 1, hg * HEAD_DIM), F32), jax.ShapeDtypeStruct((aw, LANES), F32),
                 jax.ShapeDtypeStruct(state.shape, F32)]
    assert len(out_specs) == N_MIXER_OUT
    return args, in_specs, out_specs, out_shape


def _mixer_step(*operands):
    nb, P, H, _ = operands[4].shape
    args, in_specs, out_specs, out_shape = _mixer_step_operands(lambda b: (b, 0), H, *operands)
    return pl.pallas_call(
        functools.partial(_mixer_step_kernel, P=P, aw=H * HEAD_DIM),
        grid=(nb,),
        in_specs=in_specs,
        out_specs=out_specs,
        out_shape=out_shape,
        compiler_params=_cparams(1),
        name="mixer_step",
    )(*args)


def _ffn_tail(x1, gate_pre, prev1, prev2, up, cw_ref, cb_ref, w2_ref, g_ref, b_ref, alpha):
    c = cw_ref[0:1, :] * prev2 + cw_ref[1:2, :] * prev1 + cw_ref[2:3, :] * gate_pre + cb_ref[...]
    h = c * _sigmoid(c) * up
    return _layer_norm(alpha * x1 + _dot(h, w2_ref[...]), g_ref[...], b_ref[...])


def _ffn_seq_kernel(att_ref, y_ref, x_ref, wo_ref, g1_ref, b1_ref, w1_ref, cw_ref, cb_ref, w2_ref, g_ref, b_ref,
                    *refs, aw, dff, alpha, sub_tiles, mixer):
    if mixer:
        mixer_in, (o_ref, tail_ref, *mixer_out), (carry_ref,) = (refs[:N_MIXER_IN], refs[N_MIXER_IN:-1], refs[-1:])
        step = pl.program_id(0) * pl.num_programs(1) + pl.program_id(1)

        @pl.when(step == 0)
        def _():
            mixer_out[1][...] = jnp.zeros_like(mixer_out[1])
    else:
        o_ref, tail_ref, carry_ref = refs

    @pl.when(pl.program_id(1) == 0)
    def _():
        carry_ref[...] = jnp.zeros_like(carry_ref)

    tm = x_ref.shape[1]
    hr = tm // sub_tiles
    spans = [slice(h * hr, (h + 1) * hr) for h in range(sub_tiles)]
    n = range(sub_tiles)
    row = lax.broadcasted_iota(jnp.int32, (hr, dff), 0)
    mixed = [_dot(att_ref[0, sp, :], wo_ref[0:aw, :]) + _dot(y_ref[0, sp, :], wo_ref[aw:, :]) for sp in spans]
    x1 = [_layer_norm(alpha * x_ref[0, spans[i], :] + mixed[i], g1_ref[...], b1_ref[...]) for i in n]
    hu = [_dot(x1[0], w1_ref[...])]
    last2 = carry_ref[0:2, :]
    gated, f = [], []
    for i in n:
        if i + 1 < sub_tiles:
            hu.append(_dot(x1[i + 1], w1_ref[...]))
        if mixer and i == 0:
            P, per_seq = mixer
            _mixer_step_body(step // per_seq, step % per_seq, *mixer_in, *mixer_out, P=P, aw=aw)
        gate_pre = hu[i][:, 0:dff]
        c0, c1 = last2[0:1, :], last2[1:2, :]
        prev1 = jnp.where(row == 0, c1, pltpu.roll(gate_pre, 1, axis=0))
        prev2 = jnp.where(row == 0, c0, jnp.where(row == 1, c1, pltpu.roll(gate_pre, 2, axis=0)))
        last2 = gate_pre[hr - 2:hr, :]
        c = cw_ref[0:1, :] * prev2 + cw_ref[1:2, :] * prev1 + cw_ref[2:3, :] * gate_pre + cb_ref[...]
        gated.append(c * _sigmoid(c) * hu[i][:, dff:])
        if i > 0:
            f.append(_dot(gated[i - 1], w2_ref[...]))
    f.append(_dot(gated[-1], w2_ref[...]))
    carry_ref[0:2, :] = last2
    tail_ref[0] = last2
    for i in n:
        o_ref[0, spans[i], :] = _layer_norm(alpha * x1[i] + f[i], g_ref[...], b_ref[...])


def _ffn_step_kernel(att_ref, y_ref, x_ref, wo_ref, g1_ref, b1_ref, p2_ref, p1_ref, w1_ref, cw_ref, cb_ref, w2_ref,
                     g_ref, b_ref, o_ref, gate_ref, *, dff, alpha):
    nb, aw = att_ref.shape
    h = _dot(att_ref[...], wo_ref[0:aw, :]) + _dot(y_ref[...].T[0:nb, :], wo_ref[aw:, :])
    x1 = _layer_norm(alpha * x_ref[...] + h, g1_ref[...], b1_ref[...])
    hu = _dot(x1, w1_ref[...])
    gate_pre = hu[:, 0:dff]
    gate_ref[...] = gate_pre
    o_ref[...] = _ffn_tail(x1, gate_pre, p1_ref[...], p2_ref[...], hu[:, dff:], cw_ref, cb_ref, w2_ref, g_ref,
                           b_ref, alpha)


def _resident(shape):
    nd = len(shape)
    return pl.BlockSpec(shape, lambda *_: (0,) * nd, pipeline_mode=pl.Buffered(1))


def _mixer_groups_per_seq(n_steps, mixer_operands):
    nb, _, H, _ = mixer_operands[4].shape
    per_seq = n_steps // nb
    return per_seq if per_seq >= 1 and n_steps == per_seq * nb and H % per_seq == 0 else 0


def _ffn_seq(att, y_rw, x, w_out, ln1_g, ln1_b, w1, conv_w, conv_b, w2, ln2_g, ln2_b, tm, alpha, mixer_operands=None):
    B, T, D = x.shape
    nt = T // tm
    aw = att.shape[-1]
    dff = w2.shape[0]
    half = pl.BlockSpec((1, tm, aw), lambda b, t: (b, t, 0))
    rows = pl.BlockSpec((1, tm, D), lambda b, t: (b, t, 0))
    args = [att, y_rw, x, w_out, ln1_g, ln1_b, w1, conv_w, conv_b, w2, ln2_g, ln2_b]
    in_specs = [half, half, rows, _resident(w_out.shape), _full(ln1_g.shape), _full(ln1_b.shape),
                _resident(w1.shape), _full(conv_w.shape), _full(conv_b.shape), _resident(w2.shape),
                _full(ln2_g.shape), _full(ln2_b.shape)]
    out_specs = [rows, pl.BlockSpec((1, 2, dff), lambda b, t: (b, 0, 0))]
    out_shape = [jax.ShapeDtypeStruct((B, T, D), F32), jax.ShapeDtypeStruct((B, 2, dff), F32)]
    mixer = None
    if mixer_operands is not None:
        nb, P, H, _ = mixer_operands[4].shape
        per_seq = _mixer_groups_per_seq(B * nt, mixer_operands)
        assert per_seq
        m_args, m_in, m_out, m_shape = _mixer_step_operands(
            lambda b, t: ((b * nt + t) // per_seq, (b * nt + t) % per_seq), H // per_seq, *mixer_operands)
        args, in_specs, out_specs, out_shape = args + list(m_args), in_specs + m_in, out_specs + m_out, out_shape + m_shape
        mixer = (P, per_seq)
    return pl.pallas_call(
        functools.partial(_ffn_seq_kernel, aw=aw, dff=dff, alpha=alpha, sub_tiles=tm // FFN_SUB_ROWS, mixer=mixer),
        grid=(B, nt),
        in_specs=in_specs,
        out_specs=out_specs,
        out_shape=out_shape,
        scratch_shapes=[pltpu.VMEM((8, dff), F32)],
        compiler_params=_cparams(2),
        name="ffn_seq",
    )(*args)


def _ffn_step(att, y_cols, x, w_out, ln1_g, ln1_b, prev2, prev1, w1, conv_w, conv_b, w2, ln2_g, ln2_b, alpha):
    nb, D = x.shape
    dff = w2.shape[0]
    return pl.pallas_call(
        functools.partial(_ffn_step_kernel, dff=dff, alpha=alpha),
        grid=(1,),
        in_specs=[_full(att.shape), _full(y_cols.shape), _full(x.shape), _resident(w_out.shape), _full(ln1_g.shape),
                  _full(ln1_b.shape),
                  _full(prev2.shape), _full(prev1.shape), _resident(w1.shape), _full(conv_w.shape),
                  _full(conv_b.shape), _resident(w2.shape), _full(ln2_g.shape), _full(ln2_b.shape)],
        out_specs=[_full((nb, D)), _full((nb, dff))],
        out_shape=[jax.ShapeDtypeStruct((nb, D), F32), jax.ShapeDtypeStruct((nb, dff), F32)],
        compiler_params=_cparams(1),
        name="ffn_step",
    )(att, y_cols, x, w_out, ln1_g, ln1_b, prev2, prev1, w1, conv_w, conv_b, w2, ln2_g, ln2_b)


def _layer_params(w_in, mu_shift, w0, w_lora_up, a0, a_lora_up, g_lora_up, k_k, k_a, r_k, aw):
    D, ncol = w_in.shape
    n_w, n_a, n_g = w_lora_up.shape[0], a_lora_up.shape[0], g_lora_up.shape[0]
    assert n_w + n_a == LANES and 3 * aw + 3 * aw + n_w + n_a + n_g == ncol
    g_cols = -(-n_g // LANES) * LANES
    pad = g_cols - n_g
    row = lambda a: a.reshape(1, -1).astype(F32)
    lane = jnp.arange(aw)
    bd = (lane[:, None] // HEAD_DIM == lane[None, :] // HEAD_DIM).astype(BF16)
    return dict(
        aw=aw,
        w_in=jnp.pad(w_in, ((0, 0), (0, pad))).astype(BF16),
        mu=jnp.pad(row(mu_shift), ((0, 0), (0, pad))),
        w0=row(w0), a0=row(a0), k_k=row(k_k), k_a=row(k_a), r_k=row(r_k),
        wl_pad=jnp.pad(w_lora_up, ((0, n_a), (0, 0))).astype(BF16),
        al_pad=jnp.pad(a_lora_up, ((n_w, 0), (0, 0))).astype(BF16),
        g_pad=jnp.pad(g_lora_up, ((0, pad), (0, 0))).astype(BF16),
        bd=bd,
    )


def kernel(x_prompt, x_sample, cache_k_win, cache_v_win, state_shift, state_wkv, state_conv, w_in, mu_shift, w0, w_lora_up, a0, a_lora_up, g_lora_up, k_k, k_a, r_k, gn_w, gn_b, w_out, ln1_g, ln1_b, w_ffn_in, conv_w, conv_b, w_ffn_out, ln2_g, ln2_b):
    depth = w_in.shape[0]
    alpha = (2.0 * depth) ** 0.25
    B, T, D = x_prompt.shape
    nb = x_sample.shape[0]
    assert x_sample.shape[1] == 1
    n_att = cache_k_win.shape[3]
    n_rw = state_wkv.shape[2]
    aw = n_att * HEAD_DIM
    assert n_rw * HEAD_DIM == aw and T % (Q_TILE * DIL_PATTERNS[-1][1]) == 0
    tm = min(256, T)
    tb = min(256, T)
    heads = jnp.arange(1, n_att + 1, dtype=F32)
    slopes = jnp.exp2(-8.0 * heads / n_att)
    slopes_tab = jnp.broadcast_to(slopes[:, None], (n_att, LANES))
    row = lambda a: a.reshape(1, -1)

    hp, hs = x_prompt, x_sample.reshape(nb, D)
    outs = [[] for _ in range(10)]
    for l in range(depth):
        prm = _layer_params(w_in[l], mu_shift[l], w0[l], w_lora_up[l], a0[l], a_lora_up[l], g_lora_up[l],
                            k_k[l], k_a[l], r_k[l], aw)
        w_out_b = w_out[l].astype(BF16)
        w1_b = w_ffn_in[l].astype(BF16)
        w2_b = w_ffn_out[l].astype(BF16)
        gnw, gnb = row(gn_w[l]), row(gn_b[l])

        q, k, v, kt, vt, r, lw, k2, vr, al, be, g, bonus = _inproj_seq(hp, prm, min(2 * tm, T))
        x2 = jnp.concatenate([hs, state_shift[l]], axis=0)
        qs, ks, vs, r_s, lw_s, k2_s, al_s, be_s, col = _inproj_step(x2, prm)
        mixer_operands = (qs, ks, vs, col, cache_k_win[l], cache_v_win[l], slopes, state_wkv[l], r_s, lw_s, k2_s,
                          al_s, be_s, gn_w[l], gn_b[l])

        att = _prompt_attn(q, k, v, slopes_tab)
        y_rw, wkv_p = _rwkv_chunked(r, lw, k2, vr, al, be, g, bonus, gnw, gnb, tb)

        shift_p = hp[:, -1, :]
        ffn_args = (att, y_rw, hp, w_out_b, row(ln1_g[l]), row(ln1_b[l]), w1_b, conv_w[l], row(conv_b[l]), w2_b,
                    row(ln2_g[l]), row(ln2_b[l]))
        if _mixer_groups_per_seq(B * (T // tm), mixer_operands):
            hp, conv_p, att_s, y_s, wkv_s = _ffn_seq(*ffn_args, tm, alpha, mixer_operands)
        else:
            att_s, y_s, wkv_s = _mixer_step(*mixer_operands)
            hp, conv_p = _ffn_seq(*ffn_args, min(2 * tm, T), alpha)
        shift_s = hs
        hs, gate_s = _ffn_step(att_s.reshape(nb, aw), y_s, hs, w_out_b, row(ln1_g[l]), row(ln1_b[l]),
                               state_conv[l][:, 0, :],
                               state_conv[l][:, 1, :], w1_b, conv_w[l], row(conv_b[l]), w2_b, row(ln2_g[l]),
                               row(ln2_b[l]), alpha)
        conv_s = jnp.stack([state_conv[l][:, 1, :], gate_s], axis=1)

        n_keep = min(DIL_PATTERNS[-1][0], T)
        win = lambda a: jnp.transpose(a.reshape(B, n_att, HEAD_DIM, T), (0, 3, 1, 2))[:, T - n_keep:]
        vals = (win(kt), win(vt), ks.reshape(nb, 1, n_att, HEAD_DIM), vs.reshape(nb, 1, n_att, HEAD_DIM),
                shift_p, shift_s, wkv_p, wkv_s, conv_p, conv_s)
        for lst, val in zip(outs, vals):
            lst.append(val)
    return (hp, hs.reshape(nb, 1, D)) + tuple(jnp.stack(lst) for lst in outs)
```

```python
import functools
import math

import jax
import jax.numpy as jnp
from jax import lax
from jax.experimental import pallas as pl
from jax.experimental.pallas import tpu as pltpu

F32 = jnp.float32
BF16 = jnp.bfloat16

HEAD_DIM = 64
LANES = 128
DIL_PATTERNS = ((128, 1), (512, 4), (2048, 16))
WIN_STEPS = 128
Q_TILE = 128
GROUP = 4
SUB_TILES = 4
CHUNK = 64
LN_EPS = 1e-5
GN_EPS = 64e-5
NEG = -1e30
VMEM_LIMIT = 56 * 1024 * 1024

_DN = {"NN": (((1,), (0,)), ((), ())), "NT": (((1,), (1,)), ((), ())), "TN": (((0,), (0,)), ((), ()))}


def _dot(a, b, dims="NN"):
    return lax.dot_general(a.astype(BF16), b.astype(BF16), _DN[dims], preferred_element_type=F32)


def _dot_const_l(c, x):
    hi = x.astype(BF16)
    r1 = x - hi.astype(F32)
    mid = r1.astype(BF16)
    lo = (r1 - mid.astype(F32)).astype(BF16)
    return _dot(c, hi) + _dot(c, mid) + _dot(c, lo)


def _sigmoid(x):
    return 0.5 * jnp.tanh(0.5 * x) + 0.5


def _layer_norm(x, g, b):
    mu = jnp.mean(x, axis=-1, keepdims=True)
    xc = x - mu
    var = jnp.mean(xc * xc, axis=-1, keepdims=True)
    return xc * lax.rsqrt(var + LN_EPS) * g + b


def _cparams(n_grid):
    return pltpu.CompilerParams(dimension_semantics=("arbitrary",) * n_grid, vmem_limit_bytes=VMEM_LIMIT)


def _full(shape):
    nd = len(shape)
    return pl.BlockSpec(shape, lambda *_: (0,) * nd)


def _rwkv_prep(rw, w0, a0, k_k, k_a, r_k, wl_pad, al_pad, g_pad, bd, aw):
    r = rw[:, 0:aw]
    k = rw[:, aw:2 * aw]
    v = rw[:, 2 * aw:3 * aw]
    lo = rw[:, 3 * aw:3 * aw + 128]
    glo = rw[:, 3 * aw + 128:]
    u = w0 + _dot(jnp.tanh(lo), wl_pad)
    a = _sigmoid(a0 + _dot(lo, al_pad))
    g = _dot(_sigmoid(glo), g_pad)
    kk = k * k_k
    sumsq = _dot(kk * kk, bd)
    lw = -math.exp(-0.5) * _sigmoid(u)
    k2 = k * (1.0 + (a - 1.0) * k_a)
    bonus = _dot(r * k2 * r_k, bd) * v
    kkn = kk * lax.rsqrt(jnp.maximum(sumsq, 1e-24))
    return r, lw, k2, v, -kkn, kkn * a, g, bonus


def _inproj_seq_kernel(x_ref, w_ref, mu_ref, w0_ref, a0_ref, kk_ref, ka_ref, rk_ref, wl_ref, al_ref, gp_ref,
                       bd_ref, q_ref, k_ref, v_ref, kt_ref, vt_ref, r_o, lw_o, k2_o, vr_o, al_o, be_o, g_o, bo_o,
                       carry_ref, *, aw):
    @pl.when(pl.program_id(1) == 0)
    def _():
        carry_ref[...] = jnp.zeros_like(carry_ref)

    tm = x_ref.shape[1]
    hr = tm // SUB_TILES
    spans = [slice(h * hr, (h + 1) * hr) for h in range(SUB_TILES)]
    params = (w0_ref[...], a0_ref[...], kk_ref[...], ka_ref[...], rk_ref[...], wl_ref[...], al_ref[...],
              gp_ref[...], bd_ref[...], aw)
    row = lax.broadcasted_iota(jnp.int32, (hr, w_ref.shape[1] - 3 * aw), 0)

    def finish(p, sp, last):
        q_ref[0, sp, :] = p[:, 0:aw]
        k = p[:, aw:2 * aw]
        v = p[:, 2 * aw:3 * aw]
        k_ref[0, sp, :] = k
        v_ref[0, sp, :] = v
        kt_ref[0, :, sp] = k.T
        vt_ref[0, :, sp] = v.T
        prw = p[:, 3 * aw:]
        prev = jnp.where(row == 0, last, pltpu.roll(prw, 1, axis=0))
        outs = _rwkv_prep(prw + (prev - prw) * mu_ref[...], *params)
        for o_ref, val in zip((r_o, lw_o, k2_o, vr_o, al_o, be_o, g_o, bo_o), outs):
            o_ref[0, sp, :] = val.astype(o_ref.dtype)
        return prw[hr - 1:hr, :]

    last = carry_ref[0:1, :]
    p = _dot(x_ref[0, spans[0], :], w_ref[...])
    for h in range(1, SUB_TILES):
        p_next = _dot(x_ref[0, spans[h], :], w_ref[...])
        last = finish(p, spans[h - 1], last)
        p = p_next
    carry_ref[0:1, :] = finish(p, spans[-1], last)


def _inproj_step_kernel(x_ref, w_ref, mu_ref, w0_ref, a0_ref, kk_ref, ka_ref, rk_ref, wl_ref, al_ref, gp_ref,
                        bd_ref, q_ref, k_ref, v_ref, r_o, lw_o, k2_o, al_o, be_o, col_o, *, aw, nb):
    p = _dot(x_ref[...], w_ref[...])
    q_ref[...] = p[0:nb, 0:aw]
    k_ref[...] = p[0:nb, aw:2 * aw]
    v_ref[...] = p[0:nb, 2 * aw:3 * aw]
    prw = p[0:nb, 3 * aw:]
    prev = p[nb:2 * nb, 3 * aw:]
    rw = prw + (prev - prw) * mu_ref[...]
    r, lw, k2, vr, al, be, g, bonus = _rwkv_prep(rw, w0_ref[...], a0_ref[...], kk_ref[...], ka_ref[...],
                                                 rk_ref[...], wl_ref[...], al_ref[...], gp_ref[...], bd_ref[...], aw)
    for o_ref, val in zip((r_o, lw_o, k2_o, al_o, be_o), (r, lw, k2, al, be)):
        o_ref[...] = val
    pad = jnp.zeros((LANES - nb, aw), F32)
    for i, val in enumerate((vr, g, bonus)):
        col_o[i * aw:(i + 1) * aw, :] = jnp.concatenate([val, pad], axis=0).T


_PREP_PARAMS = ("mu", "w0", "a0", "k_k", "k_a", "r_k", "wl_pad", "al_pad", "g_pad", "bd")


def _inproj_seq(x, prm, tm):
    B, T, D = x.shape
    aw = prm["aw"]
    nrw = prm["w_in"].shape[1] - 3 * aw
    small = [prm[n] for n in _PREP_PARAMS]
    row_spec = pl.BlockSpec((1, tm, aw), lambda b, t: (b, t, 0))
    col_spec = pl.BlockSpec((1, aw, tm), lambda b, t: (b, 0, t))
    row_sds = jax.ShapeDtypeStruct((B, T, aw), F32)
    col_sds = jax.ShapeDtypeStruct((B, aw, T), F32)
    return pl.pallas_call(
        functools.partial(_inproj_seq_kernel, aw=aw),
        grid=(B, T // tm),
        in_specs=[pl.BlockSpec((1, tm, D), lambda b, t: (b, t, 0)), _resident(prm["w_in"].shape)]
        + [_full(s.shape) for s in small],
        out_specs=[row_spec] * 3 + [col_spec] * 2 + [row_spec] * 8,
        out_shape=[row_sds] * 3 + [col_sds] * 2
        + [jax.ShapeDtypeStruct((B, T, aw), F32 if n == "lw" else BF16)
           for n in ("r", "lw", "k2", "v", "al", "be", "g", "bonus")],
        scratch_shapes=[pltpu.VMEM((8, nrw), F32)],
        compiler_params=_cparams(2),
        name="inproj_seq",
    )(x, prm["w_in"], *small)


def _inproj_step(x2, prm):
    nb = x2.shape[0] // 2
    aw = prm["aw"]
    small = [prm[n] for n in _PREP_PARAMS]
    assert nb <= LANES
    out_sds = jax.ShapeDtypeStruct((nb, aw), F32)
    return pl.pallas_call(
        functools.partial(_inproj_step_kernel, aw=aw, nb=nb),
        grid=(1,),
        in_specs=[_full(x2.shape), _full(prm["w_in"].shape)] + [_full(s.shape) for s in small],
        out_specs=[_full((nb, aw))] * 8 + [_full((3 * aw, LANES))],
        out_shape=[out_sds] * 8 + [jax.ShapeDtypeStruct((3 * aw, LANES), F32)],
        compiler_params=_cparams(1),
        name="inproj_step",
    )(x2, prm["w_in"], *small)


def _prompt_attn_kernel(q_ref, k_ref, v_ref, sl_ref, o_ref, acc_s, m_s, l_s, yq, yk, yv, yacc, ym, yl, *, T):
    hp = pl.program_id(1)
    scale = HEAD_DIM ** -0.5
    lane = lax.broadcasted_iota(jnp.int32, (1, LANES), 1)
    head_masks = ((lane < HEAD_DIM).astype(F32), (lane >= HEAD_DIM).astype(F32))
    qi = lax.broadcasted_iota(jnp.int32, (Q_TILE, Q_TILE), 0)
    ki = lax.broadcasted_iota(jnp.int32, (Q_TILE, Q_TILE), 1)
    steps_cur = (qi - ki).astype(F32)
    steps_prev = (qi - ki + Q_TILE).astype(F32)
    ok_cur = ki <= qi
    ok_prev = ki >= qi
    first_lanes = lax.broadcasted_iota(jnp.int32, (Q_TILE, LANES), 1) < HEAD_DIM
    slopes = [sl_ref[pl.ds(2 * hp + j, 1), :] for j in range(2)]

    def tile_group(src, tiles, bias, stats, first_branch):
        rq, rk, rv = src
        bias_c, bias_pc = bias
        acc_s, m_s, l_s = stats
        ids = range(len(tiles))
        q = [rq(rows) * scale for rows, _ in tiles]
        kk = [rk(rows) if prev is None else jnp.concatenate([rk(prev), rk(rows)], axis=0) for rows, prev in tiles]
        vv = [rv(rows) if prev is None else jnp.concatenate([rv(prev), rv(rows)], axis=0) for rows, prev in tiles]
        qq = [jnp.concatenate([q[t] * head_masks[0], q[t] * head_masks[1]], axis=0) for t in ids]
        s = [_dot(qq[t], kk[t], "NT") + (bias_c if tiles[t][1] is None else bias_pc) for t in ids]
        m = [jnp.max(s[t], axis=-1, keepdims=True) for t in ids]
        p = [jnp.exp(s[t] - m[t]) for t in ids]
        l = [jnp.sum(p[t], axis=-1, keepdims=True) for t in ids]
        acc = [_dot(p[t], vv[t]) for t in ids]
        for t, (rows, _) in enumerate(tiles):
            acc_n = jnp.where(first_lanes, acc[t][0:Q_TILE], acc[t][Q_TILE:])
            m_n = jnp.where(first_lanes, m[t][0:Q_TILE], m[t][Q_TILE:])
            l_n = jnp.where(first_lanes, l[t][0:Q_TILE], l[t][Q_TILE:])
            if first_branch:
                acc_s[rows, :] = acc_n
                m_s[rows, :] = m_n
                l_s[rows, :] = l_n
            else:
                m_o = m_s[rows, :]
                m_new = jnp.maximum(m_o, m_n)
                e_o = jnp.exp(m_o - m_new)
                e_n = jnp.exp(m_n - m_new)
                acc_s[rows, :] = acc_s[rows, :] * e_o + acc_n * e_n
                l_s[rows, :] = l_s[rows, :] * e_o + l_n * e_n
                m_s[rows, :] = m_new

    dils = [dil for _, dil in DIL_PATTERNS]
    R = dils[1]
    assert dils == [1, R, R * R] and all(win // dil == WIN_STEPS for win, dil in DIL_PATTERNS)

    def bias_of(dil):
        b_cur = [jnp.where(ok_cur, -(slopes[j] * float(dil)) * steps_cur, NEG) for j in range(2)]
        b_prev = [jnp.where(ok_prev, -(slopes[j] * float(dil)) * steps_prev, NEG) for j in range(2)]
        return (jnp.concatenate(b_cur, axis=0),
                jnp.concatenate([jnp.concatenate([b_prev[j], b_cur[j]], axis=1) for j in range(2)], axis=0))

    def groups(tiles):
        return [tiles[i:i + GROUP] for i in range(0, len(tiles), GROUP)]

    natural = tuple((lambda rows, ref=ref: ref[0, rows, :]) for ref in (q_ref, k_ref, v_ref))
    scratch = tuple((lambda rows, ref=ref: ref[rows, :]) for ref in (yq, yk, yv))
    nat_stats = (acc_s, m_s, l_s)
    y_stats = (yacc, ym, yl)

    nb1 = T // Q_TILE
    assert nb1 % GROUP == 0
    bias1 = bias_of(dils[0])
    tile_group(natural, [(pl.ds(jb * Q_TILE, Q_TILE), pl.ds((jb - 1) * Q_TILE, Q_TILE) if jb else None)
                         for jb in range(GROUP)], bias1, nat_stats, True)

    def later(g, carry):
        start = pl.multiple_of(g * (GROUP * Q_TILE), GROUP * Q_TILE)
        tile_group(natural, [(pl.ds(start + i * Q_TILE, Q_TILE), pl.ds(start + (i - 1) * Q_TILE, Q_TILE))
                             for i in range(GROUP)], bias1, nat_stats, True)
        return carry

    lax.fori_loop(1, nb1 // GROUP, later, 0)

    LR = T // R
    bias_r = bias_of(dils[1])
    bias_rr = bias_of(dils[2])

    def stream(s, carry):
        own = pl.ds(s, LR, stride=R)
        yq[...] = q_ref[0, own, :]
        yk[...] = k_ref[0, own, :]
        yv[...] = v_ref[0, own, :]
        for grp in groups([(pl.ds(jb * Q_TILE, Q_TILE), pl.ds((jb - 1) * Q_TILE, Q_TILE) if jb else None)
                           for jb in range(LR // Q_TILE)]):
            tile_group(scratch, grp, bias_r, y_stats, True)
        for grp in groups([(pl.ds(c + R * Q_TILE * jb, Q_TILE, stride=R),
                            pl.ds(c + R * Q_TILE * (jb - 1), Q_TILE, stride=R) if jb else None)
                           for c in range(R) for jb in range(LR // R // Q_TILE)]):
            tile_group(scratch, grp, bias_rr, y_stats, False)
        m_a, m_b = m_s[own, :], ym[...]
        m_new = jnp.maximum(m_a, m_b)
        e_a = jnp.exp(m_a - m_new)
        e_b = jnp.exp(m_b - m_new)
        acc_s[own, :] = (acc_s[own, :] * e_a + yacc[...] * e_b) / (l_s[own, :] * e_a + yl[...] * e_b)
        return carry

    lax.fori_loop(0, R, stream, 0)
    o_ref[0] = acc_s[...].astype(o_ref.dtype)


def _prompt_attn_scratch(T):
    return [pltpu.VMEM((T, LANES), F32)] * 3 + [pltpu.VMEM((T // DIL_PATTERNS[1][1], LANES), F32)] * 6


def _prompt_attn(q, k, v, slopes_tab):
    B, T, W = q.shape
    spec = pl.BlockSpec((1, T, LANES), lambda b, hp: (b, 0, hp))
    return pl.pallas_call(
        functools.partial(_prompt_attn_kernel, T=T),
        grid=(B, W // LANES),
        in_specs=[spec, spec, spec, _full(slopes_tab.shape)],
        out_specs=spec,
        out_shape=jax.ShapeDtypeStruct((B, T, W), BF16),
        scratch_shapes=_prompt_attn_scratch(T),
        compiler_params=_cparams(2),
        name="prompt_attn",
    )(q, k, v, slopes_tab)


def _rwkv_chunk_kernel(r_ref, lw_ref, k_ref, v_ref, al_ref, be_ref, g_ref, bo_ref, gnw_ref, gnb_ref,
                       y_ref, s_ref, st_scr, *, Tb, npair, nbb):
    C = CHUNK
    n2 = 2 * C
    t = pl.program_id(1)

    @pl.when(t == 0)
    def _():
        st_scr[...] = jnp.zeros_like(st_scr)

    lane = lax.broadcasted_iota(jnp.int32, (1, LANES), 1)
    m0 = (lane < HEAD_DIM).astype(F32)
    m1 = 1.0 - m0
    ri = lax.broadcasted_iota(jnp.int32, (n2, n2), 0)
    ci = lax.broadcasted_iota(jnp.int32, (n2, n2), 1)
    same = (ri < C) == (ci < C)
    strict = same & ((ri & (C - 1)) > (ci & (C - 1)))
    incl = same & ((ri & (C - 1)) >= (ci & (C - 1)))
    tri = (lax.broadcasted_iota(jnp.int32, (C, C), 0) >= lax.broadcasted_iota(jnp.int32, (C, C), 1)).astype(BF16)
    head_rows = ((lax.broadcasted_iota(jnp.int32, (n2, LANES), 0) < C)
                 == (lax.broadcasted_iota(jnp.int32, (n2, LANES), 1) < HEAD_DIM)).astype(F32)

    def stack(x):
        return jnp.concatenate([x * m0, x * m1], axis=0)

    def twice(x):
        return jnp.concatenate([x, x], axis=0)

    def body(c, carry):
        rows = pl.ds(pl.multiple_of(c * C, C), C)
        chains = [(bb, slice(hp * LANES, (hp + 1) * LANES)) for bb in range(nbb) for hp in range(npair)]
        ids = range(len(chains))
        S = [st_scr[i] for i in ids]
        lw_all = jnp.concatenate([lw_ref[bb, rows, :] for bb in range(nbb)], axis=1)
        cl_all = _dot_const_l(tri, lw_all)
        e_pos_all = jnp.exp(cl_all)
        e_neg_all = jnp.exp(-cl_all)
        e_exc_all = jnp.exp(cl_all - lw_all)
        wide = [slice(i * LANES, (i + 1) * LANES) for i in ids]
        g_end = [e_pos_all[C - 1:C, wide[i]] for i in ids]
        rt = [r_ref[bb, rows, cs] * e_pos_all[:, wide[i]] for i, (bb, cs) in enumerate(chains)]
        kt = [k_ref[bb, rows, cs] * e_neg_all[:, wide[i]] for i, (bb, cs) in enumerate(chains)]
        bt = [be_ref[bb, rows, cs] * e_neg_all[:, wide[i]] for i, (bb, cs) in enumerate(chains)]
        at = [al_ref[bb, rows, cs] * e_exc_all[:, wide[i]] for i, (bb, cs) in enumerate(chains)]
        ar_st = [jnp.concatenate([stack(at[i]), stack(rt[i])], axis=0) for i in ids]
        kb2 = [jnp.concatenate([twice(kt[i]), twice(bt[i])], axis=0) for i in ids]
        v_st = [stack(v_ref[bb, rows, cs]) for bb, cs in chains]
        gram = [_dot(ar_st[i], kb2[i], "NT") for i in ids]
        l_ab = [jnp.where(strict, gram[i][0:n2, n2:], 0.0) for i in ids]
        l_kk = [jnp.concatenate([jnp.where(strict, gram[i][0:n2, 0:n2], 0.0),
                                 jnp.where(incl, gram[i][n2:, 0:n2], 0.0)], axis=0) for i in ids]
        l_rb = [jnp.where(incl, gram[i][n2:, n2:], 0.0) for i in ids]
        from_state = [_dot(jnp.concatenate([at[i], rt[i]], axis=0), S[i], "NT") for i in ids]
        from_state = [jnp.concatenate([stack(from_state[i][0:C]), stack(from_state[i][C:])], axis=0) for i in ids]
        from_v = [_dot(l_kk[i], v_st[i]) for i in ids]
        x = l_ab
        u = [from_state[i][0:n2] + from_v[i][0:n2] for i in ids]
        for _ in range(C.bit_length() - 2):
            xu = [_dot(x[i], jnp.concatenate([x[i], u[i]], axis=1)) for i in ids]
            x = [xu[i][:, 0:n2] for i in ids]
            u = [u[i] + xu[i][:, n2:] for i in ids]
        u = [u[i] + _dot(x[i], u[i]) for i in ids]
        y_st = [from_state[i][n2:] + from_v[i][n2:] + _dot(l_rb[i], u[i]) for i in ids]
        for i in ids:
            vu = jnp.concatenate([v_st[i], u[i]], axis=0)
            kb_end = jnp.concatenate([stack(kt[i] * g_end[i]), stack(bt[i] * g_end[i])], axis=0)
            st_scr[i] = S[i] * g_end[i] + _dot(vu, kb_end, "TN")
        for i, (bb, cs) in enumerate(chains):
            mu = jnp.sum(y_st[i], axis=-1, keepdims=True) * (1.0 / HEAD_DIM)
            yc = (y_st[i] - mu) * head_rows
            var = jnp.sum(yc * yc, axis=-1, keepdims=True) * (1.0 / HEAD_DIM)
            yn_st = yc * lax.rsqrt(var + GN_EPS)
            yn_i = (yn_st[0:C, :] + yn_st[C:n2, :]) * gnw_ref[:, cs] + gnb_ref[:, cs]
            y_ref[bb, rows, cs] = ((yn_i + bo_ref[bb, rows, cs]) * g_ref[bb, rows, cs]).astype(y_ref.dtype)
        return carry

    lax.fori_loop(0, Tb // C, body, 0)

    @pl.when(t == pl.num_programs(1) - 1)
    def _():
        for i in range(nbb * npair):
            bb, hp = divmod(i, npair)
            pair = st_scr[i]
            s_ref[bb, 2 * hp] = pair[0:HEAD_DIM, 0:HEAD_DIM]
            s_ref[bb, 2 * hp + 1] = pltpu.roll(pair, HEAD_DIM, axis=1)[HEAD_DIM:, 0:HEAD_DIM]


def _rwkv_chunked(r, lw, k2, v, al, be, g, bonus, gn_w, gn_b, tb):
    B, T, W = r.shape
    npair = W // LANES
    nbb = next(n for n in (4, 2, 1) if B % n == 0)
    spec = pl.BlockSpec((nbb, tb, W), lambda b, t: (b, t, 0))
    st_spec = pl.BlockSpec((nbb, 2 * npair, HEAD_DIM, HEAD_DIM), lambda b, t: (b, 0, 0, 0))
    return pl.pallas_call(
        functools.partial(_rwkv_chunk_kernel, Tb=tb, npair=npair, nbb=nbb),
        grid=(B // nbb, T // tb),
        in_specs=[spec] * 8 + [_full(gn_w.shape), _full(gn_b.shape)],
        out_specs=[spec, st_spec],
        out_shape=[jax.ShapeDtypeStruct((B, T, W), BF16),
                   jax.ShapeDtypeStruct((B, 2 * npair, HEAD_DIM, HEAD_DIM), F32)],
        scratch_shapes=[pltpu.VMEM((nbb * npair, LANES, LANES), F32)],
        compiler_params=_cparams(2),
        name="rwkv_chunk",
    )(r, lw, k2, v, al, be, g, bonus, gn_w, gn_b)


def _rwkv_one_step(S, r, lw, k, al, be, v, g, bonus, gnw, gnb):
    sa = jnp.sum(S * al, axis=-1, keepdims=True)
    s_new = S * jnp.exp(lw) + sa * be + v * k
    y = jnp.sum(s_new * r, axis=-1, keepdims=True)
    mu = jnp.mean(y, axis=1, keepdims=True)
    yc = y - mu
    var = jnp.mean(yc * yc, axis=1, keepdims=True)
    yn = yc * lax.rsqrt(var + GN_EPS) * gnw + gnb
    return (yn + bonus) * g, s_new


def _attn_one_step(q, k_new, v_new, kt, vt, slopes, P):
    aw = q.shape[1]
    H = aw // HEAD_DIM
    scale = HEAD_DIM ** -0.5
    dist = P - lax.broadcasted_iota(jnp.int32, (1, P), 1)
    count = jnp.zeros((1, P), F32)
    for win, dil in DIL_PATTERNS:
        assert dil & (dil - 1) == 0
        count = count + ((dist <= win) & ((dist & (dil - 1)) == 0)).astype(F32)
    own_head = (lax.broadcasted_iota(jnp.int32, (H, aw), 1) // HEAD_DIM
                == lax.broadcasted_iota(jnp.int32, (H, aw), 0)).astype(F32)
    q_heads = q * own_head
    s = _dot(q_heads, kt) * scale - slopes * dist.astype(F32)
    s = jnp.where(count > 0.0, s, NEG)
    s0 = jnp.sum(q_heads * k_new, axis=-1, keepdims=True) * scale
    m = jnp.maximum(jnp.max(s, axis=-1, keepdims=True), s0)
    p = count * jnp.exp(s - m)
    e0 = len(DIL_PATTERNS) * jnp.exp(s0 - m)
    l = jnp.sum(p, axis=-1, keepdims=True) + e0
    acc = _dot(p, vt, "NT") + e0 * v_new
    return jnp.sum(acc * own_head, axis=0, keepdims=True) / jnp.sum(l * own_head, axis=0, keepdims=True)


def _mixer_step_kernel(*refs, P, aw):
    _mixer_step_body(pl.program_id(0), *refs, P=P, aw=aw)


N_MIXER_IN = 15


def _mixer_step_body(b, q_ref, kn_ref, vn_ref, col_ref, kt_ref, vt_ref, sl_ref, s_ref, r_ref, lw_ref, k_ref, al_ref,
                     be_ref, gnw_ref, gnb_ref, att_ref, y_ref, so_ref, *, P, aw):
    H = aw // HEAD_DIM
    own = (lax.broadcasted_iota(jnp.int32, (1, LANES), 1) == b).astype(F32)

    def column(i):
        return jnp.sum(col_ref[i * aw:(i + 1) * aw, :] * own, axis=-1, keepdims=True).reshape(H, HEAD_DIM, 1)

    @pl.when(b == 0)
    def _():
        y_ref[...] = jnp.zeros_like(y_ref)

    att_ref[0] = _attn_one_step(q_ref[0], kn_ref[0], vn_ref[0], kt_ref[0].reshape(aw, P), vt_ref[0].reshape(aw, P),
                                sl_ref[...], P)
    y, s_new = _rwkv_one_step(s_ref[0], r_ref[0], lw_ref[0], k_ref[0], al_ref[0], be_ref[0], column(0), column(1),
                              column(2), gnw_ref[...], gnb_ref[...])
    so_ref[0] = s_new
    y_ref[...] += y.reshape(aw, 1) * own


def _mixer_step_operands(seq_of, q, k_new, v_new, col, k_cache, v_cache, slopes, state, r, lw, k2, al, be, gn_w,
                         gn_b):
    nb, P, H, _ = k_cache.shape
    aw = H * HEAD_DIM
    rowv = lambda a: a.reshape(nb, H, 1, HEAD_DIM)
    vec = lambda a: a.reshape(nb, 1, aw)
    per_seq = lambda shape: pl.BlockSpec((1,) + shape, lambda *g: (seq_of(*g),) + (0,) * len(shape))
    t_spec, s_spec, r_spec = per_seq((H, HEAD_DIM, P)), per_seq((H, HEAD_DIM, HEAD_DIM)), per_seq((H, 1, HEAD_DIM))
    v_spec = per_seq((1, aw))
    p_spec = _full((H, HEAD_DIM, 1))
    args = (vec(q), vec(k_new), vec(v_new), col, jnp.transpose(k_cache, (0, 2, 3, 1)),
            jnp.transpose(v_cache, (0, 2, 3, 1)), slopes.reshape(H, 1), state, rowv(r), rowv(lw), rowv(k2), rowv(al),
            rowv(be), gn_w.reshape(H, HEAD_DIM, 1), gn_b.reshape(H, HEAD_DIM, 1))
    assert len(args) == N_MIXER_IN
    in_specs = ([v_spec] * 3 + [_full(col.shape), t_spec, t_spec, _full((H, 1)), s_spec] + [r_spec] * 5
                + [p_spec] * 2)
    out_specs = [v_spec, _full((aw, LANES)), s_spec]
    out_shape = [jax.ShapeDtypeStruct((nb, 1, aw), F32), jax.ShapeDtypeStruct((aw, LANES), F32),
                 jax.ShapeDtypeStruct(state.shape, F32)]
    return args, in_specs, out_specs, out_shape


def _mixer_step(*operands):
    args, in_specs, out_specs, out_shape = _mixer_step_operands(lambda b: b, *operands)
    nb, P, H, _ = operands[4].shape
    return pl.pallas_call(
        functools.partial(_mixer_step_kernel, P=P, aw=H * HEAD_DIM),
        grid=(nb,),
        in_specs=in_specs,
        out_specs=out_specs,
        out_shape=out_shape,
        compiler_params=_cparams(1),
        name="mixer_step",
    )(*args)


def _cast_operands(step_of, n_steps, weights):
    in_specs, out_specs, out_shape = [], [], []
    for w in weights:
        rows, cols = w.shape
        per = next(d for d in range(1, n_steps + 1)
                   if n_steps % d == 0 and rows % (n_steps // d) == 0 and rows // (n_steps // d) % 16 == 0)
        spec = pl.BlockSpec((rows * per // n_steps, cols), lambda *g, per=per: (step_of(*g) // per, 0))
        in_specs.append(spec)
        out_specs.append(spec)
        out_shape.append(jax.ShapeDtypeStruct(w.shape, BF16))
    return in_specs, out_specs, out_shape


def _attn_mixer_kernel(q_ref, k_ref, v_ref, sl_ref, *refs, T, P, aw, npair, n_cast):
    n_in = N_MIXER_IN + n_cast
    mixer_in, cast_in = refs[:N_MIXER_IN], refs[N_MIXER_IN:n_in]
    o_ref, *mixer_out = refs[n_in:n_in + 4]
    cast_out, scratch = refs[n_in + 4:n_in + 4 + n_cast], refs[n_in + 4 + n_cast:]
    for src, dst in zip(cast_in, cast_out):
        dst[...] = src[...].astype(BF16)
    _mixer_step_body(pl.program_id(0) * npair + pl.program_id(1), *mixer_in, *mixer_out, P=P, aw=aw)
    _prompt_attn_kernel(q_ref, k_ref, v_ref, sl_ref, o_ref, *scratch, T=T)


def _prompt_attn_and_mixer_step(q, k, v, slopes_tab, mixer_operands, weights):
    B, T, W = q.shape
    npair = W // LANES
    step_of = lambda b, hp: b * npair + hp
    args, in_specs, out_specs, out_shape = _mixer_step_operands(step_of, *mixer_operands)
    c_in, c_out, c_shape = _cast_operands(step_of, B * npair, weights)
    nb, P, H, _ = mixer_operands[4].shape
    assert nb == B * npair
    spec = pl.BlockSpec((1, T, LANES), lambda b, hp: (b, 0, hp))
    return pl.pallas_call(
        functools.partial(_attn_mixer_kernel, T=T, P=P, aw=H * HEAD_DIM, npair=npair, n_cast=len(weights)),
        grid=(B, npair),
        in_specs=[spec, spec, spec, _full(slopes_tab.shape)] + in_specs + c_in,
        out_specs=[spec] + out_specs + c_out,
        out_shape=[jax.ShapeDtypeStruct((B, T, W), BF16)] + out_shape + c_shape,
        scratch_shapes=_prompt_attn_scratch(T),
        compiler_params=_cparams(2),
        name="prompt_attn_mixer_step",
    )(q, k, v, slopes_tab, *args, *weights)


def _ffn_tail(x1, gate_pre, prev1, prev2, up, cw_ref, cb_ref, w2_ref, g_ref, b_ref, alpha):
    c = cw_ref[0:1, :] * prev2 + cw_ref[1:2, :] * prev1 + cw_ref[2:3, :] * gate_pre + cb_ref[...]
    h = c * _sigmoid(c) * up
    return _layer_norm(alpha * x1 + _dot(h, w2_ref[...]), g_ref[...], b_ref[...])


def _ffn_seq_kernel(att_ref, y_ref, x_ref, wo_ref, g1_ref, b1_ref, w1_ref, cw_ref, cb_ref, w2_ref, g_ref, b_ref,
                    o_ref, tail_ref, carry_ref, *, aw, dff, alpha):
    @pl.when(pl.program_id(1) == 0)
    def _():
        carry_ref[...] = jnp.zeros_like(carry_ref)

    tm = x_ref.shape[1]
    hr = tm // SUB_TILES
    spans = [slice(h * hr, (h + 1) * hr) for h in range(SUB_TILES)]
    n = range(SUB_TILES)
    row = lax.broadcasted_iota(jnp.int32, (hr, dff), 0)
    mixed = [_dot(att_ref[0, sp, :], wo_ref[0:aw, :]) + _dot(y_ref[0, sp, :], wo_ref[aw:, :]) for sp in spans]
    x1 = [_layer_norm(alpha * x_ref[0, spans[i], :] + mixed[i], g1_ref[...], b1_ref[...]) for i in n]
    hu = [_dot(x1[0], w1_ref[...])]
    last2 = carry_ref[0:2, :]
    gated, f = [], []
    for i in n:
        if i + 1 < SUB_TILES:
            hu.append(_dot(x1[i + 1], w1_ref[...]))
        gate_pre = hu[i][:, 0:dff]
        c0, c1 = last2[0:1, :], last2[1:2, :]
        prev1 = jnp.where(row == 0, c1, pltpu.roll(gate_pre, 1, axis=0))
        prev2 = jnp.where(row == 0, c0, jnp.where(row == 1, c1, pltpu.roll(gate_pre, 2, axis=0)))
        last2 = gate_pre[hr - 2:hr, :]
        c = cw_ref[0:1, :] * prev2 + cw_ref[1:2, :] * prev1 + cw_ref[2:3, :] * gate_pre + cb_ref[...]
        gated.append(c * _sigmoid(c) * hu[i][:, dff:])
        if i > 0:
            f.append(_dot(gated[i - 1], w2_ref[...]))
    f.append(_dot(gated[-1], w2_ref[...]))
    carry_ref[0:2, :] = last2
    tail_ref[0] = last2
    for i in n:
        o_ref[0, spans[i], :] = _layer_norm(alpha * x1[i] + f[i], g_ref[...], b_ref[...])


def _ffn_step_kernel(att_ref, y_ref, x_ref, wo_ref, g1_ref, b1_ref, p2_ref, p1_ref, w1_ref, cw_ref, cb_ref, w2_ref,
                     g_ref, b_ref, o_ref, gate_ref, *, dff, alpha):
    nb, aw = att_ref.shape
    h = _dot(att_ref[...], wo_ref[0:aw, :]) + _dot(y_ref[...].T[0:nb, :], wo_ref[aw:, :])
    x1 = _layer_norm(alpha * x_ref[...] + h, g1_ref[...], b1_ref[...])
    hu = _dot(x1, w1_ref[...])
    gate_pre = hu[:, 0:dff]
    gate_ref[...] = gate_pre
    o_ref[...] = _ffn_tail(x1, gate_pre, p1_ref[...], p2_ref[...], hu[:, dff:], cw_ref, cb_ref, w2_ref, g_ref,
                           b_ref, alpha)


def _resident(shape):
    nd = len(shape)
    return pl.BlockSpec(shape, lambda *_: (0,) * nd, pipeline_mode=pl.Buffered(1))


def _ffn_seq(att, y_rw, x, w_out, ln1_g, ln1_b, w1, conv_w, conv_b, w2, ln2_g, ln2_b, tm, alpha):
    B, T, D = x.shape
    aw = att.shape[-1]
    dff = w2.shape[0]
    half = pl.BlockSpec((1, tm, aw), lambda b, t: (b, t, 0))
    rows = pl.BlockSpec((1, tm, D), lambda b, t: (b, t, 0))
    return pl.pallas_call(
        functools.partial(_ffn_seq_kernel, aw=aw, dff=dff, alpha=alpha),
        grid=(B, T // tm),
        in_specs=[half, half, rows, _resident(w_out.shape), _full(ln1_g.shape), _full(ln1_b.shape),
                  _resident(w1.shape), _full(conv_w.shape), _full(conv_b.shape), _resident(w2.shape),
                  _full(ln2_g.shape), _full(ln2_b.shape)],
        out_specs=[rows, pl.BlockSpec((1, 2, dff), lambda b, t: (b, 0, 0))],
        out_shape=[jax.ShapeDtypeStruct((B, T, D), F32), jax.ShapeDtypeStruct((B, 2, dff), F32)],
        scratch_shapes=[pltpu.VMEM((8, dff), F32)],
        compiler_params=_cparams(2),
        name="ffn_seq",
    )(att, y_rw, x, w_out, ln1_g, ln1_b, w1, conv_w, conv_b, w2, ln2_g, ln2_b)


def _ffn_step(att, y_cols, x, w_out, ln1_g, ln1_b, prev2, prev1, w1, conv_w, conv_b, w2, ln2_g, ln2_b, alpha):
    nb, D = x.shape
    dff = w2.shape[0]
    return pl.pallas_call(
        functools.partial(_ffn_step_kernel, dff=dff, alpha=alpha),
        grid=(1,),
        in_specs=[_full(att.shape), _full(y_cols.shape), _full(x.shape), _resident(w_out.shape), _full(ln1_g.shape),
                  _full(ln1_b.shape),
                  _full(prev2.shape), _full(prev1.shape), _resident(w1.shape), _full(conv_w.shape),
                  _full(conv_b.shape), _resident(w2.shape), _full(ln2_g.shape), _full(ln2_b.shape)],
        out_specs=[_full((nb, D)), _full((nb, dff))],
        out_shape=[jax.ShapeDtypeStruct((nb, D), F32), jax.ShapeDtypeStruct((nb, dff), F32)],
        compiler_params=_cparams(1),
        name="ffn_step",
    )(att, y_cols, x, w_out, ln1_g, ln1_b, prev2, prev1, w1, conv_w, conv_b, w2, ln2_g, ln2_b)


def _layer_params(w_in, mu_shift, w0, w_lora_up, a0, a_lora_up, g_lora_up, k_k, k_a, r_k, aw):
    D, ncol = w_in.shape
    n_w, n_a, n_g = w_lora_up.shape[0], a_lora_up.shape[0], g_lora_up.shape[0]
    assert n_w + n_a == LANES and 3 * aw + 3 * aw + n_w + n_a + n_g == ncol
    g_cols = -(-n_g // LANES) * LANES
    pad = g_cols - n_g
    row = lambda a: a.reshape(1, -1).astype(F32)
    lane = jnp.arange(aw)
    bd = (lane[:, None] // HEAD_DIM == lane[None, :] // HEAD_DIM).astype(BF16)
    return dict(
        aw=aw,
        w_in=jnp.pad(w_in, ((0, 0), (0, pad))).astype(BF16),
        mu=jnp.pad(row(mu_shift), ((0, 0), (0, pad))),
        w0=row(w0), a0=row(a0), k_k=row(k_k), k_a=row(k_a), r_k=row(r_k),
        wl_pad=jnp.pad(w_lora_up, ((0, n_a), (0, 0))).astype(BF16),
        al_pad=jnp.pad(a_lora_up, ((n_w, 0), (0, 0))).astype(BF16),
        g_pad=jnp.pad(g_lora_up, ((0, pad), (0, 0))).astype(BF16),
        bd=bd,
    )


def kernel(x_prompt, x_sample, cache_k_win, cache_v_win, state_shift, state_wkv, state_conv, w_in, mu_shift, w0, w_lora_up, a0, a_lora_up, g_lora_up, k_k, k_a, r_k, gn_w, gn_b, w_out, ln1_g, ln1_b, w_ffn_in, conv_w, conv_b, w_ffn_out, ln2_g, ln2_b):
    depth = w_in.shape[0]
    alpha = (2.0 * depth) ** 0.25
    B, T, D = x_prompt.shape
    nb = x_sample.shape[0]
    assert x_sample.shape[1] == 1
    n_att = cache_k_win.shape[3]
    n_rw = state_wkv.shape[2]
    aw = n_att * HEAD_DIM
    assert n_rw * HEAD_DIM == aw and T % (Q_TILE * DIL_PATTERNS[-1][1]) == 0
    tm = min(256, T)
    tb = min(256, T)
    heads = jnp.arange(1, n_att + 1, dtype=F32)
    slopes = jnp.exp2(-8.0 * heads / n_att)
    slopes_tab = jnp.broadcast_to(slopes[:, None], (n_att, LANES))
    row = lambda a: a.reshape(1, -1)

    hp, hs = x_prompt, x_sample.reshape(nb, D)
    outs = [[] for _ in range(10)]
    for l in range(depth):
        prm = _layer_params(w_in[l], mu_shift[l], w0[l], w_lora_up[l], a0[l], a_lora_up[l], g_lora_up[l],
                            k_k[l], k_a[l], r_k[l], aw)
        late_weights = (w_out[l], w_ffn_in[l], w_ffn_out[l])
        gnw, gnb = row(gn_w[l]), row(gn_b[l])

        q, k, v, kt, vt, r, lw, k2, vr, al, be, g, bonus = _inproj_seq(hp, prm, min(2 * tm, T))
        x2 = jnp.concatenate([hs, state_shift[l]], axis=0)
        qs, ks, vs, r_s, lw_s, k2_s, al_s, be_s, col = _inproj_step(x2, prm)
        mixer_operands = (qs, ks, vs, col, cache_k_win[l], cache_v_win[l], slopes, state_wkv[l], r_s, lw_s, k2_s,
                          al_s, be_s, gn_w[l], gn_b[l])

        if nb == B * (aw // LANES):
            att, att_s, y_s, wkv_s, w_out_b, w1_b, w2_b = _prompt_attn_and_mixer_step(
                q, k, v, slopes_tab, mixer_operands, late_weights)
        else:
            att = _prompt_attn(q, k, v, slopes_tab)
            att_s, y_s, wkv_s = _mixer_step(*mixer_operands)
            w_out_b, w1_b, w2_b = (w.astype(BF16) for w in late_weights)
        y_rw, wkv_p = _rwkv_chunked(r, lw, k2, vr, al, be, g, bonus, gnw, gnb, tb)

        shift_p = hp[:, -1, :]
        hp, conv_p = _ffn_seq(att, y_rw, hp, w_out_b, row(ln1_g[l]), row(ln1_b[l]), w1_b, conv_w[l],
                              row(conv_b[l]), w2_b, row(ln2_g[l]), row(ln2_b[l]), min(2 * tm, T), alpha)
        shift_s = hs
        hs, gate_s = _ffn_step(att_s.reshape(nb, aw), y_s, hs, w_out_b, row(ln1_g[l]), row(ln1_b[l]),
                               state_conv[l][:, 0, :],
                               state_conv[l][:, 1, :], w1_b, conv_w[l], row(conv_b[l]), w2_b, row(ln2_g[l]),
                               row(ln2_b[l]), alpha)
        conv_s = jnp.stack([state_conv[l][:, 1, :], gate_s], axis=1)

        n_keep = min(DIL_PATTERNS[-1][0], T)
        win = lambda a: jnp.transpose(a.reshape(B, n_att, HEAD_DIM, T), (0, 3, 1, 2))[:, T - n_keep:]
        vals = (win(kt), win(vt), ks.reshape(nb, 1, n_att, HEAD_DIM), vs.reshape(nb, 1, n_att, HEAD_DIM),
                shift_p, shift_s, wkv_p, wkv_s, conv_p, conv_s)
        for lst, val in zip(outs, vals):
            lst.append(val)
    return (hp, hs.reshape(nb, 1, D)) + tuple(jnp.stack(lst) for lst in outs)
```

```python
import functools
import math

import jax
import jax.numpy as jnp
from jax import lax
from jax.experimental import pallas as pl
from jax.experimental.pallas import tpu as pltpu

F32 = jnp.float32
BF16 = jnp.bfloat16

HEAD_DIM = 64
LANES = 128
MXU_TILE = 256
DIL_PATTERNS = ((128, 1), (512, 4), (2048, 16))
WIN_STEPS = 128
Q_TILE = 128
GROUP = 4
SUB_TILES = 2
CHUNK = 64
LN_EPS = 1e-5
GN_EPS = 64e-5
NEG = -1e30
VMEM_LIMIT = 56 * 1024 * 1024

_DN = {"NN": (((1,), (0,)), ((), ())), "NT": (((1,), (1,)), ((), ())), "TN": (((0,), (0,)), ((), ()))}


def _dot(a, b, dims="NN"):
    return lax.dot_general(a.astype(BF16), b.astype(BF16), _DN[dims], preferred_element_type=F32)


def _dot_const_l(c, x):
    hi = x.astype(BF16)
    r1 = x - hi.astype(F32)
    mid = r1.astype(BF16)
    lo = (r1 - mid.astype(F32)).astype(BF16)
    return _dot(c, hi) + _dot(c, mid) + _dot(c, lo)


def _sigmoid(x):
    return 0.5 * jnp.tanh(0.5 * x) + 0.5


def _layer_norm(x, g, b):
    mu = jnp.mean(x, axis=-1, keepdims=True)
    xc = x - mu
    var = jnp.mean(xc * xc, axis=-1, keepdims=True)
    return xc * lax.rsqrt(var + LN_EPS) * g + b


def _cparams(n_grid):
    return pltpu.CompilerParams(dimension_semantics=("arbitrary",) * n_grid, vmem_limit_bytes=VMEM_LIMIT)


def _full(shape):
    nd = len(shape)
    return pl.BlockSpec(shape, lambda *_: (0,) * nd)


def _rwkv_prep(rw, w0, a0, k_k, k_a, r_k, wl_pad, al_pad, g_pad, bd, aw):
    r = rw[:, 0:aw]
    k = rw[:, aw:2 * aw]
    v = rw[:, 2 * aw:3 * aw]
    lo = rw[:, 3 * aw:3 * aw + 128]
    glo = rw[:, 3 * aw + 128:]
    u = w0 + _dot(jnp.tanh(lo), wl_pad)
    a = _sigmoid(a0 + _dot(lo, al_pad))
    g = _dot(_sigmoid(glo), g_pad)
    kk = k * k_k
    def head_sums(x):
        w = bd.shape[0]
        return jnp.concatenate([_dot(x[:, c:c + w], bd) for c in range(0, aw, w)], axis=1)

    sumsq = head_sums(kk * kk)
    lw = -math.exp(-0.5) * _sigmoid(u)
    k2 = k * (1.0 + (a - 1.0) * k_a)
    bonus = head_sums(r * k2 * r_k) * v
    kkn = kk * lax.rsqrt(jnp.maximum(sumsq, 1e-24))
    return r, lw, k2, v, -kkn, kkn * a, g, bonus


def _inproj_seq_kernel(x_ref, w_ref, mu_ref, w0_ref, a0_ref, kk_ref, ka_ref, rk_ref, wl_ref, al_ref, gp_ref,
                       bd_ref, q_ref, k_ref, v_ref, kt_ref, vt_ref, r_o, lw_o, k2_o, vr_o, al_o, be_o, g_o, bo_o,
                       carry_ref, *, aw):
    @pl.when(pl.program_id(1) == 0)
    def _():
        carry_ref[...] = jnp.zeros_like(carry_ref)

    tm = x_ref.shape[1]
    hr = tm // SUB_TILES
    spans = [slice(h * hr, (h + 1) * hr) for h in range(SUB_TILES)]
    params = (w0_ref[...], a0_ref[...], kk_ref[...], ka_ref[...], rk_ref[...], wl_ref[...], al_ref[...],
              gp_ref[...], bd_ref[...], aw)
    row = lax.broadcasted_iota(jnp.int32, (hr, w_ref.shape[1] - 3 * aw), 0)

    def finish(p, sp, last):
        q_ref[0, sp, :] = p[:, 0:aw]
        k = p[:, aw:2 * aw]
        v = p[:, 2 * aw:3 * aw]
        k_ref[0, sp, :] = k
        v_ref[0, sp, :] = v
        kt_ref[0, :, sp] = k.T
        vt_ref[0, :, sp] = v.T
        prw = p[:, 3 * aw:]
        prev = jnp.where(row == 0, last, pltpu.roll(prw, 1, axis=0))
        outs = _rwkv_prep(prw + (prev - prw) * mu_ref[...], *params)
        for o_ref, val in zip((r_o, lw_o, k2_o, vr_o, al_o, be_o, g_o, bo_o), outs):
            o_ref[0, sp, :] = val.astype(o_ref.dtype)
        return prw[hr - 1:hr, :]

    last = carry_ref[0:1, :]
    p = _dot(x_ref[0, spans[0], :], w_ref[...])
    for h in range(1, SUB_TILES):
        p_next = _dot(x_ref[0, spans[h], :], w_ref[...])
        last = finish(p, spans[h - 1], last)
        p = p_next
    carry_ref[0:1, :] = finish(p, spans[-1], last)


def _inproj_step_kernel(x_ref, w_ref, mu_ref, w0_ref, a0_ref, kk_ref, ka_ref, rk_ref, wl_ref, al_ref, gp_ref,
                        bd_ref, q_ref, k_ref, v_ref, r_o, lw_o, k2_o, al_o, be_o, col_o, *, aw, nb):
    p = _dot(x_ref[...], w_ref[...])
    q_ref[...] = p[0:nb, 0:aw]
    k_ref[...] = p[0:nb, aw:2 * aw]
    v_ref[...] = p[0:nb, 2 * aw:3 * aw]
    prw = p[0:nb, 3 * aw:]
    prev = p[nb:2 * nb, 3 * aw:]
    rw = prw + (prev - prw) * mu_ref[...]
    r, lw, k2, vr, al, be, g, bonus = _rwkv_prep(rw, w0_ref[...], a0_ref[...], kk_ref[...], ka_ref[...],
                                                 rk_ref[...], wl_ref[...], al_ref[...], gp_ref[...], bd_ref[...], aw)
    for o_ref, val in zip((r_o, lw_o, k2_o, al_o, be_o), (r, lw, k2, al, be)):
        o_ref[...] = val
    pad = jnp.zeros((LANES - nb, aw), F32)
    for i, val in enumerate((vr, g, bonus)):
        col_o[i * aw:(i + 1) * aw, :] = jnp.concatenate([val, pad], axis=0).T


_PREP_PARAMS = ("mu", "w0", "a0", "k_k", "k_a", "r_k", "wl_pad", "al_pad", "g_pad", "bd")


def _inproj_seq(x, prm, tm):
    B, T, D = x.shape
    aw = prm["aw"]
    nrw = prm["w_in"].shape[1] - 3 * aw
    small = [prm[n] for n in _PREP_PARAMS]
    row_spec = pl.BlockSpec((1, tm, aw), lambda b, t: (b, t, 0))
    col_spec = pl.BlockSpec((1, aw, tm), lambda b, t: (b, 0, t))
    row_sds = jax.ShapeDtypeStruct((B, T, aw), F32)
    col_sds = jax.ShapeDtypeStruct((B, aw, T), F32)
    return pl.pallas_call(
        functools.partial(_inproj_seq_kernel, aw=aw),
        grid=(B, T // tm),
        in_specs=[pl.BlockSpec((1, tm, D), lambda b, t: (b, t, 0)), _resident(prm["w_in"].shape)]
        + [_full(s.shape) for s in small],
        out_specs=[row_spec] * 3 + [col_spec] * 2 + [row_spec] * 8,
        out_shape=[row_sds] * 3 + [col_sds] * 2
        + [jax.ShapeDtypeStruct((B, T, aw), F32 if n == "lw" else BF16)
           for n in ("r", "lw", "k2", "v", "al", "be", "g", "bonus")],
        scratch_shapes=[pltpu.VMEM((8, nrw), F32)],
        compiler_params=_cparams(2),
        name="inproj_seq",
    )(x, prm["w_in"], *small)


def _inproj_step(x2, prm):
    nb = x2.shape[0] // 2
    aw = prm["aw"]
    small = [prm[n] for n in _PREP_PARAMS]
    assert nb <= LANES
    out_sds = jax.ShapeDtypeStruct((nb, aw), F32)
    return pl.pallas_call(
        functools.partial(_inproj_step_kernel, aw=aw, nb=nb),
        grid=(1,),
        in_specs=[_full(x2.shape), _full(prm["w_in"].shape)] + [_full(s.shape) for s in small],
        out_specs=[_full((nb, aw))] * 8 + [_full((3 * aw, LANES))],
        out_shape=[out_sds] * 8 + [jax.ShapeDtypeStruct((3 * aw, LANES), F32)],
        compiler_params=_cparams(1),
        name="inproj_step",
    )(x2, prm["w_in"], *small)


def _prompt_attn_kernel(q_ref, k_ref, v_ref, sl_ref, o_ref, acc_s, m_s, l_s, yq, yk, yv, yacc, ym, yl, *, T):
    hp = pl.program_id(1)
    scale = HEAD_DIM ** -0.5
    lane = lax.broadcasted_iota(jnp.int32, (1, LANES), 1)
    head_masks = ((lane < HEAD_DIM).astype(F32), (lane >= HEAD_DIM).astype(F32))
    qi = lax.broadcasted_iota(jnp.int32, (Q_TILE, Q_TILE), 0)
    ki = lax.broadcasted_iota(jnp.int32, (Q_TILE, Q_TILE), 1)
    steps_cur = (qi - ki).astype(F32)
    steps_prev = (qi - ki + Q_TILE).astype(F32)
    ok_cur = ki <= qi
    ok_prev = ki >= qi
    first_lanes = lax.broadcasted_iota(jnp.int32, (Q_TILE, LANES), 1) < HEAD_DIM
    slopes = [sl_ref[pl.ds(2 * hp + j, 1), :] for j in range(2)]

    def tile_group(src, tiles, bias, stats, first_branch):
        rq, rk, rv = src
        bias_c, bias_pc = bias
        acc_s, m_s, l_s = stats
        ids = range(len(tiles))
        q = [rq(rows) * scale for rows, _ in tiles]
        kk = [rk(rows) if prev is None else jnp.concatenate([rk(prev), rk(rows)], axis=0) for rows, prev in tiles]
        vv = [rv(rows) if prev is None else jnp.concatenate([rv(prev), rv(rows)], axis=0) for rows, prev in tiles]
        qq = [jnp.concatenate([q[t] * head_masks[0], q[t] * head_masks[1]], axis=0) for t in ids]
        s = [_dot(qq[t], kk[t], "NT") + (bias_c if tiles[t][1] is None else bias_pc) for t in ids]
        m = [jnp.max(s[t], axis=-1, keepdims=True) for t in ids]
        p = [jnp.exp(s[t] - m[t]) for t in ids]
        l = [jnp.sum(p[t], axis=-1, keepdims=True) for t in ids]
        acc = [_dot(p[t], vv[t]) for t in ids]
        for t, (rows, _) in enumerate(tiles):
            acc_n = jnp.where(first_lanes, acc[t][0:Q_TILE], acc[t][Q_TILE:])
            m_n = jnp.where(first_lanes, m[t][0:Q_TILE], m[t][Q_TILE:])
            l_n = jnp.where(first_lanes, l[t][0:Q_TILE], l[t][Q_TILE:])
            if first_branch:
                acc_s[rows, :] = acc_n
                m_s[rows, :] = m_n
                l_s[rows, :] = l_n
            else:
                m_o = m_s[rows, :]
                m_new = jnp.maximum(m_o, m_n)
                e_o = jnp.exp(m_o - m_new)
                e_n = jnp.exp(m_n - m_new)
                acc_s[rows, :] = acc_s[rows, :] * e_o + acc_n * e_n
                l_s[rows, :] = l_s[rows, :] * e_o + l_n * e_n
                m_s[rows, :] = m_new

    dils = [dil for _, dil in DIL_PATTERNS]
    R = dils[1]
    assert dils == [1, R, R * R] and all(win // dil == WIN_STEPS for win, dil in DIL_PATTERNS)

    def bias_of(dil):
        b_cur = [jnp.where(ok_cur, -(slopes[j] * float(dil)) * steps_cur, NEG) for j in range(2)]
        b_prev = [jnp.where(ok_prev, -(slopes[j] * float(dil)) * steps_prev, NEG) for j in range(2)]
        return (jnp.concatenate(b_cur, axis=0),
                jnp.concatenate([jnp.concatenate([b_prev[j], b_cur[j]], axis=1) for j in range(2)], axis=0))

    def groups(tiles):
        return [tiles[i:i + GROUP] for i in range(0, len(tiles), GROUP)]

    natural = tuple((lambda rows, ref=ref: ref[0, rows, :]) for ref in (q_ref, k_ref, v_ref))
    scratch = tuple((lambda rows, ref=ref: ref[rows, :]) for ref in (yq, yk, yv))
    nat_stats = (acc_s, m_s, l_s)
    y_stats = (yacc, ym, yl)

    nb1 = T // Q_TILE
    assert nb1 % GROUP == 0
    bias1 = bias_of(dils[0])
    tile_group(natural, [(pl.ds(jb * Q_TILE, Q_TILE), pl.ds((jb - 1) * Q_TILE, Q_TILE) if jb else None)
                         for jb in range(GROUP)], bias1, nat_stats, True)

    def later(g, carry):
        start = pl.multiple_of(g * (GROUP * Q_TILE), GROUP * Q_TILE)
        tile_group(natural, [(pl.ds(start + i * Q_TILE, Q_TILE), pl.ds(start + (i - 1) * Q_TILE, Q_TILE))
                             for i in range(GROUP)], bias1, nat_stats, True)
        return carry

    lax.fori_loop(1, nb1 // GROUP, later, 0)

    LR = T // R
    bias_r = bias_of(dils[1])
    bias_rr = bias_of(dils[2])

    def stream(s, carry):
        own = pl.ds(s, LR, stride=R)
        yq[...] = q_ref[0, own, :]
        yk[...] = k_ref[0, own, :]
        yv[...] = v_ref[0, own, :]
        for grp in groups([(pl.ds(jb * Q_TILE, Q_TILE), pl.ds((jb - 1) * Q_TILE, Q_TILE) if jb else None)
                           for jb in range(LR // Q_TILE)]):
            tile_group(scratch, grp, bias_r, y_stats, True)
        for grp in groups([(pl.ds(c + R * Q_TILE * jb, Q_TILE, stride=R),
                            pl.ds(c + R * Q_TILE * (jb - 1), Q_TILE, stride=R) if jb else None)
                           for c in range(R) for jb in range(LR // R // Q_TILE)]):
            tile_group(scratch, grp, bias_rr, y_stats, False)
        m_a, m_b = m_s[own, :], ym[...]
        m_new = jnp.maximum(m_a, m_b)
        e_a = jnp.exp(m_a - m_new)
        e_b = jnp.exp(m_b - m_new)
        acc_s[own, :] = (acc_s[own, :] * e_a + yacc[...] * e_b) / (l_s[own, :] * e_a + yl[...] * e_b)
        return carry

    lax.fori_loop(0, R, stream, 0)
    o_ref[0] = acc_s[...].astype(o_ref.dtype)


def _prompt_attn_scratch(T):
    return [pltpu.VMEM((T, LANES), F32)] * 3 + [pltpu.VMEM((T // DIL_PATTERNS[1][1], LANES), F32)] * 6


def _prompt_attn(q, k, v, slopes_tab):
    B, T, W = q.shape
    spec = pl.BlockSpec((1, T, LANES), lambda b, hp: (b, 0, hp))
    return pl.pallas_call(
        functools.partial(_prompt_attn_kernel, T=T),
        grid=(B, W // LANES),
        in_specs=[spec, spec, spec, _full(slopes_tab.shape)],
        out_specs=spec,
        out_shape=jax.ShapeDtypeStruct((B, T, W), BF16),
        scratch_shapes=_prompt_attn_scratch(T),
        compiler_params=_cparams(2),
        name="prompt_attn",
    )(q, k, v, slopes_tab)


def _rwkv_chunk_kernel(r_ref, lw_ref, k_ref, v_ref, al_ref, be_ref, g_ref, bo_ref, gnw_ref, gnb_ref,
                       y_ref, s_ref, st_scr, *, Tb, npair, nbb):
    C = CHUNK
    n2 = 2 * C
    t = pl.program_id(1)

    @pl.when(t == 0)
    def _():
        st_scr[...] = jnp.zeros_like(st_scr)

    lane = lax.broadcasted_iota(jnp.int32, (1, LANES), 1)
    m0 = (lane < HEAD_DIM).astype(F32)
    m1 = 1.0 - m0
    ri = lax.broadcasted_iota(jnp.int32, (n2, n2), 0)
    ci = lax.broadcasted_iota(jnp.int32, (n2, n2), 1)
    same = (ri < C) == (ci < C)
    strict = same & ((ri & (C - 1)) > (ci & (C - 1)))
    incl = same & ((ri & (C - 1)) >= (ci & (C - 1)))
    tri = (lax.broadcasted_iota(jnp.int32, (C, C), 0) >= lax.broadcasted_iota(jnp.int32, (C, C), 1)).astype(BF16)
    head_rows = ((lax.broadcasted_iota(jnp.int32, (n2, LANES), 0) < C)
                 == (lax.broadcasted_iota(jnp.int32, (n2, LANES), 1) < HEAD_DIM)).astype(F32)

    def stack(x):
        return jnp.concatenate([x * m0, x * m1], axis=0)

    def twice(x):
        return jnp.concatenate([x, x], axis=0)

    def body(c, carry):
        rows = pl.ds(pl.multiple_of(c * C, C), C)
        chains = [(bb, slice(hp * LANES, (hp + 1) * LANES)) for bb in range(nbb) for hp in range(npair)]
        ids = range(len(chains))
        S = [st_scr[i] for i in ids]
        lw_all = jnp.concatenate([lw_ref[bb, rows, :] for bb in range(nbb)], axis=1)
        cl_all = _dot_const_l(tri, lw_all)
        e_pos_all = jnp.exp(cl_all)
        e_neg_all = jnp.exp(-cl_all)
        e_exc_all = jnp.exp(cl_all - lw_all)
        wide = [slice(i * LANES, (i + 1) * LANES) for i in ids]
        g_end = [e_pos_all[C - 1:C, wide[i]] for i in ids]
        rt = [r_ref[bb, rows, cs] * e_pos_all[:, wide[i]] for i, (bb, cs) in enumerate(chains)]
        kt = [k_ref[bb, rows, cs] * e_neg_all[:, wide[i]] for i, (bb, cs) in enumerate(chains)]
        bt = [be_ref[bb, rows, cs] * e_neg_all[:, wide[i]] for i, (bb, cs) in enumerate(chains)]
        at = [al_ref[bb, rows, cs] * e_exc_all[:, wide[i]] for i, (bb, cs) in enumerate(chains)]
        ar_st = [jnp.concatenate([stack(at[i]), stack(rt[i])], axis=0) for i in ids]
        kb2 = [jnp.concatenate([twice(kt[i]), twice(bt[i])], axis=0) for i in ids]
        v_st = [stack(v_ref[bb, rows, cs]) for bb, cs in chains]
        gram = [_dot(ar_st[i], kb2[i], "NT") for i in ids]
        l_ab = [jnp.where(strict, gram[i][0:n2, n2:], 0.0) for i in ids]
        l_kk = [jnp.concatenate([jnp.where(strict, gram[i][0:n2, 0:n2], 0.0),
                                 jnp.where(incl, gram[i][n2:, 0:n2], 0.0)], axis=0) for i in ids]
        l_rb = [jnp.where(incl, gram[i][n2:, n2:], 0.0) for i in ids]
        from_state = [_dot(jnp.concatenate([at[i], rt[i]], axis=0), S[i], "NT") for i in ids]
        from_state = [jnp.concatenate([stack(from_state[i][0:C]), stack(from_state[i][C:])], axis=0) for i in ids]
        from_v = [_dot(l_kk[i], v_st[i]) for i in ids]
        x = l_ab
        u = [from_state[i][0:n2] + from_v[i][0:n2] for i in ids]
        for _ in range(C.bit_length() - 2):
            xu = [_dot(x[i], jnp.concatenate([x[i], u[i]], axis=1)) for i in ids]
            x = [xu[i][:, 0:n2] for i in ids]
            u = [u[i] + xu[i][:, n2:] for i in ids]
        u = [u[i] + _dot(x[i], u[i]) for i in ids]
        y_st = [from_state[i][n2:] + from_v[i][n2:] + _dot(l_rb[i], u[i]) for i in ids]
        for i in ids:
            vu = jnp.concatenate([v_st[i], u[i]], axis=0)
            kb_end = jnp.concatenate([stack(kt[i] * g_end[i]), stack(bt[i] * g_end[i])], axis=0)
            st_scr[i] = S[i] * g_end[i] + _dot(vu, kb_end, "TN")
        for i, (bb, cs) in enumerate(chains):
            mu = jnp.sum(y_st[i], axis=-1, keepdims=True) * (1.0 / HEAD_DIM)
            yc = (y_st[i] - mu) * head_rows
            var = jnp.sum(yc * yc, axis=-1, keepdims=True) * (1.0 / HEAD_DIM)
            yn_st = yc * lax.rsqrt(var + GN_EPS)
            yn_i = (yn_st[0:C, :] + yn_st[C:n2, :]) * gnw_ref[:, cs] + gnb_ref[:, cs]
            y_ref[bb, rows, cs] = ((yn_i + bo_ref[bb, rows, cs]) * g_ref[bb, rows, cs]).astype(y_ref.dtype)
        return carry

    lax.fori_loop(0, Tb // C, body, 0)

    @pl.when(t == pl.num_programs(1) - 1)
    def _():
        for i in range(nbb * npair):
            bb, hp = divmod(i, npair)
            pair = st_scr[i]
            s_ref[bb, 2 * hp] = pair[0:HEAD_DIM, 0:HEAD_DIM]
            s_ref[bb, 2 * hp + 1] = pltpu.roll(pair, HEAD_DIM, axis=1)[HEAD_DIM:, 0:HEAD_DIM]


def _rwkv_chunked(r, lw, k2, v, al, be, g, bonus, gn_w, gn_b, tb):
    B, T, W = r.shape
    npair = W // LANES
    nbb = next(n for n in (4, 2, 1) if B % n == 0)
    spec = pl.BlockSpec((nbb, tb, W), lambda b, t: (b, t, 0))
    st_spec = pl.BlockSpec((nbb, 2 * npair, HEAD_DIM, HEAD_DIM), lambda b, t: (b, 0, 0, 0))
    return pl.pallas_call(
        functools.partial(_rwkv_chunk_kernel, Tb=tb, npair=npair, nbb=nbb),
        grid=(B // nbb, T // tb),
        in_specs=[spec] * 8 + [_full(gn_w.shape), _full(gn_b.shape)],
        out_specs=[spec, st_spec],
        out_shape=[jax.ShapeDtypeStruct((B, T, W), BF16),
                   jax.ShapeDtypeStruct((B, 2 * npair, HEAD_DIM, HEAD_DIM), F32)],
        scratch_shapes=[pltpu.VMEM((nbb * npair, LANES, LANES), F32)],
        compiler_params=_cparams(2),
        name="rwkv_chunk",
    )(r, lw, k2, v, al, be, g, bonus, gn_w, gn_b)


def _rwkv_one_step(S, r, lw, k, al, be, v, g, bonus, gnw, gnb):
    sa = jnp.sum(S * al, axis=-1, keepdims=True)
    s_new = S * jnp.exp(lw) + sa * be + v * k
    y = jnp.sum(s_new * r, axis=-1, keepdims=True)
    mu = jnp.mean(y, axis=1, keepdims=True)
    yc = y - mu
    var = jnp.mean(yc * yc, axis=1, keepdims=True)
    yn = yc * lax.rsqrt(var + GN_EPS) * gnw + gnb
    return (yn + bonus) * g, s_new


def _attn_one_step(q, k_new, v_new, kt, vt, slopes, P):
    aw = q.shape[1]
    H = aw // HEAD_DIM
    scale = HEAD_DIM ** -0.5
    dist = P - lax.broadcasted_iota(jnp.int32, (1, P), 1)
    count = jnp.zeros((1, P), F32)
    for win, dil in DIL_PATTERNS:
        assert dil & (dil - 1) == 0
        count = count + ((dist <= win) & ((dist & (dil - 1)) == 0)).astype(F32)
    own_head = (lax.broadcasted_iota(jnp.int32, (H, aw), 1) // HEAD_DIM
                == lax.broadcasted_iota(jnp.int32, (H, aw), 0)).astype(F32)
    q_heads = q * own_head
    s = _dot(q_heads, kt) * scale - slopes * dist.astype(F32)
    s = jnp.where(count > 0.0, s, NEG)
    s0 = jnp.sum(q_heads * k_new, axis=-1, keepdims=True) * scale
    m = jnp.maximum(jnp.max(s, axis=-1, keepdims=True), s0)
    p = count * jnp.exp(s - m)
    e0 = len(DIL_PATTERNS) * jnp.exp(s0 - m)
    l = jnp.sum(p, axis=-1, keepdims=True) + e0
    acc = _dot(p, vt, "NT") + e0 * v_new
    return jnp.sum(acc * own_head, axis=0, keepdims=True) / jnp.sum(l * own_head, axis=0, keepdims=True)


def _mixer_step_kernel(*refs, P, aw):
    _mixer_step_body(pl.program_id(0), *refs, P=P, aw=aw)


N_MIXER_IN = 15


def _mixer_step_body(b, q_ref, kn_ref, vn_ref, col_ref, kt_ref, vt_ref, sl_ref, s_ref, r_ref, lw_ref, k_ref, al_ref,
                     be_ref, gnw_ref, gnb_ref, att_ref, y_ref, so_ref, *, P, aw):
    H = aw // HEAD_DIM
    own = (lax.broadcasted_iota(jnp.int32, (1, LANES), 1) == b).astype(F32)

    def column(i):
        return jnp.sum(col_ref[i * aw:(i + 1) * aw, :] * own, axis=-1, keepdims=True).reshape(H, HEAD_DIM, 1)

    @pl.when(b == 0)
    def _():
        y_ref[...] = jnp.zeros_like(y_ref)

    att_ref[0] = _attn_one_step(q_ref[0], kn_ref[0], vn_ref[0], kt_ref[0].reshape(aw, P), vt_ref[0].reshape(aw, P),
                                sl_ref[...], P)
    y, s_new = _rwkv_one_step(s_ref[0], r_ref[0], lw_ref[0], k_ref[0], al_ref[0], be_ref[0], column(0), column(1),
                              column(2), gnw_ref[...], gnb_ref[...])
    so_ref[0] = s_new
    y_ref[...] += y.reshape(aw, 1) * own


def _mixer_step_operands(seq_of, q, k_new, v_new, col, k_cache, v_cache, slopes, state, r, lw, k2, al, be, gn_w,
                         gn_b):
    nb, P, H, _ = k_cache.shape
    aw = H * HEAD_DIM
    rowv = lambda a: a.reshape(nb, H, 1, HEAD_DIM)
    vec = lambda a: a.reshape(nb, 1, aw)
    per_seq = lambda shape: pl.BlockSpec((1,) + shape, lambda *g: (seq_of(*g),) + (0,) * len(shape))
    t_spec, s_spec, r_spec = per_seq((H, HEAD_DIM, P)), per_seq((H, HEAD_DIM, HEAD_DIM)), per_seq((H, 1, HEAD_DIM))
    v_spec = per_seq((1, aw))
    p_spec = _full((H, HEAD_DIM, 1))
    args = (vec(q), vec(k_new), vec(v_new), col, jnp.transpose(k_cache, (0, 2, 3, 1)),
            jnp.transpose(v_cache, (0, 2, 3, 1)), slopes.reshape(H, 1), state, rowv(r), rowv(lw), rowv(k2), rowv(al),
            rowv(be), gn_w.reshape(H, HEAD_DIM, 1), gn_b.reshape(H, HEAD_DIM, 1))
    assert len(args) == N_MIXER_IN
    in_specs = ([v_spec] * 3 + [_full(col.shape), t_spec, t_spec, _full((H, 1)), s_spec] + [r_spec] * 5
                + [p_spec] * 2)
    out_specs = [v_spec, _full((aw, LANES)), s_spec]
    out_shape = [jax.ShapeDtypeStruct((nb, 1, aw), F32), jax.ShapeDtypeStruct((aw, LANES), F32),
                 jax.ShapeDtypeStruct(state.shape, F32)]
    return args, in_specs, out_specs, out_shape


def _mixer_step(*operands):
    args, in_specs, out_specs, out_shape = _mixer_step_operands(lambda b: b, *operands)
    nb, P, H, _ = operands[4].shape
    return pl.pallas_call(
        functools.partial(_mixer_step_kernel, P=P, aw=H * HEAD_DIM),
        grid=(nb,),
        in_specs=in_specs,
        out_specs=out_specs,
        out_shape=out_shape,
        compiler_params=_cparams(1),
        name="mixer_step",
    )(*args)


def _cast_operands(step_of, n_steps, weights):
    in_specs, out_specs, out_shape = [], [], []
    for w in weights:
        rows, cols = w.shape
        per = next(d for d in range(1, n_steps + 1)
                   if n_steps % d == 0 and rows % (n_steps // d) == 0 and rows // (n_steps // d) % 16 == 0)
        spec = pl.BlockSpec((rows * per // n_steps, cols), lambda *g, per=per: (step_of(*g) // per, 0))
        in_specs.append(spec)
        out_specs.append(spec)
        out_shape.append(jax.ShapeDtypeStruct(w.shape, BF16))
    return in_specs, out_specs, out_shape


def _attn_mixer_kernel(q_ref, k_ref, v_ref, sl_ref, *refs, T, P, aw, npair, n_cast):
    n_in = N_MIXER_IN + n_cast
    mixer_in, cast_in = refs[:N_MIXER_IN], refs[N_MIXER_IN:n_in]
    o_ref, *mixer_out = refs[n_in:n_in + 4]
    cast_out, scratch = refs[n_in + 4:n_in + 4 + n_cast], refs[n_in + 4 + n_cast:]
    for src, dst in zip(cast_in, cast_out):
        dst[...] = src[...].astype(BF16)
    _mixer_step_body(pl.program_id(0) * npair + pl.program_id(1), *mixer_in, *mixer_out, P=P, aw=aw)
    _prompt_attn_kernel(q_ref, k_ref, v_ref, sl_ref, o_ref, *scratch, T=T)


def _prompt_attn_and_mixer_step(q, k, v, slopes_tab, mixer_operands, weights):
    B, T, W = q.shape
    npair = W // LANES
    step_of = lambda b, hp: b * npair + hp
    args, in_specs, out_specs, out_shape = _mixer_step_operands(step_of, *mixer_operands)
    c_in, c_out, c_shape = _cast_operands(step_of, B * npair, weights)
    nb, P, H, _ = mixer_operands[4].shape
    assert nb == B * npair
    spec = pl.BlockSpec((1, T, LANES), lambda b, hp: (b, 0, hp))
    return pl.pallas_call(
        functools.partial(_attn_mixer_kernel, T=T, P=P, aw=H * HEAD_DIM, npair=npair, n_cast=len(weights)),
        grid=(B, npair),
        in_specs=[spec, spec, spec, _full(slopes_tab.shape)] + in_specs + c_in,
        out_specs=[spec] + out_specs + c_out,
        out_shape=[jax.ShapeDtypeStruct((B, T, W), BF16)] + out_shape + c_shape,
        scratch_shapes=_prompt_attn_scratch(T),
        compiler_params=_cparams(2),
        name="prompt_attn_mixer_step",
    )(q, k, v, slopes_tab, *args, *weights)


def _ffn_tail(x1, gate_pre, prev1, prev2, up, cw_ref, cb_ref, w2_ref, g_ref, b_ref, alpha):
    c = cw_ref[0:1, :] * prev2 + cw_ref[1:2, :] * prev1 + cw_ref[2:3, :] * gate_pre + cb_ref[...]
    h = c * _sigmoid(c) * up
    return _layer_norm(alpha * x1 + _dot(h, w2_ref[...]), g_ref[...], b_ref[...])


def _ffn_seq_kernel(att_ref, y_ref, x_ref, wo_ref, g1_ref, b1_ref, w1_ref, cw_ref, cb_ref, w2_ref, g_ref, b_ref,
                    o_ref, tail_ref, carry_ref, *, aw, dff, alpha):
    @pl.when(pl.program_id(1) == 0)
    def _():
        carry_ref[...] = jnp.zeros_like(carry_ref)

    tm = x_ref.shape[1]
    hr = tm // SUB_TILES
    spans = [slice(h * hr, (h + 1) * hr) for h in range(SUB_TILES)]
    n = range(SUB_TILES)
    row = lax.broadcasted_iota(jnp.int32, (hr, dff), 0)
    mixed = [_dot(att_ref[0, sp, :], wo_ref[0:aw, :]) + _dot(y_ref[0, sp, :], wo_ref[aw:, :]) for sp in spans]
    x1 = [_layer_norm(alpha * x_ref[0, spans[i], :] + mixed[i], g1_ref[...], b1_ref[...]) for i in n]
    hu = [_dot(x1[0], w1_ref[...])]
    last2 = carry_ref[0:2, :]
    gated, f = [], []
    for i in n:
        if i + 1 < SUB_TILES:
            hu.append(_dot(x1[i + 1], w1_ref[...]))
        gate_pre = hu[i][:, 0:dff]
        c0, c1 = last2[0:1, :], last2[1:2, :]
        prev1 = jnp.where(row == 0, c1, pltpu.roll(gate_pre, 1, axis=0))
        prev2 = jnp.where(row == 0, c0, jnp.where(row == 1, c1, pltpu.roll(gate_pre, 2, axis=0)))
        last2 = gate_pre[hr - 2:hr, :]
        c = cw_ref[0:1, :] * prev2 + cw_ref[1:2, :] * prev1 + cw_ref[2:3, :] * gate_pre + cb_ref[...]
        gated.append(c * _sigmoid(c) * hu[i][:, dff:])
        if i > 0:
            f.append(_dot(gated[i - 1], w2_ref[...]))
    f.append(_dot(gated[-1], w2_ref[...]))
    carry_ref[0:2, :] = last2
    tail_ref[0] = last2
    for i in n:
        o_ref[0, spans[i], :] = _layer_norm(alpha * x1[i] + f[i], g_ref[...], b_ref[...])


def _ffn_step_kernel(att_ref, y_ref, x_ref, wo_ref, g1_ref, b1_ref, p2_ref, p1_ref, w1_ref, cw_ref, cb_ref, w2_ref,
                     g_ref, b_ref, o_ref, gate_ref, *, dff, alpha):
    nb, aw = att_ref.shape
    h = _dot(att_ref[...], wo_ref[0:aw, :]) + _dot(y_ref[...].T[0:nb, :], wo_ref[aw:, :])
    x1 = _layer_norm(alpha * x_ref[...] + h, g1_ref[...], b1_ref[...])
    hu = _dot(x1, w1_ref[...])
    gate_pre = hu[:, 0:dff]
    gate_ref[...] = gate_pre
    o_ref[...] = _ffn_tail(x1, gate_pre, p1_ref[...], p2_ref[...], hu[:, dff:], cw_ref, cb_ref, w2_ref, g_ref,
                           b_ref, alpha)


def _resident(shape):
    nd = len(shape)
    return pl.BlockSpec(shape, lambda *_: (0,) * nd, pipeline_mode=pl.Buffered(1))


def _ffn_seq(att, y_rw, x, w_out, ln1_g, ln1_b, w1, conv_w, conv_b, w2, ln2_g, ln2_b, tm, alpha):
    B, T, D = x.shape
    aw = att.shape[-1]
    dff = w2.shape[0]
    half = pl.BlockSpec((1, tm, aw), lambda b, t: (b, t, 0))
    rows = pl.BlockSpec((1, tm, D), lambda b, t: (b, t, 0))
    return pl.pallas_call(
        functools.partial(_ffn_seq_kernel, aw=aw, dff=dff, alpha=alpha),
        grid=(B, T // tm),
        in_specs=[half, half, rows, _resident(w_out.shape), _full(ln1_g.shape), _full(ln1_b.shape),
                  _resident(w1.shape), _full(conv_w.shape), _full(conv_b.shape), _resident(w2.shape),
                  _full(ln2_g.shape), _full(ln2_b.shape)],
        out_specs=[rows, pl.BlockSpec((1, 2, dff), lambda b, t: (b, 0, 0))],
        out_shape=[jax.ShapeDtypeStruct((B, T, D), F32), jax.ShapeDtypeStruct((B, 2, dff), F32)],
        scratch_shapes=[pltpu.VMEM((8, dff), F32)],
        compiler_params=_cparams(2),
        name="ffn_seq",
    )(att, y_rw, x, w_out, ln1_g, ln1_b, w1, conv_w, conv_b, w2, ln2_g, ln2_b)


def _ffn_step(att, y_cols, x, w_out, ln1_g, ln1_b, prev2, prev1, w1, conv_w, conv_b, w2, ln2_g, ln2_b, alpha):
    nb, D = x.shape
    dff = w2.shape[0]
    return pl.pallas_call(
        functools.partial(_ffn_step_kernel, dff=dff, alpha=alpha),
        grid=(1,),
        in_specs=[_full(att.shape), _full(y_cols.shape), _full(x.shape), _resident(w_out.shape), _full(ln1_g.shape),
                  _full(ln1_b.shape),
                  _full(prev2.shape), _full(prev1.shape), _resident(w1.shape), _full(conv_w.shape),
                  _full(conv_b.shape), _resident(w2.shape), _full(ln2_g.shape), _full(ln2_b.shape)],
        out_specs=[_full((nb, D)), _full((nb, dff))],
        out_shape=[jax.ShapeDtypeStruct((nb, D), F32), jax.ShapeDtypeStruct((nb, dff), F32)],
        compiler_params=_cparams(1),
        name="ffn_step",
    )(att, y_cols, x, w_out, ln1_g, ln1_b, prev2, prev1, w1, conv_w, conv_b, w2, ln2_g, ln2_b)


def _layer_params(w_in, mu_shift, w0, w_lora_up, a0, a_lora_up, g_lora_up, k_k, k_a, r_k, aw):
    D, ncol = w_in.shape
    n_w, n_a, n_g = w_lora_up.shape[0], a_lora_up.shape[0], g_lora_up.shape[0]
    assert n_w + n_a == LANES and 3 * aw + 3 * aw + n_w + n_a + n_g == ncol
    g_cols = -(-n_g // LANES) * LANES
    pad = g_cols - n_g
    row = lambda a: a.reshape(1, -1).astype(F32)
    lane = jnp.arange(math.gcd(aw, MXU_TILE))
    bd = (lane[:, None] // HEAD_DIM == lane[None, :] // HEAD_DIM).astype(BF16)
    return dict(
        aw=aw,
        w_in=jnp.pad(w_in, ((0, 0), (0, pad))).astype(BF16),
        mu=jnp.pad(row(mu_shift), ((0, 0), (0, pad))),
        w0=row(w0), a0=row(a0), k_k=row(k_k), k_a=row(k_a), r_k=row(r_k),
        wl_pad=jnp.pad(w_lora_up, ((0, n_a), (0, 0))).astype(BF16),
        al_pad=jnp.pad(a_lora_up, ((n_w, 0), (0, 0))).astype(BF16),
        g_pad=jnp.pad(g_lora_up, ((0, pad), (0, 0))).astype(BF16),
        bd=bd,
    )


def kernel(x_prompt, x_sample, cache_k_win, cache_v_win, state_shift, state_wkv, state_conv, w_in, mu_shift, w0, w_lora_up, a0, a_lora_up, g_lora_up, k_k, k_a, r_k, gn_w, gn_b, w_out, ln1_g, ln1_b, w_ffn_in, conv_w, conv_b, w_ffn_out, ln2_g, ln2_b):
    depth = w_in.shape[0]
    alpha = (2.0 * depth) ** 0.25
    B, T, D = x_prompt.shape
    nb = x_sample.shape[0]
    assert x_sample.shape[1] == 1
    n_att = cache_k_win.shape[3]
    n_rw = state_wkv.shape[2]
    aw = n_att * HEAD_DIM
    assert n_rw * HEAD_DIM == aw and T % (Q_TILE * DIL_PATTERNS[-1][1]) == 0
    tm = min(256, T)
    tb = min(256, T)
    heads = jnp.arange(1, n_att + 1, dtype=F32)
    slopes = jnp.exp2(-8.0 * heads / n_att)
    slopes_tab = jnp.broadcast_to(slopes[:, None], (n_att, LANES))
    row = lambda a: a.reshape(1, -1)

    hp, hs = x_prompt, x_sample.reshape(nb, D)
    outs = [[] for _ in range(10)]
    for l in range(depth):
        prm = _layer_params(w_in[l], mu_shift[l], w0[l], w_lora_up[l], a0[l], a_lora_up[l], g_lora_up[l],
                            k_k[l], k_a[l], r_k[l], aw)
        late_weights = (w_out[l], w_ffn_in[l], w_ffn_out[l])
        gnw, gnb = row(gn_w[l]), row(gn_b[l])

        q, k, v, kt, vt, r, lw, k2, vr, al, be, g, bonus = _inproj_seq(hp, prm, min(2 * tm, T))
        x2 = jnp.concatenate([hs, state_shift[l]], axis=0)
        qs, ks, vs, r_s, lw_s, k2_s, al_s, be_s, col = _inproj_step(x2, prm)
        mixer_operands = (qs, ks, vs, col, cache_k_win[l], cache_v_win[l], slopes, state_wkv[l], r_s, lw_s, k2_s,
                          al_s, be_s, gn_w[l], gn_b[l])

        if nb == B * (aw // LANES):
            att, att_s, y_s, wkv_s, w_out_b, w1_b, w2_b = _prompt_attn_and_mixer_step(
                q, k, v, slopes_tab, mixer_operands, late_weights)
        else:
            att = _prompt_attn(q, k, v, slopes_tab)
            att_s, y_s, wkv_s = _mixer_step(*mixer_operands)
            w_out_b, w1_b, w2_b = (w.astype(BF16) for w in late_weights)
        y_rw, wkv_p = _rwkv_chunked(r, lw, k2, vr, al, be, g, bonus, gnw, gnb, tb)

        shift_p = hp[:, -1, :]
        hp, conv_p = _ffn_seq(att, y_rw, hp, w_out_b, row(ln1_g[l]), row(ln1_b[l]), w1_b, conv_w[l],
                              row(conv_b[l]), w2_b, row(ln2_g[l]), row(ln2_b[l]), min(2 * tm, T), alpha)
        shift_s = hs
        hs, gate_s = _ffn_step(att_s.reshape(nb, aw), y_s, hs, w_out_b, row(ln1_g[l]), row(ln1_b[l]),
                               state_conv[l][:, 0, :],
                               state_conv[l][:, 1, :], w1_b, conv_w[l], row(conv_b[l]), w2_b, row(ln2_g[l]),
                               row(ln2_b[l]), alpha)
        conv_s = jnp.stack([state_conv[l][:, 1, :], gate_s], axis=1)

        n_keep = min(DIL_PATTERNS[-1][0], T)
        win = lambda a: jnp.transpose(a.reshape(B, n_att, HEAD_DIM, T), (0, 3, 1, 2))[:, T - n_keep:]
        vals = (win(kt), win(vt), ks.reshape(nb, 1, n_att, HEAD_DIM), vs.reshape(nb, 1, n_att, HEAD_DIM),
                shift_p, shift_s, wkv_p, wkv_s, conv_p, conv_s)
        for lst, val in zip(outs, vals):
            lst.append(val)
    return (hp, hs.reshape(nb, 1, D)) + tuple(jnp.stack(lst) for lst in outs)
```

```python
import functools
import math

import jax
import jax.numpy as jnp
from jax import lax
from jax.experimental import pallas as pl
from jax.experimental.pallas import tpu as pltpu

F32 = jnp.float32
BF16 = jnp.bfloat16

HEAD_DIM = 64
LANES = 128
MXU_TILE = 256
DIL_PATTERNS = ((128, 1), (512, 4), (2048, 16))
WIN_STEPS = 128
Q_TILE = 128
GROUP = 4
SUB_ROWS = 256
CHUNK = 64
LN_EPS = 1e-5
GN_EPS = 64e-5
NEG = -1e30
VMEM_LIMIT = 56 * 1024 * 1024

_DN = {"NN": (((1,), (0,)), ((), ())), "NT": (((1,), (1,)), ((), ())), "TN": (((0,), (0,)), ((), ()))}


def _dot(a, b, dims="NN"):
    return lax.dot_general(a.astype(BF16), b.astype(BF16), _DN[dims], preferred_element_type=F32)


def _dot_const_l(c, x):
    hi = x.astype(BF16)
    r1 = x - hi.astype(F32)
    mid = r1.astype(BF16)
    lo = (r1 - mid.astype(F32)).astype(BF16)
    return _dot(c, hi) + _dot(c, mid) + _dot(c, lo)


def _sigmoid(x):
    return 0.5 * jnp.tanh(0.5 * x) + 0.5


def _layer_norm(x, g, b):
    mu = jnp.mean(x, axis=-1, keepdims=True)
    xc = x - mu
    var = jnp.mean(xc * xc, axis=-1, keepdims=True)
    return xc * lax.rsqrt(var + LN_EPS) * g + b


def _cparams(n_grid):
    return pltpu.CompilerParams(dimension_semantics=("arbitrary",) * n_grid, vmem_limit_bytes=VMEM_LIMIT)


def _full(shape):
    nd = len(shape)
    return pl.BlockSpec(shape, lambda *_: (0,) * nd)


def _rwkv_prep(rw, w0, a0, k_k, k_a, r_k, wl_pad, al_pad, g_pad, bd, aw):
    r = rw[:, 0:aw]
    k = rw[:, aw:2 * aw]
    v = rw[:, 2 * aw:3 * aw]
    lo = rw[:, 3 * aw:3 * aw + 128]
    glo = rw[:, 3 * aw + 128:]
    u = w0 + _dot(jnp.tanh(lo), wl_pad)
    a = _sigmoid(a0 + _dot(lo, al_pad))
    g = _dot(_sigmoid(glo), g_pad)
    kk = k * k_k
    def head_sums(x):
        w = bd.shape[0]
        return jnp.concatenate([_dot(x[:, c:c + w], bd) for c in range(0, aw, w)], axis=1)

    sumsq = head_sums(kk * kk)
    lw = -math.exp(-0.5) * _sigmoid(u)
    k2 = k * (1.0 + (a - 1.0) * k_a)
    bonus = head_sums(r * k2 * r_k) * v
    kkn = kk * lax.rsqrt(jnp.maximum(sumsq, 1e-24))
    return r, lw, k2, v, -kkn, kkn * a, g, bonus


def _inproj_seq_kernel(x_ref, w_ref, mu_ref, w0_ref, a0_ref, kk_ref, ka_ref, rk_ref, wl_ref, al_ref, gp_ref,
                       bd_ref, q_ref, k_ref, v_ref, kt_ref, vt_ref, r_o, lw_o, k2_o, vr_o, al_o, be_o, g_o, bo_o,
                       carry_ref, *, aw):
    @pl.when(pl.program_id(1) == 0)
    def _():
        carry_ref[...] = jnp.zeros_like(carry_ref)

    tm = x_ref.shape[1]
    hr = min(SUB_ROWS, tm)
    n_sub = tm // hr
    spans = [slice(h * hr, (h + 1) * hr) for h in range(n_sub)]
    params = (w0_ref[...], a0_ref[...], kk_ref[...], ka_ref[...], rk_ref[...], wl_ref[...], al_ref[...],
              gp_ref[...], bd_ref[...], aw)
    row = lax.broadcasted_iota(jnp.int32, (hr, w_ref.shape[1] - 3 * aw), 0)

    def finish(p, sp, last):
        q_ref[0, sp, :] = p[:, 0:aw]
        k = p[:, aw:2 * aw]
        v = p[:, 2 * aw:3 * aw]
        k_ref[0, sp, :] = k
        v_ref[0, sp, :] = v
        kt_ref[0, :, sp] = k.T
        vt_ref[0, :, sp] = v.T
        prw = p[:, 3 * aw:]
        prev = jnp.where(row == 0, last, pltpu.roll(prw, 1, axis=0))
        outs = _rwkv_prep(prw + (prev - prw) * mu_ref[...], *params)
        for o_ref, val in zip((r_o, lw_o, k2_o, vr_o, al_o, be_o, g_o, bo_o), outs):
            o_ref[0, sp, :] = val.astype(o_ref.dtype)
        return prw[hr - 1:hr, :]

    last = carry_ref[0:1, :]
    p = _dot(x_ref[0, spans[0], :], w_ref[...])
    for h in range(1, n_sub):
        p_next = _dot(x_ref[0, spans[h], :], w_ref[...])
        last = finish(p, spans[h - 1], last)
        p = p_next
    carry_ref[0:1, :] = finish(p, spans[-1], last)


def _inproj_step_kernel(x_ref, w_ref, mu_ref, w0_ref, a0_ref, kk_ref, ka_ref, rk_ref, wl_ref, al_ref, gp_ref,
                        bd_ref, q_ref, k_ref, v_ref, r_o, lw_o, k2_o, al_o, be_o, col_o, *, aw, nb):
    p = _dot(x_ref[...], w_ref[...])
    q_ref[...] = p[0:nb, 0:aw]
    k_ref[...] = p[0:nb, aw:2 * aw]
    v_ref[...] = p[0:nb, 2 * aw:3 * aw]
    prw = p[0:nb, 3 * aw:]
    prev = p[nb:2 * nb, 3 * aw:]
    rw = prw + (prev - prw) * mu_ref[...]
    r, lw, k2, vr, al, be, g, bonus = _rwkv_prep(rw, w0_ref[...], a0_ref[...], kk_ref[...], ka_ref[...],
                                                 rk_ref[...], wl_ref[...], al_ref[...], gp_ref[...], bd_ref[...], aw)
    for o_ref, val in zip((r_o, lw_o, k2_o, al_o, be_o), (r, lw, k2, al, be)):
        o_ref[...] = val
    pad = jnp.zeros((LANES - nb, aw), F32)
    for i, val in enumerate((vr, g, bonus)):
        col_o[i * aw:(i + 1) * aw, :] = jnp.concatenate([val, pad], axis=0).T


_PREP_PARAMS = ("mu", "w0", "a0", "k_k", "k_a", "r_k", "wl_pad", "al_pad", "g_pad", "bd")


def _inproj_seq(x, prm, tm):
    B, T, D = x.shape
    aw = prm["aw"]
    nrw = prm["w_in"].shape[1] - 3 * aw
    small = [prm[n] for n in _PREP_PARAMS]
    row_spec = pl.BlockSpec((1, tm, aw), lambda b, t: (b, t, 0))
    col_spec = pl.BlockSpec((1, aw, tm), lambda b, t: (b, 0, t))
    row_sds = jax.ShapeDtypeStruct((B, T, aw), F32)
    col_sds = jax.ShapeDtypeStruct((B, aw, T), F32)
    return pl.pallas_call(
        functools.partial(_inproj_seq_kernel, aw=aw),
        grid=(B, T // tm),
        in_specs=[pl.BlockSpec((1, tm, D), lambda b, t: (b, t, 0)), _resident(prm["w_in"].shape)]
        + [_full(s.shape) for s in small],
        out_specs=[row_spec] * 3 + [col_spec] * 2 + [row_spec] * 8,
        out_shape=[row_sds] * 3 + [col_sds] * 2
        + [jax.ShapeDtypeStruct((B, T, aw), F32 if n == "lw" else BF16)
           for n in ("r", "lw", "k2", "v", "al", "be", "g", "bonus")],
        scratch_shapes=[pltpu.VMEM((8, nrw), F32)],
        compiler_params=_cparams(2),
        name="inproj_seq",
    )(x, prm["w_in"], *small)


def _inproj_step(x2, prm):
    nb = x2.shape[0] // 2
    aw = prm["aw"]
    small = [prm[n] for n in _PREP_PARAMS]
    assert nb <= LANES
    out_sds = jax.ShapeDtypeStruct((nb, aw), F32)
    return pl.pallas_call(
        functools.partial(_inproj_step_kernel, aw=aw, nb=nb),
        grid=(1,),
        in_specs=[_full(x2.shape), _full(prm["w_in"].shape)] + [_full(s.shape) for s in small],
        out_specs=[_full((nb, aw))] * 8 + [_full((3 * aw, LANES))],
        out_shape=[out_sds] * 8 + [jax.ShapeDtypeStruct((3 * aw, LANES), F32)],
        compiler_params=_cparams(1),
        name="inproj_step",
    )(x2, prm["w_in"], *small)


def _prompt_attn_kernel(q_ref, k_ref, v_ref, sl_ref, o_ref, acc_s, m_s, l_s, yq, yk, yv, yacc, ym, yl, *, T):
    hp = pl.program_id(1)
    scale = HEAD_DIM ** -0.5
    lane = lax.broadcasted_iota(jnp.int32, (1, LANES), 1)
    head_masks = ((lane < HEAD_DIM).astype(F32), (lane >= HEAD_DIM).astype(F32))
    qi = lax.broadcasted_iota(jnp.int32, (Q_TILE, Q_TILE), 0)
    ki = lax.broadcasted_iota(jnp.int32, (Q_TILE, Q_TILE), 1)
    steps_cur = (qi - ki).astype(F32)
    steps_prev = (qi - ki + Q_TILE).astype(F32)
    ok_cur = ki <= qi
    ok_prev = ki >= qi
    first_lanes = lax.broadcasted_iota(jnp.int32, (Q_TILE, LANES), 1) < HEAD_DIM
    slopes = [sl_ref[pl.ds(2 * hp + j, 1), :] for j in range(2)]

    def tile_group(src, tiles, bias, stats, first_branch):
        rq, rk, rv = src
        bias_c, bias_pc = bias
        acc_s, m_s, l_s = stats
        ids = range(len(tiles))
        q = [rq(rows) * scale for rows, _ in tiles]
        kk = [rk(rows) if prev is None else jnp.concatenate([rk(prev), rk(rows)], axis=0) for rows, prev in tiles]
        vv = [rv(rows) if prev is None else jnp.concatenate([rv(prev), rv(rows)], axis=0) for rows, prev in tiles]
        qq = [jnp.concatenate([q[t] * head_masks[0], q[t] * head_masks[1]], axis=0) for t in ids]
        s = [_dot(qq[t], kk[t], "NT") + (bias_c if tiles[t][1] is None else bias_pc) for t in ids]
        m = [jnp.max(s[t], axis=-1, keepdims=True) for t in ids]
        p = [jnp.exp(s[t] - m[t]) for t in ids]
        l = [jnp.sum(p[t], axis=-1, keepdims=True) for t in ids]
        acc = [_dot(p[t], vv[t]) for t in ids]
        for t, (rows, _) in enumerate(tiles):
            acc_n = jnp.where(first_lanes, acc[t][0:Q_TILE], acc[t][Q_TILE:])
            m_n = jnp.where(first_lanes, m[t][0:Q_TILE], m[t][Q_TILE:])
            l_n = jnp.where(first_lanes, l[t][0:Q_TILE], l[t][Q_TILE:])
            if first_branch:
                acc_s[rows, :] = acc_n
                m_s[rows, :] = m_n
                l_s[rows, :] = l_n
            else:
                m_o = m_s[rows, :]
                m_new = jnp.maximum(m_o, m_n)
                e_o = jnp.exp(m_o - m_new)
                e_n = jnp.exp(m_n - m_new)
                acc_s[rows, :] = acc_s[rows, :] * e_o + acc_n * e_n
                l_s[rows, :] = l_s[rows, :] * e_o + l_n * e_n
                m_s[rows, :] = m_new

    dils = [dil for _, dil in DIL_PATTERNS]
    R = dils[1]
    assert dils == [1, R, R * R] and all(win // dil == WIN_STEPS for win, dil in DIL_PATTERNS)

    def bias_of(dil):
        b_cur = [jnp.where(ok_cur, -(slopes[j] * float(dil)) * steps_cur, NEG) for j in range(2)]
        b_prev = [jnp.where(ok_prev, -(slopes[j] * float(dil)) * steps_prev, NEG) for j in range(2)]
        return (jnp.concatenate(b_cur, axis=0),
                jnp.concatenate([jnp.concatenate([b_prev[j], b_cur[j]], axis=1) for j in range(2)], axis=0))

    def groups(tiles):
        return [tiles[i:i + GROUP] for i in range(0, len(tiles), GROUP)]

    natural = tuple((lambda rows, ref=ref: ref[0, rows, :]) for ref in (q_ref, k_ref, v_ref))
    scratch = tuple((lambda rows, ref=ref: ref[rows, :]) for ref in (yq, yk, yv))
    nat_stats = (acc_s, m_s, l_s)
    y_stats = (yacc, ym, yl)

    nb1 = T // Q_TILE
    assert nb1 % GROUP == 0
    bias1 = bias_of(dils[0])
    tile_group(natural, [(pl.ds(jb * Q_TILE, Q_TILE), pl.ds((jb - 1) * Q_TILE, Q_TILE) if jb else None)
                         for jb in range(GROUP)], bias1, nat_stats, True)

    def later(g, carry):
        start = pl.multiple_of(g * (GROUP * Q_TILE), GROUP * Q_TILE)
        tile_group(natural, [(pl.ds(start + i * Q_TILE, Q_TILE), pl.ds(start + (i - 1) * Q_TILE, Q_TILE))
                             for i in range(GROUP)], bias1, nat_stats, True)
        return carry

    lax.fori_loop(1, nb1 // GROUP, later, 0)

    LR = T // R
    bias_r = bias_of(dils[1])
    bias_rr = bias_of(dils[2])

    def stream(s, carry):
        own = pl.ds(s, LR, stride=R)
        yq[...] = q_ref[0, own, :]
        yk[...] = k_ref[0, own, :]
        yv[...] = v_ref[0, own, :]
        for grp in groups([(pl.ds(jb * Q_TILE, Q_TILE), pl.ds((jb - 1) * Q_TILE, Q_TILE) if jb else None)
                           for jb in range(LR // Q_TILE)]):
            tile_group(scratch, grp, bias_r, y_stats, True)
        for grp in groups([(pl.ds(c + R * Q_TILE * jb, Q_TILE, stride=R),
                            pl.ds(c + R * Q_TILE * (jb - 1), Q_TILE, stride=R) if jb else None)
                           for c in range(R) for jb in range(LR // R // Q_TILE)]):
            tile_group(scratch, grp, bias_rr, y_stats, False)
        m_a, m_b = m_s[own, :], ym[...]
        m_new = jnp.maximum(m_a, m_b)
        e_a = jnp.exp(m_a - m_new)
        e_b = jnp.exp(m_b - m_new)
        acc_s[own, :] = (acc_s[own, :] * e_a + yacc[...] * e_b) / (l_s[own, :] * e_a + yl[...] * e_b)
        return carry

    lax.fori_loop(0, R, stream, 0)
    o_ref[0] = acc_s[...].astype(o_ref.dtype)


def _prompt_attn_scratch(T):
    return [pltpu.VMEM((T, LANES), F32)] * 3 + [pltpu.VMEM((T // DIL_PATTERNS[1][1], LANES), F32)] * 6


def _prompt_attn(q, k, v, slopes_tab):
    B, T, W = q.shape
    spec = pl.BlockSpec((1, T, LANES), lambda b, hp: (b, 0, hp))
    return pl.pallas_call(
        functools.partial(_prompt_attn_kernel, T=T),
        grid=(B, W // LANES),
        in_specs=[spec, spec, spec, _full(slopes_tab.shape)],
        out_specs=spec,
        out_shape=jax.ShapeDtypeStruct((B, T, W), BF16),
        scratch_shapes=_prompt_attn_scratch(T),
        compiler_params=_cparams(2),
        name="prompt_attn",
    )(q, k, v, slopes_tab)


def _rwkv_chunk_kernel(r_ref, lw_ref, k_ref, v_ref, al_ref, be_ref, g_ref, bo_ref, gnw_ref, gnb_ref,
                       y_ref, s_ref, st_scr, *, Tb, npair, nbb):
    C = CHUNK
    n2 = 2 * C
    t = pl.program_id(1)

    @pl.when(t == 0)
    def _():
        st_scr[...] = jnp.zeros_like(st_scr)

    lane = lax.broadcasted_iota(jnp.int32, (1, LANES), 1)
    m0 = (lane < HEAD_DIM).astype(F32)
    m1 = 1.0 - m0
    ri = lax.broadcasted_iota(jnp.int32, (n2, n2), 0)
    ci = lax.broadcasted_iota(jnp.int32, (n2, n2), 1)
    same = (ri < C) == (ci < C)
    strict = same & ((ri & (C - 1)) > (ci & (C - 1)))
    incl = same & ((ri & (C - 1)) >= (ci & (C - 1)))
    tri = (lax.broadcasted_iota(jnp.int32, (C, C), 0) >= lax.broadcasted_iota(jnp.int32, (C, C), 1)).astype(BF16)
    head_rows = ((lax.broadcasted_iota(jnp.int32, (n2, LANES), 0) < C)
                 == (lax.broadcasted_iota(jnp.int32, (n2, LANES), 1) < HEAD_DIM)).astype(F32)

    def stack(x):
        return jnp.concatenate([x * m0, x * m1], axis=0)

    def twice(x):
        return jnp.concatenate([x, x], axis=0)

    def body(c, carry):
        rows = pl.ds(pl.multiple_of(c * C, C), C)
        chains = [(bb, slice(hp * LANES, (hp + 1) * LANES)) for bb in range(nbb) for hp in range(npair)]
        ids = range(len(chains))
        S = [st_scr[i] for i in ids]
        lw_all = jnp.concatenate([lw_ref[bb, rows, :] for bb in range(nbb)], axis=1)
        cl_all = _dot_const_l(tri, lw_all)
        e_pos_all = jnp.exp(cl_all)
        e_neg_all = jnp.exp(-cl_all)
        e_exc_all = jnp.exp(cl_all - lw_all)
        wide = [slice(i * LANES, (i + 1) * LANES) for i in ids]
        g_end = [e_pos_all[C - 1:C, wide[i]] for i in ids]
        rt = [r_ref[bb, rows, cs] * e_pos_all[:, wide[i]] for i, (bb, cs) in enumerate(chains)]
        kt = [k_ref[bb, rows, cs] * e_neg_all[:, wide[i]] for i, (bb, cs) in enumerate(chains)]
        bt = [be_ref[bb, rows, cs] * e_neg_all[:, wide[i]] for i, (bb, cs) in enumerate(chains)]
        at = [al_ref[bb, rows, cs] * e_exc_all[:, wide[i]] for i, (bb, cs) in enumerate(chains)]
        ar_st = [jnp.concatenate([stack(at[i]), stack(rt[i])], axis=0) for i in ids]
        kb2 = [jnp.concatenate([twice(kt[i]), twice(bt[i])], axis=0) for i in ids]
        v_st = [stack(v_ref[bb, rows, cs]) for bb, cs in chains]
        gram = [_dot(ar_st[i], kb2[i], "NT") for i in ids]
        l_ab = [jnp.where(strict, gram[i][0:n2, n2:], 0.0) for i in ids]
        l_kk = [jnp.concatenate([jnp.where(strict, gram[i][0:n2, 0:n2], 0.0),
                                 jnp.where(incl, gram[i][n2:, 0:n2], 0.0)], axis=0) for i in ids]
        l_rb = [jnp.where(incl, gram[i][n2:, n2:], 0.0) for i in ids]
        from_state = [_dot(jnp.concatenate([at[i], rt[i]], axis=0), S[i], "NT") for i in ids]
        from_state = [jnp.concatenate([stack(from_state[i][0:C]), stack(from_state[i][C:])], axis=0) for i in ids]
        from_v = [_dot(l_kk[i], v_st[i]) for i in ids]
        x = l_ab
        u = [from_state[i][0:n2] + from_v[i][0:n2] for i in ids]
        for _ in range(C.bit_length() - 2):
            xu = [_dot(x[i], jnp.concatenate([x[i], u[i]], axis=1)) for i in ids]
            x = [xu[i][:, 0:n2] for i in ids]
            u = [u[i] + xu[i][:, n2:] for i in ids]
        u = [u[i] + _dot(x[i], u[i]) for i in ids]
        y_st = [from_state[i][n2:] + from_v[i][n2:] + _dot(l_rb[i], u[i]) for i in ids]
        for i in ids:
            vu = jnp.concatenate([v_st[i], u[i]], axis=0)
            kb_end = jnp.concatenate([stack(kt[i] * g_end[i]), stack(bt[i] * g_end[i])], axis=0)
            st_scr[i] = S[i] * g_end[i] + _dot(vu, kb_end, "TN")
        for i, (bb, cs) in enumerate(chains):
            mu = jnp.sum(y_st[i], axis=-1, keepdims=True) * (1.0 / HEAD_DIM)
            yc = (y_st[i] - mu) * head_rows
            var = jnp.sum(yc * yc, axis=-1, keepdims=True) * (1.0 / HEAD_DIM)
            yn_st = yc * lax.rsqrt(var + GN_EPS)
            yn_i = (yn_st[0:C, :] + yn_st[C:n2, :]) * gnw_ref[:, cs] + gnb_ref[:, cs]
            y_ref[bb, rows, cs] = ((yn_i + bo_ref[bb, rows, cs]) * g_ref[bb, rows, cs]).astype(y_ref.dtype)
        return carry

    lax.fori_loop(0, Tb // C, body, 0)

    @pl.when(t == pl.num_programs(1) - 1)
    def _():
        for i in range(nbb * npair):
            bb, hp = divmod(i, npair)
            pair = st_scr[i]
            s_ref[bb, 2 * hp] = pair[0:HEAD_DIM, 0:HEAD_DIM]
            s_ref[bb, 2 * hp + 1] = pltpu.roll(pair, HEAD_DIM, axis=1)[HEAD_DIM:, 0:HEAD_DIM]


def _rwkv_chunked(r, lw, k2, v, al, be, g, bonus, gn_w, gn_b, tb):
    B, T, W = r.shape
    npair = W // LANES
    nbb = next(n for n in (4, 2, 1) if B % n == 0)
    spec = pl.BlockSpec((nbb, tb, W), lambda b, t: (b, t, 0))
    st_spec = pl.BlockSpec((nbb, 2 * npair, HEAD_DIM, HEAD_DIM), lambda b, t: (b, 0, 0, 0))
    return pl.pallas_call(
        functools.partial(_rwkv_chunk_kernel, Tb=tb, npair=npair, nbb=nbb),
        grid=(B // nbb, T // tb),
        in_specs=[spec] * 8 + [_full(gn_w.shape), _full(gn_b.shape)],
        out_specs=[spec, st_spec],
        out_shape=[jax.ShapeDtypeStruct((B, T, W), BF16),
                   jax.ShapeDtypeStruct((B, 2 * npair, HEAD_DIM, HEAD_DIM), F32)],
        scratch_shapes=[pltpu.VMEM((nbb * npair, LANES, LANES), F32)],
        compiler_params=_cparams(2),
        name="rwkv_chunk",
    )(r, lw, k2, v, al, be, g, bonus, gn_w, gn_b)


def _rwkv_one_step(S, r, lw, k, al, be, v, g, bonus, gnw, gnb):
    sa = jnp.sum(S * al, axis=-1, keepdims=True)
    s_new = S * jnp.exp(lw) + sa * be + v * k
    y = jnp.sum(s_new * r, axis=-1, keepdims=True)
    mu = jnp.mean(y, axis=1, keepdims=True)
    yc = y - mu
    var = jnp.mean(yc * yc, axis=1, keepdims=True)
    yn = yc * lax.rsqrt(var + GN_EPS) * gnw + gnb
    return (yn + bonus) * g, s_new


def _attn_one_step(q, k_new, v_new, kt, vt, slopes, P):
    aw = q.shape[1]
    H = aw // HEAD_DIM
    scale = HEAD_DIM ** -0.5
    dist = P - lax.broadcasted_iota(jnp.int32, (1, P), 1)
    count = jnp.zeros((1, P), F32)
    for win, dil in DIL_PATTERNS:
        assert dil & (dil - 1) == 0
        count = count + ((dist <= win) & ((dist & (dil - 1)) == 0)).astype(F32)
    own_head = (lax.broadcasted_iota(jnp.int32, (H, aw), 1) // HEAD_DIM
                == lax.broadcasted_iota(jnp.int32, (H, aw), 0)).astype(F32)
    q_heads = q * own_head
    s = _dot(q_heads, kt) * scale - slopes * dist.astype(F32)
    s = jnp.where(count > 0.0, s, NEG)
    s0 = jnp.sum(q_heads * k_new, axis=-1, keepdims=True) * scale
    m = jnp.maximum(jnp.max(s, axis=-1, keepdims=True), s0)
    p = count * jnp.exp(s - m)
    e0 = len(DIL_PATTERNS) * jnp.exp(s0 - m)
    l = jnp.sum(p, axis=-1, keepdims=True) + e0
    acc = _dot(p, vt, "NT") + e0 * v_new
    return jnp.sum(acc * own_head, axis=0, keepdims=True) / jnp.sum(l * own_head, axis=0, keepdims=True)


def _mixer_step_kernel(*refs, P, aw):
    _mixer_step_body(pl.program_id(0), *refs, P=P, aw=aw)


N_MIXER_IN = 15


def _mixer_step_body(b, q_ref, kn_ref, vn_ref, col_ref, kt_ref, vt_ref, sl_ref, s_ref, r_ref, lw_ref, k_ref, al_ref,
                     be_ref, gnw_ref, gnb_ref, att_ref, y_ref, so_ref, *, P, aw):
    H = aw // HEAD_DIM
    own = (lax.broadcasted_iota(jnp.int32, (1, LANES), 1) == b).astype(F32)

    def column(i):
        return jnp.sum(col_ref[i * aw:(i + 1) * aw, :] * own, axis=-1, keepdims=True).reshape(H, HEAD_DIM, 1)

    @pl.when(b == 0)
    def _():
        y_ref[...] = jnp.zeros_like(y_ref)

    att_ref[0] = _attn_one_step(q_ref[0], kn_ref[0], vn_ref[0], kt_ref[0].reshape(aw, P), vt_ref[0].reshape(aw, P),
                                sl_ref[...], P)
    y, s_new = _rwkv_one_step(s_ref[0], r_ref[0], lw_ref[0], k_ref[0], al_ref[0], be_ref[0], column(0), column(1),
                              column(2), gnw_ref[...], gnb_ref[...])
    so_ref[0] = s_new
    y_ref[...] += y.reshape(aw, 1) * own


def _mixer_step_operands(seq_of, q, k_new, v_new, col, k_cache, v_cache, slopes, state, r, lw, k2, al, be, gn_w,
                         gn_b):
    nb, P, H, _ = k_cache.shape
    aw = H * HEAD_DIM
    rowv = lambda a: a.reshape(nb, H, 1, HEAD_DIM)
    vec = lambda a: a.reshape(nb, 1, aw)
    per_seq = lambda shape: pl.BlockSpec((1,) + shape, lambda *g: (seq_of(*g),) + (0,) * len(shape))
    t_spec, s_spec, r_spec = per_seq((H, HEAD_DIM, P)), per_seq((H, HEAD_DIM, HEAD_DIM)), per_seq((H, 1, HEAD_DIM))
    v_spec = per_seq((1, aw))
    p_spec = _full((H, HEAD_DIM, 1))
    args = (vec(q), vec(k_new), vec(v_new), col, jnp.transpose(k_cache, (0, 2, 3, 1)),
            jnp.transpose(v_cache, (0, 2, 3, 1)), slopes.reshape(H, 1), state, rowv(r), rowv(lw), rowv(k2), rowv(al),
            rowv(be), gn_w.reshape(H, HEAD_DIM, 1), gn_b.reshape(H, HEAD_DIM, 1))
    assert len(args) == N_MIXER_IN
    in_specs = ([v_spec] * 3 + [_full(col.shape), t_spec, t_spec, _full((H, 1)), s_spec] + [r_spec] * 5
                + [p_spec] * 2)
    out_specs = [v_spec, _full((aw, LANES)), s_spec]
    out_shape = [jax.ShapeDtypeStruct((nb, 1, aw), F32), jax.ShapeDtypeStruct((aw, LANES), F32),
                 jax.ShapeDtypeStruct(state.shape, F32)]
    return args, in_specs, out_specs, out_shape


def _mixer_step(*operands):
    args, in_specs, out_specs, out_shape = _mixer_step_operands(lambda b: b, *operands)
    nb, P, H, _ = operands[4].shape
    return pl.pallas_call(
        functools.partial(_mixer_step_kernel, P=P, aw=H * HEAD_DIM),
        grid=(nb,),
        in_specs=in_specs,
        out_specs=out_specs,
        out_shape=out_shape,
        compiler_params=_cparams(1),
        name="mixer_step",
    )(*args)


def _cast_operands(step_of, n_steps, weights):
    in_specs, out_specs, out_shape = [], [], []
    for w in weights:
        rows, cols = w.shape
        per = next(d for d in range(1, n_steps + 1)
                   if n_steps % d == 0 and rows % (n_steps // d) == 0 and rows // (n_steps // d) % 16 == 0)
        spec = pl.BlockSpec((rows * per // n_steps, cols), lambda *g, per=per: (step_of(*g) // per, 0))
        in_specs.append(spec)
        out_specs.append(spec)
        out_shape.append(jax.ShapeDtypeStruct(w.shape, BF16))
    return in_specs, out_specs, out_shape


def _attn_mixer_kernel(q_ref, k_ref, v_ref, sl_ref, *refs, T, P, aw, npair, n_cast):
    n_in = N_MIXER_IN + n_cast
    mixer_in, cast_in = refs[:N_MIXER_IN], refs[N_MIXER_IN:n_in]
    o_ref, *mixer_out = refs[n_in:n_in + 4]
    cast_out, scratch = refs[n_in + 4:n_in + 4 + n_cast], refs[n_in + 4 + n_cast:]
    for src, dst in zip(cast_in, cast_out):
        dst[...] = src[...].astype(BF16)
    _mixer_step_body(pl.program_id(0) * npair + pl.program_id(1), *mixer_in, *mixer_out, P=P, aw=aw)
    _prompt_attn_kernel(q_ref, k_ref, v_ref, sl_ref, o_ref, *scratch, T=T)


def _prompt_attn_and_mixer_step(q, k, v, slopes_tab, mixer_operands, weights):
    B, T, W = q.shape
    npair = W // LANES
    step_of = lambda b, hp: b * npair + hp
    args, in_specs, out_specs, out_shape = _mixer_step_operands(step_of, *mixer_operands)
    c_in, c_out, c_shape = _cast_operands(step_of, B * npair, weights)
    nb, P, H, _ = mixer_operands[4].shape
    assert nb == B * npair
    spec = pl.BlockSpec((1, T, LANES), lambda b, hp: (b, 0, hp))
    return pl.pallas_call(
        functools.partial(_attn_mixer_kernel, T=T, P=P, aw=H * HEAD_DIM, npair=npair, n_cast=len(weights)),
        grid=(B, npair),
        in_specs=[spec, spec, spec, _full(slopes_tab.shape)] + in_specs + c_in,
        out_specs=[spec] + out_specs + c_out,
        out_shape=[jax.ShapeDtypeStruct((B, T, W), BF16)] + out_shape + c_shape,
        scratch_shapes=_prompt_attn_scratch(T),
        compiler_params=_cparams(2),
        name="prompt_attn_mixer_step",
    )(q, k, v, slopes_tab, *args, *weights)


def _ffn_tail(x1, gate_pre, prev1, prev2, up, cw_ref, cb_ref, w2_ref, g_ref, b_ref, alpha):
    c = cw_ref[0:1, :] * prev2 + cw_ref[1:2, :] * prev1 + cw_ref[2:3, :] * gate_pre + cb_ref[...]
    h = c * _sigmoid(c) * up
    return _layer_norm(alpha * x1 + _dot(h, w2_ref[...]), g_ref[...], b_ref[...])


def _ffn_seq_kernel(att_ref, y_ref, x_ref, wo_ref, g1_ref, b1_ref, w1_ref, cw_ref, cb_ref, w2_ref, g_ref, b_ref,
                    o_ref, tail_ref, carry_ref, *, aw, dff, alpha):
    @pl.when(pl.program_id(1) == 0)
    def _():
        carry_ref[...] = jnp.zeros_like(carry_ref)

    tm = x_ref.shape[1]
    hr = min(SUB_ROWS, tm)
    n_sub = tm // hr
    spans = [slice(h * hr, (h + 1) * hr) for h in range(n_sub)]
    n = range(n_sub)
    row = lax.broadcasted_iota(jnp.int32, (hr, dff), 0)
    mixed = [_dot(att_ref[0, sp, :], wo_ref[0:aw, :]) + _dot(y_ref[0, sp, :], wo_ref[aw:, :]) for sp in spans]
    x1 = [_layer_norm(alpha * x_ref[0, spans[i], :] + mixed[i], g1_ref[...], b1_ref[...]) for i in n]
    hu = [_dot(x1[0], w1_ref[...])]
    last2 = carry_ref[0:2, :]
    gated, f = [], []
    for i in n:
        if i + 1 < n_sub:
            hu.append(_dot(x1[i + 1], w1_ref[...]))
        gate_pre = hu[i][:, 0:dff]
        c0, c1 = last2[0:1, :], last2[1:2, :]
        prev1 = jnp.where(row == 0, c1, pltpu.roll(gate_pre, 1, axis=0))
        prev2 = jnp.where(row == 0, c0, jnp.where(row == 1, c1, pltpu.roll(gate_pre, 2, axis=0)))
        last2 = gate_pre[hr - 2:hr, :]
        c = cw_ref[0:1, :] * prev2 + cw_ref[1:2, :] * prev1 + cw_ref[2:3, :] * gate_pre + cb_ref[...]
        gated.append(c * _sigmoid(c) * hu[i][:, dff:])
        if i > 0:
            f.append(_dot(gated[i - 1], w2_ref[...]))
    f.append(_dot(gated[-1], w2_ref[...]))
    carry_ref[0:2, :] = last2
    tail_ref[0] = last2
    for i in n:
        o_ref[0, spans[i], :] = _layer_norm(alpha * x1[i] + f[i], g_ref[...], b_ref[...])


def _ffn_step_kernel(att_ref, y_ref, x_ref, wo_ref, g1_ref, b1_ref, p2_ref, p1_ref, w1_ref, cw_ref, cb_ref, w2_ref,
                     g_ref, b_ref, o_ref, gate_ref, *, dff, alpha):
    nb, aw = att_ref.shape
    h = _dot(att_ref[...], wo_ref[0:aw, :]) + _dot(y_ref[...].T[0:nb, :], wo_ref[aw:, :])
    x1 = _layer_norm(alpha * x_ref[...] + h, g1_ref[...], b1_ref[...])
    hu = _dot(x1, w1_ref[...])
    gate_pre = hu[:, 0:dff]
    gate_ref[...] = gate_pre
    o_ref[...] = _ffn_tail(x1, gate_pre, p1_ref[...], p2_ref[...], hu[:, dff:], cw_ref, cb_ref, w2_ref, g_ref,
                           b_ref, alpha)


def _resident(shape):
    nd = len(shape)
    return pl.BlockSpec(shape, lambda *_: (0,) * nd, pipeline_mode=pl.Buffered(1))


def _ffn_seq(att, y_rw, x, w_out, ln1_g, ln1_b, w1, conv_w, conv_b, w2, ln2_g, ln2_b, tm, alpha):
    B, T, D = x.shape
    aw = att.shape[-1]
    dff = w2.shape[0]
    half = pl.BlockSpec((1, tm, aw), lambda b, t: (b, t, 0))
    rows = pl.BlockSpec((1, tm, D), lambda b, t: (b, t, 0))
    return pl.pallas_call(
        functools.partial(_ffn_seq_kernel, aw=aw, dff=dff, alpha=alpha),
        grid=(B, T // tm),
        in_specs=[half, half, rows, _resident(w_out.shape), _full(ln1_g.shape), _full(ln1_b.shape),
                  _resident(w1.shape), _full(conv_w.shape), _full(conv_b.shape), _resident(w2.shape),
                  _full(ln2_g.shape), _full(ln2_b.shape)],
        out_specs=[rows, pl.BlockSpec((1, 2, dff), lambda b, t: (b, 0, 0))],
        out_shape=[jax.ShapeDtypeStruct((B, T, D), F32), jax.ShapeDtypeStruct((B, 2, dff), F32)],
        scratch_shapes=[pltpu.VMEM((8, dff), F32)],
        compiler_params=_cparams(2),
        name="ffn_seq",
    )(att, y_rw, x, w_out, ln1_g, ln1_b, w1, conv_w, conv_b, w2, ln2_g, ln2_b)


def _ffn_step(att, y_cols, x, w_out, ln1_g, ln1_b, prev2, prev1, w1, conv_w, conv_b, w2, ln2_g, ln2_b, alpha):
    nb, D = x.shape
    dff = w2.shape[0]
    return pl.pallas_call(
        functools.partial(_ffn_step_kernel, dff=dff, alpha=alpha),
        grid=(1,),
        in_specs=[_full(att.shape), _full(y_cols.shape), _full(x.shape), _resident(w_out.shape), _full(ln1_g.shape),
                  _full(ln1_b.shape),
                  _full(prev2.shape), _full(prev1.shape), _resident(w1.shape), _full(conv_w.shape),
                  _full(conv_b.shape), _resident(w2.shape), _full(ln2_g.shape), _full(ln2_b.shape)],
        out_specs=[_full((nb, D)), _full((nb, dff))],
        out_shape=[jax.ShapeDtypeStruct((nb, D), F32), jax.ShapeDtypeStruct((nb, dff), F32)],
        compiler_params=_cparams(1),
        name="ffn_step",
    )(att, y_cols, x, w_out, ln1_g, ln1_b, prev2, prev1, w1, conv_w, conv_b, w2, ln2_g, ln2_b)


def _layer_params(w_in, mu_shift, w0, w_lora_up, a0, a_lora_up, g_lora_up, k_k, k_a, r_k, aw):
    D, ncol = w_in.shape
    n_w, n_a, n_g = w_lora_up.shape[0], a_lora_up.shape[0], g_lora_up.shape[0]
    assert n_w + n_a == LANES and 3 * aw + 3 * aw + n_w + n_a + n_g == ncol
    g_cols = -(-n_g // LANES) * LANES
    pad = g_cols - n_g
    row = lambda a: a.reshape(1, -1).astype(F32)
    lane = jnp.arange(math.gcd(aw, MXU_TILE))
    bd = (lane[:, None] // HEAD_DIM == lane[None, :] // HEAD_DIM).astype(BF16)
    return dict(
        aw=aw,
        w_in=jnp.pad(w_in, ((0, 0), (0, pad))).astype(BF16),
        mu=jnp.pad(row(mu_shift), ((0, 0), (0, pad))),
        w0=row(w0), a0=row(a0), k_k=row(k_k), k_a=row(k_a), r_k=row(r_k),
        wl_pad=jnp.pad(w_lora_up, ((0, n_a), (0, 0))).astype(BF16),
        al_pad=jnp.pad(a_lora_up, ((n_w, 0), (0, 0))).astype(BF16),
        g_pad=jnp.pad(g_lora_up, ((0, pad), (0, 0))).astype(BF16),
        bd=bd,
    )


def kernel(x_prompt, x_sample, cache_k_win, cache_v_win, state_shift, state_wkv, state_conv, w_in, mu_shift, w0, w_lora_up, a0, a_lora_up, g_lora_up, k_k, k_a, r_k, gn_w, gn_b, w_out, ln1_g, ln1_b, w_ffn_in, conv_w, conv_b, w_ffn_out, ln2_g, ln2_b):
    depth = w_in.shape[0]
    alpha = (2.0 * depth) ** 0.25
    B, T, D = x_prompt.shape
    nb = x_sample.shape[0]
    assert x_sample.shape[1] == 1
    n_att = cache_k_win.shape[3]
    n_rw = state_wkv.shape[2]
    aw = n_att * HEAD_DIM
    assert n_rw * HEAD_DIM == aw and T % (Q_TILE * DIL_PATTERNS[-1][1]) == 0
    tm = min(256, T)
    tb = min(256, T)
    heads = jnp.arange(1, n_att + 1, dtype=F32)
    slopes = jnp.exp2(-8.0 * heads / n_att)
    slopes_tab = jnp.broadcast_to(slopes[:, None], (n_att, LANES))
    row = lambda a: a.reshape(1, -1)

    hp, hs = x_prompt, x_sample.reshape(nb, D)
    outs = [[] for _ in range(10)]
    for l in range(depth):
        prm = _layer_params(w_in[l], mu_shift[l], w0[l], w_lora_up[l], a0[l], a_lora_up[l], g_lora_up[l],
                            k_k[l], k_a[l], r_k[l], aw)
        late_weights = (w_out[l], w_ffn_in[l], w_ffn_out[l])
        gnw, gnb = row(gn_w[l]), row(gn_b[l])

        q, k, v, kt, vt, r, lw, k2, vr, al, be, g, bonus = _inproj_seq(hp, prm, min(2 * tm, T))
        x2 = jnp.concatenate([hs, state_shift[l]], axis=0)
        qs, ks, vs, r_s, lw_s, k2_s, al_s, be_s, col = _inproj_step(x2, prm)
        mixer_operands = (qs, ks, vs, col, cache_k_win[l], cache_v_win[l], slopes, state_wkv[l], r_s, lw_s, k2_s,
                          al_s, be_s, gn_w[l], gn_b[l])

        if nb == B * (aw // LANES):
            att, att_s, y_s, wkv_s, w_out_b, w1_b, w2_b = _prompt_attn_and_mixer_step(
                q, k, v, slopes_tab, mixer_operands, late_weights)
        else:
            att = _prompt_attn(q, k, v, slopes_tab)
            att_s, y_s, wkv_s = _mixer_step(*mixer_operands)
            w_out_b, w1_b, w2_b = (w.astype(BF16) for w in late_weights)
        y_rw, wkv_p = _rwkv_chunked(r, lw, k2, vr, al, be, g, bonus, gnw, gnb, tb)

        shift_p = hp[:, -1, :]
        hp, conv_p = _ffn_seq(att, y_rw, hp, w_out_b, row(ln1_g[l]), row(ln1_b[l]), w1_b, conv_w[l],
                              row(conv_b[l]), w2_b, row(ln2_g[l]), row(ln2_b[l]), min(4 * tm, T), alpha)
        shift_s = hs
        hs, gate_s = _ffn_step(att_s.reshape(nb, aw), y_s, hs, w_out_b, row(ln1_g[l]), row(ln1_b[l]),
                               state_conv[l][:, 0, :],
                               state_conv[l][:, 1, :], w1_b, conv_w[l], row(conv_b[l]), w2_b, row(ln2_g[l]),
                               row(ln2_b[l]), alpha)
        conv_s = jnp.stack([state_conv[l][:, 1, :], gate_s], axis=1)

        n_keep = min(DIL_PATTERNS[-1][0], T)
        win = lambda a: jnp.transpose(a.reshape(B, n_att, HEAD_DIM, T), (0, 3, 1, 2))[:, T - n_keep:]
        vals = (win(kt), win(vt), ks.reshape(nb, 1, n_att, HEAD_DIM), vs.reshape(nb, 1, n_att, HEAD_DIM),
                shift_p, shift_s, wkv_p, wkv_s, conv_p, conv_s)
        for lst, val in zip(outs, vals):
            lst.append(val)
    return (hp, hs.reshape(nb, 1, D)) + tuple(jnp.stack(lst) for lst in outs)
```

```python
import functools
import math

import jax
import jax.numpy as jnp
from jax import lax
from jax.experimental import pallas as pl
from jax.experimental.pallas import tpu as pltpu

F32 = jnp.float32
BF16 = jnp.bfloat16

HEAD_DIM = 64
LANES = 128
MXU_TILE = 256
DIL_PATTERNS = ((128, 1), (512, 4), (2048, 16))
WIN_STEPS = 128
Q_TILE = 128
GROUP = 4
SUB_ROWS = 256
FFN_SUB_ROWS = 512
CHUNK = 64
LN_EPS = 1e-5
GN_EPS = 64e-5
NEG = -1e30
VMEM_LIMIT = 56 * 1024 * 1024

_DN = {"NN": (((1,), (0,)), ((), ())), "NT": (((1,), (1,)), ((), ())), "TN": (((0,), (0,)), ((), ()))}


def _dot(a, b, dims="NN"):
    return lax.dot_general(a.astype(BF16), b.astype(BF16), _DN[dims], preferred_element_type=F32)


def _dot_const_l(c, x):
    hi = x.astype(BF16)
    r1 = x - hi.astype(F32)
    mid = r1.astype(BF16)
    lo = (r1 - mid.astype(F32)).astype(BF16)
    return _dot(c, hi) + _dot(c, mid) + _dot(c, lo)


def _sigmoid(x):
    return 0.5 * jnp.tanh(0.5 * x) + 0.5


def _layer_norm(x, g, b):
    mu = jnp.mean(x, axis=-1, keepdims=True)
    xc = x - mu
    var = jnp.mean(xc * xc, axis=-1, keepdims=True)
    return xc * lax.rsqrt(var + LN_EPS) * g + b


def _cparams(n_grid):
    return pltpu.CompilerParams(dimension_semantics=("arbitrary",) * n_grid, vmem_limit_bytes=VMEM_LIMIT)


def _full(shape):
    nd = len(shape)
    return pl.BlockSpec(shape, lambda *_: (0,) * nd)


def _rwkv_prep(rw, w0, a0, k_k, k_a, r_k, wl_pad, al_pad, g_pad, bd, aw):
    r = rw[:, 0:aw]
    k = rw[:, aw:2 * aw]
    v = rw[:, 2 * aw:3 * aw]
    lo = rw[:, 3 * aw:3 * aw + 128]
    glo = rw[:, 3 * aw + 128:]
    u = w0 + _dot(jnp.tanh(lo), wl_pad)
    a = _sigmoid(a0 + _dot(lo, al_pad))
    g = _dot(_sigmoid(glo), g_pad)
    kk = k * k_k
    def head_sums(x):
        w = bd.shape[0]
        return jnp.concatenate([_dot(x[:, c:c + w], bd) for c in range(0, aw, w)], axis=1)

    sumsq = head_sums(kk * kk)
    lw = -math.exp(-0.5) * _sigmoid(u)
    k2 = k * (1.0 + (a - 1.0) * k_a)
    bonus = head_sums(r * k2 * r_k) * v
    kkn = kk * lax.rsqrt(jnp.maximum(sumsq, 1e-24))
    return r, lw, k2, v, -kkn, kkn * a, g, bonus


def _inproj_seq_kernel(x_ref, w_ref, mu_ref, w0_ref, a0_ref, kk_ref, ka_ref, rk_ref, wl_ref, al_ref, gp_ref,
                       bd_ref, q_ref, k_ref, v_ref, kt_ref, vt_ref, r_o, lw_o, k2_o, vr_o, al_o, be_o, g_o, bo_o,
                       carry_ref, *, aw):
    @pl.when(pl.program_id(1) == 0)
    def _():
        carry_ref[...] = jnp.zeros_like(carry_ref)

    tm = x_ref.shape[1]
    hr = min(SUB_ROWS, tm)
    n_sub = tm // hr
    spans = [slice(h * hr, (h + 1) * hr) for h in range(n_sub)]
    params = (w0_ref[...], a0_ref[...], kk_ref[...], ka_ref[...], rk_ref[...], wl_ref[...], al_ref[...],
              gp_ref[...], bd_ref[...], aw)
    row = lax.broadcasted_iota(jnp.int32, (hr, w_ref.shape[1] - 3 * aw), 0)

    def finish(p, sp, last):
        q_ref[0, sp, :] = p[:, 0:aw]
        k = p[:, aw:2 * aw]
        v = p[:, 2 * aw:3 * aw]
        k_ref[0, sp, :] = k
        v_ref[0, sp, :] = v
        kt_ref[0, :, sp] = k.T
        vt_ref[0, :, sp] = v.T
        prw = p[:, 3 * aw:]
        prev = jnp.where(row == 0, last, pltpu.roll(prw, 1, axis=0))
        outs = _rwkv_prep(prw + (prev - prw) * mu_ref[...], *params)
        for o_ref, val in zip((r_o, lw_o, k2_o, vr_o, al_o, be_o, g_o, bo_o), outs):
            o_ref[0, sp, :] = val.astype(o_ref.dtype)
        return prw[hr - 1:hr, :]

    last = carry_ref[0:1, :]
    p = _dot(x_ref[0, spans[0], :], w_ref[...])
    for h in range(1, n_sub):
        p_next = _dot(x_ref[0, spans[h], :], w_ref[...])
        last = finish(p, spans[h - 1], last)
        p = p_next
    carry_ref[0:1, :] = finish(p, spans[-1], last)


def _inproj_step_kernel(x_ref, w_ref, mu_ref, w0_ref, a0_ref, kk_ref, ka_ref, rk_ref, wl_ref, al_ref, gp_ref,
                        bd_ref, q_ref, k_ref, v_ref, r_o, lw_o, k2_o, al_o, be_o, col_o, *, aw, nb):
    p = _dot(x_ref[...], w_ref[...])
    q_ref[...] = p[0:nb, 0:aw]
    k_ref[...] = p[0:nb, aw:2 * aw]
    v_ref[...] = p[0:nb, 2 * aw:3 * aw]
    prw = p[0:nb, 3 * aw:]
    prev = p[nb:2 * nb, 3 * aw:]
    rw = prw + (prev - prw) * mu_ref[...]
    r, lw, k2, vr, al, be, g, bonus = _rwkv_prep(rw, w0_ref[...], a0_ref[...], kk_ref[...], ka_ref[...],
                                                 rk_ref[...], wl_ref[...], al_ref[...], gp_ref[...], bd_ref[...], aw)
    for o_ref, val in zip((r_o, lw_o, k2_o, al_o, be_o), (r, lw, k2, al, be)):
        o_ref[...] = val
    pad = jnp.zeros((LANES - nb, aw), F32)
    for i, val in enumerate((vr, g, bonus)):
        col_o[i * aw:(i + 1) * aw, :] = jnp.concatenate([val, pad], axis=0).T


_PREP_PARAMS = ("mu", "w0", "a0", "k_k", "k_a", "r_k", "wl_pad", "al_pad", "g_pad", "bd")


def _inproj_seq(x, prm, tm):
    B, T, D = x.shape
    aw = prm["aw"]
    nrw = prm["w_in"].shape[1] - 3 * aw
    small = [prm[n] for n in _PREP_PARAMS]
    row_spec = pl.BlockSpec((1, tm, aw), lambda b, t: (b, t, 0))
    col_spec = pl.BlockSpec((1, aw, tm), lambda b, t: (b, 0, t))
    row_sds = jax.ShapeDtypeStruct((B, T, aw), F32)
    col_sds = jax.ShapeDtypeStruct((B, aw, T), F32)
    return pl.pallas_call(
        functools.partial(_inproj_seq_kernel, aw=aw),
        grid=(B, T // tm),
        in_specs=[pl.BlockSpec((1, tm, D), lambda b, t: (b, t, 0)), _resident(prm["w_in"].shape)]
        + [_full(s.shape) for s in small],
        out_specs=[row_spec] * 3 + [col_spec] * 2 + [row_spec] * 8,
        out_shape=[row_sds] * 3 + [col_sds] * 2
        + [jax.ShapeDtypeStruct((B, T, aw), F32 if n == "lw" else BF16)
           for n in ("r", "lw", "k2", "v", "al", "be", "g", "bonus")],
        scratch_shapes=[pltpu.VMEM((8, nrw), F32)],
        compiler_params=_cparams(2),
        name="inproj_seq",
    )(x, prm["w_in"], *small)


def _inproj_step(x2, prm):
    nb = x2.shape[0] // 2
    aw = prm["aw"]
    small = [prm[n] for n in _PREP_PARAMS]
    assert nb <= LANES
    out_sds = jax.ShapeDtypeStruct((nb, aw), F32)
    return pl.pallas_call(
        functools.partial(_inproj_step_kernel, aw=aw, nb=nb),
        grid=(1,),
        in_specs=[_full(x2.shape), _full(prm["w_in"].shape)] + [_full(s.shape) for s in small],
        out_specs=[_full((nb, aw))] * 8 + [_full((3 * aw, LANES))],
        out_shape=[out_sds] * 8 + [jax.ShapeDtypeStruct((3 * aw, LANES), F32)],
        compiler_params=_cparams(1),
        name="inproj_step",
    )(x2, prm["w_in"], *small)


def _prompt_attn_kernel(q_ref, k_ref, v_ref, sl_ref, o_ref, acc_s, m_s, l_s, yq, yk, yv, yacc, ym, yl, *, T):
    hp = pl.program_id(1)
    scale = HEAD_DIM ** -0.5
    lane = lax.broadcasted_iota(jnp.int32, (1, LANES), 1)
    head_masks = ((lane < HEAD_DIM).astype(F32), (lane >= HEAD_DIM).astype(F32))
    qi = lax.broadcasted_iota(jnp.int32, (Q_TILE, Q_TILE), 0)
    ki = lax.broadcasted_iota(jnp.int32, (Q_TILE, Q_TILE), 1)
    steps_cur = (qi - ki).astype(F32)
    steps_prev = (qi - ki + Q_TILE).astype(F32)
    ok_cur = ki <= qi
    ok_prev = ki >= qi
    first_lanes = lax.broadcasted_iota(jnp.int32, (Q_TILE, LANES), 1) < HEAD_DIM
    slopes = [sl_ref[pl.ds(2 * hp + j, 1), :] for j in range(2)]

    def tile_group(src, tiles, bias, stats, first_branch):
        rq, rk, rv = src
        bias_c, bias_pc = bias
        acc_s, m_s, l_s = stats
        ids = range(len(tiles))
        q = [rq(rows) * scale for rows, _ in tiles]
        kk = [rk(rows) if prev is None else jnp.concatenate([rk(prev), rk(rows)], axis=0) for rows, prev in tiles]
        vv = [rv(rows) if prev is None else jnp.concatenate([rv(prev), rv(rows)], axis=0) for rows, prev in tiles]
        qq = [jnp.concatenate([q[t] * head_masks[0], q[t] * head_masks[1]], axis=0) for t in ids]
        s = [_dot(qq[t], kk[t], "NT") + (bias_c if tiles[t][1] is None else bias_pc) for t in ids]
        m = [jnp.max(s[t], axis=-1, keepdims=True) for t in ids]
        p = [jnp.exp(s[t] - m[t]) for t in ids]
        l = [jnp.sum(p[t], axis=-1, keepdims=True) for t in ids]
        acc = [_dot(p[t], vv[t]) for t in ids]
        for t, (rows, _) in enumerate(tiles):
            acc_n = jnp.where(first_lanes, acc[t][0:Q_TILE], acc[t][Q_TILE:])
            m_n = jnp.where(first_lanes, m[t][0:Q_TILE], m[t][Q_TILE:])
            l_n = jnp.where(first_lanes, l[t][0:Q_TILE], l[t][Q_TILE:])
            if first_branch:
                acc_s[rows, :] = acc_n
                m_s[rows, :] = m_n
                l_s[rows, :] = l_n
            else:
                m_o = m_s[rows, :]
                m_new = jnp.maximum(m_o, m_n)
                e_o = jnp.exp(m_o - m_new)
                e_n = jnp.exp(m_n - m_new)
                acc_s[rows, :] = acc_s[rows, :] * e_o + acc_n * e_n
                l_s[rows, :] = l_s[rows, :] * e_o + l_n * e_n
                m_s[rows, :] = m_new

    dils = [dil for _, dil in DIL_PATTERNS]
    R = dils[1]
    assert dils == [1, R, R * R] and all(win // dil == WIN_STEPS for win, dil in DIL_PATTERNS)

    def bias_of(dil):
        b_cur = [jnp.where(ok_cur, -(slopes[j] * float(dil)) * steps_cur, NEG) for j in range(2)]
        b_prev = [jnp.where(ok_prev, -(slopes[j] * float(dil)) * steps_prev, NEG) for j in range(2)]
        return (jnp.concatenate(b_cur, axis=0),
                jnp.concatenate([jnp.concatenate([b_prev[j], b_cur[j]], axis=1) for j in range(2)], axis=0))

    def groups(tiles):
        return [tiles[i:i + GROUP] for i in range(0, len(tiles), GROUP)]

    natural = tuple((lambda rows, ref=ref: ref[0, rows, :]) for ref in (q_ref, k_ref, v_ref))
    scratch = tuple((lambda rows, ref=ref: ref[rows, :]) for ref in (yq, yk, yv))
    nat_stats = (acc_s, m_s, l_s)
    y_stats = (yacc, ym, yl)

    nb1 = T // Q_TILE
    assert nb1 % GROUP == 0
    bias1 = bias_of(dils[0])
    tile_group(natural, [(pl.ds(jb * Q_TILE, Q_TILE), pl.ds((jb - 1) * Q_TILE, Q_TILE) if jb else None)
                         for jb in range(GROUP)], bias1, nat_stats, True)

    def later(g, carry):
        start = pl.multiple_of(g * (GROUP * Q_TILE), GROUP * Q_TILE)
        tile_group(natural, [(pl.ds(start + i * Q_TILE, Q_TILE), pl.ds(start + (i - 1) * Q_TILE, Q_TILE))
                             for i in range(GROUP)], bias1, nat_stats, True)
        return carry

    lax.fori_loop(1, nb1 // GROUP, later, 0)

    LR = T // R
    bias_r = bias_of(dils[1])
    bias_rr = bias_of(dils[2])

    def stream(s, carry):
        own = pl.ds(s, LR, stride=R)
        yq[...] = q_ref[0, own, :]
        yk[...] = k_ref[0, own, :]
        yv[...] = v_ref[0, own, :]
        for grp in groups([(pl.ds(jb * Q_TILE, Q_TILE), pl.ds((jb - 1) * Q_TILE, Q_TILE) if jb else None)
                           for jb in range(LR // Q_TILE)]):
            tile_group(scratch, grp, bias_r, y_stats, True)
        for grp in groups([(pl.ds(c + R * Q_TILE * jb, Q_TILE, stride=R),
                            pl.ds(c + R * Q_TILE * (jb - 1), Q_TILE, stride=R) if jb else None)
                           for c in range(R) for jb in range(LR // R // Q_TILE)]):
            tile_group(scratch, grp, bias_rr, y_stats, False)
        m_a, m_b = m_s[own, :], ym[...]
        m_new = jnp.maximum(m_a, m_b)
        e_a = jnp.exp(m_a - m_new)
        e_b = jnp.exp(m_b - m_new)
        acc_s[own, :] = (acc_s[own, :] * e_a + yacc[...] * e_b) / (l_s[own, :] * e_a + yl[...] * e_b)
        return carry

    lax.fori_loop(0, R, stream, 0)
    o_ref[0] = acc_s[...].astype(o_ref.dtype)


def _prompt_attn_scratch(T):
    return [pltpu.VMEM((T, LANES), F32)] * 3 + [pltpu.VMEM((T // DIL_PATTERNS[1][1], LANES), F32)] * 6


def _prompt_attn(q, k, v, slopes_tab):
    B, T, W = q.shape
    spec = pl.BlockSpec((1, T, LANES), lambda b, hp: (b, 0, hp))
    return pl.pallas_call(
        functools.partial(_prompt_attn_kernel, T=T),
        grid=(B, W // LANES),
        in_specs=[spec, spec, spec, _full(slopes_tab.shape)],
        out_specs=spec,
        out_shape=jax.ShapeDtypeStruct((B, T, W), BF16),
        scratch_shapes=_prompt_attn_scratch(T),
        compiler_params=_cparams(2),
        name="prompt_attn",
    )(q, k, v, slopes_tab)


def _rwkv_chunk_kernel(r_ref, lw_ref, k_ref, v_ref, al_ref, be_ref, g_ref, bo_ref, gnw_ref, gnb_ref,
                       y_ref, s_ref, st_scr, *, Tb, npair, nbb):
    C = CHUNK
    n2 = 2 * C
    t = pl.program_id(1)

    @pl.when(t == 0)
    def _():
        st_scr[...] = jnp.zeros_like(st_scr)

    lane = lax.broadcasted_iota(jnp.int32, (1, LANES), 1)
    m0 = (lane < HEAD_DIM).astype(F32)
    m1 = 1.0 - m0
    ri = lax.broadcasted_iota(jnp.int32, (n2, n2), 0)
    ci = lax.broadcasted_iota(jnp.int32, (n2, n2), 1)
    same = (ri < C) == (ci < C)
    strict = same & ((ri & (C - 1)) > (ci & (C - 1)))
    incl = same & ((ri & (C - 1)) >= (ci & (C - 1)))
    tri = (lax.broadcasted_iota(jnp.int32, (C, C), 0) >= lax.broadcasted_iota(jnp.int32, (C, C), 1)).astype(BF16)
    head_rows = ((lax.broadcasted_iota(jnp.int32, (n2, LANES), 0) < C)
                 == (lax.broadcasted_iota(jnp.int32, (n2, LANES), 1) < HEAD_DIM)).astype(F32)

    def stack(x):
        return jnp.concatenate([x * m0, x * m1], axis=0)

    def twice(x):
        return jnp.concatenate([x, x], axis=0)

    def body(c, carry):
        rows = pl.ds(pl.multiple_of(c * C, C), C)
        chains = [(bb, slice(hp * LANES, (hp + 1) * LANES)) for bb in range(nbb) for hp in range(npair)]
        ids = range(len(chains))
        S = [st_scr[i] for i in ids]
        lw_all = jnp.concatenate([lw_ref[bb, rows, :] for bb in range(nbb)], axis=1)
        cl_all = _dot_const_l(tri, lw_all)
        e_pos_all = jnp.exp(cl_all)
        e_neg_all = jnp.exp(-cl_all)
        e_exc_all = jnp.exp(cl_all - lw_all)
        wide = [slice(i * LANES, (i + 1) * LANES) for i in ids]
        g_end = [e_pos_all[C - 1:C, wide[i]] for i in ids]
        rt = [r_ref[bb, rows, cs] * e_pos_all[:, wide[i]] for i, (bb, cs) in enumerate(chains)]
        kt = [k_ref[bb, rows, cs] * e_neg_all[:, wide[i]] for i, (bb, cs) in enumerate(chains)]
        bt = [be_ref[bb, rows, cs] * e_neg_all[:, wide[i]] for i, (bb, cs) in enumerate(chains)]
        at = [al_ref[bb, rows, cs] * e_exc_all[:, wide[i]] for i, (bb, cs) in enumerate(chains)]
        ar_st = [jnp.concatenate([stack(at[i]), stack(rt[i])], axis=0) for i in ids]
        kb2 = [jnp.concatenate([twice(kt[i]), twice(bt[i])], axis=0) for i in ids]
        v_st = [stack(v_ref[bb, rows, cs]) for bb, cs in chains]
        gram = [_dot(ar_st[i], kb2[i], "NT") for i in ids]
        l_ab = [jnp.where(strict, gram[i][0:n2, n2:], 0.0) for i in ids]
        l_kk = [jnp.concatenate([jnp.where(strict, gram[i][0:n2, 0:n2], 0.0),
                                 jnp.where(incl, gram[i][n2:, 0:n2], 0.0)], axis=0) for i in ids]
        l_rb = [jnp.where(incl, gram[i][n2:, n2:], 0.0) for i in ids]
        from_state = [_dot(jnp.concatenate([at[i], rt[i]], axis=0), S[i], "NT") for i in ids]
        from_state = [jnp.concatenate([stack(from_state[i][0:C]), stack(from_state[i][C:])], axis=0) for i in ids]
        from_v = [_dot(l_kk[i], v_st[i]) for i in ids]
        x = l_ab
        u = [from_state[i][0:n2] + from_v[i][0:n2] for i in ids]
        for _ in range(C.bit_length() - 2):
            xu = [_dot(x[i], jnp.concatenate([x[i], u[i]], axis=1)) for i in ids]
            x = [xu[i][:, 0:n2] for i in ids]
            u = [u[i] + xu[i][:, n2:] for i in ids]
        u = [u[i] + _dot(x[i], u[i]) for i in ids]
        y_st = [from_state[i][n2:] + from_v[i][n2:] + _dot(l_rb[i], u[i]) for i in ids]
        for i in ids:
            vu = jnp.concatenate([v_st[i], u[i]], axis=0)
            kb_end = jnp.concatenate([stack(kt[i] * g_end[i]), stack(bt[i] * g_end[i])], axis=0)
            st_scr[i] = S[i] * g_end[i] + _dot(vu, kb_end, "TN")
        for i, (bb, cs) in enumerate(chains):
            mu = jnp.sum(y_st[i], axis=-1, keepdims=True) * (1.0 / HEAD_DIM)
            yc = (y_st[i] - mu) * head_rows
            var = jnp.sum(yc * yc, axis=-1, keepdims=True) * (1.0 / HEAD_DIM)
            yn_st = yc * lax.rsqrt(var + GN_EPS)
            yn_i = (yn_st[0:C, :] + yn_st[C:n2, :]) * gnw_ref[:, cs] + gnb_ref[:, cs]
            y_ref[bb, rows, cs] = ((yn_i + bo_ref[bb, rows, cs]) * g_ref[bb, rows, cs]).astype(y_ref.dtype)
        return carry

    lax.fori_loop(0, Tb // C, body, 0)

    @pl.when(t == pl.num_programs(1) - 1)
    def _():
        for i in range(nbb * npair):
            bb, hp = divmod(i, npair)
            pair = st_scr[i]
            s_ref[bb, 2 * hp] = pair[0:HEAD_DIM, 0:HEAD_DIM]
            s_ref[bb, 2 * hp + 1] = pltpu.roll(pair, HEAD_DIM, axis=1)[HEAD_DIM:, 0:HEAD_DIM]


def _rwkv_chunked(r, lw, k2, v, al, be, g, bonus, gn_w, gn_b, tb):
    B, T, W = r.shape
    npair = W // LANES
    nbb = next(n for n in (4, 2, 1) if B % n == 0)
    spec = pl.BlockSpec((nbb, tb, W), lambda b, t: (b, t, 0))
    st_spec = pl.BlockSpec((nbb, 2 * npair, HEAD_DIM, HEAD_DIM), lambda b, t: (b, 0, 0, 0))
    return pl.pallas_call(
        functools.partial(_rwkv_chunk_kernel, Tb=tb, npair=npair, nbb=nbb),
        grid=(B // nbb, T // tb),
        in_specs=[spec] * 8 + [_full(gn_w.shape), _full(gn_b.shape)],
        out_specs=[spec, st_spec],
        out_shape=[jax.ShapeDtypeStruct((B, T, W), BF16),
                   jax.ShapeDtypeStruct((B, 2 * npair, HEAD_DIM, HEAD_DIM), F32)],
        scratch_shapes=[pltpu.VMEM((nbb * npair, LANES, LANES), F32)],
        compiler_params=_cparams(2),
        name="rwkv_chunk",
    )(r, lw, k2, v, al, be, g, bonus, gn_w, gn_b)


def _rwkv_one_step(S, r, lw, k, al, be, v, g, bonus, gnw, gnb):
    sa = jnp.sum(S * al, axis=-1, keepdims=True)
    s_new = S * jnp.exp(lw) + sa * be + v * k
    y = jnp.sum(s_new * r, axis=-1, keepdims=True)
    mu = jnp.mean(y, axis=1, keepdims=True)
    yc = y - mu
    var = jnp.mean(yc * yc, axis=1, keepdims=True)
    yn = yc * lax.rsqrt(var + GN_EPS) * gnw + gnb
    return (yn + bonus) * g, s_new


def _attn_one_step(q, k_new, v_new, kt, vt, slopes, P):
    aw = q.shape[1]
    H = aw // HEAD_DIM
    scale = HEAD_DIM ** -0.5
    dist = P - lax.broadcasted_iota(jnp.int32, (1, P), 1)
    count = jnp.zeros((1, P), F32)
    for win, dil in DIL_PATTERNS:
        assert dil & (dil - 1) == 0
        count = count + ((dist <= win) & ((dist & (dil - 1)) == 0)).astype(F32)
    own_head = (lax.broadcasted_iota(jnp.int32, (H, aw), 1) // HEAD_DIM
                == lax.broadcasted_iota(jnp.int32, (H, aw), 0)).astype(F32)
    q_heads = q * own_head
    s = _dot(q_heads, kt) * scale - slopes * dist.astype(F32)
    s = jnp.where(count > 0.0, s, NEG)
    s0 = jnp.sum(q_heads * k_new, axis=-1, keepdims=True) * scale
    m = jnp.maximum(jnp.max(s, axis=-1, keepdims=True), s0)
    p = count * jnp.exp(s - m)
    e0 = len(DIL_PATTERNS) * jnp.exp(s0 - m)
    l = jnp.sum(p, axis=-1, keepdims=True) + e0
    acc = _dot(p, vt, "NT") + e0 * v_new
    return jnp.sum(acc * own_head, axis=0, keepdims=True) / jnp.sum(l * own_head, axis=0, keepdims=True)


def _mixer_step_kernel(*refs, P, aw):
    _mixer_step_body(pl.program_id(0), *refs, P=P, aw=aw)


N_MIXER_IN = 15


def _mixer_step_body(b, q_ref, kn_ref, vn_ref, col_ref, kt_ref, vt_ref, sl_ref, s_ref, r_ref, lw_ref, k_ref, al_ref,
                     be_ref, gnw_ref, gnb_ref, att_ref, y_ref, so_ref, *, P, aw):
    H = aw // HEAD_DIM
    own = (lax.broadcasted_iota(jnp.int32, (1, LANES), 1) == b).astype(F32)

    def column(i):
        return jnp.sum(col_ref[i * aw:(i + 1) * aw, :] * own, axis=-1, keepdims=True).reshape(H, HEAD_DIM, 1)

    @pl.when(b == 0)
    def _():
        y_ref[...] = jnp.zeros_like(y_ref)

    att_ref[0] = _attn_one_step(q_ref[0], kn_ref[0], vn_ref[0], kt_ref[0].reshape(aw, P), vt_ref[0].reshape(aw, P),
                                sl_ref[...], P)
    y, s_new = _rwkv_one_step(s_ref[0], r_ref[0], lw_ref[0], k_ref[0], al_ref[0], be_ref[0], column(0), column(1),
                              column(2), gnw_ref[...], gnb_ref[...])
    so_ref[0] = s_new
    y_ref[...] += y.reshape(aw, 1) * own


def _mixer_step_operands(seq_of, q, k_new, v_new, col, k_cache, v_cache, slopes, state, r, lw, k2, al, be, gn_w,
                         gn_b):
    nb, P, H, _ = k_cache.shape
    aw = H * HEAD_DIM
    rowv = lambda a: a.reshape(nb, H, 1, HEAD_DIM)
    vec = lambda a: a.reshape(nb, 1, aw)
    per_seq = lambda shape: pl.BlockSpec((1,) + shape, lambda *g: (seq_of(*g),) + (0,) * len(shape))
    t_spec, s_spec, r_spec = per_seq((H, HEAD_DIM, P)), per_seq((H, HEAD_DIM, HEAD_DIM)), per_seq((H, 1, HEAD_DIM))
    v_spec = per_seq((1, aw))
    p_spec = _full((H, HEAD_DIM, 1))
    args = (vec(q), vec(k_new), vec(v_new), col, jnp.transpose(k_cache, (0, 2, 3, 1)),
            jnp.transpose(v_cache, (0, 2, 3, 1)), slopes.reshape(H, 1), state, rowv(r), rowv(lw), rowv(k2), rowv(al),
            rowv(be), gn_w.reshape(H, HEAD_DIM, 1), gn_b.reshape(H, HEAD_DIM, 1))
    assert len(args) == N_MIXER_IN
    in_specs = ([v_spec] * 3 + [_full(col.shape), t_spec, t_spec, _full((H, 1)), s_spec] + [r_spec] * 5
                + [p_spec] * 2)
    out_specs = [v_spec, _full((aw, LANES)), s_spec]
    out_shape = [jax.ShapeDtypeStruct((nb, 1, aw), F32), jax.ShapeDtypeStruct((aw, LANES), F32),
                 jax.ShapeDtypeStruct(state.shape, F32)]
    return args, in_specs, out_specs, out_shape


def _mixer_step(*operands):
    args, in_specs, out_specs, out_shape = _mixer_step_operands(lambda b: b, *operands)
    nb, P, H, _ = operands[4].shape
    return pl.pallas_call(
        functools.partial(_mixer_step_kernel, P=P, aw=H * HEAD_DIM),
        grid=(nb,),
        in_specs=in_specs,
        out_specs=out_specs,
        out_shape=out_shape,
        compiler_params=_cparams(1),
        name="mixer_step",
    )(*args)


def _cast_operands(step_of, n_steps, weights):
    in_specs, out_specs, out_shape = [], [], []
    for w in weights:
        rows, cols = w.shape
        per = next(d for d in range(1, n_steps + 1)
                   if n_steps % d == 0 and rows % (n_steps // d) == 0 and rows // (n_steps // d) % 16 == 0)
        spec = pl.BlockSpec((rows * per // n_steps, cols), lambda *g, per=per: (step_of(*g) // per, 0))
        in_specs.append(spec)
        out_specs.append(spec)
        out_shape.append(jax.ShapeDtypeStruct(w.shape, BF16))
    return in_specs, out_specs, out_shape


def _attn_mixer_kernel(q_ref, k_ref, v_ref, sl_ref, *refs, T, P, aw, npair, n_cast):
    n_in = N_MIXER_IN + n_cast
    mixer_in, cast_in = refs[:N_MIXER_IN], refs[N_MIXER_IN:n_in]
    o_ref, *mixer_out = refs[n_in:n_in + 4]
    cast_out, scratch = refs[n_in + 4:n_in + 4 + n_cast], refs[n_in + 4 + n_cast:]
    for src, dst in zip(cast_in, cast_out):
        dst[...] = src[...].astype(BF16)
    _mixer_step_body(pl.program_id(0) * npair + pl.program_id(1), *mixer_in, *mixer_out, P=P, aw=aw)
    _prompt_attn_kernel(q_ref, k_ref, v_ref, sl_ref, o_ref, *scratch, T=T)


def _prompt_attn_and_mixer_step(q, k, v, slopes_tab, mixer_operands, weights):
    B, T, W = q.shape
    npair = W // LANES
    step_of = lambda b, hp: b * npair + hp
    args, in_specs, out_specs, out_shape = _mixer_step_operands(step_of, *mixer_operands)
    c_in, c_out, c_shape = _cast_operands(step_of, B * npair, weights)
    nb, P, H, _ = mixer_operands[4].shape
    assert nb == B * npair
    spec = pl.BlockSpec((1, T, LANES), lambda b, hp: (b, 0, hp))
    return pl.pallas_call(
        functools.partial(_attn_mixer_kernel, T=T, P=P, aw=H * HEAD_DIM, npair=npair, n_cast=len(weights)),
        grid=(B, npair),
        in_specs=[spec, spec, spec, _full(slopes_tab.shape)] + in_specs + c_in,
        out_specs=[spec] + out_specs + c_out,
        out_shape=[jax.ShapeDtypeStruct((B, T, W), BF16)] + out_shape + c_shape,
        scratch_shapes=_prompt_attn_scratch(T),
        compiler_params=_cparams(2),
        name="prompt_attn_mixer_step",
    )(q, k, v, slopes_tab, *args, *weights)


def _ffn_tail(x1, gate_pre, prev1, prev2, up, cw_ref, cb_ref, w2_ref, g_ref, b_ref, alpha):
    c = cw_ref[0:1, :] * prev2 + cw_ref[1:2, :] * prev1 + cw_ref[2:3, :] * gate_pre + cb_ref[...]
    h = c * _sigmoid(c) * up
    return _layer_norm(alpha * x1 + _dot(h, w2_ref[...]), g_ref[...], b_ref[...])


def _ffn_seq_kernel(att_ref, y_ref, x_ref, wo_ref, g1_ref, b1_ref, w1_ref, cw_ref, cb_ref, w2_ref, g_ref, b_ref,
                    o_ref, tail_ref, carry_ref, *, aw, dff, alpha):
    @pl.when(pl.program_id(1) == 0)
    def _():
        carry_ref[...] = jnp.zeros_like(carry_ref)

    tm = x_ref.shape[1]
    hr = min(FFN_SUB_ROWS, tm)
    n_sub = tm // hr
    spans = [slice(h * hr, (h + 1) * hr) for h in range(n_sub)]
    n = range(n_sub)
    row = lax.broadcasted_iota(jnp.int32, (hr, dff), 0)
    mixed = [_dot(att_ref[0, sp, :], wo_ref[0:aw, :]) + _dot(y_ref[0, sp, :], wo_ref[aw:, :]) for sp in spans]
    x1 = [_layer_norm(alpha * x_ref[0, spans[i], :] + mixed[i], g1_ref[...], b1_ref[...]) for i in n]
    hu = [_dot(x1[0], w1_ref[...])]
    last2 = carry_ref[0:2, :]
    gated, f = [], []
    for i in n:
        if i + 1 < n_sub:
            hu.append(_dot(x1[i + 1], w1_ref[...]))
        gate_pre = hu[i][:, 0:dff]
        c0, c1 = last2[0:1, :], last2[1:2, :]
        prev1 = jnp.where(row == 0, c1, pltpu.roll(gate_pre, 1, axis=0))
        prev2 = jnp.where(row == 0, c0, jnp.where(row == 1, c1, pltpu.roll(gate_pre, 2, axis=0)))
        last2 = gate_pre[hr - 2:hr, :]
        c = cw_ref[0:1, :] * prev2 + cw_ref[1:2, :] * prev1 + cw_ref[2:3, :] * gate_pre + cb_ref[...]
        gated.append(c * _sigmoid(c) * hu[i][:, dff:])
        if i > 0:
            f.append(_dot(gated[i - 1], w2_ref[...]))
    f.append(_dot(gated[-1], w2_ref[...]))
    carry_ref[0:2, :] = last2
    tail_ref[0] = last2
    for i in n:
        o_ref[0, spans[i], :] = _layer_norm(alpha * x1[i] + f[i], g_ref[...], b_ref[...])


def _ffn_step_kernel(att_ref, y_ref, x_ref, wo_ref, g1_ref, b1_ref, p2_ref, p1_ref, w1_ref, cw_ref, cb_ref, w2_ref,
                     g_ref, b_ref, o_ref, gate_ref, *, dff, alpha):
    nb, aw = att_ref.shape
    h = _dot(att_ref[...], wo_ref[0:aw, :]) + _dot(y_ref[...].T[0:nb, :], wo_ref[aw:, :])
    x1 = _layer_norm(alpha * x_ref[...] + h, g1_ref[...], b1_ref[...])
    hu = _dot(x1, w1_ref[...])
    gate_pre = hu[:, 0:dff]
    gate_ref[...] = gate_pre
    o_ref[...] = _ffn_tail(x1, gate_pre, p1_ref[...], p2_ref[...], hu[:, dff:], cw_ref, cb_ref, w2_ref, g_ref,
                           b_ref, alpha)


def _resident(shape):
    nd = len(shape)
    return pl.BlockSpec(shape, lambda *_: (0,) * nd, pipeline_mode=pl.Buffered(1))


def _ffn_seq(att, y_rw, x, w_out, ln1_g, ln1_b, w1, conv_w, conv_b, w2, ln2_g, ln2_b, tm, alpha):
    B, T, D = x.shape
    aw = att.shape[-1]
    dff = w2.shape[0]
    half = pl.BlockSpec((1, tm, aw), lambda b, t: (b, t, 0))
    rows = pl.BlockSpec((1, tm, D), lambda b, t: (b, t, 0))
    return pl.pallas_call(
        functools.partial(_ffn_seq_kernel, aw=aw, dff=dff, alpha=alpha),
        grid=(B, T // tm),
        in_specs=[half, half, rows, _resident(w_out.shape), _full(ln1_g.shape), _full(ln1_b.shape),
                  _resident(w1.shape), _full(conv_w.shape), _full(conv_b.shape), _resident(w2.shape),
                  _full(ln2_g.shape), _full(ln2_b.shape)],
        out_specs=[rows, pl.BlockSpec((1, 2, dff), lambda b, t: (b, 0, 0))],
        out_shape=[jax.ShapeDtypeStruct((B, T, D), F32), jax.ShapeDtypeStruct((B, 2, dff), F32)],
        scratch_shapes=[pltpu.VMEM((8, dff), F32)],
        compiler_params=_cparams(2),
        name="ffn_seq",
    )(att, y_rw, x, w_out, ln1_g, ln1_b, w1, conv_w, conv_b, w2, ln2_g, ln2_b)


def _ffn_step(att, y_cols, x, w_out, ln1_g, ln1_b, prev2, prev1, w1, conv_w, conv_b, w2, ln2_g, ln2_b, alpha):
    nb, D = x.shape
    dff = w2.shape[0]
    return pl.pallas_call(
        functools.partial(_ffn_step_kernel, dff=dff, alpha=alpha),
        grid=(1,),
        in_specs=[_full(att.shape), _full(y_cols.shape), _full(x.shape), _resident(w_out.shape), _full(ln1_g.shape),
                  _full(ln1_b.shape),
                  _full(prev2.shape), _full(prev1.shape), _resident(w1.shape), _full(conv_w.shape),
                  _full(conv_b.shape), _resident(w2.shape), _full(ln2_g.shape), _full(ln2_b.shape)],
        out_specs=[_full((nb, D)), _full((nb, dff))],
        out_shape=[jax.ShapeDtypeStruct((nb, D), F32), jax.ShapeDtypeStruct((nb, dff), F32)],
        compiler_params=_cparams(1),
        name="ffn_step",
    )(att, y_cols, x, w_out, ln1_g, ln1_b, prev2, prev1, w1, conv_w, conv_b, w2, ln2_g, ln2_b)


def _layer_params(w_in, mu_shift, w0, w_lora_up, a0, a_lora_up, g_lora_up, k_k, k_a, r_k, aw):
    D, ncol = w_in.shape
    n_w, n_a, n_g = w_lora_up.shape[0], a_lora_up.shape[0], g_lora_up.shape[0]
    assert n_w + n_a == LANES and 3 * aw + 3 * aw + n_w + n_a + n_g == ncol
    g_cols = -(-n_g // LANES) * LANES
    pad = g_cols - n_g
    row = lambda a: a.reshape(1, -1).astype(F32)
    lane = jnp.arange(math.gcd(aw, MXU_TILE))
    bd = (lane[:, None] // HEAD_DIM == lane[None, :] // HEAD_DIM).astype(BF16)
    return dict(
        aw=aw,
        w_in=jnp.pad(w_in, ((0, 0), (0, pad))).astype(BF16),
        mu=jnp.pad(row(mu_shift), ((0, 0), (0, pad))),
        w0=row(w0), a0=row(a0), k_k=row(k_k), k_a=row(k_a), r_k=row(r_k),
        wl_pad=jnp.pad(w_lora_up, ((0, n_a), (0, 0))).astype(BF16),
        al_pad=jnp.pad(a_lora_up, ((n_w, 0), (0, 0))).astype(BF16),
        g_pad=jnp.pad(g_lora_up, ((0, pad), (0, 0))).astype(BF16),
        bd=bd,
    )


def kernel(x_prompt, x_sample, cache_k_win, cache_v_win, state_shift, state_wkv, state_conv, w_in, mu_shift, w0, w_lora_up, a0, a_lora_up, g_lora_up, k_k, k_a, r_k, gn_w, gn_b, w_out, ln1_g, ln1_b, w_ffn_in, conv_w, conv_b, w_ffn_out, ln2_g, ln2_b):
    depth = w_in.shape[0]
    alpha = (2.0 * depth) ** 0.25
    B, T, D = x_prompt.shape
    nb = x_sample.shape[0]
    assert x_sample.shape[1] == 1
    n_att = cache_k_win.shape[3]
    n_rw = state_wkv.shape[2]
    aw = n_att * HEAD_DIM
    assert n_rw * HEAD_DIM == aw and T % (Q_TILE * DIL_PATTERNS[-1][1]) == 0
    tm = min(256, T)
    tb = min(256, T)
    heads = jnp.arange(1, n_att + 1, dtype=F32)
    slopes = jnp.exp2(-8.0 * heads / n_att)
    slopes_tab = jnp.broadcast_to(slopes[:, None], (n_att, LANES))
    row = lambda a: a.reshape(1, -1)

    hp, hs = x_prompt, x_sample.reshape(nb, D)
    outs = [[] for _ in range(10)]
    for l in range(depth):
        prm = _layer_params(w_in[l], mu_shift[l], w0[l], w_lora_up[l], a0[l], a_lora_up[l], g_lora_up[l],
                            k_k[l], k_a[l], r_k[l], aw)
        late_weights = (w_out[l], w_ffn_in[l], w_ffn_out[l])
        gnw, gnb = row(gn_w[l]), row(gn_b[l])

        q, k, v, kt, vt, r, lw, k2, vr, al, be, g, bonus = _inproj_seq(hp, prm, min(2 * tm, T))
        x2 = jnp.concatenate([hs, state_shift[l]], axis=0)
        qs, ks, vs, r_s, lw_s, k2_s, al_s, be_s, col = _inproj_step(x2, prm)
        mixer_operands = (qs, ks, vs, col, cache_k_win[l], cache_v_win[l], slopes, state_wkv[l], r_s, lw_s, k2_s,
                          al_s, be_s, gn_w[l], gn_b[l])

        if nb == B * (aw // LANES):
            att, att_s, y_s, wkv_s, w_out_b, w1_b, w2_b = _prompt_attn_and_mixer_step(
                q, k, v, slopes_tab, mixer_operands, late_weights)
        else:
            att = _prompt_attn(q, k, v, slopes_tab)
            att_s, y_s, wkv_s = _mixer_step(*mixer_operands)
            w_out_b, w1_b, w2_b = (w.astype(BF16) for w in late_weights)
        y_rw, wkv_p = _rwkv_chunked(r, lw, k2, vr, al, be, g, bonus, gnw, gnb, tb)

        shift_p = hp[:, -1, :]
        hp, conv_p = _ffn_seq(att, y_rw, hp, w_out_b, row(ln1_g[l]), row(ln1_b[l]), w1_b, conv_w[l],
                              row(conv_b[l]), w2_b, row(ln2_g[l]), row(ln2_b[l]), min(2 * tm, T), alpha)
        shift_s = hs
        hs, gate_s = _ffn_step(att_s.reshape(nb, aw), y_s, hs, w_out_b, row(ln1_g[l]), row(ln1_b[l]),
                               state_conv[l][:, 0, :],
                               state_conv[l][:, 1, :], w1_b, conv_w[l], row(conv_b[l]), w2_b, row(ln2_g[l]),
                               row(ln2_b[l]), alpha)
        conv_s = jnp.stack([state_conv[l][:, 1, :], gate_s], axis=1)

        n_keep = min(DIL_PATTERNS[-1][0], T)
        win = lambda a: jnp.transpose(a.reshape(B, n_att, HEAD_DIM, T), (0, 3, 1, 2))[:, T - n_keep:]
        vals = (win(kt), win(vt), ks.reshape(nb, 1, n_att, HEAD_DIM), vs.reshape(nb, 1, n_att, HEAD_DIM),
                shift_p, shift_s, wkv_p, wkv_s, conv_p, conv_s)
        for lst, val in zip(outs, vals):
            lst.append(val)
    return (hp, hs.reshape(nb, 1, D)) + tuple(jnp.stack(lst) for lst in outs)
```

```python
import functools
import math

import jax
import jax.numpy as jnp
from jax import lax
from jax.experimental import pallas as pl
from jax.experimental.pallas import tpu as pltpu

F32 = jnp.float32
BF16 = jnp.bfloat16

HEAD_DIM = 64
LANES = 128
MXU_TILE = 256
DIL_PATTERNS = ((128, 1), (512, 4), (2048, 16))
WIN_STEPS = 128
Q_TILE = 128
GROUP = 4
SUB_TILES = 2
CHUNK = 64
LN_EPS = 1e-5
GN_EPS = 64e-5
NEG = -1e30
VMEM_LIMIT = 56 * 1024 * 1024

_DN = {"NN": (((1,), (0,)), ((), ())), "NT": (((1,), (1,)), ((), ())), "TN": (((0,), (0,)), ((), ()))}


def _dot(a, b, dims="NN"):
    return lax.dot_general(a.astype(BF16), b.astype(BF16), _DN[dims], preferred_element_type=F32)


def _dot_const_l(c, x):
    hi = x.astype(BF16)
    r1 = x - hi.astype(F32)
    mid = r1.astype(BF16)
    lo = (r1 - mid.astype(F32)).astype(BF16)
    return _dot(c, hi) + _dot(c, mid) + _dot(c, lo)


def _sigmoid(x):
    return 0.5 * jnp.tanh(0.5 * x) + 0.5


def _layer_norm(x, g, b):
    mu = jnp.mean(x, axis=-1, keepdims=True)
    xc = x - mu
    var = jnp.mean(xc * xc, axis=-1, keepdims=True)
    return xc * lax.rsqrt(var + LN_EPS) * g + b


def _cparams(n_grid):
    return pltpu.CompilerParams(dimension_semantics=("arbitrary",) * n_grid, vmem_limit_bytes=VMEM_LIMIT)


def _full(shape):
    nd = len(shape)
    return pl.BlockSpec(shape, lambda *_: (0,) * nd)


def _rwkv_prep(rw, w0, a0, k_k, k_a, r_k, wl_pad, al_pad, g_pad, bd, aw):
    r = rw[:, 0:aw]
    k = rw[:, aw:2 * aw]
    v = rw[:, 2 * aw:3 * aw]
    lo = rw[:, 3 * aw:3 * aw + 128]
    glo = rw[:, 3 * aw + 128:]
    u = w0 + _dot(jnp.tanh(lo), wl_pad)
    a = _sigmoid(a0 + _dot(lo, al_pad))
    g = _dot(_sigmoid(glo), g_pad)
    kk = k * k_k
    def head_sums(x):
        w = bd.shape[0]
        return jnp.concatenate([_dot(x[:, c:c + w], bd) for c in range(0, aw, w)], axis=1)

    sumsq = head_sums(kk * kk)
    lw = -math.exp(-0.5) * _sigmoid(u)
    k2 = k * (1.0 + (a - 1.0) * k_a)
    bonus = head_sums(r * k2 * r_k) * v
    kkn = kk * lax.rsqrt(jnp.maximum(sumsq, 1e-24))
    return r, lw, k2, v, -kkn, kkn * a, g, bonus


def _inproj_seq_kernel(x_ref, w_ref, mu_ref, w0_ref, a0_ref, kk_ref, ka_ref, rk_ref, wl_ref, al_ref, gp_ref,
                       bd_ref, q_ref, k_ref, v_ref, kt_ref, vt_ref, r_o, lw_o, k2_o, vr_o, al_o, be_o, g_o, bo_o,
                       carry_ref, *, aw):
    @pl.when(pl.program_id(1) == 0)
    def _():
        carry_ref[...] = jnp.zeros_like(carry_ref)

    tm = x_ref.shape[1]
    hr = tm // SUB_TILES
    spans = [slice(h * hr, (h + 1) * hr) for h in range(SUB_TILES)]
    params = (w0_ref[...], a0_ref[...], kk_ref[...], ka_ref[...], rk_ref[...], wl_ref[...], al_ref[...],
              gp_ref[...], bd_ref[...], aw)
    row = lax.broadcasted_iota(jnp.int32, (hr, w_ref.shape[1] - 3 * aw), 0)

    def finish(p, sp, last):
        q_ref[0, sp, :] = p[:, 0:aw]
        k = p[:, aw:2 * aw]
        v = p[:, 2 * aw:3 * aw]
        k_ref[0, sp, :] = k
        v_ref[0, sp, :] = v
        kt_ref[0, :, sp] = k.T
        vt_ref[0, :, sp] = v.T
        prw = p[:, 3 * aw:]
        prev = jnp.where(row == 0, last, pltpu.roll(prw, 1, axis=0))
        outs = _rwkv_prep(prw + (prev - prw) * mu_ref[...], *params)
        for o_ref, val in zip((r_o, lw_o, k2_o, vr_o, al_o, be_o, g_o, bo_o), outs):
            o_ref[0, sp, :] = val.astype(o_ref.dtype)
        return prw[hr - 1:hr, :]

    last = carry_ref[0:1, :]
    p = _dot(x_ref[0, spans[0], :], w_ref[...])
    for h in range(1, SUB_TILES):
        p_next = _dot(x_ref[0, spans[h], :], w_ref[...])
        last = finish(p, spans[h - 1], last)
        p = p_next
    carry_ref[0:1, :] = finish(p, spans[-1], last)


def _inproj_step_kernel(x_ref, w_ref, mu_ref, w0_ref, a0_ref, kk_ref, ka_ref, rk_ref, wl_ref, al_ref, gp_ref,
                        bd_ref, q_ref, k_ref, v_ref, r_o, lw_o, k2_o, al_o, be_o, col_o, *, aw, nb):
    p = _dot(x_ref[...], w_ref[...])
    q_ref[...] = p[0:nb, 0:aw]
    k_ref[...] = p[0:nb, aw:2 * aw]
    v_ref[...] = p[0:nb, 2 * aw:3 * aw]
    prw = p[0:nb, 3 * aw:]
    prev = p[nb:2 * nb, 3 * aw:]
    rw = prw + (prev - prw) * mu_ref[...]
    r, lw, k2, vr, al, be, g, bonus = _rwkv_prep(rw, w0_ref[...], a0_ref[...], kk_ref[...], ka_ref[...],
                                                 rk_ref[...], wl_ref[...], al_ref[...], gp_ref[...], bd_ref[...], aw)
    for o_ref, val in zip((r_o, lw_o, k2_o, al_o, be_o), (r, lw, k2, al, be)):
        o_ref[...] = val
    pad = jnp.zeros((LANES - nb, aw), F32)
    for i, val in enumerate((vr, g, bonus)):
        col_o[i * aw:(i + 1) * aw, :] = jnp.concatenate([val, pad], axis=0).T


_PREP_PARAMS = ("mu", "w0", "a0", "k_k", "k_a", "r_k", "wl_pad", "al_pad", "g_pad", "bd")


def _inproj_seq(x, prm, tm):
    B, T, D = x.shape
    aw = prm["aw"]
    nrw = prm["w_in"].shape[1] - 3 * aw
    small = [prm[n] for n in _PREP_PARAMS]
    row_spec = pl.BlockSpec((1, tm, aw), lambda b, t: (b, t, 0))
    col_spec = pl.BlockSpec((1, aw, tm), lambda b, t: (b, 0, t))
    row_sds = jax.ShapeDtypeStruct((B, T, aw), F32)
    col_sds = jax.ShapeDtypeStruct((B, aw, T), F32)
    return pl.pallas_call(
        functools.partial(_inproj_seq_kernel, aw=aw),
        grid=(B, T // tm),
        in_specs=[pl.BlockSpec((1, tm, D), lambda b, t: (b, t, 0)), _resident(prm["w_in"].shape)]
        + [_full(s.shape) for s in small],
        out_specs=[row_spec] * 3 + [col_spec] * 2 + [row_spec] * 8,
        out_shape=[row_sds] * 3 + [col_sds] * 2
        + [jax.ShapeDtypeStruct((B, T, aw), F32 if n == "lw" else BF16)
           for n in ("r", "lw", "k2", "v", "al", "be", "g", "bonus")],
        scratch_shapes=[pltpu.VMEM((8, nrw), F32)],
        compiler_params=_cparams(2),
        name="inproj_seq",
    )(x, prm["w_in"], *small)


def _inproj_step(x2, prm):
    nb = x2.shape[0] // 2
    aw = prm["aw"]
    small = [prm[n] for n in _PREP_PARAMS]
    assert nb <= LANES
    out_sds = jax.ShapeDtypeStruct((nb, aw), F32)
    return pl.pallas_call(
        functools.partial(_inproj_step_kernel, aw=aw, nb=nb),
        grid=(1,),
        in_specs=[_full(x2.shape), _full(prm["w_in"].shape)] + [_full(s.shape) for s in small],
        out_specs=[_full((nb, aw))] * 8 + [_full((3 * aw, LANES))],
        out_shape=[out_sds] * 8 + [jax.ShapeDtypeStruct((3 * aw, LANES), F32)],
        compiler_params=_cparams(1),
        name="inproj_step",
    )(x2, prm["w_in"], *small)


def _prompt_attn_kernel(q_ref, k_ref, v_ref, sl_ref, o_ref, acc_s, m_s, l_s, yq, yk, yv, yacc, ym, yl, *, T):
    hp = pl.program_id(1)
    scale = HEAD_DIM ** -0.5
    lane = lax.broadcasted_iota(jnp.int32, (1, LANES), 1)
    head_masks = ((lane < HEAD_DIM).astype(F32), (lane >= HEAD_DIM).astype(F32))
    qi = lax.broadcasted_iota(jnp.int32, (Q_TILE, Q_TILE), 0)
    ki = lax.broadcasted_iota(jnp.int32, (Q_TILE, Q_TILE), 1)
    steps_cur = (qi - ki).astype(F32)
    steps_prev = (qi - ki + Q_TILE).astype(F32)
    ok_cur = ki <= qi
    ok_prev = ki >= qi
    first_lanes = lax.broadcasted_iota(jnp.int32, (Q_TILE, LANES), 1) < HEAD_DIM
    slopes = [sl_ref[pl.ds(2 * hp + j, 1), :] for j in range(2)]

    def tile_group(src, tiles, bias, stats, first_branch):
        rq, rk, rv = src
        bias_c, bias_pc = bias
        acc_s, m_s, l_s = stats
        ids = range(len(tiles))
        q = [rq(rows) * scale for rows, _ in tiles]
        kk = [rk(rows) if prev is None else jnp.concatenate([rk(prev), rk(rows)], axis=0) for rows, prev in tiles]
        vv = [rv(rows) if prev is None else jnp.concatenate([rv(prev), rv(rows)], axis=0) for rows, prev in tiles]
        qq = [jnp.concatenate([q[t] * head_masks[0], q[t] * head_masks[1]], axis=0) for t in ids]
        s = [_dot(qq[t], kk[t], "NT") + (bias_c if tiles[t][1] is None else bias_pc) for t in ids]
        m = [jnp.max(s[t], axis=-1, keepdims=True) for t in ids]
        p = [jnp.exp(s[t] - m[t]) for t in ids]
        l = [jnp.sum(p[t], axis=-1, keepdims=True) for t in ids]
        acc = [_dot(p[t], vv[t]) for t in ids]
        for t, (rows, _) in enumerate(tiles):
            acc_n = jnp.where(first_lanes, acc[t][0:Q_TILE], acc[t][Q_TILE:])
            m_n = jnp.where(first_lanes, m[t][0:Q_TILE], m[t][Q_TILE:])
            l_n = jnp.where(first_lanes, l[t][0:Q_TILE], l[t][Q_TILE:])
            if first_branch:
                acc_s[rows, :] = acc_n
                m_s[rows, :] = m_n
                l_s[rows, :] = l_n
            else:
                m_o = m_s[rows, :]
                m_new = jnp.maximum(m_o, m_n)
                e_o = jnp.exp(m_o - m_new)
                e_n = jnp.exp(m_n - m_new)
                acc_s[rows, :] = acc_s[rows, :] * e_o + acc_n * e_n
                l_s[rows, :] = l_s[rows, :] * e_o + l_n * e_n
                m_s[rows, :] = m_new

    dils = [dil for _, dil in DIL_PATTERNS]
    R = dils[1]
    assert dils == [1, R, R * R] and all(win // dil == WIN_STEPS for win, dil in DIL_PATTERNS)

    def bias_of(dil):
        b_cur = [jnp.where(ok_cur, -(slopes[j] * float(dil)) * steps_cur, NEG) for j in range(2)]
        b_prev = [jnp.where(ok_prev, -(slopes[j] * float(dil)) * steps_prev, NEG) for j in range(2)]
        return (jnp.concatenate(b_cur, axis=0),
                jnp.concatenate([jnp.concatenate([b_prev[j], b_cur[j]], axis=1) for j in range(2)], axis=0))

    def groups(tiles):
        return [tiles[i:i + GROUP] for i in range(0, len(tiles), GROUP)]

    natural = tuple((lambda rows, ref=ref: ref[0, rows, :]) for ref in (q_ref, k_ref, v_ref))
    scratch = tuple((lambda rows, ref=ref: ref[rows, :]) for ref in (yq, yk, yv))
    nat_stats = (acc_s, m_s, l_s)
    y_stats = (yacc, ym, yl)

    nb1 = T // Q_TILE
    assert nb1 % GROUP == 0
    bias1 = bias_of(dils[0])
    tile_group(natural, [(pl.ds(jb * Q_TILE, Q_TILE), pl.ds((jb - 1) * Q_TILE, Q_TILE) if jb else None)
                         for jb in range(GROUP)], bias1, nat_stats, True)

    def later(g, carry):
        start = pl.multiple_of(g * (GROUP * Q_TILE), GROUP * Q_TILE)
        tile_group(natural, [(pl.ds(start + i * Q_TILE, Q_TILE), pl.ds(start + (i - 1) * Q_TILE, Q_TILE))
                             for i in range(GROUP)], bias1, nat_stats, True)
        return carry

    lax.fori_loop(1, nb1 // GROUP, later, 0)

    LR = T // R
    bias_r = bias_of(dils[1])
    bias_rr = bias_of(dils[2])

    def stream(s, carry):
        own = pl.ds(s, LR, stride=R)
        yq[...] = q_ref[0, own, :]
        yk[...] = k_ref[0, own, :]
        yv[...] = v_ref[0, own, :]
        for grp in groups([(pl.ds(jb * Q_TILE, Q_TILE), pl.ds((jb - 1) * Q_TILE, Q_TILE) if jb else None)
                           for jb in range(LR // Q_TILE)]):
            tile_group(scratch, grp, bias_r, y_stats, True)
        for grp in groups([(pl.ds(c + R * Q_TILE * jb, Q_TILE, stride=R),
                            pl.ds(c + R * Q_TILE * (jb - 1), Q_TILE, stride=R) if jb else None)
                           for c in range(R) for jb in range(LR // R // Q_TILE)]):
            tile_group(scratch, grp, bias_rr, y_stats, False)
        m_a, m_b = m_s[own, :], ym[...]
        m_new = jnp.maximum(m_a, m_b)
        e_a = jnp.exp(m_a - m_new)
        e_b = jnp.exp(m_b - m_new)
        acc_s[own, :] = (acc_s[own, :] * e_a + yacc[...] * e_b) / (l_s[own, :] * e_a + yl[...] * e_b)
        return carry

    lax.fori_loop(0, R, stream, 0)
    o_ref[0] = acc_s[...].astype(o_ref.dtype)


def _prompt_attn_scratch(T):
    return [pltpu.VMEM((T, LANES), F32)] * 3 + [pltpu.VMEM((T // DIL_PATTERNS[1][1], LANES), F32)] * 6


def _prompt_attn(q, k, v, slopes_tab):
    B, T, W = q.shape
    spec = pl.BlockSpec((1, T, LANES), lambda b, hp: (b, 0, hp))
    return pl.pallas_call(
        functools.partial(_prompt_attn_kernel, T=T),
        grid=(B, W // LANES),
        in_specs=[spec, spec, spec, _full(slopes_tab.shape)],
        out_specs=spec,
        out_shape=jax.ShapeDtypeStruct((B, T, W), BF16),
        scratch_shapes=_prompt_attn_scratch(T),
        compiler_params=_cparams(2),
        name="prompt_attn",
    )(q, k, v, slopes_tab)


def _rwkv_chunk_kernel(r_ref, lw_ref, k_ref, v_ref, al_ref, be_ref, g_ref, bo_ref, gnw_ref, gnb_ref,
                       y_ref, s_ref, st_scr, *, Tb, npair, nbb):
    C = CHUNK
    n2 = 2 * C
    t = pl.program_id(1)

    @pl.when(t == 0)
    def _():
        st_scr[...] = jnp.zeros_like(st_scr)

    lane = lax.broadcasted_iota(jnp.int32, (1, LANES), 1)
    m0 = (lane < HEAD_DIM).astype(F32)
    m1 = 1.0 - m0
    ri = lax.broadcasted_iota(jnp.int32, (n2, n2), 0)
    ci = lax.broadcasted_iota(jnp.int32, (n2, n2), 1)
    same = (ri < C) == (ci < C)
    strict = same & ((ri & (C - 1)) > (ci & (C - 1)))
    incl = same & ((ri & (C - 1)) >= (ci & (C - 1)))
    tri = (lax.broadcasted_iota(jnp.int32, (C, C), 0) >= lax.broadcasted_iota(jnp.int32, (C, C), 1)).astype(BF16)
    head_rows = ((lax.broadcasted_iota(jnp.int32, (n2, LANES), 0) < C)
                 == (lax.broadcasted_iota(jnp.int32, (n2, LANES), 1) < HEAD_DIM)).astype(F32)

    low_lanes = lax.broadcasted_iota(jnp.int32, (2 * n2, LANES), 1) < C

    def stack(x):
        return jnp.concatenate([x * m0, x * m1], axis=0)

    def body(c, carry):
        rows = pl.ds(pl.multiple_of(c * C, C), C)
        chains = [(bb, slice(hp * LANES, (hp + 1) * LANES)) for bb in range(nbb) for hp in range(npair)]
        ids = range(len(chains))
        S = [st_scr[i] for i in ids]
        lw_all = jnp.concatenate([lw_ref[bb, rows, :] for bb in range(nbb)], axis=1)
        cl_all = _dot_const_l(tri, lw_all)
        e_pos_all = jnp.exp(cl_all)
        e_neg_all = jnp.exp(-cl_all)
        e_exc_all = jnp.exp(cl_all - lw_all)
        wide = [slice(i * LANES, (i + 1) * LANES) for i in ids]
        g_end = [e_pos_all[C - 1:C, wide[i]] for i in ids]
        rt = [r_ref[bb, rows, cs] * e_pos_all[:, wide[i]] for i, (bb, cs) in enumerate(chains)]
        kt = [k_ref[bb, rows, cs] * e_neg_all[:, wide[i]] for i, (bb, cs) in enumerate(chains)]
        bt = [be_ref[bb, rows, cs] * e_neg_all[:, wide[i]] for i, (bb, cs) in enumerate(chains)]
        at = [al_ref[bb, rows, cs] * e_exc_all[:, wide[i]] for i, (bb, cs) in enumerate(chains)]
        ar_st = [jnp.concatenate([stack(at[i]), stack(rt[i])], axis=0) for i in ids]
        kb = [jnp.concatenate([kt[i], bt[i]], axis=0) for i in ids]
        v_st = [stack(v_ref[bb, rows, cs]) for bb, cs in chains]
        g = [_dot(ar_st[i], kb[i], "NT") for i in ids]
        g_sw = [pltpu.roll(g[i], C, axis=1) for i in ids]
        g_k = [jnp.where(low_lanes, g[i], g_sw[i]) for i in ids]
        g_b = [jnp.where(low_lanes, g_sw[i], g[i]) for i in ids]
        l_ab = [jnp.where(strict, g_b[i][0:n2], 0.0) for i in ids]
        l_kk = [jnp.concatenate([jnp.where(strict, g_k[i][0:n2], 0.0),
                                 jnp.where(incl, g_k[i][n2:], 0.0)], axis=0) for i in ids]
        l_rb = [jnp.where(incl, g_b[i][n2:], 0.0) for i in ids]
        from_state = [_dot(jnp.concatenate([at[i], rt[i]], axis=0), S[i], "NT") for i in ids]
        from_state = [jnp.concatenate([stack(from_state[i][0:C]), stack(from_state[i][C:])], axis=0) for i in ids]
        from_v = [_dot(l_kk[i], v_st[i]) for i in ids]
        x = l_ab
        u = [from_state[i][0:n2] + from_v[i][0:n2] for i in ids]
        for _ in range(C.bit_length() - 2):
            xu = [_dot(x[i], jnp.concatenate([x[i], u[i]], axis=1)) for i in ids]
            x = [xu[i][:, 0:n2] for i in ids]
            u = [u[i] + xu[i][:, n2:] for i in ids]
        u = [u[i] + _dot(x[i], u[i]) for i in ids]
        y_st = [from_state[i][n2:] + from_v[i][n2:] + _dot(l_rb[i], u[i]) for i in ids]
        for i in ids:
            vu = jnp.concatenate([v_st[i], u[i]], axis=0)
            kb_end = jnp.concatenate([stack(kt[i] * g_end[i]), stack(bt[i] * g_end[i])], axis=0)
            st_scr[i] = S[i] * g_end[i] + _dot(vu, kb_end, "TN")
        for i, (bb, cs) in enumerate(chains):
            mu = jnp.sum(y_st[i], axis=-1, keepdims=True) * (1.0 / HEAD_DIM)
            yc = (y_st[i] - mu) * head_rows
            var = jnp.sum(yc * yc, axis=-1, keepdims=True) * (1.0 / HEAD_DIM)
            yn_st = yc * lax.rsqrt(var + GN_EPS)
            yn_i = (yn_st[0:C, :] + yn_st[C:n2, :]) * gnw_ref[:, cs] + gnb_ref[:, cs]
            y_ref[bb, rows, cs] = ((yn_i + bo_ref[bb, rows, cs]) * g_ref[bb, rows, cs]).astype(y_ref.dtype)
        return carry

    lax.fori_loop(0, Tb // C, body, 0)

    @pl.when(t == pl.num_programs(1) - 1)
    def _():
        for i in range(nbb * npair):
            bb, hp = divmod(i, npair)
            pair = st_scr[i]
            s_ref[bb, 2 * hp] = pair[0:HEAD_DIM, 0:HEAD_DIM]
            s_ref[bb, 2 * hp + 1] = pltpu.roll(pair, HEAD_DIM, axis=1)[HEAD_DIM:, 0:HEAD_DIM]


def _rwkv_chunked(r, lw, k2, v, al, be, g, bonus, gn_w, gn_b, tb):
    B, T, W = r.shape
    npair = W // LANES
    nbb = next(n for n in (4, 2, 1) if B % n == 0)
    spec = pl.BlockSpec((nbb, tb, W), lambda b, t: (b, t, 0))
    st_spec = pl.BlockSpec((nbb, 2 * npair, HEAD_DIM, HEAD_DIM), lambda b, t: (b, 0, 0, 0))
    return pl.pallas_call(
        functools.partial(_rwkv_chunk_kernel, Tb=tb, npair=npair, nbb=nbb),
        grid=(B // nbb, T // tb),
        in_specs=[spec] * 8 + [_full(gn_w.shape), _full(gn_b.shape)],
        out_specs=[spec, st_spec],
        out_shape=[jax.ShapeDtypeStruct((B, T, W), BF16),
                   jax.ShapeDtypeStruct((B, 2 * npair, HEAD_DIM, HEAD_DIM), F32)],
        scratch_shapes=[pltpu.VMEM((nbb * npair, LANES, LANES), F32)],
        compiler_params=_cparams(2),
        name="rwkv_chunk",
    )(r, lw, k2, v, al, be, g, bonus, gn_w, gn_b)


def _rwkv_one_step(S, r, lw, k, al, be, v, g, bonus, gnw, gnb):
    sa = jnp.sum(S * al, axis=-1, keepdims=True)
    s_new = S * jnp.exp(lw) + sa * be + v * k
    y = jnp.sum(s_new * r, axis=-1, keepdims=True)
    mu = jnp.mean(y, axis=1, keepdims=True)
    yc = y - mu
    var = jnp.mean(yc * yc, axis=1, keepdims=True)
    yn = yc * lax.rsqrt(var + GN_EPS) * gnw + gnb
    return (yn + bonus) * g, s_new


def _attn_one_step(q, k_new, v_new, kt, vt, slopes, P):
    aw = q.shape[1]
    H = aw // HEAD_DIM
    scale = HEAD_DIM ** -0.5
    dist = P - lax.broadcasted_iota(jnp.int32, (1, P), 1)
    count = jnp.zeros((1, P), F32)
    for win, dil in DIL_PATTERNS:
        assert dil & (dil - 1) == 0
        count = count + ((dist <= win) & ((dist & (dil - 1)) == 0)).astype(F32)
    own_head = (lax.broadcasted_iota(jnp.int32, (H, aw), 1) // HEAD_DIM
                == lax.broadcasted_iota(jnp.int32, (H, aw), 0)).astype(F32)
    q_heads = q * own_head
    s = _dot(q_heads, kt) * scale - slopes * dist.astype(F32)
    s = jnp.where(count > 0.0, s, NEG)
    s0 = jnp.sum(q_heads * k_new, axis=-1, keepdims=True) * scale
    m = jnp.maximum(jnp.max(s, axis=-1, keepdims=True), s0)
    p = count * jnp.exp(s - m)
    e0 = len(DIL_PATTERNS) * jnp.exp(s0 - m)
    l = jnp.sum(p, axis=-1, keepdims=True) + e0
    acc = _dot(p, vt, "NT") + e0 * v_new
    return jnp.sum(acc * own_head, axis=0, keepdims=True) / jnp.sum(l * own_head, axis=0, keepdims=True)


def _mixer_step_kernel(*refs, P, aw):
    _mixer_step_body(pl.program_id(0), *refs, P=P, aw=aw)


N_MIXER_IN = 15


def _mixer_step_body(b, q_ref, kn_ref, vn_ref, col_ref, kt_ref, vt_ref, sl_ref, s_ref, r_ref, lw_ref, k_ref, al_ref,
                     be_ref, gnw_ref, gnb_ref, att_ref, y_ref, so_ref, *, P, aw):
    H = aw // HEAD_DIM
    own = (lax.broadcasted_iota(jnp.int32, (1, LANES), 1) == b).astype(F32)

    def column(i):
        return jnp.sum(col_ref[i * aw:(i + 1) * aw, :] * own, axis=-1, keepdims=True).reshape(H, HEAD_DIM, 1)

    @pl.when(b == 0)
    def _():
        y_ref[...] = jnp.zeros_like(y_ref)

    att_ref[0] = _attn_one_step(q_ref[0], kn_ref[0], vn_ref[0], kt_ref[0].reshape(aw, P), vt_ref[0].reshape(aw, P),
                                sl_ref[...], P)
    y, s_new = _rwkv_one_step(s_ref[0], r_ref[0], lw_ref[0], k_ref[0], al_ref[0], be_ref[0], column(0), column(1),
                              column(2), gnw_ref[...], gnb_ref[...])
    so_ref[0] = s_new
    y_ref[...] += y.reshape(aw, 1) * own


def _mixer_step_operands(seq_of, q, k_new, v_new, col, k_cache, v_cache, slopes, state, r, lw, k2, al, be, gn_w,
                         gn_b):
    nb, P, H, _ = k_cache.shape
    aw = H * HEAD_DIM
    rowv = lambda a: a.reshape(nb, H, 1, HEAD_DIM)
    vec = lambda a: a.reshape(nb, 1, aw)
    per_seq = lambda shape: pl.BlockSpec((1,) + shape, lambda *g: (seq_of(*g),) + (0,) * len(shape))
    t_spec, s_spec, r_spec = per_seq((H, HEAD_DIM, P)), per_seq((H, HEAD_DIM, HEAD_DIM)), per_seq((H, 1, HEAD_DIM))
    v_spec = per_seq((1, aw))
    p_spec = _full((H, HEAD_DIM, 1))
    args = (vec(q), vec(k_new), vec(v_new), col, jnp.transpose(k_cache, (0, 2, 3, 1)),
            jnp.transpose(v_cache, (0, 2, 3, 1)), slopes.reshape(H, 1), state, rowv(r), rowv(lw), rowv(k2), rowv(al),
            rowv(be), gn_w.reshape(H, HEAD_DIM, 1), gn_b.reshape(H, HEAD_DIM, 1))
    assert len(args) == N_MIXER_IN
    in_specs = ([v_spec] * 3 + [_full(col.shape), t_spec, t_spec, _full((H, 1)), s_spec] + [r_spec] * 5
                + [p_spec] * 2)
    out_specs = [v_spec, _full((aw, LANES)), s_spec]
    out_shape = [jax.ShapeDtypeStruct((nb, 1, aw), F32), jax.ShapeDtypeStruct((aw, LANES), F32),
                 jax.ShapeDtypeStruct(state.shape, F32)]
    return args, in_specs, out_specs, out_shape


def _mixer_step(*operands):
    args, in_specs, out_specs, out_shape = _mixer_step_operands(lambda b: b, *operands)
    nb, P, H, _ = operands[4].shape
    return pl.pallas_call(
        functools.partial(_mixer_step_kernel, P=P, aw=H * HEAD_DIM),
        grid=(nb,),
        in_specs=in_specs,
        out_specs=out_specs,
        out_shape=out_shape,
        compiler_params=_cparams(1),
        name="mixer_step",
    )(*args)


def _cast_operands(step_of, n_steps, weights):
    in_specs, out_specs, out_shape = [], [], []
    for w in weights:
        rows, cols = w.shape
        per = next(d for d in range(1, n_steps + 1)
                   if n_steps % d == 0 and rows % (n_steps // d) == 0 and rows // (n_steps // d) % 16 == 0)
        spec = pl.BlockSpec((rows * per // n_steps, cols), lambda *g, per=per: (step_of(*g) // per, 0))
        in_specs.append(spec)
        out_specs.append(spec)
        out_shape.append(jax.ShapeDtypeStruct(w.shape, BF16))
    return in_specs, out_specs, out_shape


def _attn_mixer_kernel(q_ref, k_ref, v_ref, sl_ref, *refs, T, P, aw, npair, n_cast):
    n_in = N_MIXER_IN + n_cast
    mixer_in, cast_in = refs[:N_MIXER_IN], refs[N_MIXER_IN:n_in]
    o_ref, *mixer_out = refs[n_in:n_in + 4]
    cast_out, scratch = refs[n_in + 4:n_in + 4 + n_cast], refs[n_in + 4 + n_cast:]
    for src, dst in zip(cast_in, cast_out):
        dst[...] = src[...].astype(BF16)
    _mixer_step_body(pl.program_id(0) * npair + pl.program_id(1), *mixer_in, *mixer_out, P=P, aw=aw)
    _prompt_attn_kernel(q_ref, k_ref, v_ref, sl_ref, o_ref, *scratch, T=T)


def _prompt_attn_and_mixer_step(q, k, v, slopes_tab, mixer_operands, weights):
    B, T, W = q.shape
    npair = W // LANES
    step_of = lambda b, hp: b * npair + hp
    args, in_specs, out_specs, out_shape = _mixer_step_operands(step_of, *mixer_operands)
    c_in, c_out, c_shape = _cast_operands(step_of, B * npair, weights)
    nb, P, H, _ = mixer_operands[4].shape
    assert nb == B * npair
    spec = pl.BlockSpec((1, T, LANES), lambda b, hp: (b, 0, hp))
    return pl.pallas_call(
        functools.partial(_attn_mixer_kernel, T=T, P=P, aw=H * HEAD_DIM, npair=npair, n_cast=len(weights)),
        grid=(B, npair),
        in_specs=[spec, spec, spec, _full(slopes_tab.shape)] + in_specs + c_in,
        out_specs=[spec] + out_specs + c_out,
        out_shape=[jax.ShapeDtypeStruct((B, T, W), BF16)] + out_shape + c_shape,
        scratch_shapes=_prompt_attn_scratch(T),
        compiler_params=_cparams(2),
        name="prompt_attn_mixer_step",
    )(q, k, v, slopes_tab, *args, *weights)


def _ffn_tail(x1, gate_pre, prev1, prev2, up, cw_ref, cb_ref, w2_ref, g_ref, b_ref, alpha):
    c = cw_ref[0:1, :] * prev2 + cw_ref[1:2, :] * prev1 + cw_ref[2:3, :] * gate_pre + cb_ref[...]
    h = c * _sigmoid(c) * up
    return _layer_norm(alpha * x1 + _dot(h, w2_ref[...]), g_ref[...], b_ref[...])


def _ffn_seq_kernel(att_ref, y_ref, x_ref, wo_ref, g1_ref, b1_ref, w1_ref, cw_ref, cb_ref, w2_ref, g_ref, b_ref,
                    o_ref, tail_ref, carry_ref, *, aw, dff, alpha):
    @pl.when(pl.program_id(1) == 0)
    def _():
        carry_ref[...] = jnp.zeros_like(carry_ref)

    tm = x_ref.shape[1]
    hr = tm // SUB_TILES
    spans = [slice(h * hr, (h + 1) * hr) for h in range(SUB_TILES)]
    n = range(SUB_TILES)
    row = lax.broadcasted_iota(jnp.int32, (hr, dff), 0)
    mixed = [_dot(att_ref[0, sp, :], wo_ref[0:aw, :]) + _dot(y_ref[0, sp, :], wo_ref[aw:, :]) for sp in spans]
    x1 = [_layer_norm(alpha * x_ref[0, spans[i], :] + mixed[i], g1_ref[...], b1_ref[...]) for i in n]
    hu = [_dot(x1[0], w1_ref[...])]
    last2 = carry_ref[0:2, :]
    gated, f = [], []
    for i in n:
        if i + 1 < SUB_TILES:
            hu.append(_dot(x1[i + 1], w1_ref[...]))
        gate_pre = hu[i][:, 0:dff]
        c0, c1 = last2[0:1, :], last2[1:2, :]
        prev1 = jnp.where(row == 0, c1, pltpu.roll(gate_pre, 1, axis=0))
        prev2 = jnp.where(row == 0, c0, jnp.where(row == 1, c1, pltpu.roll(gate_pre, 2, axis=0)))
        last2 = gate_pre[hr - 2:hr, :]
        c = cw_ref[0:1, :] * prev2 + cw_ref[1:2, :] * prev1 + cw_ref[2:3, :] * gate_pre + cb_ref[...]
        gated.append(c * _sigmoid(c) * hu[i][:, dff:])
        if i > 0:
            f.append(_dot(gated[i - 1], w2_ref[...]))
    f.append(_dot(gated[-1], w2_ref[...]))
    carry_ref[0:2, :] = last2
    tail_ref[0] = last2
    for i in n:
        o_ref[0, spans[i], :] = _layer_norm(alpha * x1[i] + f[i], g_ref[...], b_ref[...])


def _ffn_step_kernel(att_ref, y_ref, x_ref, wo_ref, g1_ref, b1_ref, p2_ref, p1_ref, w1_ref, cw_ref, cb_ref, w2_ref,
                     g_ref, b_ref, o_ref, gate_ref, *, dff, alpha):
    nb, aw = att_ref.shape
    h = _dot(att_ref[...], wo_ref[0:aw, :]) + _dot(y_ref[...].T[0:nb, :], wo_ref[aw:, :])
    x1 = _layer_norm(alpha * x_ref[...] + h, g1_ref[...], b1_ref[...])
    hu = _dot(x1, w1_ref[...])
    gate_pre = hu[:, 0:dff]
    gate_ref[...] = gate_pre
    o_ref[...] = _ffn_tail(x1, gate_pre, p1_ref[...], p2_ref[...], hu[:, dff:], cw_ref, cb_ref, w2_ref, g_ref,
                           b_ref, alpha)


def _resident(shape):
    nd = len(shape)
    return pl.BlockSpec(shape, lambda *_: (0,) * nd, pipeline_mode=pl.Buffered(1))


def _ffn_seq(att, y_rw, x, w_out, ln1_g, ln1_b, w1, conv_w, conv_b, w2, ln2_g, ln2_b, tm, alpha):
    B, T, D = x.shape
    aw = att.shape[-1]
    dff = w2.shape[0]
    half = pl.BlockSpec((1, tm, aw), lambda b, t: (b, t, 0))
    rows = pl.BlockSpec((1, tm, D), lambda b, t: (b, t, 0))
    return pl.pallas_call(
        functools.partial(_ffn_seq_kernel, aw=aw, dff=dff, alpha=alpha),
        grid=(B, T // tm),
        in_specs=[half, half, rows, _resident(w_out.shape), _full(ln1_g.shape), _full(ln1_b.shape),
                  _resident(w1.shape), _full(conv_w.shape), _full(conv_b.shape), _resident(w2.shape),
                  _full(ln2_g.shape), _full(ln2_b.shape)],
        out_specs=[rows, pl.BlockSpec((1, 2, dff), lambda b, t: (b, 0, 0))],
        out_shape=[jax.ShapeDtypeStruct((B, T, D), F32), jax.ShapeDtypeStruct((B, 2, dff), F32)],
        scratch_shapes=[pltpu.VMEM((8, dff), F32)],
        compiler_params=_cparams(2),
        name="ffn_seq",
    )(att, y_rw, x, w_out, ln1_g, ln1_b, w1, conv_w, conv_b, w2, ln2_g, ln2_b)


def _ffn_step(att, y_cols, x, w_out, ln1_g, ln1_b, prev2, prev1, w1, conv_w, conv_b, w2, ln2_g, ln2_b, alpha):
    nb, D = x.shape
    dff = w2.shape[0]
    return pl.pallas_call(
        functools.partial(_ffn_step_kernel, dff=dff, alpha=alpha),
        grid=(1,),
        in_specs=[_full(att.shape), _full(y_cols.shape), _full(x.shape), _resident(w_out.shape), _full(ln1_g.shape),
                  _full(ln1_b.shape),
                  _full(prev2.shape), _full(prev1.shape), _resident(w1.shape), _full(conv_w.shape),
                  _full(conv_b.shape), _resident(w2.shape), _full(ln2_g.shape), _full(ln2_b.shape)],
        out_specs=[_full((nb, D)), _full((nb, dff))],
        out_shape=[jax.ShapeDtypeStruct((nb, D), F32), jax.ShapeDtypeStruct((nb, dff), F32)],
        compiler_params=_cparams(1),
        name="ffn_step",
    )(att, y_cols, x, w_out, ln1_g, ln1_b, prev2, prev1, w1, conv_w, conv_b, w2, ln2_g, ln2_b)


def _layer_params(w_in, mu_shift, w0, w_lora_up, a0, a_lora_up, g_lora_up, k_k, k_a, r_k, aw):
    D, ncol = w_in.shape
    n_w, n_a, n_g = w_lora_up.shape[0], a_lora_up.shape[0], g_lora_up.shape[0]
    assert n_w + n_a == LANES and 3 * aw + 3 * aw + n_w + n_a + n_g == ncol
    g_cols = -(-n_g // LANES) * LANES
    pad = g_cols - n_g
    row = lambda a: a.reshape(1, -1).astype(F32)
    lane = jnp.arange(math.gcd(aw, MXU_TILE))
    bd = (lane[:, None] // HEAD_DIM == lane[None, :] // HEAD_DIM).astype(BF16)
    return dict(
        aw=aw,
        w_in=jnp.pad(w_in, ((0, 0), (0, pad))).astype(BF16),
        mu=jnp.pad(row(mu_shift), ((0, 0), (0, pad))),
        w0=row(w0), a0=row(a0), k_k=row(k_k), k_a=row(k_a), r_k=row(r_k),
        wl_pad=jnp.pad(w_lora_up, ((0, n_a), (0, 0))).astype(BF16),
        al_pad=jnp.pad(a_lora_up, ((n_w, 0), (0, 0))).astype(BF16),
        g_pad=jnp.pad(g_lora_up, ((0, pad), (0, 0))).astype(BF16),
        bd=bd,
    )


def kernel(x_prompt, x_sample, cache_k_win, cache_v_win, state_shift, state_wkv, state_conv, w_in, mu_shift, w0, w_lora_up, a0, a_lora_up, g_lora_up, k_k, k_a, r_k, gn_w, gn_b, w_out, ln1_g, ln1_b, w_ffn_in, conv_w, conv_b, w_ffn_out, ln2_g, ln2_b):
    depth = w_in.shape[0]
    alpha = (2.0 * depth) ** 0.25
    B, T, D = x_prompt.shape
    nb = x_sample.shape[0]
    assert x_sample.shape[1] == 1
    n_att = cache_k_win.shape[3]
    n_rw = state_wkv.shape[2]
    aw = n_att * HEAD_DIM
    assert n_rw * HEAD_DIM == aw and T % (Q_TILE * DIL_PATTERNS[-1][1]) == 0
    tm = min(256, T)
    tb = min(256, T)
    heads = jnp.arange(1, n_att + 1, dtype=F32)
    slopes = jnp.exp2(-8.0 * heads / n_att)
    slopes_tab = jnp.broadcast_to(slopes[:, None], (n_att, LANES))
    row = lambda a: a.reshape(1, -1)

    hp, hs = x_prompt, x_sample.reshape(nb, D)
    outs = [[] for _ in range(10)]
    for l in range(depth):
        prm = _layer_params(w_in[l], mu_shift[l], w0[l], w_lora_up[l], a0[l], a_lora_up[l], g_lora_up[l],
                            k_k[l], k_a[l], r_k[l], aw)
        late_weights = (w_out[l], w_ffn_in[l], w_ffn_out[l])
        gnw, gnb = row(gn_w[l]), row(gn_b[l])

        q, k, v, kt, vt, r, lw, k2, vr, al, be, g, bonus = _inproj_seq(hp, prm, min(2 * tm, T))
        x2 = jnp.concatenate([hs, state_shift[l]], axis=0)
        qs, ks, vs, r_s, lw_s, k2_s, al_s, be_s, col = _inproj_step(x2, prm)
        mixer_operands = (qs, ks, vs, col, cache_k_win[l], cache_v_win[l], slopes, state_wkv[l], r_s, lw_s, k2_s,
                          al_s, be_s, gn_w[l], gn_b[l])

        if nb == B * (aw // LANES):
            att, att_s, y_s, wkv_s, w_out_b, w1_b, w2_b = _prompt_attn_and_mixer_step(
                q, k, v, slopes_tab, mixer_operands, late_weights)
        else:
            att = _prompt_attn(q, k, v, slopes_tab)
            att_s, y_s, wkv_s = _mixer_step(*mixer_operands)
            w_out_b, w1_b, w2_b = (w.astype(BF16) for w in late_weights)
        y_rw, wkv_p = _rwkv_chunked(r, lw, k2, vr, al, be, g, bonus, gnw, gnb, tb)

        shift_p = hp[:, -1, :]
        hp, conv_p = _ffn_seq(att, y_rw, hp, w_out_b, row(ln1_g[l]), row(ln1_b[l]), w1_b, conv_w[l],
                              row(conv_b[l]), w2_b, row(ln2_g[l]), row(ln2_b[l]), min(2 * tm, T), alpha)
        shift_s = hs
        hs, gate_s = _ffn_step(att_s.reshape(nb, aw), y_s, hs, w_out_b, row(ln1_g[l]), row(ln1_b[l]),
                               state_conv[l][:, 0, :],
                               state_conv[l][:, 1, :], w1_b, conv_w[l], row(conv_b[l]), w2_b, row(ln2_g[l]),
                               row(ln2_b[l]), alpha)
        conv_s = jnp.stack([state_conv[l][:, 1, :], gate_s], axis=1)

        n_keep = min(DIL_PATTERNS[-1][0], T)
        win = lambda a: jnp.transpose(a.reshape(B, n_att, HEAD_DIM, T), (0, 3, 1, 2))[:, T - n_keep:]
        vals = (win(kt), win(vt), ks.reshape(nb, 1, n_att, HEAD_DIM), vs.reshape(nb, 1, n_att, HEAD_DIM),
                shift_p, shift_s, wkv_p, wkv_s, conv_p, conv_s)
        for lst, val in zip(outs, vals):
            lst.append(val)
    return (hp, hs.reshape(nb, 1, D)) + tuple(jnp.stack(lst) for lst in outs)
```

```python
import functools
import math

import jax
import jax.numpy as jnp
from jax import lax
from jax.experimental import pallas as pl
from jax.experimental.pallas import tpu as pltpu

F32 = jnp.float32
BF16 = jnp.bfloat16

HEAD_DIM = 64
LANES = 128
MXU_TILE = 256
DIL_PATTERNS = ((128, 1), (512, 4), (2048, 16))
WIN_STEPS = 128
Q_TILE = 128
GROUP = 4
SUB_TILES = 2
CHUNK = 64
LN_EPS = 1e-5
GN_EPS = 64e-5
NEG = -1e30
VMEM_LIMIT = 56 * 1024 * 1024

_DN = {"NN": (((1,), (0,)), ((), ())), "NT": (((1,), (1,)), ((), ())), "TN": (((0,), (0,)), ((), ()))}


def _dot(a, b, dims="NN"):
    return lax.dot_general(a.astype(BF16), b.astype(BF16), _DN[dims], preferred_element_type=F32)


def _dot_const_l(c, x):
    hi = x.astype(BF16)
    r1 = x - hi.astype(F32)
    mid = r1.astype(BF16)
    lo = (r1 - mid.astype(F32)).astype(BF16)
    return _dot(c, hi) + _dot(c, mid) + _dot(c, lo)


def _sigmoid(x):
    return 0.5 * jnp.tanh(0.5 * x) + 0.5


def _layer_norm(x, g, b):
    mu = jnp.mean(x, axis=-1, keepdims=True)
    xc = x - mu
    var = jnp.mean(xc * xc, axis=-1, keepdims=True)
    return xc * lax.rsqrt(var + LN_EPS) * g + b


def _cparams(n_grid):
    return pltpu.CompilerParams(dimension_semantics=("arbitrary",) * n_grid, vmem_limit_bytes=VMEM_LIMIT)


def _full(shape):
    nd = len(shape)
    return pl.BlockSpec(shape, lambda *_: (0,) * nd)


def _rwkv_prep(rw, w0, a0, k_k, k_a, r_k, wl_pad, al_pad, g_pad, bd, aw):
    r = rw[:, 0:aw]
    k = rw[:, aw:2 * aw]
    v = rw[:, 2 * aw:3 * aw]
    lo = rw[:, 3 * aw:3 * aw + 128]
    glo = rw[:, 3 * aw + 128:]
    u = w0 + _dot(jnp.tanh(lo), wl_pad)
    a = _sigmoid(a0 + _dot(lo, al_pad))
    g = _dot(_sigmoid(glo), g_pad)
    kk = k * k_k
    def head_sums(x):
        w = bd.shape[0]
        return jnp.concatenate([_dot(x[:, c:c + w], bd) for c in range(0, aw, w)], axis=1)

    sumsq = head_sums(kk * kk)
    lw = -math.exp(-0.5) * _sigmoid(u)
    k2 = k * (1.0 + (a - 1.0) * k_a)
    bonus = head_sums(r * k2 * r_k) * v
    kkn = kk * lax.rsqrt(jnp.maximum(sumsq, 1e-24))
    return r, lw, k2, v, -kkn, kkn * a, g, bonus


def _inproj_seq_kernel(x_ref, w_ref, mu_ref, w0_ref, a0_ref, kk_ref, ka_ref, rk_ref, wl_ref, al_ref, gp_ref,
                       bd_ref, q_ref, k_ref, v_ref, kt_ref, vt_ref, r_o, lw_o, k2_o, vr_o, al_o, be_o, g_o, bo_o,
                       carry_ref, *, aw):
    @pl.when(pl.program_id(1) == 0)
    def _():
        carry_ref[...] = jnp.zeros_like(carry_ref)

    tm = x_ref.shape[1]
    hr = tm // SUB_TILES
    spans = [slice(h * hr, (h + 1) * hr) for h in range(SUB_TILES)]
    params = (w0_ref[...], a0_ref[...], kk_ref[...], ka_ref[...], rk_ref[...], wl_ref[...], al_ref[...],
              gp_ref[...], bd_ref[...], aw)
    row = lax.broadcasted_iota(jnp.int32, (hr, w_ref.shape[1] - 3 * aw), 0)

    def finish(p, sp, last):
        q_ref[0, sp, :] = p[:, 0:aw]
        k = p[:, aw:2 * aw]
        v = p[:, 2 * aw:3 * aw]
        k_ref[0, sp, :] = k
        v_ref[0, sp, :] = v
        kt_ref[0, :, sp] = k.T
        vt_ref[0, :, sp] = v.T
        prw = p[:, 3 * aw:]
        prev = jnp.where(row == 0, last, pltpu.roll(prw, 1, axis=0))
        outs = _rwkv_prep(prw + (prev - prw) * mu_ref[...], *params)
        for o_ref, val in zip((r_o, lw_o, k2_o, vr_o, al_o, be_o, g_o, bo_o), outs):
            o_ref[0, sp, :] = val.astype(o_ref.dtype)
        return prw[hr - 1:hr, :]

    last = carry_ref[0:1, :]
    p = _dot(x_ref[0, spans[0], :], w_ref[...])
    for h in range(1, SUB_TILES):
        p_next = _dot(x_ref[0, spans[h], :], w_ref[...])
        last = finish(p, spans[h - 1], last)
        p = p_next
    carry_ref[0:1, :] = finish(p, spans[-1], last)


def _inproj_step_kernel(x_ref, w_ref, mu_ref, w0_ref, a0_ref, kk_ref, ka_ref, rk_ref, wl_ref, al_ref, gp_ref,
                        bd_ref, q_ref, k_ref, v_ref, r_o, lw_o, k2_o, al_o, be_o, col_o, *, aw, nb):
    p = _dot(x_ref[...], w_ref[...])
    q_ref[...] = p[0:nb, 0:aw]
    k_ref[...] = p[0:nb, aw:2 * aw]
    v_ref[...] = p[0:nb, 2 * aw:3 * aw]
    prw = p[0:nb, 3 * aw:]
    prev = p[nb:2 * nb, 3 * aw:]
    rw = prw + (prev - prw) * mu_ref[...]
    r, lw, k2, vr, al, be, g, bonus = _rwkv_prep(rw, w0_ref[...], a0_ref[...], kk_ref[...], ka_ref[...],
                                                 rk_ref[...], wl_ref[...], al_ref[...], gp_ref[...], bd_ref[...], aw)
    for o_ref, val in zip((r_o, lw_o, k2_o, al_o, be_o), (r, lw, k2, al, be)):
        o_ref[...] = val
    pad = jnp.zeros((LANES - nb, aw), F32)
    for i, val in enumerate((vr, g, bonus)):
        col_o[i * aw:(i + 1) * aw, :] = jnp.concatenate([val, pad], axis=0).T


_PREP_PARAMS = ("mu", "w0", "a0", "k_k", "k_a", "r_k", "wl_pad", "al_pad", "g_pad", "bd")


def _inproj_seq(x, prm, tm):
    B, T, D = x.shape
    aw = prm["aw"]
    nrw = prm["w_in"].shape[1] - 3 * aw
    small = [prm[n] for n in _PREP_PARAMS]
    row_spec = pl.BlockSpec((1, tm, aw), lambda b, t: (b, t, 0))
    col_spec = pl.BlockSpec((1, aw, tm), lambda b, t: (b, 0, t))
    row_sds = jax.ShapeDtypeStruct((B, T, aw), F32)
    col_sds = jax.ShapeDtypeStruct((B, aw, T), F32)
    return pl.pallas_call(
        functools.partial(_inproj_seq_kernel, aw=aw),
        grid=(B, T // tm),
        in_specs=[pl.BlockSpec((1, tm, D), lambda b, t: (b, t, 0)), _resident(prm["w_in"].shape)]
        + [_full(s.shape) for s in small],
        out_specs=[row_spec] * 3 + [col_spec] * 2 + [row_spec] * 8,
        out_shape=[row_sds] * 3 + [col_sds] * 2
        + [jax.ShapeDtypeStruct((B, T, aw), F32 if n == "lw" else BF16)
           for n in ("r", "lw", "k2", "v", "al", "be", "g", "bonus")],
        scratch_shapes=[pltpu.VMEM((8, nrw), F32)],
        compiler_params=_cparams(2),
        name="inproj_seq",
    )(x, prm["w_in"], *small)


def _inproj_step(x2, prm):
    nb = x2.shape[0] // 2
    aw = prm["aw"]
    small = [prm[n] for n in _PREP_PARAMS]
    assert nb <= LANES
    out_sds = jax.ShapeDtypeStruct((nb, aw), F32)
    return pl.pallas_call(
        functools.partial(_inproj_step_kernel, aw=aw, nb=nb),
        grid=(1,),
        in_specs=[_full(x2.shape), _full(prm["w_in"].shape)] + [_full(s.shape) for s in small],
        out_specs=[_full((nb, aw))] * 8 + [_full((3 * aw, LANES))],
        out_shape=[out_sds] * 8 + [jax.ShapeDtypeStruct((3 * aw, LANES), F32)],
        compiler_params=_cparams(1),
        name="inproj_step",
    )(x2, prm["w_in"], *small)


def _prompt_attn_kernel(q_ref, k_ref, v_ref, sl_ref, o_ref, acc_s, m_s, l_s, yq, yk, yv, yacc, ym, yl, *, T):
    hp = pl.program_id(1)
    scale = HEAD_DIM ** -0.5
    lane = lax.broadcasted_iota(jnp.int32, (1, LANES), 1)
    head_masks = ((lane < HEAD_DIM).astype(F32), (lane >= HEAD_DIM).astype(F32))
    qi = lax.broadcasted_iota(jnp.int32, (Q_TILE, Q_TILE), 0)
    ki = lax.broadcasted_iota(jnp.int32, (Q_TILE, Q_TILE), 1)
    steps_cur = (qi - ki).astype(F32)
    steps_prev = (qi - ki + Q_TILE).astype(F32)
    ok_cur = ki <= qi
    ok_prev = ki >= qi
    first_lanes = lax.broadcasted_iota(jnp.int32, (Q_TILE, LANES), 1) < HEAD_DIM
    slopes = [sl_ref[pl.ds(2 * hp + j, 1), :] for j in range(2)]

    def tile_group(src, tiles, bias, stats, first_branch):
        rq, rk, rv = src
        bias_c, bias_pc = bias
        acc_s, m_s, l_s = stats
        ids = range(len(tiles))
        q = [rq(rows) * scale for rows, _ in tiles]
        kk = [rk(rows) if prev is None else jnp.concatenate([rk(prev), rk(rows)], axis=0) for rows, prev in tiles]
        vv = [rv(rows) if prev is None else jnp.concatenate([rv(prev), rv(rows)], axis=0) for rows, prev in tiles]
        qq = [jnp.concatenate([q[t] * head_masks[0], q[t] * head_masks[1]], axis=0) for t in ids]
        s = [_dot(qq[t], kk[t], "NT") + (bias_c if tiles[t][1] is None else bias_pc) for t in ids]
        m = [jnp.max(s[t], axis=-1, keepdims=True) for t in ids]
        p = [jnp.exp(s[t] - m[t]) for t in ids]
        l = [jnp.sum(p[t], axis=-1, keepdims=True) for t in ids]
        acc = [_dot(p[t], vv[t]) for t in ids]
        for t, (rows, _) in enumerate(tiles):
            acc_n = jnp.where(first_lanes, acc[t][0:Q_TILE], acc[t][Q_TILE:])
            m_n = jnp.where(first_lanes, m[t][0:Q_TILE], m[t][Q_TILE:])
            l_n = jnp.where(first_lanes, l[t][0:Q_TILE], l[t][Q_TILE:])
            if first_branch:
                acc_s[rows, :] = acc_n
                m_s[rows, :] = m_n
                l_s[rows, :] = l_n
            else:
                m_o = m_s[rows, :]
                m_new = jnp.maximum(m_o, m_n)
                e_o = jnp.exp(m_o - m_new)
                e_n = jnp.exp(m_n - m_new)
                acc_s[rows, :] = acc_s[rows, :] * e_o + acc_n * e_n
                l_s[rows, :] = l_s[rows, :] * e_o + l_n * e_n
                m_s[rows, :] = m_new

    dils = [dil for _, dil in DIL_PATTERNS]
    R = dils[1]
    assert dils == [1, R, R * R] and all(win // dil == WIN_STEPS for win, dil in DIL_PATTERNS)

    def bias_of(dil):
        b_cur = [jnp.where(ok_cur, -(slopes[j] * float(dil)) * steps_cur, NEG) for j in range(2)]
        b_prev = [jnp.where(ok_prev, -(slopes[j] * float(dil)) * steps_prev, NEG) for j in range(2)]
        return (jnp.concatenate(b_cur, axis=0),
                jnp.concatenate([jnp.concatenate([b_prev[j], b_cur[j]], axis=1) for j in range(2)], axis=0))

    def groups(tiles):
        return [tiles[i:i + GROUP] for i in range(0, len(tiles), GROUP)]

    natural = tuple((lambda rows, ref=ref: ref[0, rows, :]) for ref in (q_ref, k_ref, v_ref))
    scratch = tuple((lambda rows, ref=ref: ref[rows, :]) for ref in (yq, yk, yv))
    nat_stats = (acc_s, m_s, l_s)
    y_stats = (yacc, ym, yl)

    nb1 = T // Q_TILE
    assert nb1 % GROUP == 0
    bias1 = bias_of(dils[0])
    tile_group(natural, [(pl.ds(jb * Q_TILE, Q_TILE), pl.ds((jb - 1) * Q_TILE, Q_TILE) if jb else None)
                         for jb in range(GROUP)], bias1, nat_stats, True)

    def later(g, carry):
        start = pl.multiple_of(g * (GROUP * Q_TILE), GROUP * Q_TILE)
        tile_group(natural, [(pl.ds(start + i * Q_TILE, Q_TILE), pl.ds(start + (i - 1) * Q_TILE, Q_TILE))
                             for i in range(GROUP)], bias1, nat_stats, True)
        return carry

    lax.fori_loop(1, nb1 // GROUP, later, 0)

    LR = T // R
    bias_r = bias_of(dils[1])
    bias_rr = bias_of(dils[2])

    def stream(s, carry):
        own = pl.ds(s, LR, stride=R)
        yq[...] = q_ref[0, own, :]
        yk[...] = k_ref[0, own, :]
        yv[...] = v_ref[0, own, :]
        for grp in groups([(pl.ds(jb * Q_TILE, Q_TILE), pl.ds((jb - 1) * Q_TILE, Q_TILE) if jb else None)
                           for jb in range(LR // Q_TILE)]):
            tile_group(scratch, grp, bias_r, y_stats, True)
        for grp in groups([(pl.ds(c + R * Q_TILE * jb, Q_TILE, stride=R),
                            pl.ds(c + R * Q_TILE * (jb - 1), Q_TILE, stride=R) if jb else None)
                           for c in range(R) for jb in range(LR // R // Q_TILE)]):
            tile_group(scratch, grp, bias_rr, y_stats, False)
        m_a, m_b = m_s[own, :], ym[...]
        m_new = jnp.maximum(m_a, m_b)
        e_a = jnp.exp(m_a - m_new)
        e_b = jnp.exp(m_b - m_new)
        acc_s[own, :] = (acc_s[own, :] * e_a + yacc[...] * e_b) / (l_s[own, :] * e_a + yl[...] * e_b)
        return carry

    lax.fori_loop(0, R, stream, 0)
    o_ref[0] = acc_s[...].astype(o_ref.dtype)


def _prompt_attn_scratch(T):
    return [pltpu.VMEM((T, LANES), F32)] * 3 + [pltpu.VMEM((T // DIL_PATTERNS[1][1], LANES), F32)] * 6


def _prompt_attn(q, k, v, slopes_tab):
    B, T, W = q.shape
    spec = pl.BlockSpec((1, T, LANES), lambda b, hp: (b, 0, hp))
    return pl.pallas_call(
        functools.partial(_prompt_attn_kernel, T=T),
        grid=(B, W // LANES),
        in_specs=[spec, spec, spec, _full(slopes_tab.shape)],
        out_specs=spec,
        out_shape=jax.ShapeDtypeStruct((B, T, W), BF16),
        scratch_shapes=_prompt_attn_scratch(T),
        compiler_params=_cparams(2),
        name="prompt_attn",
    )(q, k, v, slopes_tab)


def _rwkv_chunk_kernel(r_ref, lw_ref, k_ref, v_ref, al_ref, be_ref, g_ref, bo_ref, gnw_ref, gnb_ref,
                       y_ref, s_ref, st_scr, *, Tb, npair, nbb):
    C = CHUNK
    n2 = 2 * C
    t = pl.program_id(1)

    @pl.when(t == 0)
    def _():
        st_scr[...] = jnp.zeros_like(st_scr)

    lane = lax.broadcasted_iota(jnp.int32, (1, LANES), 1)
    m0 = (lane < HEAD_DIM).astype(F32)
    m1 = 1.0 - m0
    ri = lax.broadcasted_iota(jnp.int32, (n2, n2), 0)
    ci = lax.broadcasted_iota(jnp.int32, (n2, n2), 1)
    same = (ri < C) == (ci < C)
    strict = same & ((ri & (C - 1)) > (ci & (C - 1)))
    incl = same & ((ri & (C - 1)) >= (ci & (C - 1)))
    tri = (lax.broadcasted_iota(jnp.int32, (C, C), 0) >= lax.broadcasted_iota(jnp.int32, (C, C), 1)).astype(BF16)
    head_rows = ((lax.broadcasted_iota(jnp.int32, (n2, LANES), 0) < C)
                 == (lax.broadcasted_iota(jnp.int32, (n2, LANES), 1) < HEAD_DIM)).astype(F32)

    def stack(x):
        return jnp.concatenate([x * m0, x * m1], axis=0)

    def twice(x):
        return jnp.concatenate([x, x], axis=0)

    def body(c, carry):
        rows = pl.ds(pl.multiple_of(c * C, C), C)
        chains = [(bb, slice(hp * LANES, (hp + 1) * LANES)) for bb in range(nbb) for hp in range(npair)]
        ids = range(len(chains))
        S = [st_scr[i] for i in ids]
        lw_all = jnp.concatenate([lw_ref[bb, rows, :] for bb in range(nbb)], axis=1)
        cl_all = _dot_const_l(tri, lw_all)
        e_pos_all = jnp.exp(cl_all)
        e_neg_all = jnp.exp(-cl_all)
        e_exc_all = jnp.exp(cl_all - lw_all)
        wide = [slice(i * LANES, (i + 1) * LANES) for i in ids]
        g_end = [e_pos_all[C - 1:C, wide[i]] for i in ids]
        rt = [r_ref[bb, rows, cs] * e_pos_all[:, wide[i]] for i, (bb, cs) in enumerate(chains)]
        kt = [k_ref[bb, rows, cs] * e_neg_all[:, wide[i]] for i, (bb, cs) in enumerate(chains)]
        bt = [be_ref[bb, rows, cs] * e_neg_all[:, wide[i]] for i, (bb, cs) in enumerate(chains)]
        at = [al_ref[bb, rows, cs] * e_exc_all[:, wide[i]] for i, (bb, cs) in enumerate(chains)]
        ar_st = [jnp.concatenate([stack(at[i]), stack(rt[i])], axis=0) for i in ids]
        kb2 = [jnp.concatenate([twice(kt[i]), twice(bt[i])], axis=0) for i in ids]
        v_st = [stack(v_ref[bb, rows, cs]) for bb, cs in chains]
        gram = [_dot(ar_st[i], kb2[i], "NT") for i in ids]
        l_ab = [jnp.where(strict, gram[i][0:n2, n2:], 0.0) for i in ids]
        l_kk = [jnp.concatenate([jnp.where(strict, gram[i][0:n2, 0:n2], 0.0),
                                 jnp.where(incl, gram[i][n2:, 0:n2], 0.0)], axis=0) for i in ids]
        l_rb = [jnp.where(incl, gram[i][n2:, n2:], 0.0) for i in ids]
        from_state = [_dot(jnp.concatenate([at[i], rt[i]], axis=0), S[i], "NT") for i in ids]
        from_state = [jnp.concatenate([stack(from_state[i][0:C]), stack(from_state[i][C:])], axis=0) for i in ids]
        from_v = [_dot(l_kk[i], v_st[i]) for i in ids]
        x = l_ab
        u = [from_state[i][0:n2] + from_v[i][0:n2] for i in ids]
        for _ in range(C.bit_length() - 2):
            xu = [_dot(x[i], jnp.concatenate([x[i], u[i]], axis=1)) for i in ids]
            x = [xu[i][:, 0:n2] for i in ids]
            u = [u[i] + xu[i][:, n2:] for i in ids]
        u = [u[i] + _dot(x[i], u[i]) for i in ids]
        y_st = [from_state[i][n2:] + from_v[i][n2:] + _dot(l_rb[i], u[i]) for i in ids]
        for i in ids:
            vu = jnp.concatenate([v_st[i], u[i]], axis=0)
            kb_end = jnp.concatenate([stack(kt[i] * g_end[i]), stack(bt[i] * g_end[i])], axis=0)
            st_scr[i] = S[i] * g_end[i] + _dot(vu, kb_end, "TN")
        for i, (bb, cs) in enumerate(chains):
            mu = jnp.sum(y_st[i], axis=-1, keepdims=True) * (1.0 / HEAD_DIM)
            yc = (y_st[i] - mu) * head_rows
            var = jnp.sum(yc * yc, axis=-1, keepdims=True) * (1.0 / HEAD_DIM)
            yn_st = yc * lax.rsqrt(var + GN_EPS)
            yn_i = (yn_st[0:C, :] + yn_st[C:n2, :]) * gnw_ref[:, cs] + gnb_ref[:, cs]
            y_ref[bb, rows, cs] = ((yn_i + bo_ref[bb, rows, cs]) * g_ref[bb, rows, cs]).astype(y_ref.dtype)
        return carry

    lax.fori_loop(0, Tb // C, body, 0)

    @pl.when(t == pl.num_programs(1) - 1)
    def _():
        for i in range(nbb * npair):
            bb, hp = divmod(i, npair)
            pair = st_scr[i]
            s_ref[bb, 2 * hp] = pair[0:HEAD_DIM, 0:HEAD_DIM]
            s_ref[bb, 2 * hp + 1] = pltpu.roll(pair, HEAD_DIM, axis=1)[HEAD_DIM:, 0:HEAD_DIM]


def _rwkv_chunked(r, lw, k2, v, al, be, g, bonus, gn_w, gn_b, tb):
    B, T, W = r.shape
    npair = W // LANES
    nbb = next(n for n in (4, 2, 1) if B % n == 0)
    spec = pl.BlockSpec((nbb, tb, W), lambda b, t: (b, t, 0))
    st_spec = pl.BlockSpec((nbb, 2 * npair, HEAD_DIM, HEAD_DIM), lambda b, t: (b, 0, 0, 0))
    return pl.pallas_call(
        functools.partial(_rwkv_chunk_kernel, Tb=tb, npair=npair, nbb=nbb),
        grid=(B // nbb, T // tb),
        in_specs=[spec] * 8 + [_full(gn_w.shape), _full(gn_b.shape)],
        out_specs=[spec, st_spec],
        out_shape=[jax.ShapeDtypeStruct((B, T, W), BF16),
                   jax.ShapeDtypeStruct((B, 2 * npair, HEAD_DIM, HEAD_DIM), F32)],
        scratch_shapes=[pltpu.VMEM((nbb * npair, LANES, LANES), F32)],
        compiler_params=_cparams(2),
        name="rwkv_chunk",
    )(r, lw, k2, v, al, be, g, bonus, gn_w, gn_b)


def _rwkv_one_step(S, r, lw, k, al, be, v, g, bonus, gnw, gnb):
    sa = jnp.sum(S * al, axis=-1, keepdims=True)
    s_new = S * jnp.exp(lw) + sa * be + v * k
    y = jnp.sum(s_new * r, axis=-1, keepdims=True)
    mu = jnp.mean(y, axis=1, keepdims=True)
    yc = y - mu
    var = jnp.mean(yc * yc, axis=1, keepdims=True)
    yn = yc * lax.rsqrt(var + GN_EPS) * gnw + gnb
    return (yn + bonus) * g, s_new


def _attn_one_step(q, k_new, v_new, kt, vt, slopes, P):
    aw = q.shape[1]
    H = aw // HEAD_DIM
    scale = HEAD_DIM ** -0.5
    dist = P - lax.broadcasted_iota(jnp.int32, (1, P), 1)
    count = jnp.zeros((1, P), F32)
    for win, dil in DIL_PATTERNS:
        assert dil & (dil - 1) == 0
        count = count + ((dist <= win) & ((dist & (dil - 1)) == 0)).astype(F32)
    own_head = (lax.broadcasted_iota(jnp.int32, (H, aw), 1) // HEAD_DIM
                == lax.broadcasted_iota(jnp.int32, (H, aw), 0)).astype(F32)
    q_heads = q * own_head
    s = _dot(q_heads, kt) * scale - slopes * dist.astype(F32)
    s = jnp.where(count > 0.0, s, NEG)
    s0 = jnp.sum(q_heads * k_new, axis=-1, keepdims=True) * scale
    m = jnp.maximum(jnp.max(s, axis=-1, keepdims=True), s0)
    p = count * jnp.exp(s - m)
    e0 = len(DIL_PATTERNS) * jnp.exp(s0 - m)
    l = jnp.sum(p, axis=-1, keepdims=True) + e0
    acc = _dot(p, vt, "NT") + e0 * v_new
    return jnp.sum(acc * own_head, axis=0, keepdims=True) / jnp.sum(l * own_head, axis=0, keepdims=True)


def _mixer_step_kernel(*refs, P, aw):
    _mixer_step_body(pl.program_id(0), *refs, P=P, aw=aw)


N_MIXER_IN = 15


def _mixer_step_body(b, q_ref, kn_ref, vn_ref, col_ref, kt_ref, vt_ref, sl_ref, s_ref, r_ref, lw_ref, k_ref, al_ref,
                     be_ref, gnw_ref, gnb_ref, att_ref, y_ref, so_ref, *, P, aw):
    H = aw // HEAD_DIM
    own = (lax.broadcasted_iota(jnp.int32, (1, LANES), 1) == b).astype(F32)

    def column(i):
        return jnp.sum(col_ref[i * aw:(i + 1) * aw, :] * own, axis=-1, keepdims=True).reshape(H, HEAD_DIM, 1)

    @pl.when(b == 0)
    def _():
        y_ref[...] = jnp.zeros_like(y_ref)

    att_ref[0] = _attn_one_step(q_ref[0], kn_ref[0], vn_ref[0], kt_ref[0].reshape(aw, P), vt_ref[0].reshape(aw, P),
                                sl_ref[...], P)
    y, s_new = _rwkv_one_step(s_ref[0], r_ref[0], lw_ref[0], k_ref[0], al_ref[0], be_ref[0], column(0), column(1),
                              column(2), gnw_ref[...], gnb_ref[...])
    so_ref[0] = s_new
    y_ref[...] += y.reshape(aw, 1) * own


def _mixer_step_operands(seq_of, q, k_new, v_new, col, k_cache, v_cache, slopes, state, r, lw, k2, al, be, gn_w,
                         gn_b):
    nb, P, H, _ = k_cache.shape
    aw = H * HEAD_DIM
    rowv = lambda a: a.reshape(nb, H, 1, HEAD_DIM)
    vec = lambda a: a.reshape(nb, 1, aw)
    per_seq = lambda shape: pl.BlockSpec((1,) + shape, lambda *g: (seq_of(*g),) + (0,) * len(shape))
    t_spec, s_spec, r_spec = per_seq((H, HEAD_DIM, P)), per_seq((H, HEAD_DIM, HEAD_DIM)), per_seq((H, 1, HEAD_DIM))
    v_spec = per_seq((1, aw))
    p_spec = _full((H, HEAD_DIM, 1))
    args = (vec(q), vec(k_new), vec(v_new), col, jnp.transpose(k_cache, (0, 2, 3, 1)),
            jnp.transpose(v_cache, (0, 2, 3, 1)), slopes.reshape(H, 1), state, rowv(r), rowv(lw), rowv(k2), rowv(al),
            rowv(be), gn_w.reshape(H, HEAD_DIM, 1), gn_b.reshape(H, HEAD_DIM, 1))
    assert len(args) == N_MIXER_IN
    in_specs = ([v_spec] * 3 + [_full(col.shape), t_spec, t_spec, _full((H, 1)), s_spec] + [r_spec] * 5
                + [p_spec] * 2)
    out_specs = [v_spec, _full((aw, LANES)), s_spec]
    out_shape = [jax.ShapeDtypeStruct((nb, 1, aw), F32), jax.ShapeDtypeStruct((aw, LANES), F32),
                 jax.ShapeDtypeStruct(state.shape, F32)]
    return args, in_specs, out_specs, out_shape


def _mixer_step(*operands):
    args, in_specs, out_specs, out_shape = _mixer_step_operands(lambda b: b, *operands)
    nb, P, H, _ = operands[4].shape
    return pl.pallas_call(
        functools.partial(_mixer_step_kernel, P=P, aw=H * HEAD_DIM),
        grid=(nb,),
        in_specs=in_specs,
        out_specs=out_specs,
        out_shape=out_shape,
        compiler_params=_cparams(1),
        name="mixer_step",
    )(*args)


def _cast_operands(step_of, n_steps, weights):
    in_specs, out_specs, out_shape = [], [], []
    for w in weights:
        rows, cols = w.shape
        per = next(d for d in range(1, n_steps + 1)
                   if n_steps % d == 0 and rows % (n_steps // d) == 0 and rows // (n_steps // d) % 16 == 0)
        spec = pl.BlockSpec((rows * per // n_steps, cols), lambda *g, per=per: (step_of(*g) // per, 0))
        in_specs.append(spec)
        out_specs.append(spec)
        out_shape.append(jax.ShapeDtypeStruct(w.shape, BF16))
    return in_specs, out_specs, out_shape


def _attn_mixer_kernel(q_ref, k_ref, v_ref, sl_ref, *refs, T, P, aw, npair, n_cast):
    n_in = N_MIXER_IN + n_cast
    mixer_in, cast_in = refs[:N_MIXER_IN], refs[N_MIXER_IN:n_in]
    o_ref, *mixer_out = refs[n_in:n_in + 4]
    cast_out, scratch = refs[n_in + 4:n_in + 4 + n_cast], refs[n_in + 4 + n_cast:]
    for src, dst in zip(cast_in, cast_out):
        dst[...] = src[...].astype(BF16)
    _mixer_step_body(pl.program_id(0) * npair + pl.program_id(1), *mixer_in, *mixer_out, P=P, aw=aw)
    _prompt_attn_kernel(q_ref, k_ref, v_ref, sl_ref, o_ref, *scratch, T=T)


def _prompt_attn_and_mixer_step(q, k, v, slopes_tab, mixer_operands, weights):
    B, T, W = q.shape
    npair = W // LANES
    step_of = lambda b, hp: b * npair + hp
    args, in_specs, out_specs, out_shape = _mixer_step_operands(step_of, *mixer_operands)
    c_in, c_out, c_shape = _cast_operands(step_of, B * npair, weights)
    nb, P, H, _ = mixer_operands[4].shape
    assert nb == B * npair
    spec = pl.BlockSpec((1, T, LANES), lambda b, hp: (b, 0, hp))
    return pl.pallas_call(
        functools.partial(_attn_mixer_kernel, T=T, P=P, aw=H * HEAD_DIM, npair=npair, n_cast=len(weights)),
        grid=(B, npair),
        in_specs=[spec, spec, spec, _full(slopes_tab.shape)] + in_specs + c_in,
        out_specs=[spec] + out_specs + c_out,
        out_shape=[jax.ShapeDtypeStruct((B, T, W), BF16)] + out_shape + c_shape,
        scratch_shapes=_prompt_attn_scratch(T),
        compiler_params=_cparams(2),
        name="prompt_attn_mixer_step",
    )(q, k, v, slopes_tab, *args, *weights)


def _ffn_tail(x1, gate_pre, prev1, prev2, up, cw_ref, cb_ref, w2_ref, g_ref, b_ref, alpha):
    c = cw_ref[0:1, :] * prev2 + cw_ref[1:2, :] * prev1 + cw_ref[2:3, :] * gate_pre + cb_ref[...]
    h = c * _sigmoid(c) * up
    return _layer_norm(alpha * x1 + _dot(h, w2_ref[...]), g_ref[...], b_ref[...])


def _ffn_seq_kernel(att_ref, y_ref, x_ref, wo_ref, g1_ref, b1_ref, w1_ref, cw_ref, cb_ref, w2_ref, g_ref, b_ref,
                    o_ref, tail_ref, carry_ref, *, aw, dff, alpha):
    @pl.when(pl.program_id(1) == 0)
    def _():
        carry_ref[...] = jnp.zeros_like(carry_ref)

    tm = x_ref.shape[1]
    hr = tm // SUB_TILES
    spans = [slice(h * hr, (h + 1) * hr) for h in range(SUB_TILES)]
    n = range(SUB_TILES)
    row = lax.broadcasted_iota(jnp.int32, (hr, dff), 0)
    mixed = [_dot(att_ref[0, sp, :], wo_ref[0:aw, :]) + _dot(y_ref[0, sp, :], wo_ref[aw:, :]) for sp in spans]
    x1 = [_layer_norm(alpha * x_ref[0, spans[i], :] + mixed[i], g1_ref[...], b1_ref[...]) for i in n]
    hu = [_dot(x1[0], w1_ref[...])]
    last2 = carry_ref[0:2, :]
    gated, f = [], []
    for i in n:
        if i + 1 < SUB_TILES:
            hu.append(_dot(x1[i + 1], w1_ref[...]))
        gate_pre = hu[i][:, 0:dff]
        c0, c1 = last2[0:1, :], last2[1:2, :]
        prev1 = jnp.where(row == 0, c1, pltpu.roll(gate_pre, 1, axis=0))
        prev2 = jnp.where(row == 0, c0, jnp.where(row == 1, c1, pltpu.roll(gate_pre, 2, axis=0)))
        last2 = gate_pre[hr - 2:hr, :]
        c = cw_ref[0:1, :] * prev2 + cw_ref[1:2, :] * prev1 + cw_ref[2:3, :] * gate_pre + cb_ref[...]
        gated.append(c * _sigmoid(c) * hu[i][:, dff:])
        if i > 0:
            f.append(_dot(gated[i - 1], w2_ref[...]))
    f.append(_dot(gated[-1], w2_ref[...]))
    carry_ref[0:2, :] = last2
    tail_ref[0] = last2
    for i in n:
        o_ref[0, spans[i], :] = _layer_norm(alpha * x1[i] + f[i], g_ref[...], b_ref[...])


def _ffn_step_kernel(att_ref, y_ref, x_ref, wo_ref, g1_ref, b1_ref, p2_ref, p1_ref, w1_ref, cw_ref, cb_ref, w2_ref,
                     g_ref, b_ref, o_ref, gate_ref, *, dff, alpha):
    nb, aw = att_ref.shape
    h = _dot(att_ref[...], wo_ref[0:aw, :]) + _dot(y_ref[...].T[0:nb, :], wo_ref[aw:, :])
    x1 = _layer_norm(alpha * x_ref[...] + h, g1_ref[...], b1_ref[...])
    hu = _dot(x1, w1_ref[...])
    gate_pre = hu[:, 0:dff]
    gate_ref[...] = gate_pre
    o_ref[...] = _ffn_tail(x1, gate_pre, p1_ref[...], p2_ref[...], hu[:, dff:], cw_ref, cb_ref, w2_ref, g_ref,
                           b_ref, alpha)


def _resident(shape):
    nd = len(shape)
    return pl.BlockSpec(shape, lambda *_: (0,) * nd, pipeline_mode=pl.Buffered(1))


def _ffn_seq(att, y_rw, x, w_out, ln1_g, ln1_b, w1, conv_w, conv_b, w2, ln2_g, ln2_b, tm, alpha):
    B, T, D = x.shape
    aw = att.shape[-1]
    dff = w2.shape[0]
    half = pl.BlockSpec((1, tm, aw), lambda b, t: (b, t, 0))
    rows = pl.BlockSpec((1, tm, D), lambda b, t: (b, t, 0))
    return pl.pallas_call(
        functools.partial(_ffn_seq_kernel, aw=aw, dff=dff, alpha=alpha),
        grid=(B, T // tm),
        in_specs=[half, half, rows, _resident(w_out.shape), _full(ln1_g.shape), _full(ln1_b.shape),
                  _resident(w1.shape), _full(conv_w.shape), _full(conv_b.shape), _resident(w2.shape),
                  _full(ln2_g.shape), _full(ln2_b.shape)],
        out_specs=[rows, pl.BlockSpec((1, 2, dff), lambda b, t: (b, 0, 0))],
        out_shape=[jax.ShapeDtypeStruct((B, T, D), F32), jax.ShapeDtypeStruct((B, 2, dff), F32)],
        scratch_shapes=[pltpu.VMEM((8, dff), F32)],
        compiler_params=_cparams(2),
        name="ffn_seq",
    )(att, y_rw, x, w_out, ln1_g, ln1_b, w1, conv_w, conv_b, w2, ln2_g, ln2_b)


def _ffn_step(att, y_cols, x, w_out, ln1_g, ln1_b, prev2, prev1, w1, conv_w, conv_b, w2, ln2_g, ln2_b, alpha):
    nb, D = x.shape
    dff = w2.shape[0]
    return pl.pallas_call(
        functools.partial(_ffn_step_kernel, dff=dff, alpha=alpha),
        grid=(1,),
        in_specs=[_full(att.shape), _full(y_cols.shape), _full(x.shape), _resident(w_out.shape), _full(ln1_g.shape),
                  _full(ln1_b.shape),
                  _full(prev2.shape), _full(prev1.shape), _resident(w1.shape), _full(conv_w.shape),
                  _full(conv_b.shape), _resident(w2.shape), _full(ln2_g.shape), _full(ln2_b.shape)],
        out_specs=[_full((nb, D)), _full((nb, dff))],
        out_shape=[jax.ShapeDtypeStruct((nb, D), F32), jax.ShapeDtypeStruct((nb, dff), F32)],
        compiler_params=_cparams(1),
        name="ffn_step",
    )(att, y_cols, x, w_out, ln1_g, ln1_b, prev2, prev1, w1, conv_w, conv_b, w2, ln2_g, ln2_b)


def _layer_params(w_in, mu_shift, w0, w_lora_up, a0, a_lora_up, g_lora_up, k_k, k_a, r_k, aw):
    D, ncol = w_in.shape
    n_w, n_a, n_g = w_lora_up.shape[0], a_lora_up.shape[0], g_lora_up.shape[0]
    assert n_w + n_a == LANES and 3 * aw + 3 * aw + n_w + n_a + n_g == ncol
    pad = 0
    row = lambda a: a.reshape(1, -1).astype(F32)
    lane = jnp.arange(math.gcd(aw, MXU_TILE))
    bd = (lane[:, None] // HEAD_DIM == lane[None, :] // HEAD_DIM).astype(BF16)
    return dict(
        aw=aw,
        w_in=jnp.pad(w_in, ((0, 0), (0, pad))).astype(BF16),
        mu=jnp.pad(row(mu_shift), ((0, 0), (0, pad))),
        w0=row(w0), a0=row(a0), k_k=row(k_k), k_a=row(k_a), r_k=row(r_k),
        wl_pad=jnp.pad(w_lora_up, ((0, n_a), (0, 0))).astype(BF16),
        al_pad=jnp.pad(a_lora_up, ((n_w, 0), (0, 0))).astype(BF16),
        g_pad=jnp.pad(g_lora_up, ((0, pad), (0, 0))).astype(BF16),
        bd=bd,
    )


def kernel(x_prompt, x_sample, cache_k_win, cache_v_win, state_shift, state_wkv, state_conv, w_in, mu_shift, w0, w_lora_up, a0, a_lora_up, g_lora_up, k_k, k_a, r_k, gn_w, gn_b, w_out, ln1_g, ln1_b, w_ffn_in, conv_w, conv_b, w_ffn_out, ln2_g, ln2_b):
    depth = w_in.shape[0]
    alpha = (2.0 * depth) ** 0.25
    B, T, D = x_prompt.shape
    nb = x_sample.shape[0]
    assert x_sample.shape[1] == 1
    n_att = cache_k_win.shape[3]
    n_rw = state_wkv.shape[2]
    aw = n_att * HEAD_DIM
    assert n_rw * HEAD_DIM == aw and T % (Q_TILE * DIL_PATTERNS[-1][1]) == 0
    tm = min(256, T)
    tb = min(256, T)
    heads = jnp.arange(1, n_att + 1, dtype=F32)
    slopes = jnp.exp2(-8.0 * heads / n_att)
    slopes_tab = jnp.broadcast_to(slopes[:, None], (n_att, LANES))
    row = lambda a: a.reshape(1, -1)

    hp, hs = x_prompt, x_sample.reshape(nb, D)
    outs = [[] for _ in range(10)]
    for l in range(depth):
        prm = _layer_params(w_in[l], mu_shift[l], w0[l], w_lora_up[l], a0[l], a_lora_up[l], g_lora_up[l],
                            k_k[l], k_a[l], r_k[l], aw)
        late_weights = (w_out[l], w_ffn_in[l], w_ffn_out[l])
        gnw, gnb = row(gn_w[l]), row(gn_b[l])

        q, k, v, kt, vt, r, lw, k2, vr, al, be, g, bonus = _inproj_seq(hp, prm, min(2 * tm, T))
        x2 = jnp.concatenate([hs, state_shift[l]], axis=0)
        qs, ks, vs, r_s, lw_s, k2_s, al_s, be_s, col = _inproj_step(x2, prm)
        mixer_operands = (qs, ks, vs, col, cache_k_win[l], cache_v_win[l], slopes, state_wkv[l], r_s, lw_s, k2_s,
                          al_s, be_s, gn_w[l], gn_b[l])

        if nb == B * (aw // LANES):
            att, att_s, y_s, wkv_s, w_out_b, w1_b, w2_b = _prompt_attn_and_mixer_step(
                q, k, v, slopes_tab, mixer_operands, late_weights)
        else:
            att = _prompt_attn(q, k, v, slopes_tab)
            att_s, y_s, wkv_s = _mixer_step(*mixer_operands)
            w_out_b, w1_b, w2_b = (w.astype(BF16) for w in late_weights)
        y_rw, wkv_p = _rwkv_chunked(r, lw, k2, vr, al, be, g, bonus, gnw, gnb, tb)

        shift_p = hp[:, -1, :]
        hp, conv_p = _ffn_seq(att, y_rw, hp, w_out_b, row(ln1_g[l]), row(ln1_b[l]), w1_b, conv_w[l],
                              row(conv_b[l]), w2_b, row(ln2_g[l]), row(ln2_b[l]), min(2 * tm, T), alpha)
        shift_s = hs
        hs, gate_s = _ffn_step(att_s.reshape(nb, aw), y_s, hs, w_out_b, row(ln1_g[l]), row(ln1_b[l]),
                               state_conv[l][:, 0, :],
                               state_conv[l][:, 1, :], w1_b, conv_w[l], row(conv_b[l]), w2_b, row(ln2_g[l]),
                               row(ln2_b[l]), alpha)
        conv_s = jnp.stack([state_conv[l][:, 1, :], gate_s], axis=1)

        n_keep = min(DIL_PATTERNS[-1][0], T)
        win = lambda a: jnp.transpose(a.reshape(B, n_att, HEAD_DIM, T), (0, 3, 1, 2))[:, T - n_keep:]
        vals = (win(kt), win(vt), ks.reshape(nb, 1, n_att, HEAD_DIM), vs.reshape(nb, 1, n_att, HEAD_DIM),
                shift_p, shift_s, wkv_p, wkv_s, conv_p, conv_s)
        for lst, val in zip(outs, vals):
            lst.append(val)
    return (hp, hs.reshape(nb, 1, D)) + tuple(jnp.stack(lst) for lst in outs)
```

```python
import functools
import math

import jax
import jax.numpy as jnp
from jax import lax
from jax.experimental import pallas as pl
from jax.experimental.pallas import tpu as pltpu

F32 = jnp.float32
BF16 = jnp.bfloat16

HEAD_DIM = 64
LANES = 128
MXU_TILE = 256
DIL_PATTERNS = ((128, 1), (512, 4), (2048, 16))
WIN_STEPS = 128
Q_TILE = 128
GROUP = 4
SUB_TILES = 2
CHUNK = 64
LN_EPS = 1e-5
GN_EPS = 64e-5
NEG = -1e30
VMEM_LIMIT = 56 * 1024 * 1024

_DN = {"NN": (((1,), (0,)), ((), ())), "NT": (((1,), (1,)), ((), ())), "TN": (((0,), (0,)), ((), ()))}


def _dot(a, b, dims="NN"):
    return lax.dot_general(a.astype(BF16), b.astype(BF16), _DN[dims], preferred_element_type=F32)


def _dot_const_l(c, x):
    hi = x.astype(BF16)
    r1 = x - hi.astype(F32)
    mid = r1.astype(BF16)
    lo = (r1 - mid.astype(F32)).astype(BF16)
    return _dot(c, hi) + _dot(c, mid) + _dot(c, lo)


def _sigmoid(x):
    return 0.5 * jnp.tanh(0.5 * x) + 0.5


def _layer_norm(x, g, b):
    mu = jnp.mean(x, axis=-1, keepdims=True)
    xc = x - mu
    var = jnp.mean(xc * xc, axis=-1, keepdims=True)
    return xc * lax.rsqrt(var + LN_EPS) * g + b


def _cparams(n_grid):
    return pltpu.CompilerParams(dimension_semantics=("arbitrary",) * n_grid, vmem_limit_bytes=VMEM_LIMIT)


def _full(shape):
    nd = len(shape)
    return pl.BlockSpec(shape, lambda *_: (0,) * nd)


def _rwkv_prep(rw, w0, a0, k_k, k_a, r_k, wl_pad, al_pad, g_pad, bd, aw):
    r = rw[:, 0:aw]
    k = rw[:, aw:2 * aw]
    v = rw[:, 2 * aw:3 * aw]
    lo = rw[:, 3 * aw:3 * aw + 128]
    glo = rw[:, 3 * aw + 128:]
    u = w0 + _dot(jnp.tanh(lo), wl_pad)
    a = _sigmoid(a0 + _dot(lo, al_pad))
    g = _dot(_sigmoid(glo), g_pad)
    kk = k * k_k
    def head_sums(x):
        w = bd.shape[0]
        return jnp.concatenate([_dot(x[:, c:c + w], bd) for c in range(0, aw, w)], axis=1)

    sumsq = head_sums(kk * kk)
    lw = -math.exp(-0.5) * _sigmoid(u)
    k2 = k * (1.0 + (a - 1.0) * k_a)
    bonus = head_sums(r * k2 * r_k) * v
    kkn = kk * lax.rsqrt(jnp.maximum(sumsq, 1e-24))
    return r, lw, k2, v, -kkn, kkn * a, g, bonus


def _inproj_seq_kernel(x_ref, w_ref, mu_ref, w0_ref, a0_ref, kk_ref, ka_ref, rk_ref, wl_ref, al_ref, gp_ref,
                       bd_ref, q_ref, k_ref, v_ref, kt_ref, vt_ref, r_o, lw_o, k2_o, vr_o, al_o, be_o, g_o, bo_o,
                       carry_ref, *, aw):
    @pl.when(pl.program_id(1) == 0)
    def _():
        carry_ref[...] = jnp.zeros_like(carry_ref)

    tm = x_ref.shape[1]
    hr = tm // SUB_TILES
    spans = [slice(h * hr, (h + 1) * hr) for h in range(SUB_TILES)]
    params = (w0_ref[...], a0_ref[...], kk_ref[...], ka_ref[...], rk_ref[...], wl_ref[...], al_ref[...],
              gp_ref[...], bd_ref[...], aw)
    row = lax.broadcasted_iota(jnp.int32, (hr, w_ref.shape[1] - 3 * aw), 0)

    def finish(p, sp, last):
        q_ref[0, sp, :] = p[:, 0:aw]
        k = p[:, aw:2 * aw]
        v = p[:, 2 * aw:3 * aw]
        k_ref[0, sp, :] = k
        v_ref[0, sp, :] = v
        kt_ref[0, :, sp] = k.T
        vt_ref[0, :, sp] = v.T
        prw = p[:, 3 * aw:]
        prev = jnp.where(row == 0, last, pltpu.roll(prw, 1, axis=0))
        outs = _rwkv_prep(prw + (prev - prw) * mu_ref[...], *params)
        for o_ref, val in zip((r_o, lw_o, k2_o, vr_o, al_o, be_o, g_o, bo_o), outs):
            o_ref[0, sp, :] = val.astype(o_ref.dtype)
        return prw[hr - 1:hr, :]

    last = carry_ref[0:1, :]
    p = _dot(x_ref[0, spans[0], :], w_ref[...])
    for h in range(1, SUB_TILES):
        p_next = _dot(x_ref[0, spans[h], :], w_ref[...])
        last = finish(p, spans[h - 1], last)
        p = p_next
    carry_ref[0:1, :] = finish(p, spans[-1], last)


def _inproj_step_kernel(x_ref, w_ref, mu_ref, w0_ref, a0_ref, kk_ref, ka_ref, rk_ref, wl_ref, al_ref, gp_ref,
                        bd_ref, q_ref, k_ref, v_ref, rows_o, col_o, *, aw, nb):
    p = _dot(x_ref[...], w_ref[...])
    q_ref[...] = p[0:nb, 0:aw]
    k_ref[...] = p[0:nb, aw:2 * aw]
    v_ref[...] = p[0:nb, 2 * aw:3 * aw]
    prw = p[0:nb, 3 * aw:]
    prev = p[nb:2 * nb, 3 * aw:]
    rw = prw + (prev - prw) * mu_ref[...]
    r, lw, k2, vr, al, be, g, bonus = _rwkv_prep(rw, w0_ref[...], a0_ref[...], kk_ref[...], ka_ref[...],
                                                 rk_ref[...], wl_ref[...], al_ref[...], gp_ref[...], bd_ref[...], aw)
    for i, val in enumerate((r, lw, k2, al, be)):
        rows_o[:, i * aw:(i + 1) * aw] = val
    pad = jnp.zeros((LANES - nb, aw), F32)
    for i, val in enumerate((vr, g, bonus)):
        col_o[i * aw:(i + 1) * aw, :] = jnp.concatenate([val, pad], axis=0).T


_PREP_PARAMS = ("mu", "w0", "a0", "k_k", "k_a", "r_k", "wl_pad", "al_pad", "g_pad", "bd")


def _inproj_seq(x, prm, tm):
    B, T, D = x.shape
    aw = prm["aw"]
    nrw = prm["w_in"].shape[1] - 3 * aw
    small = [prm[n] for n in _PREP_PARAMS]
    row_spec = pl.BlockSpec((1, tm, aw), lambda b, t: (b, t, 0))
    col_spec = pl.BlockSpec((1, aw, tm), lambda b, t: (b, 0, t))
    row_sds = jax.ShapeDtypeStruct((B, T, aw), F32)
    col_sds = jax.ShapeDtypeStruct((B, aw, T), F32)
    return pl.pallas_call(
        functools.partial(_inproj_seq_kernel, aw=aw),
        grid=(B, T // tm),
        in_specs=[pl.BlockSpec((1, tm, D), lambda b, t: (b, t, 0)), _resident(prm["w_in"].shape)]
        + [_full(s.shape) for s in small],
        out_specs=[row_spec] * 3 + [col_spec] * 2 + [row_spec] * 8,
        out_shape=[row_sds] * 3 + [col_sds] * 2
        + [jax.ShapeDtypeStruct((B, T, aw), F32 if n == "lw" else BF16)
           for n in ("r", "lw", "k2", "v", "al", "be", "g", "bonus")],
        scratch_shapes=[pltpu.VMEM((8, nrw), F32)],
        compiler_params=_cparams(2),
        name="inproj_seq",
    )(x, prm["w_in"], *small)


def _inproj_step(x2, prm):
    nb = x2.shape[0] // 2
    aw = prm["aw"]
    small = [prm[n] for n in _PREP_PARAMS]
    assert nb <= LANES
    out_sds = jax.ShapeDtypeStruct((nb, aw), F32)
    return pl.pallas_call(
        functools.partial(_inproj_step_kernel, aw=aw, nb=nb),
        grid=(1,),
        in_specs=[_full(x2.shape), _full(prm["w_in"].shape)] + [_full(s.shape) for s in small],
        out_specs=[_full((nb, aw))] * 3 + [_full((nb, 5 * aw)), _full((3 * aw, LANES))],
        out_shape=[out_sds] * 3 + [jax.ShapeDtypeStruct((nb, 5 * aw), F32),
                                   jax.ShapeDtypeStruct((3 * aw, LANES), F32)],
        compiler_params=_cparams(1),
        name="inproj_step",
    )(x2, prm["w_in"], *small)


def _prompt_attn_kernel(q_ref, k_ref, v_ref, sl_ref, o_ref, acc_s, m_s, l_s, yq, yk, yv, yacc, ym, yl, *, T):
    hp = pl.program_id(1)
    scale = HEAD_DIM ** -0.5
    lane = lax.broadcasted_iota(jnp.int32, (1, LANES), 1)
    head_masks = ((lane < HEAD_DIM).astype(F32), (lane >= HEAD_DIM).astype(F32))
    qi = lax.broadcasted_iota(jnp.int32, (Q_TILE, Q_TILE), 0)
    ki = lax.broadcasted_iota(jnp.int32, (Q_TILE, Q_TILE), 1)
    steps_cur = (qi - ki).astype(F32)
    steps_prev = (qi - ki + Q_TILE).astype(F32)
    ok_cur = ki <= qi
    ok_prev = ki >= qi
    first_lanes = lax.broadcasted_iota(jnp.int32, (Q_TILE, LANES), 1) < HEAD_DIM
    slopes = [sl_ref[pl.ds(2 * hp + j, 1), :] for j in range(2)]

    def tile_group(src, tiles, bias, stats, first_branch):
        rq, rk, rv = src
        bias_c, bias_pc = bias
        acc_s, m_s, l_s = stats
        ids = range(len(tiles))
        q = [rq(rows) * scale for rows, _ in tiles]
        kk = [rk(rows) if prev is None else jnp.concatenate([rk(prev), rk(rows)], axis=0) for rows, prev in tiles]
        vv = [rv(rows) if prev is None else jnp.concatenate([rv(prev), rv(rows)], axis=0) for rows, prev in tiles]
        qq = [jnp.concatenate([q[t] * head_masks[0], q[t] * head_masks[1]], axis=0) for t in ids]
        s = [_dot(qq[t], kk[t], "NT") + (bias_c if tiles[t][1] is None else bias_pc) for t in ids]
        m = [jnp.max(s[t], axis=-1, keepdims=True) for t in ids]
        p = [jnp.exp(s[t] - m[t]) for t in ids]
        l = [jnp.sum(p[t], axis=-1, keepdims=True) for t in ids]
        acc = [_dot(p[t], vv[t]) for t in ids]
        for t, (rows, _) in enumerate(tiles):
            acc_n = jnp.where(first_lanes, acc[t][0:Q_TILE], acc[t][Q_TILE:])
            m_n = jnp.where(first_lanes, m[t][0:Q_TILE], m[t][Q_TILE:])
            l_n = jnp.where(first_lanes, l[t][0:Q_TILE], l[t][Q_TILE:])
            if first_branch:
                acc_s[rows, :] = acc_n
                m_s[rows, :] = m_n
                l_s[rows, :] = l_n
            else:
                m_o = m_s[rows, :]
                m_new = jnp.maximum(m_o, m_n)
                e_o = jnp.exp(m_o - m_new)
                e_n = jnp.exp(m_n - m_new)
                acc_s[rows, :] = acc_s[rows, :] * e_o + acc_n * e_n
                l_s[rows, :] = l_s[rows, :] * e_o + l_n * e_n
                m_s[rows, :] = m_new

    dils = [dil for _, dil in DIL_PATTERNS]
    R = dils[1]
    assert dils == [1, R, R * R] and all(win // dil == WIN_STEPS for win, dil in DIL_PATTERNS)

    def bias_of(dil):
        b_cur = [jnp.where(ok_cur, -(slopes[j] * float(dil)) * steps_cur, NEG) for j in range(2)]
        b_prev = [jnp.where(ok_prev, -(slopes[j] * float(dil)) * steps_prev, NEG) for j in range(2)]
        return (jnp.concatenate(b_cur, axis=0),
                jnp.concatenate([jnp.concatenate([b_prev[j], b_cur[j]], axis=1) for j in range(2)], axis=0))

    def groups(tiles):
        return [tiles[i:i + GROUP] for i in range(0, len(tiles), GROUP)]

    natural = tuple((lambda rows, ref=ref: ref[0, rows, :]) for ref in (q_ref, k_ref, v_ref))
    scratch = tuple((lambda rows, ref=ref: ref[rows, :]) for ref in (yq, yk, yv))
    nat_stats = (acc_s, m_s, l_s)
    y_stats = (yacc, ym, yl)

    nb1 = T // Q_TILE
    assert nb1 % GROUP == 0
    bias1 = bias_of(dils[0])
    tile_group(natural, [(pl.ds(jb * Q_TILE, Q_TILE), pl.ds((jb - 1) * Q_TILE, Q_TILE) if jb else None)
                         for jb in range(GROUP)], bias1, nat_stats, True)

    def later(g, carry):
        start = pl.multiple_of(g * (GROUP * Q_TILE), GROUP * Q_TILE)
        tile_group(natural, [(pl.ds(start + i * Q_TILE, Q_TILE), pl.ds(start + (i - 1) * Q_TILE, Q_TILE))
                             for i in range(GROUP)], bias1, nat_stats, True)
        return carry

    lax.fori_loop(1, nb1 // GROUP, later, 0)

    LR = T // R
    bias_r = bias_of(dils[1])
    bias_rr = bias_of(dils[2])

    def stream(s, carry):
        own = pl.ds(s, LR, stride=R)
        yq[...] = q_ref[0, own, :]
        yk[...] = k_ref[0, own, :]
        yv[...] = v_ref[0, own, :]
        for grp in groups([(pl.ds(jb * Q_TILE, Q_TILE), pl.ds((jb - 1) * Q_TILE, Q_TILE) if jb else None)
                           for jb in range(LR // Q_TILE)]):
            tile_group(scratch, grp, bias_r, y_stats, True)
        for grp in groups([(pl.ds(c + R * Q_TILE * jb, Q_TILE, stride=R),
                            pl.ds(c + R * Q_TILE * (jb - 1), Q_TILE, stride=R) if jb else None)
                           for c in range(R) for jb in range(LR // R // Q_TILE)]):
            tile_group(scratch, grp, bias_rr, y_stats, False)
        m_a, m_b = m_s[own, :], ym[...]
        m_new = jnp.maximum(m_a, m_b)
        e_a = jnp.exp(m_a - m_new)
        e_b = jnp.exp(m_b - m_new)
        acc_s[own, :] = (acc_s[own, :] * e_a + yacc[...] * e_b) / (l_s[own, :] * e_a + yl[...] * e_b)
        return carry

    lax.fori_loop(0, R, stream, 0)
    o_ref[0] = acc_s[...].astype(o_ref.dtype)


def _prompt_attn_scratch(T):
    return [pltpu.VMEM((T, LANES), F32)] * 3 + [pltpu.VMEM((T // DIL_PATTERNS[1][1], LANES), F32)] * 6


def _prompt_attn(q, k, v, slopes_tab):
    B, T, W = q.shape
    spec = pl.BlockSpec((1, T, LANES), lambda b, hp: (b, 0, hp))
    return pl.pallas_call(
        functools.partial(_prompt_attn_kernel, T=T),
        grid=(B, W // LANES),
        in_specs=[spec, spec, spec, _full(slopes_tab.shape)],
        out_specs=spec,
        out_shape=jax.ShapeDtypeStruct((B, T, W), BF16),
        scratch_shapes=_prompt_attn_scratch(T),
        compiler_params=_cparams(2),
        name="prompt_attn",
    )(q, k, v, slopes_tab)


def _rwkv_chunk_kernel(r_ref, lw_ref, k_ref, v_ref, al_ref, be_ref, g_ref, bo_ref, gnw_ref, gnb_ref,
                       y_ref, s_ref, st_scr, *, Tb, npair, nbb):
    C = CHUNK
    n2 = 2 * C
    t = pl.program_id(1)

    @pl.when(t == 0)
    def _():
        st_scr[...] = jnp.zeros_like(st_scr)

    lane = lax.broadcasted_iota(jnp.int32, (1, LANES), 1)
    m0 = (lane < HEAD_DIM).astype(F32)
    m1 = 1.0 - m0
    ri = lax.broadcasted_iota(jnp.int32, (n2, n2), 0)
    ci = lax.broadcasted_iota(jnp.int32, (n2, n2), 1)
    same = (ri < C) == (ci < C)
    strict = same & ((ri & (C - 1)) > (ci & (C - 1)))
    incl = same & ((ri & (C - 1)) >= (ci & (C - 1)))
    tri = (lax.broadcasted_iota(jnp.int32, (C, C), 0) >= lax.broadcasted_iota(jnp.int32, (C, C), 1)).astype(BF16)
    head_rows = ((lax.broadcasted_iota(jnp.int32, (n2, LANES), 0) < C)
                 == (lax.broadcasted_iota(jnp.int32, (n2, LANES), 1) < HEAD_DIM)).astype(F32)

    def stack(x):
        return jnp.concatenate([x * m0, x * m1], axis=0)

    def twice(x):
        return jnp.concatenate([x, x], axis=0)

    def body(c, carry):
        rows = pl.ds(pl.multiple_of(c * C, C), C)
        chains = [(bb, slice(hp * LANES, (hp + 1) * LANES)) for bb in range(nbb) for hp in range(npair)]
        ids = range(len(chains))
        S = [st_scr[i] for i in ids]
        lw_all = jnp.concatenate([lw_ref[bb, rows, :] for bb in range(nbb)], axis=1)
        cl_all = _dot_const_l(tri, lw_all)
        e_pos_all = jnp.exp(cl_all)
        e_neg_all = jnp.exp(-cl_all)
        e_exc_all = jnp.exp(cl_all - lw_all)
        wide = [slice(i * LANES, (i + 1) * LANES) for i in ids]
        g_end = [e_pos_all[C - 1:C, wide[i]] for i in ids]
        rt = [r_ref[bb, rows, cs] * e_pos_all[:, wide[i]] for i, (bb, cs) in enumerate(chains)]
        kt = [k_ref[bb, rows, cs] * e_neg_all[:, wide[i]] for i, (bb, cs) in enumerate(chains)]
        bt = [be_ref[bb, rows, cs] * e_neg_all[:, wide[i]] for i, (bb, cs) in enumerate(chains)]
        at = [al_ref[bb, rows, cs] * e_exc_all[:, wide[i]] for i, (bb, cs) in enumerate(chains)]
        ar_st = [jnp.concatenate([stack(at[i]), stack(rt[i])], axis=0) for i in ids]
        kb2 = [jnp.concatenate([twice(kt[i]), twice(bt[i])], axis=0) for i in ids]
        v_st = [stack(v_ref[bb, rows, cs]) for bb, cs in chains]
        gram = [_dot(ar_st[i], kb2[i], "NT") for i in ids]
        l_ab = [jnp.where(strict, gram[i][0:n2, n2:], 0.0) for i in ids]
        l_kk = [jnp.concatenate([jnp.where(strict, gram[i][0:n2, 0:n2], 0.0),
                                 jnp.where(incl, gram[i][n2:, 0:n2], 0.0)], axis=0) for i in ids]
        l_rb = [jnp.where(incl, gram[i][n2:, n2:], 0.0) for i in ids]
        from_state = [_dot(jnp.concatenate([at[i], rt[i]], axis=0), S[i], "NT") for i in ids]
        from_state = [jnp.concatenate([stack(from_state[i][0:C]), stack(from_state[i][C:])], axis=0) for i in ids]
        from_v = [_dot(l_kk[i], v_st[i]) for i in ids]
        x = l_ab
        u = [from_state[i][0:n2] + from_v[i][0:n2] for i in ids]
        for _ in range(C.bit_length() - 2):
            xu = [_dot(x[i], jnp.concatenate([x[i], u[i]], axis=1)) for i in ids]
            x = [xu[i][:, 0:n2] for i in ids]
            u = [u[i] + xu[i][:, n2:] for i in ids]
        u = [u[i] + _dot(x[i], u[i]) for i in ids]
        y_st = [from_state[i][n2:] + from_v[i][n2:] + _dot(l_rb[i], u[i]) for i in ids]
        for i in ids:
            vu = jnp.concatenate([v_st[i], u[i]], axis=0)
            kb_end = jnp.concatenate([stack(kt[i] * g_end[i]), stack(bt[i] * g_end[i])], axis=0)
            st_scr[i] = S[i] * g_end[i] + _dot(vu, kb_end, "TN")
        for i, (bb, cs) in enumerate(chains):
            mu = jnp.sum(y_st[i], axis=-1, keepdims=True) * (1.0 / HEAD_DIM)
            yc = (y_st[i] - mu) * head_rows
            var = jnp.sum(yc * yc, axis=-1, keepdims=True) * (1.0 / HEAD_DIM)
            yn_st = yc * lax.rsqrt(var + GN_EPS)
            yn_i = (yn_st[0:C, :] + yn_st[C:n2, :]) * gnw_ref[:, cs] + gnb_ref[:, cs]
            y_ref[bb, rows, cs] = ((yn_i + bo_ref[bb, rows, cs]) * g_ref[bb, rows, cs]).astype(y_ref.dtype)
        return carry

    lax.fori_loop(0, Tb // C, body, 0)

    @pl.when(t == pl.num_programs(1) - 1)
    def _():
        for i in range(nbb * npair):
            bb, hp = divmod(i, npair)
            pair = st_scr[i]
            s_ref[bb, 2 * hp] = pair[0:HEAD_DIM, 0:HEAD_DIM]
            s_ref[bb, 2 * hp + 1] = pltpu.roll(pair, HEAD_DIM, axis=1)[HEAD_DIM:, 0:HEAD_DIM]


def _rwkv_chunked(r, lw, k2, v, al, be, g, bonus, gn_w, gn_b, tb):
    B, T, W = r.shape
    npair = W // LANES
    nbb = next(n for n in (4, 2, 1) if B % n == 0)
    spec = pl.BlockSpec((nbb, tb, W), lambda b, t: (b, t, 0))
    st_spec = pl.BlockSpec((nbb, 2 * npair, HEAD_DIM, HEAD_DIM), lambda b, t: (b, 0, 0, 0))
    return pl.pallas_call(
        functools.partial(_rwkv_chunk_kernel, Tb=tb, npair=npair, nbb=nbb),
        grid=(B // nbb, T // tb),
        in_specs=[spec] * 8 + [_full(gn_w.shape), _full(gn_b.shape)],
        out_specs=[spec, st_spec],
        out_shape=[jax.ShapeDtypeStruct((B, T, W), BF16),
                   jax.ShapeDtypeStruct((B, 2 * npair, HEAD_DIM, HEAD_DIM), F32)],
        scratch_shapes=[pltpu.VMEM((nbb * npair, LANES, LANES), F32)],
        compiler_params=_cparams(2),
        name="rwkv_chunk",
    )(r, lw, k2, v, al, be, g, bonus, gn_w, gn_b)


def _rwkv_one_step(S, r, lw, k, al, be, v, g, bonus, gnw, gnb):
    sa = jnp.sum(S * al, axis=-1, keepdims=True)
    s_new = S * jnp.exp(lw) + sa * be + v * k
    y = jnp.sum(s_new * r, axis=-1, keepdims=True)
    mu = jnp.mean(y, axis=1, keepdims=True)
    yc = y - mu
    var = jnp.mean(yc * yc, axis=1, keepdims=True)
    yn = yc * lax.rsqrt(var + GN_EPS) * gnw + gnb
    return (yn + bonus) * g, s_new


def _attn_one_step(q, k_new, v_new, kt, vt, slopes, P):
    aw = q.shape[1]
    H = aw // HEAD_DIM
    scale = HEAD_DIM ** -0.5
    dist = P - lax.broadcasted_iota(jnp.int32, (1, P), 1)
    count = jnp.zeros((1, P), F32)
    for win, dil in DIL_PATTERNS:
        assert dil & (dil - 1) == 0
        count = count + ((dist <= win) & ((dist & (dil - 1)) == 0)).astype(F32)
    own_head = (lax.broadcasted_iota(jnp.int32, (H, aw), 1) // HEAD_DIM
                == lax.broadcasted_iota(jnp.int32, (H, aw), 0)).astype(F32)
    q_heads = q * own_head
    s = _dot(q_heads, kt) * scale - slopes * dist.astype(F32)
    s = jnp.where(count > 0.0, s, NEG)
    s0 = jnp.sum(q_heads * k_new, axis=-1, keepdims=True) * scale
    m = jnp.maximum(jnp.max(s, axis=-1, keepdims=True), s0)
    p = count * jnp.exp(s - m)
    e0 = len(DIL_PATTERNS) * jnp.exp(s0 - m)
    l = jnp.sum(p, axis=-1, keepdims=True) + e0
    acc = _dot(p, vt, "NT") + e0 * v_new
    return jnp.sum(acc * own_head, axis=0, keepdims=True) / jnp.sum(l * own_head, axis=0, keepdims=True)


def _mixer_step_kernel(*refs, P, aw):
    _mixer_step_body(pl.program_id(0), *refs, P=P, aw=aw)


N_MIXER_IN = 11


def _mixer_step_body(b, q_ref, kn_ref, vn_ref, col_ref, kt_ref, vt_ref, sl_ref, s_ref, rows_ref, gnw_ref, gnb_ref,
                     att_ref, y_ref, so_ref, *, P, aw):
    H = aw // HEAD_DIM
    seq_row = lambda ref: ref[pl.ds(b, 1), :]
    r, lw, k2, al, be = (rows_ref[0, i * H:(i + 1) * H] for i in range(5))
    own = (lax.broadcasted_iota(jnp.int32, (1, LANES), 1) == b).astype(F32)

    def column(i):
        return jnp.sum(col_ref[i * aw:(i + 1) * aw, :] * own, axis=-1, keepdims=True).reshape(H, HEAD_DIM, 1)

    @pl.when(b == 0)
    def _():
        y_ref[...] = jnp.zeros_like(y_ref)

    att_ref[0] = _attn_one_step(seq_row(q_ref), seq_row(kn_ref), seq_row(vn_ref), kt_ref[0].reshape(aw, P),
                                vt_ref[0].reshape(aw, P), sl_ref[...], P)
    y, s_new = _rwkv_one_step(s_ref[0], r, lw, k2, al, be, column(0), column(1),
                              column(2), gnw_ref[...], gnb_ref[...])
    so_ref[0] = s_new
    y_ref[...] += y.reshape(aw, 1) * own


def _mixer_step_operands(seq_of, q, k_new, v_new, col, k_cache, v_cache, slopes, state, rows, gn_w, gn_b):
    nb, P, H, _ = k_cache.shape
    aw = H * HEAD_DIM
    per_seq = lambda shape: pl.BlockSpec((1,) + shape, lambda *g: (seq_of(*g),) + (0,) * len(shape))
    t_spec, s_spec = per_seq((H, HEAD_DIM, P)), per_seq((H, HEAD_DIM, HEAD_DIM))
    r_spec = per_seq((5 * H, 1, HEAD_DIM))
    v_spec = per_seq((1, aw))
    p_spec = _full((H, HEAD_DIM, 1))
    args = (q, k_new, v_new, col, jnp.transpose(k_cache, (0, 2, 3, 1)),
            jnp.transpose(v_cache, (0, 2, 3, 1)), slopes.reshape(H, 1), state,
            rows.reshape(nb, 5 * H, 1, HEAD_DIM), gn_w.reshape(H, HEAD_DIM, 1), gn_b.reshape(H, HEAD_DIM, 1))
    assert len(args) == N_MIXER_IN
    in_specs = ([_full((nb, aw))] * 3 + [_full(col.shape), t_spec, t_spec, _full((H, 1)), s_spec, r_spec]
                + [p_spec] * 2)
    out_specs = [v_spec, _full((aw, LANES)), s_spec]
    out_shape = [jax.ShapeDtypeStruct((nb, 1, aw), F32), jax.ShapeDtypeStruct((aw, LANES), F32),
                 jax.ShapeDtypeStruct(state.shape, F32)]
    return args, in_specs, out_specs, out_shape


def _mixer_step(*operands):
    args, in_specs, out_specs, out_shape = _mixer_step_operands(lambda b: b, *operands)
    nb, P, H, _ = operands[4].shape
    return pl.pallas_call(
        functools.partial(_mixer_step_kernel, P=P, aw=H * HEAD_DIM),
        grid=(nb,),
        in_specs=in_specs,
        out_specs=out_specs,
        out_shape=out_shape,
        compiler_params=_cparams(1),
        name="mixer_step",
    )(*args)


def _cast_operands(step_of, n_steps, weights):
    in_specs, out_specs, out_shape = [], [], []
    for w in weights:
        rows, cols = w.shape
        per = next(d for d in range(1, n_steps + 1)
                   if n_steps % d == 0 and rows % (n_steps // d) == 0 and rows // (n_steps // d) % 16 == 0)
        spec = pl.BlockSpec((rows * per // n_steps, cols), lambda *g, per=per: (step_of(*g) // per, 0))
        in_specs.append(spec)
        out_specs.append(spec)
        out_shape.append(jax.ShapeDtypeStruct(w.shape, BF16))
    return in_specs, out_specs, out_shape


def _attn_mixer_kernel(q_ref, k_ref, v_ref, sl_ref, *refs, T, P, aw, npair, n_cast):
    n_in = N_MIXER_IN + n_cast
    mixer_in, cast_in = refs[:N_MIXER_IN], refs[N_MIXER_IN:n_in]
    o_ref, *mixer_out = refs[n_in:n_in + 4]
    cast_out, scratch = refs[n_in + 4:n_in + 4 + n_cast], refs[n_in + 4 + n_cast:]
    for src, dst in zip(cast_in, cast_out):
        dst[...] = src[...].astype(BF16)
    _mixer_step_body(pl.program_id(0) * npair + pl.program_id(1), *mixer_in, *mixer_out, P=P, aw=aw)
    _prompt_attn_kernel(q_ref, k_ref, v_ref, sl_ref, o_ref, *scratch, T=T)


def _prompt_attn_and_mixer_step(q, k, v, slopes_tab, mixer_operands, weights):
    B, T, W = q.shape
    npair = W // LANES
    step_of = lambda b, hp: b * npair + hp
    args, in_specs, out_specs, out_shape = _mixer_step_operands(step_of, *mixer_operands)
    c_in, c_out, c_shape = _cast_operands(step_of, B * npair, weights)
    nb, P, H, _ = mixer_operands[4].shape
    assert nb == B * npair
    spec = pl.BlockSpec((1, T, LANES), lambda b, hp: (b, 0, hp))
    return pl.pallas_call(
        functools.partial(_attn_mixer_kernel, T=T, P=P, aw=H * HEAD_DIM, npair=npair, n_cast=len(weights)),
        grid=(B, npair),
        in_specs=[spec, spec, spec, _full(slopes_tab.shape)] + in_specs + c_in,
        out_specs=[spec] + out_specs + c_out,
        out_shape=[jax.ShapeDtypeStruct((B, T, W), BF16)] + out_shape + c_shape,
        scratch_shapes=_prompt_attn_scratch(T),
        compiler_params=_cparams(2),
        name="prompt_attn_mixer_step",
    )(q, k, v, slopes_tab, *args, *weights)


def _ffn_tail(x1, gate_pre, prev1, prev2, up, cw_ref, cb_ref, w2_ref, g_ref, b_ref, alpha):
    c = cw_ref[0:1, :] * prev2 + cw_ref[1:2, :] * prev1 + cw_ref[2:3, :] * gate_pre + cb_ref[...]
    h = c * _sigmoid(c) * up
    return _layer_norm(alpha * x1 + _dot(h, w2_ref[...]), g_ref[...], b_ref[...])


def _ffn_seq_kernel(att_ref, y_ref, x_ref, wo_ref, g1_ref, b1_ref, w1_ref, cw_ref, cb_ref, w2_ref, g_ref, b_ref,
                    o_ref, tail_ref, carry_ref, *, aw, dff, alpha):
    @pl.when(pl.program_id(1) == 0)
    def _():
        carry_ref[...] = jnp.zeros_like(carry_ref)

    tm = x_ref.shape[1]
    hr = tm // SUB_TILES
    spans = [slice(h * hr, (h + 1) * hr) for h in range(SUB_TILES)]
    n = range(SUB_TILES)
    row = lax.broadcasted_iota(jnp.int32, (hr, dff), 0)
    mixed = [_dot(att_ref[0, sp, :], wo_ref[0:aw, :]) + _dot(y_ref[0, sp, :], wo_ref[aw:, :]) for sp in spans]
    x1 = [_layer_norm(alpha * x_ref[0, spans[i], :] + mixed[i], g1_ref[...], b1_ref[...]) for i in n]
    hu = [_dot(x1[0], w1_ref[...])]
    last2 = carry_ref[0:2, :]
    gated, f = [], []
    for i in n:
        if i + 1 < SUB_TILES:
            hu.append(_dot(x1[i + 1], w1_ref[...]))
        gate_pre = hu[i][:, 0:dff]
        c0, c1 = last2[0:1, :], last2[1:2, :]
        prev1 = jnp.where(row == 0, c1, pltpu.roll(gate_pre, 1, axis=0))
        prev2 = jnp.where(row == 0, c0, jnp.where(row == 1, c1, pltpu.roll(gate_pre, 2, axis=0)))
        last2 = gate_pre[hr - 2:hr, :]
        c = cw_ref[0:1, :] * prev2 + cw_ref[1:2, :] * prev1 + cw_ref[2:3, :] * gate_pre + cb_ref[...]
        gated.append(c * _sigmoid(c) * hu[i][:, dff:])
        if i > 0:
            f.append(_dot(gated[i - 1], w2_ref[...]))
    f.append(_dot(gated[-1], w2_ref[...]))
    carry_ref[0:2, :] = last2
    tail_ref[0] = last2
    for i in n:
        o_ref[0, spans[i], :] = _layer_norm(alpha * x1[i] + f[i], g_ref[...], b_ref[...])


def _ffn_step_kernel(att_ref, y_ref, x_ref, wo_ref, g1_ref, b1_ref, p2_ref, p1_ref, w1_ref, cw_ref, cb_ref, w2_ref,
                     g_ref, b_ref, o_ref, gate_ref, *, dff, alpha):
    nb, aw = att_ref.shape
    h = _dot(att_ref[...], wo_ref[0:aw, :]) + _dot(y_ref[...].T[0:nb, :], wo_ref[aw:, :])
    x1 = _layer_norm(alpha * x_ref[...] + h, g1_ref[...], b1_ref[...])
    hu = _dot(x1, w1_ref[...])
    gate_pre = hu[:, 0:dff]
    gate_ref[...] = gate_pre
    o_ref[...] = _ffn_tail(x1, gate_pre, p1_ref[...], p2_ref[...], hu[:, dff:], cw_ref, cb_ref, w2_ref, g_ref,
                           b_ref, alpha)


def _resident(shape):
    nd = len(shape)
    return pl.BlockSpec(shape, lambda *_: (0,) * nd, pipeline_mode=pl.Buffered(1))


def _ffn_seq(att, y_rw, x, w_out, ln1_g, ln1_b, w1, conv_w, conv_b, w2, ln2_g, ln2_b, tm, alpha):
    B, T, D = x.shape
    aw = att.shape[-1]
    dff = w2.shape[0]
    half = pl.BlockSpec((1, tm, aw), lambda b, t: (b, t, 0))
    rows = pl.BlockSpec((1, tm, D), lambda b, t: (b, t, 0))
    return pl.pallas_call(
        functools.partial(_ffn_seq_kernel, aw=aw, dff=dff, alpha=alpha),
        grid=(B, T // tm),
        in_specs=[half, half, rows, _resident(w_out.shape), _full(ln1_g.shape), _full(ln1_b.shape),
                  _resident(w1.shape), _full(conv_w.shape), _full(conv_b.shape), _resident(w2.shape),
                  _full(ln2_g.shape), _full(ln2_b.shape)],
        out_specs=[rows, pl.BlockSpec((1, 2, dff), lambda b, t: (b, 0, 0))],
        out_shape=[jax.ShapeDtypeStruct((B, T, D), F32), jax.ShapeDtypeStruct((B, 2, dff), F32)],
        scratch_shapes=[pltpu.VMEM((8, dff), F32)],
        compiler_params=_cparams(2),
        name="ffn_seq",
    )(att, y_rw, x, w_out, ln1_g, ln1_b, w1, conv_w, conv_b, w2, ln2_g, ln2_b)


def _ffn_step(att, y_cols, x, w_out, ln1_g, ln1_b, prev2, prev1, w1, conv_w, conv_b, w2, ln2_g, ln2_b, alpha):
    nb, D = x.shape
    dff = w2.shape[0]
    return pl.pallas_call(
        functools.partial(_ffn_step_kernel, dff=dff, alpha=alpha),
        grid=(1,),
        in_specs=[_full(att.shape), _full(y_cols.shape), _full(x.shape), _resident(w_out.shape), _full(ln1_g.shape),
                  _full(ln1_b.shape),
                  _full(prev2.shape), _full(prev1.shape), _resident(w1.shape), _full(conv_w.shape),
                  _full(conv_b.shape), _resident(w2.shape), _full(ln2_g.shape), _full(ln2_b.shape)],
        out_specs=[_full((nb, D)), _full((nb, dff))],
        out_shape=[jax.ShapeDtypeStruct((nb, D), F32), jax.ShapeDtypeStruct((nb, dff), F32)],
        compiler_params=_cparams(1),
        name="ffn_step",
    )(att, y_cols, x, w_out, ln1_g, ln1_b, prev2, prev1, w1, conv_w, conv_b, w2, ln2_g, ln2_b)


def _layer_params(w_in, mu_shift, w0, w_lora_up, a0, a_lora_up, g_lora_up, k_k, k_a, r_k, aw):
    D, ncol = w_in.shape
    n_w, n_a, n_g = w_lora_up.shape[0], a_lora_up.shape[0], g_lora_up.shape[0]
    assert n_w + n_a == LANES and 3 * aw + 3 * aw + n_w + n_a + n_g == ncol
    pad = 0
    row = lambda a: a.reshape(1, -1).astype(F32)
    lane = jnp.arange(math.gcd(aw, MXU_TILE))
    bd = (lane[:, None] // HEAD_DIM == lane[None, :] // HEAD_DIM).astype(BF16)
    return dict(
        aw=aw,
        w_in=jnp.pad(w_in, ((0, 0), (0, pad))).astype(BF16),
        mu=jnp.pad(row(mu_shift), ((0, 0), (0, pad))),
        w0=row(w0), a0=row(a0), k_k=row(k_k), k_a=row(k_a), r_k=row(r_k),
        wl_pad=jnp.pad(w_lora_up, ((0, n_a), (0, 0))).astype(BF16),
        al_pad=jnp.pad(a_lora_up, ((n_w, 0), (0, 0))).astype(BF16),
        g_pad=jnp.pad(g_lora_up, ((0, pad), (0, 0))).astype(BF16),
        bd=bd,
    )


def kernel(x_prompt, x_sample, cache_k_win, cache_v_win, state_shift, state_wkv, state_conv, w_in, mu_shift, w0, w_lora_up, a0, a_lora_up, g_lora_up, k_k, k_a, r_k, gn_w, gn_b, w_out, ln1_g, ln1_b, w_ffn_in, conv_w, conv_b, w_ffn_out, ln2_g, ln2_b):
    depth = w_in.shape[0]
    alpha = (2.0 * depth) ** 0.25
    B, T, D = x_prompt.shape
    nb = x_sample.shape[0]
    assert x_sample.shape[1] == 1
    n_att = cache_k_win.shape[3]
    n_rw = state_wkv.shape[2]
    aw = n_att * HEAD_DIM
    assert n_rw * HEAD_DIM == aw and T % (Q_TILE * DIL_PATTERNS[-1][1]) == 0
    tm = min(256, T)
    tb = min(256, T)
    heads = jnp.arange(1, n_att + 1, dtype=F32)
    slopes = jnp.exp2(-8.0 * heads / n_att)
    slopes_tab = jnp.broadcast_to(slopes[:, None], (n_att, LANES))
    row = lambda a: a.reshape(1, -1)

    hp, hs = x_prompt, x_sample.reshape(nb, D)
    outs = [[] for _ in range(10)]
    for l in range(depth):
        prm = _layer_params(w_in[l], mu_shift[l], w0[l], w_lora_up[l], a0[l], a_lora_up[l], g_lora_up[l],
                            k_k[l], k_a[l], r_k[l], aw)
        late_weights = (w_out[l], w_ffn_in[l], w_ffn_out[l])
        gnw, gnb = row(gn_w[l]), row(gn_b[l])

        q, k, v, kt, vt, r, lw, k2, vr, al, be, g, bonus = _inproj_seq(hp, prm, min(2 * tm, T))
        x2 = jnp.concatenate([hs, state_shift[l]], axis=0)
        qs, ks, vs, rows_s, col = _inproj_step(x2, prm)
        mixer_operands = (qs, ks, vs, col, cache_k_win[l], cache_v_win[l], slopes, state_wkv[l], rows_s,
                          gn_w[l], gn_b[l])

        if nb == B * (aw // LANES):
            att, att_s, y_s, wkv_s, w_out_b, w1_b, w2_b = _prompt_attn_and_mixer_step(
                q, k, v, slopes_tab, mixer_operands, late_weights)
        else:
            att = _prompt_attn(q, k, v, slopes_tab)
            att_s, y_s, wkv_s = _mixer_step(*mixer_operands)
            w_out_b, w1_b, w2_b = (w.astype(BF16) for w in late_weights)
        y_rw, wkv_p = _rwkv_chunked(r, lw, k2, vr, al, be, g, bonus, gnw, gnb, tb)

        shift_p = hp[:, -1, :]
        hp, conv_p = _ffn_seq(att, y_rw, hp, w_out_b, row(ln1_g[l]), row(ln1_b[l]), w1_b, conv_w[l],
                              row(conv_b[l]), w2_b, row(ln2_g[l]), row(ln2_b[l]), min(2 * tm, T), alpha)
        shift_s = hs
        hs, gate_s = _ffn_step(att_s.reshape(nb, aw), y_s, hs, w_out_b, row(ln1_g[l]), row(ln1_b[l]),
                               state_conv[l][:, 0, :],
                               state_conv[l][:, 1, :], w1_b, conv_w[l], row(conv_b[l]), w2_b, row(ln2_g[l]),
                               row(ln2_b[l]), alpha)
        conv_s = jnp.stack([state_conv[l][:, 1, :], gate_s], axis=1)

        n_keep = min(DIL_PATTERNS[-1][0], T)
        win = lambda a: jnp.transpose(a.reshape(B, n_att, HEAD_DIM, T), (0, 3, 1, 2))[:, T - n_keep:]
        vals = (win(kt), win(vt), ks.reshape(nb, 1, n_att, HEAD_DIM), vs.reshape(nb, 1, n_att, HEAD_DIM),
                shift_p, shift_s, wkv_p, wkv_s, conv_p, conv_s)
        for lst, val in zip(outs, vals):
            lst.append(val)
    return (hp, hs.reshape(nb, 1, D)) + tuple(jnp.stack(lst) for lst in outs)
```

```python
import functools
import math

import jax
import jax.numpy as jnp
from jax import lax
from jax.experimental import pallas as pl
from jax.experimental.pallas import tpu as pltpu

F32 = jnp.float32
BF16 = jnp.bfloat16

HEAD_DIM = 64
LANES = 128
MXU_TILE = 256
DIL_PATTERNS = ((128, 1), (512, 4), (2048, 16))
WIN_STEPS = 128
Q_TILE = 128
GROUP = 4
SUB_TILES = 2
CHUNK = 64
LN_EPS = 1e-5
GN_EPS = 64e-5
NEG = -1e30
VMEM_LIMIT = 56 * 1024 * 1024

_DN = {"NN": (((1,), (0,)), ((), ())), "NT": (((1,), (1,)), ((), ())), "TN": (((0,), (0,)), ((), ()))}


def _dot(a, b, dims="NN"):
    return lax.dot_general(a.astype(BF16), b.astype(BF16), _DN[dims], preferred_element_type=F32)


def _dot_const_l(c, x):
    hi = x.astype(BF16)
    r1 = x - hi.astype(F32)
    mid = r1.astype(BF16)
    lo = (r1 - mid.astype(F32)).astype(BF16)
    return _dot(c, hi) + _dot(c, mid) + _dot(c, lo)


def _sigmoid(x):
    return 0.5 * jnp.tanh(0.5 * x) + 0.5


def _layer_norm(x, g, b):
    mu = jnp.mean(x, axis=-1, keepdims=True)
    xc = x - mu
    var = jnp.mean(xc * xc, axis=-1, keepdims=True)
    return xc * lax.rsqrt(var + LN_EPS) * g + b


def _cparams(n_grid):
    return pltpu.CompilerParams(dimension_semantics=("arbitrary",) * n_grid, vmem_limit_bytes=VMEM_LIMIT)


def _full(shape):
    nd = len(shape)
    return pl.BlockSpec(shape, lambda *_: (0,) * nd)


def _rwkv_prep(rw, w0, a0, k_k, k_a, r_k, wl_pad, al_pad, g_pad, bd, aw):
    r = rw[:, 0:aw]
    k = rw[:, aw:2 * aw]
    v = rw[:, 2 * aw:3 * aw]
    lo = rw[:, 3 * aw:3 * aw + 128]
    glo = rw[:, 3 * aw + 128:]
    u = w0 + _dot(jnp.tanh(lo), wl_pad)
    a = _sigmoid(a0 + _dot(lo, al_pad))
    g = _dot(_sigmoid(glo), g_pad)
    kk = k * k_k
    def head_sums(x):
        w = bd.shape[0]
        return jnp.concatenate([_dot(x[:, c:c + w], bd) for c in range(0, aw, w)], axis=1)

    sumsq = head_sums(kk * kk)
    lw = -math.exp(-0.5) * _sigmoid(u)
    k2 = k * (1.0 + (a - 1.0) * k_a)
    bonus = head_sums(r * k2 * r_k) * v
    kkn = kk * lax.rsqrt(jnp.maximum(sumsq, 1e-24))
    return r, lw, k2, v, -kkn, kkn * a, g, bonus


def _inproj_seq_kernel(x_ref, w_ref, mu_ref, w0_ref, a0_ref, kk_ref, ka_ref, rk_ref, wl_ref, al_ref, gp_ref,
                       bd_ref, q_ref, k_ref, v_ref, kt_ref, vt_ref, r_o, lw_o, k2_o, vr_o, al_o, be_o, g_o, bo_o,
                       carry_ref, *, aw):
    @pl.when(pl.program_id(1) == 0)
    def _():
        carry_ref[...] = jnp.zeros_like(carry_ref)

    tm = x_ref.shape[1]
    hr = tm // SUB_TILES
    spans = [slice(h * hr, (h + 1) * hr) for h in range(SUB_TILES)]
    params = (w0_ref[...], a0_ref[...], kk_ref[...], ka_ref[...], rk_ref[...], wl_ref[...], al_ref[...],
              gp_ref[...], bd_ref[...], aw)
    row = lax.broadcasted_iota(jnp.int32, (hr, w_ref.shape[1] - 3 * aw), 0)

    def finish(p, sp, last):
        q_ref[0, sp, :] = p[:, 0:aw]
        k = p[:, aw:2 * aw]
        v = p[:, 2 * aw:3 * aw]
        k_ref[0, sp, :] = k
        v_ref[0, sp, :] = v
        kt_ref[0, :, sp] = k.T
        vt_ref[0, :, sp] = v.T
        prw = p[:, 3 * aw:]
        prev = jnp.where(row == 0, last, pltpu.roll(prw, 1, axis=0))
        outs = _rwkv_prep(prw + (prev - prw) * mu_ref[...], *params)
        for o_ref, val in zip((r_o, lw_o, k2_o, vr_o, al_o, be_o, g_o, bo_o), outs):
            o_ref[0, sp, :] = val.astype(o_ref.dtype)
        return prw[hr - 1:hr, :]

    last = carry_ref[0:1, :]
    p = _dot(x_ref[0, spans[0], :], w_ref[...])
    for h in range(1, SUB_TILES):
        p_next = _dot(x_ref[0, spans[h], :], w_ref[...])
        last = finish(p, spans[h - 1], last)
        p = p_next
    carry_ref[0:1, :] = finish(p, spans[-1], last)


def _inproj_step_kernel(x_ref, w_ref, mu_ref, w0_ref, a0_ref, kk_ref, ka_ref, rk_ref, wl_ref, al_ref, gp_ref,
                        bd_ref, qkv_o, k_ref, v_ref, rows_o, col_o, *, aw, nb):
    p = _dot(x_ref[...], w_ref[...])
    qkv_o[...] = p[0:nb, 0:3 * aw]
    k_ref[...] = p[0:nb, aw:2 * aw]
    v_ref[...] = p[0:nb, 2 * aw:3 * aw]
    prw = p[0:nb, 3 * aw:]
    prev = p[nb:2 * nb, 3 * aw:]
    rw = prw + (prev - prw) * mu_ref[...]
    r, lw, k2, vr, al, be, g, bonus = _rwkv_prep(rw, w0_ref[...], a0_ref[...], kk_ref[...], ka_ref[...],
                                                 rk_ref[...], wl_ref[...], al_ref[...], gp_ref[...], bd_ref[...], aw)
    for i, val in enumerate((r, lw, k2, al, be)):
        rows_o[:, i * aw:(i + 1) * aw] = val
    pad = jnp.zeros((LANES - nb, aw), F32)
    for i, val in enumerate((vr, g, bonus)):
        col_o[i * aw:(i + 1) * aw, :] = jnp.concatenate([val, pad], axis=0).T


_PREP_PARAMS = ("mu", "w0", "a0", "k_k", "k_a", "r_k", "wl_pad", "al_pad", "g_pad", "bd")


def _inproj_seq(x, prm, tm):
    B, T, D = x.shape
    aw = prm["aw"]
    nrw = prm["w_in"].shape[1] - 3 * aw
    small = [prm[n] for n in _PREP_PARAMS]
    row_spec = pl.BlockSpec((1, tm, aw), lambda b, t: (b, t, 0))
    col_spec = pl.BlockSpec((1, aw, tm), lambda b, t: (b, 0, t))
    row_sds = jax.ShapeDtypeStruct((B, T, aw), F32)
    col_sds = jax.ShapeDtypeStruct((B, aw, T), F32)
    return pl.pallas_call(
        functools.partial(_inproj_seq_kernel, aw=aw),
        grid=(B, T // tm),
        in_specs=[pl.BlockSpec((1, tm, D), lambda b, t: (b, t, 0)), _resident(prm["w_in"].shape)]
        + [_full(s.shape) for s in small],
        out_specs=[row_spec] * 3 + [col_spec] * 2 + [row_spec] * 8,
        out_shape=[row_sds] * 3 + [col_sds] * 2
        + [jax.ShapeDtypeStruct((B, T, aw), F32 if n == "lw" else BF16)
           for n in ("r", "lw", "k2", "v", "al", "be", "g", "bonus")],
        scratch_shapes=[pltpu.VMEM((8, nrw), F32)],
        compiler_params=_cparams(2),
        name="inproj_seq",
    )(x, prm["w_in"], *small)


def _inproj_step(x2, prm):
    nb = x2.shape[0] // 2
    aw = prm["aw"]
    small = [prm[n] for n in _PREP_PARAMS]
    assert nb <= LANES
    out_sds = jax.ShapeDtypeStruct((nb, aw), F32)
    return pl.pallas_call(
        functools.partial(_inproj_step_kernel, aw=aw, nb=nb),
        grid=(1,),
        in_specs=[_full(x2.shape), _full(prm["w_in"].shape)] + [_full(s.shape) for s in small],
        out_specs=[_full((nb, 3 * aw))] + [_full((nb, aw))] * 2 + [_full((nb, 5 * aw)), _full((3 * aw, LANES))],
        out_shape=[jax.ShapeDtypeStruct((nb, 3 * aw), F32)] + [out_sds] * 2
        + [jax.ShapeDtypeStruct((nb, 5 * aw), F32), jax.ShapeDtypeStruct((3 * aw, LANES), F32)],
        compiler_params=_cparams(1),
        name="inproj_step",
    )(x2, prm["w_in"], *small)


def _prompt_attn_kernel(q_ref, k_ref, v_ref, sl_ref, o_ref, acc_s, m_s, l_s, yq, yk, yv, yacc, ym, yl, *, T):
    hp = pl.program_id(1)
    scale = HEAD_DIM ** -0.5
    lane = lax.broadcasted_iota(jnp.int32, (1, LANES), 1)
    head_masks = ((lane < HEAD_DIM).astype(F32), (lane >= HEAD_DIM).astype(F32))
    qi = lax.broadcasted_iota(jnp.int32, (Q_TILE, Q_TILE), 0)
    ki = lax.broadcasted_iota(jnp.int32, (Q_TILE, Q_TILE), 1)
    steps_cur = (qi - ki).astype(F32)
    steps_prev = (qi - ki + Q_TILE).astype(F32)
    ok_cur = ki <= qi
    ok_prev = ki >= qi
    first_lanes = lax.broadcasted_iota(jnp.int32, (Q_TILE, LANES), 1) < HEAD_DIM
    slopes = [sl_ref[pl.ds(2 * hp + j, 1), :] for j in range(2)]

    def tile_group(src, tiles, bias, stats, first_branch):
        rq, rk, rv = src
        bias_c, bias_pc = bias
        acc_s, m_s, l_s = stats
        ids = range(len(tiles))
        q = [rq(rows) * scale for rows, _ in tiles]
        kk = [rk(rows) if prev is None else jnp.concatenate([rk(prev), rk(rows)], axis=0) for rows, prev in tiles]
        vv = [rv(rows) if prev is None else jnp.concatenate([rv(prev), rv(rows)], axis=0) for rows, prev in tiles]
        qq = [jnp.concatenate([q[t] * head_masks[0], q[t] * head_masks[1]], axis=0) for t in ids]
        s = [_dot(qq[t], kk[t], "NT") + (bias_c if tiles[t][1] is None else bias_pc) for t in ids]
        m = [jnp.max(s[t], axis=-1, keepdims=True) for t in ids]
        p = [jnp.exp(s[t] - m[t]) for t in ids]
        l = [jnp.sum(p[t], axis=-1, keepdims=True) for t in ids]
        acc = [_dot(p[t], vv[t]) for t in ids]
        for t, (rows, _) in enumerate(tiles):
            acc_n = jnp.where(first_lanes, acc[t][0:Q_TILE], acc[t][Q_TILE:])
            m_n = jnp.where(first_lanes, m[t][0:Q_TILE], m[t][Q_TILE:])
            l_n = jnp.where(first_lanes, l[t][0:Q_TILE], l[t][Q_TILE:])
            if first_branch:
                acc_s[rows, :] = acc_n
                m_s[rows, :] = m_n
                l_s[rows, :] = l_n
            else:
                m_o = m_s[rows, :]
                m_new = jnp.maximum(m_o, m_n)
                e_o = jnp.exp(m_o - m_new)
                e_n = jnp.exp(m_n - m_new)
                acc_s[rows, :] = acc_s[rows, :] * e_o + acc_n * e_n
                l_s[rows, :] = l_s[rows, :] * e_o + l_n * e_n
                m_s[rows, :] = m_new

    dils = [dil for _, dil in DIL_PATTERNS]
    R = dils[1]
    assert dils == [1, R, R * R] and all(win // dil == WIN_STEPS for win, dil in DIL_PATTERNS)

    def bias_of(dil):
        b_cur = [jnp.where(ok_cur, -(slopes[j] * float(dil)) * steps_cur, NEG) for j in range(2)]
        b_prev = [jnp.where(ok_prev, -(slopes[j] * float(dil)) * steps_prev, NEG) for j in range(2)]
        return (jnp.concatenate(b_cur, axis=0),
                jnp.concatenate([jnp.concatenate([b_prev[j], b_cur[j]], axis=1) for j in range(2)], axis=0))

    def groups(tiles):
        return [tiles[i:i + GROUP] for i in range(0, len(tiles), GROUP)]

    natural = tuple((lambda rows, ref=ref: ref[0, rows, :]) for ref in (q_ref, k_ref, v_ref))
    scratch = tuple((lambda rows, ref=ref: ref[rows, :]) for ref in (yq, yk, yv))
    nat_stats = (acc_s, m_s, l_s)
    y_stats = (yacc, ym, yl)

    nb1 = T // Q_TILE
    assert nb1 % GROUP == 0
    bias1 = bias_of(dils[0])
    tile_group(natural, [(pl.ds(jb * Q_TILE, Q_TILE), pl.ds((jb - 1) * Q_TILE, Q_TILE) if jb else None)
                         for jb in range(GROUP)], bias1, nat_stats, True)

    def later(g, carry):
        start = pl.multiple_of(g * (GROUP * Q_TILE), GROUP * Q_TILE)
        tile_group(natural, [(pl.ds(start + i * Q_TILE, Q_TILE), pl.ds(start + (i - 1) * Q_TILE, Q_TILE))
                             for i in range(GROUP)], bias1, nat_stats, True)
        return carry

    lax.fori_loop(1, nb1 // GROUP, later, 0)

    LR = T // R
    bias_r = bias_of(dils[1])
    bias_rr = bias_of(dils[2])

    def stream(s, carry):
        own = pl.ds(s, LR, stride=R)
        yq[...] = q_ref[0, own, :]
        yk[...] = k_ref[0, own, :]
        yv[...] = v_ref[0, own, :]
        for grp in groups([(pl.ds(jb * Q_TILE, Q_TILE), pl.ds((jb - 1) * Q_TILE, Q_TILE) if jb else None)
                           for jb in range(LR // Q_TILE)]):
            tile_group(scratch, grp, bias_r, y_stats, True)
        for grp in groups([(pl.ds(c + R * Q_TILE * jb, Q_TILE, stride=R),
                            pl.ds(c + R * Q_TILE * (jb - 1), Q_TILE, stride=R) if jb else None)
                           for c in range(R) for jb in range(LR // R // Q_TILE)]):
            tile_group(scratch, grp, bias_rr, y_stats, False)
        m_a, m_b = m_s[own, :], ym[...]
        m_new = jnp.maximum(m_a, m_b)
        e_a = jnp.exp(m_a - m_new)
        e_b = jnp.exp(m_b - m_new)
        acc_s[own, :] = (acc_s[own, :] * e_a + yacc[...] * e_b) / (l_s[own, :] * e_a + yl[...] * e_b)
        return carry

    lax.fori_loop(0, R, stream, 0)
    o_ref[0] = acc_s[...].astype(o_ref.dtype)


def _prompt_attn_scratch(T):
    return [pltpu.VMEM((T, LANES), F32)] * 3 + [pltpu.VMEM((T // DIL_PATTERNS[1][1], LANES), F32)] * 6


def _prompt_attn(q, k, v, slopes_tab):
    B, T, W = q.shape
    spec = pl.BlockSpec((1, T, LANES), lambda b, hp: (b, 0, hp))
    return pl.pallas_call(
        functools.partial(_prompt_attn_kernel, T=T),
        grid=(B, W // LANES),
        in_specs=[spec, spec, spec, _full(slopes_tab.shape)],
        out_specs=spec,
        out_shape=jax.ShapeDtypeStruct((B, T, W), BF16),
        scratch_shapes=_prompt_attn_scratch(T),
        compiler_params=_cparams(2),
        name="prompt_attn",
    )(q, k, v, slopes_tab)


def _rwkv_chunk_kernel(r_ref, lw_ref, k_ref, v_ref, al_ref, be_ref, g_ref, bo_ref, gnw_ref, gnb_ref,
                       y_ref, s_ref, st_scr, *, Tb, npair, nbb):
    C = CHUNK
    n2 = 2 * C
    t = pl.program_id(1)

    @pl.when(t == 0)
    def _():
        st_scr[...] = jnp.zeros_like(st_scr)

    lane = lax.broadcasted_iota(jnp.int32, (1, LANES), 1)
    m0 = (lane < HEAD_DIM).astype(F32)
    m1 = 1.0 - m0
    ri = lax.broadcasted_iota(jnp.int32, (n2, n2), 0)
    ci = lax.broadcasted_iota(jnp.int32, (n2, n2), 1)
    same = (ri < C) == (ci < C)
    strict = same & ((ri & (C - 1)) > (ci & (C - 1)))
    incl = same & ((ri & (C - 1)) >= (ci & (C - 1)))
    tri = (lax.broadcasted_iota(jnp.int32, (C, C), 0) >= lax.broadcasted_iota(jnp.int32, (C, C), 1)).astype(BF16)
    head_rows = ((lax.broadcasted_iota(jnp.int32, (n2, LANES), 0) < C)
                 == (lax.broadcasted_iota(jnp.int32, (n2, LANES), 1) < HEAD_DIM)).astype(F32)

    def stack(x):
        return jnp.concatenate([x * m0, x * m1], axis=0)

    def twice(x):
        return jnp.concatenate([x, x], axis=0)

    def body(c, carry):
        rows = pl.ds(pl.multiple_of(c * C, C), C)
        chains = [(bb, slice(hp * LANES, (hp + 1) * LANES)) for bb in range(nbb) for hp in range(npair)]
        ids = range(len(chains))
        S = [st_scr[i] for i in ids]
        lw_all = jnp.concatenate([lw_ref[bb, rows, :] for bb in range(nbb)], axis=1)
        cl_all = _dot_const_l(tri, lw_all)
        e_pos_all = jnp.exp(cl_all)
        e_neg_all = jnp.exp(-cl_all)
        e_exc_all = jnp.exp(cl_all - lw_all)
        wide = [slice(i * LANES, (i + 1) * LANES) for i in ids]
        g_end = [e_pos_all[C - 1:C, wide[i]] for i in ids]
        rt = [r_ref[bb, rows, cs] * e_pos_all[:, wide[i]] for i, (bb, cs) in enumerate(chains)]
        kt = [k_ref[bb, rows, cs] * e_neg_all[:, wide[i]] for i, (bb, cs) in enumerate(chains)]
        bt = [be_ref[bb, rows, cs] * e_neg_all[:, wide[i]] for i, (bb, cs) in enumerate(chains)]
        at = [al_ref[bb, rows, cs] * e_exc_all[:, wide[i]] for i, (bb, cs) in enumerate(chains)]
        ar_st = [jnp.concatenate([stack(at[i]), stack(rt[i])], axis=0) for i in ids]
        kb2 = [jnp.concatenate([twice(kt[i]), twice(bt[i])], axis=0) for i in ids]
        v_st = [stack(v_ref[bb, rows, cs]) for bb, cs in chains]
        gram = [_dot(ar_st[i], kb2[i], "NT") for i in ids]
        l_ab = [jnp.where(strict, gram[i][0:n2, n2:], 0.0) for i in ids]
        l_kk = [jnp.concatenate([jnp.where(strict, gram[i][0:n2, 0:n2], 0.0),
                                 jnp.where(incl, gram[i][n2:, 0:n2], 0.0)], axis=0) for i in ids]
        l_rb = [jnp.where(incl, gram[i][n2:, n2:], 0.0) for i in ids]
        from_state = [_dot(jnp.concatenate([at[i], rt[i]], axis=0), S[i], "NT") for i in ids]
        from_state = [jnp.concatenate([stack(from_state[i][0:C]), stack(from_state[i][C:])], axis=0) for i in ids]
        from_v = [_dot(l_kk[i], v_st[i]) for i in ids]
        x = l_ab
        u = [from_state[i][0:n2] + from_v[i][0:n2] for i in ids]
        for _ in range(C.bit_length() - 2):
            xu = [_dot(x[i], jnp.concatenate([x[i], u[i]], axis=1)) for i in ids]
            x = [xu[i][:, 0:n2] for i in ids]
            u = [u[i] + xu[i][:, n2:] for i in ids]
        u = [u[i] + _dot(x[i], u[i]) for i in ids]
        y_st = [from_state[i][n2:] + from_v[i][n2:] + _dot(l_rb[i], u[i]) for i in ids]
        for i in ids:
            vu = jnp.concatenate([v_st[i], u[i]], axis=0)
            kb_end = jnp.concatenate([stack(kt[i] * g_end[i]), stack(bt[i] * g_end[i])], axis=0)
            st_scr[i] = S[i] * g_end[i] + _dot(vu, kb_end, "TN")
        for i, (bb, cs) in enumerate(chains):
            mu = jnp.sum(y_st[i], axis=-1, keepdims=True) * (1.0 / HEAD_DIM)
            yc = (y_st[i] - mu) * head_rows
            var = jnp.sum(yc * yc, axis=-1, keepdims=True) * (1.0 / HEAD_DIM)
            yn_st = yc * lax.rsqrt(var + GN_EPS)
            yn_i = (yn_st[0:C, :] + yn_st[C:n2, :]) * gnw_ref[:, cs] + gnb_ref[:, cs]
            y_ref[bb, rows, cs] = ((yn_i + bo_ref[bb, rows, cs]) * g_ref[bb, rows, cs]).astype(y_ref.dtype)
        return carry

    lax.fori_loop(0, Tb // C, body, 0)

    @pl.when(t == pl.num_programs(1) - 1)
    def _():
        for i in range(nbb * npair):
            bb, hp = divmod(i, npair)
            pair = st_scr[i]
            s_ref[bb, 2 * hp] = pair[0:HEAD_DIM, 0:HEAD_DIM]
            s_ref[bb, 2 * hp + 1] = pltpu.roll(pair, HEAD_DIM, axis=1)[HEAD_DIM:, 0:HEAD_DIM]


def _rwkv_chunked(r, lw, k2, v, al, be, g, bonus, gn_w, gn_b, tb):
    B, T, W = r.shape
    npair = W // LANES
    nbb = next(n for n in (4, 2, 1) if B % n == 0)
    spec = pl.BlockSpec((nbb, tb, W), lambda b, t: (b, t, 0))
    st_spec = pl.BlockSpec((nbb, 2 * npair, HEAD_DIM, HEAD_DIM), lambda b, t: (b, 0, 0, 0))
    return pl.pallas_call(
        functools.partial(_rwkv_chunk_kernel, Tb=tb, npair=npair, nbb=nbb),
        grid=(B // nbb, T // tb),
        in_specs=[spec] * 8 + [_full(gn_w.shape), _full(gn_b.shape)],
        out_specs=[spec, st_spec],
        out_shape=[jax.ShapeDtypeStruct((B, T, W), BF16),
                   jax.ShapeDtypeStruct((B, 2 * npair, HEAD_DIM, HEAD_DIM), F32)],
        scratch_shapes=[pltpu.VMEM((nbb * npair, LANES, LANES), F32)],
        compiler_params=_cparams(2),
        name="rwkv_chunk",
    )(r, lw, k2, v, al, be, g, bonus, gn_w, gn_b)


def _rwkv_one_step(S, r, lw, k, al, be, v, g, bonus, gnw, gnb):
    sa = jnp.sum(S * al, axis=-1, keepdims=True)
    s_new = S * jnp.exp(lw) + sa * be + v * k
    y = jnp.sum(s_new * r, axis=-1, keepdims=True)
    mu = jnp.mean(y, axis=1, keepdims=True)
    yc = y - mu
    var = jnp.mean(yc * yc, axis=1, keepdims=True)
    yn = yc * lax.rsqrt(var + GN_EPS) * gnw + gnb
    return (yn + bonus) * g, s_new


def _attn_one_step(q, k_new, v_new, kt, vt, slopes, P):
    aw = q.shape[1]
    H = aw // HEAD_DIM
    scale = HEAD_DIM ** -0.5
    dist = P - lax.broadcasted_iota(jnp.int32, (1, P), 1)
    count = jnp.zeros((1, P), F32)
    for win, dil in DIL_PATTERNS:
        assert dil & (dil - 1) == 0
        count = count + ((dist <= win) & ((dist & (dil - 1)) == 0)).astype(F32)
    own_head = (lax.broadcasted_iota(jnp.int32, (H, aw), 1) // HEAD_DIM
                == lax.broadcasted_iota(jnp.int32, (H, aw), 0)).astype(F32)
    q_heads = q * own_head
    s = _dot(q_heads, kt) * scale - slopes * dist.astype(F32)
    s = jnp.where(count > 0.0, s, NEG)
    s0 = jnp.sum(q_heads * k_new, axis=-1, keepdims=True) * scale
    m = jnp.maximum(jnp.max(s, axis=-1, keepdims=True), s0)
    p = count * jnp.exp(s - m)
    e0 = len(DIL_PATTERNS) * jnp.exp(s0 - m)
    l = jnp.sum(p, axis=-1, keepdims=True) + e0
    acc = _dot(p, vt, "NT") + e0 * v_new
    return jnp.sum(acc * own_head, axis=0, keepdims=True) / jnp.sum(l * own_head, axis=0, keepdims=True)


def _mixer_step_kernel(*refs, P, aw):
    _mixer_step_body(pl.program_id(0), *refs, P=P, aw=aw)


N_MIXER_IN = 9


def _mixer_step_body(b, qkv_ref, col_ref, kt_ref, vt_ref, sl_ref, s_ref, rows_ref, gnw_ref, gnb_ref,
                     att_ref, y_ref, so_ref, *, P, aw):
    H = aw // HEAD_DIM
    q, k_new, v_new = (qkv_ref[0, :, i * aw:(i + 1) * aw] for i in range(3))
    r, lw, k2, al, be = (rows_ref[0, i * H:(i + 1) * H] for i in range(5))
    own = (lax.broadcasted_iota(jnp.int32, (1, LANES), 1) == b).astype(F32)

    def column(i):
        return jnp.sum(col_ref[i * aw:(i + 1) * aw, :] * own, axis=-1, keepdims=True).reshape(H, HEAD_DIM, 1)

    @pl.when(b == 0)
    def _():
        y_ref[...] = jnp.zeros_like(y_ref)

    att_ref[0] = _attn_one_step(q, k_new, v_new, kt_ref[0].reshape(aw, P),
                                vt_ref[0].reshape(aw, P), sl_ref[...], P)
    y, s_new = _rwkv_one_step(s_ref[0], r, lw, k2, al, be, column(0), column(1),
                              column(2), gnw_ref[...], gnb_ref[...])
    so_ref[0] = s_new
    y_ref[...] += y.reshape(aw, 1) * own


def _mixer_step_operands(seq_of, qkv, col, k_cache, v_cache, slopes, state, rows, gn_w, gn_b):
    nb, P, H, _ = k_cache.shape
    aw = H * HEAD_DIM
    per_seq = lambda shape: pl.BlockSpec((1,) + shape, lambda *g: (seq_of(*g),) + (0,) * len(shape))
    t_spec, s_spec = per_seq((H, HEAD_DIM, P)), per_seq((H, HEAD_DIM, HEAD_DIM))
    r_spec = per_seq((5 * H, 1, HEAD_DIM))
    v_spec = per_seq((1, aw))
    p_spec = _full((H, HEAD_DIM, 1))
    args = (qkv.reshape(nb, 1, 3 * aw), col, jnp.transpose(k_cache, (0, 2, 3, 1)),
            jnp.transpose(v_cache, (0, 2, 3, 1)), slopes.reshape(H, 1), state,
            rows.reshape(nb, 5 * H, 1, HEAD_DIM), gn_w.reshape(H, HEAD_DIM, 1), gn_b.reshape(H, HEAD_DIM, 1))
    assert len(args) == N_MIXER_IN
    in_specs = ([per_seq((1, 3 * aw)), _full(col.shape), t_spec, t_spec, _full((H, 1)), s_spec, r_spec]
                + [p_spec] * 2)
    out_specs = [v_spec, _full((aw, LANES)), s_spec]
    out_shape = [jax.ShapeDtypeStruct((nb, 1, aw), F32), jax.ShapeDtypeStruct((aw, LANES), F32),
                 jax.ShapeDtypeStruct(state.shape, F32)]
    return args, in_specs, out_specs, out_shape


def _mixer_step(*operands):
    args, in_specs, out_specs, out_shape = _mixer_step_operands(lambda b: b, *operands)
    nb, P, H, _ = operands[2].shape
    return pl.pallas_call(
        functools.partial(_mixer_step_kernel, P=P, aw=H * HEAD_DIM),
        grid=(nb,),
        in_specs=in_specs,
        out_specs=out_specs,
        out_shape=out_shape,
        compiler_params=_cparams(1),
        name="mixer_step",
    )(*args)


def _cast_operands(step_of, n_steps, weights):
    in_specs, out_specs, out_shape = [], [], []
    for w in weights:
        rows, cols = w.shape
        per = next(d for d in range(1, n_steps + 1)
                   if n_steps % d == 0 and rows % (n_steps // d) == 0 and rows // (n_steps // d) % 16 == 0)
        spec = pl.BlockSpec((rows * per // n_steps, cols), lambda *g, per=per: (step_of(*g) // per, 0))
        in_specs.append(spec)
        out_specs.append(spec)
        out_shape.append(jax.ShapeDtypeStruct(w.shape, BF16))
    return in_specs, out_specs, out_shape


def _attn_mixer_kernel(q_ref, k_ref, v_ref, sl_ref, *refs, T, P, aw, npair, n_cast):
    n_in = N_MIXER_IN + n_cast
    mixer_in, cast_in = refs[:N_MIXER_IN], refs[N_MIXER_IN:n_in]
    o_ref, *mixer_out = refs[n_in:n_in + 4]
    cast_out, scratch = refs[n_in + 4:n_in + 4 + n_cast], refs[n_in + 4 + n_cast:]
    for src, dst in zip(cast_in, cast_out):
        dst[...] = src[...].astype(BF16)
    _mixer_step_body(pl.program_id(0) * npair + pl.program_id(1), *mixer_in, *mixer_out, P=P, aw=aw)
    _prompt_attn_kernel(q_ref, k_ref, v_ref, sl_ref, o_ref, *scratch, T=T)


def _prompt_attn_and_mixer_step(q, k, v, slopes_tab, mixer_operands, weights):
    B, T, W = q.shape
    npair = W // LANES
    step_of = lambda b, hp: b * npair + hp
    args, in_specs, out_specs, out_shape = _mixer_step_operands(step_of, *mixer_operands)
    c_in, c_out, c_shape = _cast_operands(step_of, B * npair, weights)
    nb, P, H, _ = mixer_operands[2].shape
    assert nb == B * npair
    spec = pl.BlockSpec((1, T, LANES), lambda b, hp: (b, 0, hp))
    return pl.pallas_call(
        functools.partial(_attn_mixer_kernel, T=T, P=P, aw=H * HEAD_DIM, npair=npair, n_cast=len(weights)),
        grid=(B, npair),
        in_specs=[spec, spec, spec, _full(slopes_tab.shape)] + in_specs + c_in,
        out_specs=[spec] + out_specs + c_out,
        out_shape=[jax.ShapeDtypeStruct((B, T, W), BF16)] + out_shape + c_shape,
        scratch_shapes=_prompt_attn_scratch(T),
        compiler_params=_cparams(2),
        name="prompt_attn_mixer_step",
    )(q, k, v, slopes_tab, *args, *weights)


def _ffn_tail(x1, gate_pre, prev1, prev2, up, cw_ref, cb_ref, w2_ref, g_ref, b_ref, alpha):
    c = cw_ref[0:1, :] * prev2 + cw_ref[1:2, :] * prev1 + cw_ref[2:3, :] * gate_pre + cb_ref[...]
    h = c * _sigmoid(c) * up
    return _layer_norm(alpha * x1 + _dot(h, w2_ref[...]), g_ref[...], b_ref[...])


def _ffn_seq_kernel(att_ref, y_ref, x_ref, wo_ref, g1_ref, b1_ref, w1_ref, cw_ref, cb_ref, w2_ref, g_ref, b_ref,
                    o_ref, tail_ref, carry_ref, *, aw, dff, alpha):
    @pl.when(pl.program_id(1) == 0)
    def _():
        carry_ref[...] = jnp.zeros_like(carry_ref)

    tm = x_ref.shape[1]
    hr = tm // SUB_TILES
    spans = [slice(h * hr, (h + 1) * hr) for h in range(SUB_TILES)]
    n = range(SUB_TILES)
    row = lax.broadcasted_iota(jnp.int32, (hr, dff), 0)
    mixed = [_dot(att_ref[0, sp, :], wo_ref[0:aw, :]) + _dot(y_ref[0, sp, :], wo_ref[aw:, :]) for sp in spans]
    x1 = [_layer_norm(alpha * x_ref[0, spans[i], :] + mixed[i], g1_ref[...], b1_ref[...]) for i in n]
    hu = [_dot(x1[0], w1_ref[...])]
    last2 = carry_ref[0:2, :]
    gated, f = [], []
    for i in n:
        if i + 1 < SUB_TILES:
            hu.append(_dot(x1[i + 1], w1_ref[...]))
        gate_pre = hu[i][:, 0:dff]
        c0, c1 = last2[0:1, :], last2[1:2, :]
        prev1 = jnp.where(row == 0, c1, pltpu.roll(gate_pre, 1, axis=0))
        prev2 = jnp.where(row == 0, c0, jnp.where(row == 1, c1, pltpu.roll(gate_pre, 2, axis=0)))
        last2 = gate_pre[hr - 2:hr, :]
        c = cw_ref[0:1, :] * prev2 + cw_ref[1:2, :] * prev1 + cw_ref[2:3, :] * gate_pre + cb_ref[...]
        gated.append(c * _sigmoid(c) * hu[i][:, dff:])
        if i > 0:
            f.append(_dot(gated[i - 1], w2_ref[...]))
    f.append(_dot(gated[-1], w2_ref[...]))
    carry_ref[0:2, :] = last2
    tail_ref[0] = last2
    for i in n:
        o_ref[0, spans[i], :] = _layer_norm(alpha * x1[i] + f[i], g_ref[...], b_ref[...])


def _ffn_step_kernel(att_ref, y_ref, x_ref, wo_ref, g1_ref, b1_ref, p2_ref, p1_ref, w1_ref, cw_ref, cb_ref, w2_ref,
                     g_ref, b_ref, o_ref, gate_ref, *, dff, alpha):
    nb, aw = att_ref.shape
    h = _dot(att_ref[...], wo_ref[0:aw, :]) + _dot(y_ref[...].T[0:nb, :], wo_ref[aw:, :])
    x1 = _layer_norm(alpha * x_ref[...] + h, g1_ref[...], b1_ref[...])
    hu = _dot(x1, w1_ref[...])
    gate_pre = hu[:, 0:dff]
    gate_ref[...] = gate_pre
    o_ref[...] = _ffn_tail(x1, gate_pre, p1_ref[...], p2_ref[...], hu[:, dff:], cw_ref, cb_ref, w2_ref, g_ref,
                           b_ref, alpha)


def _resident(shape):
    nd = len(shape)
    return pl.BlockSpec(shape, lambda *_: (0,) * nd, pipeline_mode=pl.Buffered(1))


def _ffn_seq(att, y_rw, x, w_out, ln1_g, ln1_b, w1, conv_w, conv_b, w2, ln2_g, ln2_b, tm, alpha):
    B, T, D = x.shape
    aw = att.shape[-1]
    dff = w2.shape[0]
    half = pl.BlockSpec((1, tm, aw), lambda b, t: (b, t, 0))
    rows = pl.BlockSpec((1, tm, D), lambda b, t: (b, t, 0))
    return pl.pallas_call(
        functools.partial(_ffn_seq_kernel, aw=aw, dff=dff, alpha=alpha),
        grid=(B, T // tm),
        in_specs=[half, half, rows, _resident(w_out.shape), _full(ln1_g.shape), _full(ln1_b.shape),
                  _resident(w1.shape), _full(conv_w.shape), _full(conv_b.shape), _resident(w2.shape),
                  _full(ln2_g.shape), _full(ln2_b.shape)],
        out_specs=[rows, pl.BlockSpec((1, 2, dff), lambda b, t: (b, 0, 0))],
        out_shape=[jax.ShapeDtypeStruct((B, T, D), F32), jax.ShapeDtypeStruct((B, 2, dff), F32)],
        scratch_shapes=[pltpu.VMEM((8, dff), F32)],
        compiler_params=_cparams(2),
        name="ffn_seq",
    )(att, y_rw, x, w_out, ln1_g, ln1_b, w1, conv_w, conv_b, w2, ln2_g, ln2_b)


def _ffn_step(att, y_cols, x, w_out, ln1_g, ln1_b, prev2, prev1, w1, conv_w, conv_b, w2, ln2_g, ln2_b, alpha):
    nb, D = x.shape
    dff = w2.shape[0]
    return pl.pallas_call(
        functools.partial(_ffn_step_kernel, dff=dff, alpha=alpha),
        grid=(1,),
        in_specs=[_full(att.shape), _full(y_cols.shape), _full(x.shape), _resident(w_out.shape), _full(ln1_g.shape),
                  _full(ln1_b.shape),
                  _full(prev2.shape), _full(prev1.shape), _resident(w1.shape), _full(conv_w.shape),
                  _full(conv_b.shape), _resident(w2.shape), _full(ln2_g.shape), _full(ln2_b.shape)],
        out_specs=[_full((nb, D)), _full((nb, dff))],
        out_shape=[jax.ShapeDtypeStruct((nb, D), F32), jax.ShapeDtypeStruct((nb, dff), F32)],
        compiler_params=_cparams(1),
        name="ffn_step",
    )(att, y_cols, x, w_out, ln1_g, ln1_b, prev2, prev1, w1, conv_w, conv_b, w2, ln2_g, ln2_b)


def _layer_params(w_in, mu_shift, w0, w_lora_up, a0, a_lora_up, g_lora_up, k_k, k_a, r_k, aw):
    D, ncol = w_in.shape
    n_w, n_a, n_g = w_lora_up.shape[0], a_lora_up.shape[0], g_lora_up.shape[0]
    assert n_w + n_a == LANES and 3 * aw + 3 * aw + n_w + n_a + n_g == ncol
    pad = 0
    row = lambda a: a.reshape(1, -1).astype(F32)
    lane = jnp.arange(math.gcd(aw, MXU_TILE))
    bd = (lane[:, None] // HEAD_DIM == lane[None, :] // HEAD_DIM).astype(BF16)
    return dict(
        aw=aw,
        w_in=jnp.pad(w_in, ((0, 0), (0, pad))).astype(BF16),
        mu=jnp.pad(row(mu_shift), ((0, 0), (0, pad))),
        w0=row(w0), a0=row(a0), k_k=row(k_k), k_a=row(k_a), r_k=row(r_k),
        wl_pad=jnp.pad(w_lora_up, ((0, n_a), (0, 0))).astype(BF16),
        al_pad=jnp.pad(a_lora_up, ((n_w, 0), (0, 0))).astype(BF16),
        g_pad=jnp.pad(g_lora_up, ((0, pad), (0, 0))).astype(BF16),
        bd=bd,
    )


def kernel(x_prompt, x_sample, cache_k_win, cache_v_win, state_shift, state_wkv, state_conv, w_in, mu_shift, w0, w_lora_up, a0, a_lora_up, g_lora_up, k_k, k_a, r_k, gn_w, gn_b, w_out, ln1_g, ln1_b, w_ffn_in, conv_w, conv_b, w_ffn_out, ln2_g, ln2_b):
    depth = w_in.shape[0]
    alpha = (2.0 * depth) ** 0.25
    B, T, D = x_prompt.shape
    nb = x_sample.shape[0]
    assert x_sample.shape[1] == 1
    n_att = cache_k_win.shape[3]
    n_rw = state_wkv.shape[2]
    aw = n_att * HEAD_DIM
    assert n_rw * HEAD_DIM == aw and T % (Q_TILE * DIL_PATTERNS[-1][1]) == 0
    tm = min(256, T)
    tb = min(256, T)
    heads = jnp.arange(1, n_att + 1, dtype=F32)
    slopes = jnp.exp2(-8.0 * heads / n_att)
    slopes_tab = jnp.broadcast_to(slopes[:, None], (n_att, LANES))
    row = lambda a: a.reshape(1, -1)

    hp, hs = x_prompt, x_sample.reshape(nb, D)
    outs = [[] for _ in range(10)]
    for l in range(depth):
        prm = _layer_params(w_in[l], mu_shift[l], w0[l], w_lora_up[l], a0[l], a_lora_up[l], g_lora_up[l],
                            k_k[l], k_a[l], r_k[l], aw)
        late_weights = (w_out[l], w_ffn_in[l], w_ffn_out[l])
        gnw, gnb = row(gn_w[l]), row(gn_b[l])

        q, k, v, kt, vt, r, lw, k2, vr, al, be, g, bonus = _inproj_seq(hp, prm, min(2 * tm, T))
        x2 = jnp.concatenate([hs, state_shift[l]], axis=0)
        qkv_s, ks, vs, rows_s, col = _inproj_step(x2, prm)
        mixer_operands = (qkv_s, col, cache_k_win[l], cache_v_win[l], slopes, state_wkv[l], rows_s,
                          gn_w[l], gn_b[l])

        if nb == B * (aw // LANES):
            att, att_s, y_s, wkv_s, w_out_b, w1_b, w2_b = _prompt_attn_and_mixer_step(
                q, k, v, slopes_tab, mixer_operands, late_weights)
        else:
            att = _prompt_attn(q, k, v, slopes_tab)
            att_s, y_s, wkv_s = _mixer_step(*mixer_operands)
            w_out_b, w1_b, w2_b = (w.astype(BF16) for w in late_weights)
        y_rw, wkv_p = _rwkv_chunked(r, lw, k2, vr, al, be, g, bonus, gnw, gnb, tb)

        shift_p = hp[:, -1, :]
        hp, conv_p = _ffn_seq(att, y_rw, hp, w_out_b, row(ln1_g[l]), row(ln1_b[l]), w1_b, conv_w[l],
                              row(conv_b[l]), w2_b, row(ln2_g[l]), row(ln2_b[l]), min(2 * tm, T), alpha)
        shift_s = hs
        hs, gate_s = _ffn_step(att_s.reshape(nb, aw), y_s, hs, w_out_b, row(ln1_g[l]), row(ln1_b[l]),
                               state_conv[l][:, 0, :],
                               state_conv[l][:, 1, :], w1_b, conv_w[l], row(conv_b[l]), w2_b, row(ln2_g[l]),
                               row(ln2_b[l]), alpha)
        conv_s = jnp.stack([state_conv[l][:, 1, :], gate_s], axis=1)

        n_keep = min(DIL_PATTERNS[-1][0], T)
        win = lambda a: jnp.transpose(a.reshape(B, n_att, HEAD_DIM, T), (0, 3, 1, 2))[:, T - n_keep:]
        vals = (win(kt), win(vt), ks.reshape(nb, 1, n_att, HEAD_DIM), vs.reshape(nb, 1, n_att, HEAD_DIM),
                shift_p, shift_s, wkv_p, wkv_s, conv_p, conv_s)
        for lst, val in zip(outs, vals):
            lst.append(val)
    return (hp, hs.reshape(nb, 1, D)) + tuple(jnp.stack(lst) for lst in outs)
```
